```python
import jax, jax.numpy as jnp
from jax import lax
import numpy as np

D_MODEL = 1024
BATCH = 8
SEQ = 2048
DEPTH = 1
DEC_BATCH = 128
DEC_SEQ = 8
PAST_LEN = 16384
PAGE_SIZE = 128

MIX_WIDTH = D_MODEL
POOL_CH = MIX_WIDTH // 2
CONV_CH = MIX_WIDTH - POOL_CH
POOL_WINDOWS = (2, 4, 8, 16)
N_POOL_GROUPS = len(POOL_WINDOWS)
POOL_GROUP = POOL_CH // N_POOL_GROUPS
POOL_BUF = max(POOL_WINDOWS) - 1
CONV_TAPS = 31
CONV_BUF = CONV_TAPS - 1
IN_COLS = POOL_CH + 2 * CONV_CH
N_EXPERTS = 32
TOP_K = 4
D_FF = D_MODEL
SWIGLU_LIMIT = 7.0
SWIGLU_ALPHA = 1.702
MOE_BLOCK = 128
RMS_EPS = 1e-5
LN_EPS = 1e-5

kernel_name = "hymba_pool_conformer_moe_step"


def rmsnorm(x, g):
    xf = x.astype(jnp.float32)
    y = xf * lax.rsqrt(jnp.mean(xf * xf, axis=-1, keepdims=True) + RMS_EPS)
    return (y * g.astype(jnp.float32)).astype(x.dtype)


def pool_mix(u, buf, pos0, pool_w, pool_scale):
    B, T, _ = u.shape
    cat = jnp.concatenate([buf.astype(u.dtype), u], axis=1)
    cs = jnp.cumsum(cat.astype(jnp.float32), axis=1)
    cs = jnp.pad(cs, ((0, 0), (1, 0), (0, 0)))
    pos = pos0 + jnp.arange(T, dtype=jnp.int32)
    means = []
    for g, w in enumerate(POOL_WINDOWS):
        c0, c1 = g * POOL_GROUP, (g + 1) * POOL_GROUP
        hi = cs[:, POOL_BUF + 1:POOL_BUF + 1 + T, c0:c1]
        lo = cs[:, POOL_BUF + 1 - w:POOL_BUF + 1 - w + T, c0:c1]
        cnt = jnp.minimum(w, pos + 1).astype(jnp.float32)
        means.append((hi - lo) / cnt[None, :, None])
    mean = jnp.stack(means, axis=2)
    z = mean - u.reshape(B, T, N_POOL_GROUPS, POOL_GROUP).astype(jnp.float32)
    y = jnp.einsum('btgc,gcd->btgd', z.astype(u.dtype), pool_w).reshape(B, T, POOL_CH)
    return y * pool_scale, cat[:, -POOL_BUF:]


def conv_mix(v, buf, conv_w, conv_b, ln_g, ln_b):
    a, gt = v[..., :CONV_CH], v[..., CONV_CH:]
    glu = a * jax.nn.sigmoid(gt)
    cat = jnp.concatenate([buf.astype(glu.dtype), glu], axis=1)
    d = lax.conv_general_dilated(cat, conv_w[:, None, :].astype(cat.dtype), window_strides=(1,),
                                 padding='VALID', dimension_numbers=('NWC', 'WIO', 'NWC'),
                                 feature_group_count=CONV_CH) + conv_b
    df = d.astype(jnp.float32)
    mu = jnp.mean(df, axis=-1, keepdims=True)
    var = jnp.mean(jnp.square(df - mu), axis=-1, keepdims=True)
    n = ((df - mu) * lax.rsqrt(var + LN_EPS) * ln_g.astype(jnp.float32) + ln_b.astype(jnp.float32)).astype(d.dtype)
    return jax.nn.silu(n), cat[:, -CONV_BUF:]


def moe(x2, w_router, b_router, w_gate_up, b_gate_up, w_down, b_down):
    N, D = x2.shape
    logits = (x2 @ w_router + b_router).astype(jnp.float32)
    top_val, top_idx = lax.top_k(logits, TOP_K)
    gate = jax.nn.softmax(top_val, axis=-1)
    M = N * TOP_K
    flat_e = top_idx.reshape(-1).astype(jnp.int32)
    flat_tok = jnp.repeat(jnp.arange(N, dtype=jnp.int32), TOP_K)
    flat_w = gate.reshape(-1)
    order = jnp.argsort(flat_e)
    sorted_e = flat_e[order]
    counts = jnp.bincount(flat_e, length=N_EXPERTS).astype(jnp.int32)
    padded = (counts + MOE_BLOCK - 1) // MOE_BLOCK * MOE_BLOCK
    offs = jnp.cumsum(counts) - counts
    pad_end = jnp.cumsum(padded)
    pad_offs = pad_end - padded
    dest = pad_offs[sorted_e] + (jnp.arange(M, dtype=jnp.int32) - offs[sorted_e])
    n_blocks = -(-M // MOE_BLOCK) + N_EXPERTS
    P = n_blocks * MOE_BLOCK
    row_tok = jnp.full((P,), N, jnp.int32).at[dest].set(flat_tok[order])
    row_w = jnp.zeros((P,), jnp.float32).at[dest].set(flat_w[order])
    block_start = jnp.arange(n_blocks, dtype=jnp.int32) * MOE_BLOCK
    block_e = jnp.minimum(jnp.searchsorted(pad_end, block_start, side='right'), N_EXPERTS - 1).astype(jnp.int32)
    x_pad = jnp.concatenate([x2, jnp.zeros((1, D), x2.dtype)], axis=0)
    xb = x_pad[row_tok].reshape(n_blocks, MOE_BLOCK, D)

    def expert_block(args):
        xblk, e = args
        gu = xblk @ w_gate_up[e] + b_gate_up[e]
        g, up = gu[..., :D_FF], gu[..., D_FF:]
        g = jnp.minimum(g, SWIGLU_LIMIT)
        up = jnp.clip(up, -SWIGLU_LIMIT, SWIGLU_LIMIT)
        hdn = (up + 1.0) * (g * jax.nn.sigmoid(SWIGLU_ALPHA * g))
        return hdn @ w_down[e] + b_down[e]

    yb = lax.map(expert_block, (xb, block_e)).reshape(P, D)
    yb = yb * row_w[:, None].astype(yb.dtype)
    return jax.ops.segment_sum(yb, row_tok, num_segments=N + 1)[:N]


def trunk(x, pool_buf, conv_buf, pos0, g_mix, w_in, pool_w, pool_scale, conv_w, conv_b,
          conv_ln_g, conv_ln_b, w_out, g_moe, w_router, b_router, w_gate_up, b_gate_up,
          w_down, b_down, g_final):
    B, T, D = x.shape
    new_pool, new_conv = [], []
    for l in range(DEPTH):
        xn = rmsnorm(x, g_mix[l])
        proj = xn @ w_in[l]
        y_pool, nb_pool = pool_mix(proj[..., :POOL_CH], pool_buf[l], pos0, pool_w[l], pool_scale[l])
        y_conv, nb_conv = conv_mix(proj[..., POOL_CH:], conv_buf[l], conv_w[l], conv_b[l],
                                   conv_ln_g[l], conv_ln_b[l])
        h = x + jnp.concatenate([y_pool, y_conv], axis=-1) @ w_out[l]
        hn = rmsnorm(h, g_moe[l]).reshape(B * T, D)
        x = h + moe(hn, w_router[l], b_router[l], w_gate_up[l], b_gate_up[l],
                    w_down[l], b_down[l]).reshape(B, T, D)
        new_pool.append(nb_pool)
        new_conv.append(nb_conv)
    return rmsnorm(x, g_final), jnp.stack(new_pool), jnp.stack(new_conv)


def setup_inputs(seed: int = 0) -> dict:
    key = jax.random.key(seed)
    ks = jax.random.split(key, 24)
    f = jnp.float32
    nrm = lambda k, shape, s: jax.random.normal(k, shape, f) * s
    return {
        "x_prompt": nrm(ks[0], (BATCH, SEQ, D_MODEL), 1.0),
        "x_sample": nrm(ks[1], (DEC_BATCH, DEC_SEQ, D_MODEL), 1.0),
        "state_pool": nrm(ks[2], (DEPTH, DEC_BATCH, POOL_BUF, POOL_CH), 1.0),
        "state_conv": nrm(ks[3], (DEPTH, DEC_BATCH, CONV_BUF, CONV_CH), 0.5),
        "g_mix": 1.0 + nrm(ks[4], (DEPTH, D_MODEL), 0.1),
        "w_in": nrm(ks[5], (DEPTH, D_MODEL, IN_COLS), D_MODEL ** -0.5),
        "pool_w": nrm(ks[6], (DEPTH, N_POOL_GROUPS, POOL_GROUP, POOL_GROUP), POOL_GROUP ** -0.5),
        "pool_scale": 1.0 + nrm(ks[7], (DEPTH, POOL_CH), 0.1),
        "conv_w": nrm(ks[8], (DEPTH, CONV_TAPS, CONV_CH), CONV_TAPS ** -0.5),
        "conv_b": nrm(ks[9], (DEPTH, CONV_CH), 0.01),
        "conv_ln_g": 1.0 + nrm(ks[10], (DEPTH, CONV_CH), 0.1),
        "conv_ln_b": nrm(ks[11], (DEPTH, CONV_CH), 0.01),
        "w_out": nrm(ks[12], (DEPTH, MIX_WIDTH, D_MODEL), MIX_WIDTH ** -0.5),
        "g_moe": 1.0 + nrm(ks[13], (DEPTH, D_MODEL), 0.1),
        "w_router": nrm(ks[14], (DEPTH, D_MODEL, N_EXPERTS), D_MODEL ** -0.5),
        "b_router": nrm(ks[15], (DEPTH, N_EXPERTS), 0.01),
        "w_gate_up": nrm(ks[16], (DEPTH, N_EXPERTS, D_MODEL, 2 * D_FF), D_MODEL ** -0.5),
        "b_gate_up": nrm(ks[17], (DEPTH, N_EXPERTS, 2 * D_FF), 0.01),
        "w_down": nrm(ks[18], (DEPTH, N_EXPERTS, D_FF, D_MODEL), D_FF ** -0.5),
        "b_down": nrm(ks[19], (DEPTH, N_EXPERTS, D_MODEL), 0.01),
        "g_final": 1.0 + nrm(ks[20], (D_MODEL,), 0.1),
    }


def reference(x_prompt, x_sample, state_pool, state_conv, g_mix, w_in, pool_w, pool_scale,
              conv_w, conv_b, conv_ln_g, conv_ln_b, w_out, g_moe, w_router, b_router,
              w_gate_up, b_gate_up, w_down, b_down, g_final):
    B = x_prompt.shape[0]
    zero_pool = jnp.zeros((DEPTH, B, POOL_BUF, POOL_CH), x_prompt.dtype)
    zero_conv = jnp.zeros((DEPTH, B, CONV_BUF, CONV_CH), x_prompt.dtype)
    y_prompt, pool_p, conv_p = trunk(x_prompt, zero_pool, zero_conv, 0, g_mix, w_in, pool_w,
                                     pool_scale, conv_w, conv_b, conv_ln_g, conv_ln_b, w_out,
                                     g_moe, w_router, b_router, w_gate_up, b_gate_up, w_down,
                                     b_down, g_final)
    y_sample, pool_s, conv_s = trunk(x_sample, state_pool, state_conv, PAST_LEN, g_mix, w_in,
                                     pool_w, pool_scale, conv_w, conv_b, conv_ln_g, conv_ln_b,
                                     w_out, g_moe, w_router, b_router, w_gate_up, b_gate_up,
                                     w_down, b_down, g_final)
    return (y_prompt, y_sample, pool_p, conv_p, pool_s, conv_s)
```

```python
import functools

import jax
import jax.numpy as jnp
from jax import lax
from jax.experimental import pallas as pl
from jax.experimental.pallas import tpu as pltpu

D_MODEL = 1024
POOL_CH = 512
CONV_CH = 512
POOL_WINDOWS = (2, 4, 8, 16)
POOL_GROUP = 128
POOL_BUF = 15
CONV_TAPS = 31
CONV_BUF = 30
IN_COLS = POOL_CH + 2 * CONV_CH
N_EXPERTS = 32
TOP_K = 4
D_FF = 1024
SWIGLU_LIMIT = 7.0
SWIGLU_ALPHA = 1.702
RMS_EPS = 1e-5
LN_EPS = 1e-5

LANES = 128
SUBLANES = 8
VMEM_LIMIT_BYTES = 56 * 1024 * 1024

PROMPT_ROWS = 512
SAMPLE_SEQS = 64
CONV_CHUNK = 32
MOE_ROWS = 256
DUMP_ROWS = 2 * MOE_ROWS
FINAL_ROWS = 256

POOL_HIST = 16
CONV_HIST = 32

BF16 = jnp.bfloat16
F32 = jnp.float32


def _rmsnorm(x, g):
    ms = jnp.mean(x * x, axis=-1, keepdims=True)
    return x * lax.rsqrt(ms + RMS_EPS) * g


def _sigmoid(x):
    return 1.0 / (1.0 + jnp.exp(-x))


def _layernorm_silu(d, g, b):
    mu = jnp.mean(d, axis=-1, keepdims=True)
    c = d - mu
    var = jnp.mean(c * c, axis=-1, keepdims=True)
    n = c * lax.rsqrt(var + LN_EPS) * g + b
    return n * _sigmoid(n)


def _route(hn_bf, wr_t_ref, br_ref, idx_ref, gate_ref):
    rows = hn_bf.shape[0]
    lt = lax.dot_general(wr_t_ref[...], hn_bf, (((1,), (1,)), ((), ())),
                         preferred_element_type=F32)
    lt = lt + br_ref[...]
    eidx = lax.broadcasted_iota(jnp.int32, (N_EXPERTS, rows), 0)
    vals, idxs = [], []
    for _ in range(TOP_K):
        m = jnp.max(lt, axis=0, keepdims=True)
        sel = jnp.min(jnp.where(lt == m, eidx, N_EXPERTS), axis=0, keepdims=True)
        vals.append(m)
        idxs.append(sel)
        lt = jnp.where(eidx == sel, -jnp.inf, lt)
    v = jnp.concatenate(vals, axis=0)
    e = jnp.exp(v - vals[0])
    gate_ref[...] = e / jnp.sum(e, axis=0, keepdims=True)
    idx_ref[...] = jnp.concatenate(idxs, axis=0)


def _mix_out_and_route(x, mixbuf, w_out_ref, g_moe_ref, wr_t_ref, br_ref,
                       h_ref, hn_ref, idx_ref, gate_ref):
    h = x + jnp.dot(mixbuf[...], w_out_ref[...], preferred_element_type=F32)
    h_ref[...] = h
    hn = _rmsnorm(h, g_moe_ref[...])
    hn_ref[...] = hn
    _route(hn.astype(BF16), wr_t_ref, br_ref, idx_ref, gate_ref)


def _front_prompt_kernel(x_ref, sp_ref, sc_ref, g_mix_ref, w_in_ref, pool_w_ref, pool_scale_ref,
                         conv_w_ref, conv_b_ref, ln_g_ref, ln_b_ref, w_out_ref, g_moe_ref,
                         wr_t_ref, br_ref,
                         h_ref, hn_ref, idx_ref, gate_ref, np_ref, nc_ref,
                         ubuf, gbuf, dbuf, mixbuf, *, pos0, rows):
    t = pl.program_id(1)

    @pl.when(t == 0)
    def _():
        ubuf[pl.ds(POOL_HIST - POOL_BUF, POOL_BUF), :] = sp_ref[0, 0]
        gbuf[pl.ds(CONV_HIST - CONV_BUF, CONV_BUF), :] = sc_ref[0, 0]

    x = x_ref[0]
    xn = _rmsnorm(x, g_mix_ref[...]).astype(BF16)
    proj = jnp.dot(xn, w_in_ref[...], preferred_element_type=F32)

    u = proj[:, :POOL_CH]
    ubuf[pl.ds(POOL_HIST, rows), :] = u
    pos = pos0 + t * rows + lax.broadcasted_iota(jnp.int32, (rows, 1), 0)
    for g, w in enumerate(POOL_WINDOWS):
        cs = slice(g * POOL_GROUP, (g + 1) * POOL_GROUP)
        ug = u[:, cs]
        s = ug
        for j in range(1, w):
            s = s + ubuf[pl.ds(POOL_HIST - j, rows), cs]
        cnt = jnp.minimum(w, pos + 1).astype(F32)
        z = (s / cnt - ug).astype(BF16)
        yg = jnp.dot(z, pool_w_ref[g], preferred_element_type=F32)
        mixbuf[:, cs] = (yg * pool_scale_ref[:, cs]).astype(BF16)

    glu = proj[:, POOL_CH:POOL_CH + CONV_CH] * _sigmoid(proj[:, POOL_CH + CONV_CH:])
    gbuf[pl.ds(CONV_HIST, rows), :] = glu

    for r0 in range(0, rows, CONV_CHUNK):
        for ct in range(CONV_CH // LANES):
            cs = slice(ct * LANES, (ct + 1) * LANES)
            acc = jnp.broadcast_to(conv_b_ref[:, cs], (CONV_CHUNK, LANES))
            for k in range(CONV_TAPS):
                acc = acc + conv_w_ref[k:k + 1, cs] * gbuf[pl.ds(r0 + (CONV_HIST - CONV_BUF + k), CONV_CHUNK), cs]
            dbuf[pl.ds(r0, CONV_CHUNK), cs] = acc
    y_conv = _layernorm_silu(dbuf[...], ln_g_ref[...], ln_b_ref[...])
    mixbuf[:, POOL_CH:] = y_conv.astype(BF16)

    new_pool = ubuf[pl.ds(rows + POOL_HIST - POOL_BUF, POOL_BUF), :]
    new_conv = gbuf[pl.ds(rows + CONV_HIST - CONV_BUF, CONV_BUF), :]
    np_ref[0, 0] = new_pool
    nc_ref[0, 0] = new_conv
    ubuf[pl.ds(POOL_HIST - POOL_BUF, POOL_BUF), :] = new_pool
    gbuf[pl.ds(CONV_HIST - CONV_BUF, CONV_BUF), :] = new_conv

    _mix_out_and_route(x, mixbuf, w_out_ref, g_moe_ref, wr_t_ref, br_ref,
                       h_ref, hn_ref, idx_ref, gate_ref)


def _front_sample_kernel(x_ref, sp_ref, sc_ref, g_mix_ref, w_in_ref, pool_w_ref, pool_scale_ref,
                         conv_w_ref, conv_b_ref, ln_g_ref, ln_b_ref, w_out_ref, g_moe_ref,
                         wr_t_ref, br_ref, hbuf_in, hnbuf_in,
                         h_ref, hn_ref, idx_ref, gate_ref, np_ref, nc_ref,
                         xbuf, dbuf, mixbuf, *, pos0, seqs, steps):
    del hbuf_in, hnbuf_in
    for t in range(steps):
        xbuf[pl.ds(t * seqs, seqs), :] = x_ref[:, t * D_MODEL:(t + 1) * D_MODEL]
    x = xbuf[...]
    xn = _rmsnorm(x, g_mix_ref[...]).astype(BF16)
    proj = jnp.dot(xn, w_in_ref[...], preferred_element_type=F32)

    def slab(a, t):
        return a[t * seqs:(t + 1) * seqs]

    u = proj[:, :POOL_CH]
    cat = [sp_ref[:, j * POOL_CH:(j + 1) * POOL_CH] for j in range(POOL_BUF)]
    cat += [slab(u, t) for t in range(steps)]
    for t in range(steps):
        for g, w in enumerate(POOL_WINDOWS):
            cs = slice(g * POOL_GROUP, (g + 1) * POOL_GROUP)
            s = cat[POOL_BUF + t][:, cs]
            for j in range(1, w):
                s = s + cat[POOL_BUF + t - j][:, cs]
            cnt = float(min(w, pos0 + t + 1))
            z = (s / cnt - cat[POOL_BUF + t][:, cs]).astype(BF16)
            yg = jnp.dot(z, pool_w_ref[g], preferred_element_type=F32)
            mixbuf[pl.ds(t * seqs, seqs), cs] = (yg * pool_scale_ref[:, cs]).astype(BF16)
    for j in range(POOL_BUF):
        np_ref[:, j * POOL_CH:(j + 1) * POOL_CH] = cat[len(cat) - POOL_BUF + j]

    glu = proj[:, POOL_CH:POOL_CH + CONV_CH] * _sigmoid(proj[:, POOL_CH + CONV_CH:])
    ccat = [sc_ref[:, j * CONV_CH:(j + 1) * CONV_CH] for j in range(CONV_BUF)]
    ccat += [slab(glu, t) for t in range(steps)]
    for t in range(steps):
        acc = jnp.broadcast_to(conv_b_ref[...], (seqs, CONV_CH))
        for k in range(CONV_TAPS):
            acc = acc + conv_w_ref[k:k + 1, :] * ccat[t + k]
        dbuf[pl.ds(t * seqs, seqs), :] = acc
    for j in range(CONV_BUF):
        nc_ref[:, j * CONV_CH:(j + 1) * CONV_CH] = ccat[len(ccat) - CONV_BUF + j]
    y_conv = _layernorm_silu(dbuf[...], ln_g_ref[...], ln_b_ref[...])
    mixbuf[:, POOL_CH:] = y_conv.astype(BF16)

    _mix_out_and_route(x, mixbuf, w_out_ref, g_moe_ref, wr_t_ref, br_ref,
                       h_ref, hn_ref, idx_ref, gate_ref)


def _const_spec(shape):
    return pl.BlockSpec(shape, lambda *_: (0,) * len(shape))


def _front_weight_specs():
    return [
        _const_spec((1, D_MODEL)),
        _const_spec((D_MODEL, IN_COLS)),
        _const_spec((len(POOL_WINDOWS), POOL_GROUP, POOL_GROUP)),
        _const_spec((1, POOL_CH)),
        _const_spec((CONV_TAPS, CONV_CH)),
        _const_spec((1, CONV_CH)),
        _const_spec((1, CONV_CH)),
        _const_spec((1, CONV_CH)),
        _const_spec((D_MODEL, D_MODEL)),
        _const_spec((1, D_MODEL)),
        _const_spec((N_EXPERTS, D_MODEL)),
        _const_spec((N_EXPERTS, 1)),
    ]


def _front_prompt(x, state_pool, state_conv, weights, *, pos0, n_all):
    nb, nt, _ = x.shape
    rows = PROMPT_ROWS
    tiles = nt // rows
    row_map = lambda b, t: (b * tiles + t, 0)
    kern = functools.partial(_front_prompt_kernel, pos0=pos0, rows=rows)
    outs = pl.pallas_call(
        kern,
        grid=(nb, tiles),
        in_specs=[pl.BlockSpec((1, rows, D_MODEL), lambda b, t: (b, t, 0)),
                  pl.BlockSpec((1, 1, POOL_BUF, POOL_CH), lambda b, t: (0, b, 0, 0)),
                  pl.BlockSpec((1, 1, CONV_BUF, CONV_CH), lambda b, t: (0, b, 0, 0))]
                 + _front_weight_specs(),
        out_specs=[pl.BlockSpec((rows, D_MODEL), row_map),
                   pl.BlockSpec((rows, D_MODEL), row_map),
                   pl.BlockSpec((TOP_K, rows), lambda b, t: (0, b * tiles + t)),
                   pl.BlockSpec((TOP_K, rows), lambda b, t: (0, b * tiles + t)),
                   pl.BlockSpec((1, 1, POOL_BUF, POOL_CH), lambda b, t: (0, b, 0, 0)),
                   pl.BlockSpec((1, 1, CONV_BUF, CONV_CH), lambda b, t: (0, b, 0, 0))],
        out_shape=[jax.ShapeDtypeStruct((n_all, D_MODEL), F32),
                   jax.ShapeDtypeStruct((n_all, D_MODEL), F32),
                   jax.ShapeDtypeStruct((TOP_K, nb * nt), jnp.int32),
                   jax.ShapeDtypeStruct((TOP_K, nb * nt), F32),
                   jax.ShapeDtypeStruct((1, nb, POOL_BUF, POOL_CH), F32),
                   jax.ShapeDtypeStruct((1, nb, CONV_BUF, CONV_CH), F32)],
        scratch_shapes=[pltpu.VMEM((POOL_HIST + rows, POOL_CH), F32),
                        pltpu.VMEM((CONV_HIST + rows, CONV_CH), F32),
                        pltpu.VMEM((rows, CONV_CH), F32),
                        pltpu.VMEM((rows, D_MODEL), BF16)],
        compiler_params=pltpu.CompilerParams(
            dimension_semantics=("arbitrary", "arbitrary"), vmem_limit_bytes=VMEM_LIMIT_BYTES),
        name="front_prompt",
    )(x, state_pool, state_conv, *weights)
    return outs


def _front_sample(x, state_pool, state_conv, weights, h_all, hn_all, *, pos0, row0):
    nb, nt, _ = x.shape
    seqs = SAMPLE_SEQS
    rows = seqs * nt
    blk0 = row0 // rows
    n_all = h_all.shape[0]
    kern = functools.partial(_front_sample_kernel, pos0=pos0, seqs=seqs, steps=nt)
    any_spec = pl.BlockSpec(memory_space=pl.ANY)
    outs = pl.pallas_call(
        kern,
        grid=(nb // seqs,),
        in_specs=[pl.BlockSpec((seqs, nt * D_MODEL), lambda i: (i, 0)),
                  pl.BlockSpec((seqs, POOL_BUF * POOL_CH), lambda i: (i, 0)),
                  pl.BlockSpec((seqs, CONV_BUF * CONV_CH), lambda i: (i, 0))]
                 + _front_weight_specs() + [any_spec, any_spec],
        out_specs=[pl.BlockSpec((rows, D_MODEL), lambda i: (blk0 + i, 0)),
                   pl.BlockSpec((rows, D_MODEL), lambda i: (blk0 + i, 0)),
                   pl.BlockSpec((TOP_K, rows), lambda i: (0, i)),
                   pl.BlockSpec((TOP_K, rows), lambda i: (0, i)),
                   pl.BlockSpec((seqs, POOL_BUF * POOL_CH), lambda i: (i, 0)),
                   pl.BlockSpec((seqs, CONV_BUF * CONV_CH), lambda i: (i, 0))],
        out_shape=[jax.ShapeDtypeStruct((n_all, D_MODEL), F32),
                   jax.ShapeDtypeStruct((n_all, D_MODEL), F32),
                   jax.ShapeDtypeStruct((TOP_K, nb * nt), jnp.int32),
                   jax.ShapeDtypeStruct((TOP_K, nb * nt), F32),
                   jax.ShapeDtypeStruct((nb, POOL_BUF * POOL_CH), F32),
                   jax.ShapeDtypeStruct((nb, CONV_BUF * CONV_CH), F32)],
        scratch_shapes=[pltpu.VMEM((rows, D_MODEL), F32),
                        pltpu.VMEM((rows, CONV_CH), F32),
                        pltpu.VMEM((rows, D_MODEL), BF16)],
        input_output_aliases={15: 0, 16: 1},
        compiler_params=pltpu.CompilerParams(
            dimension_semantics=("arbitrary",), vmem_limit_bytes=VMEM_LIMIT_BYTES),
        name="front_sample",
    )(x.reshape(nb, nt * D_MODEL), state_pool.reshape(nb, POOL_BUF * POOL_CH),
      state_conv.reshape(nb, CONV_BUF * CONV_CH), *weights, h_all, hn_all)
    return outs


def _moe_kernel(be_ref, nused_ref, rt_cur, rt_nxt, rd_cur, hn_hbm, wgu_ref, bgu_ref, wd_ref, bd_ref,
                y4_hbm, xbuf, obuf, wgu_bf, wd_bf, gsem, ssem):
    b = pl.program_id(0)
    n_used = nused_ref[0]
    slot = b % 2

    def gather_start(rt, s):
        def body(r, c):
            tok = rt[0, 0, r]
            pltpu.make_async_copy(hn_hbm.at[pl.ds(tok, 1)], xbuf.at[s, pl.ds(r, 1)], gsem.at[s]).start()
            return c
        lax.fori_loop(0, MOE_ROWS, body, 0, unroll=8)

    def gather_wait(s):
        pltpu.make_async_copy(hn_hbm.at[pl.ds(0, MOE_ROWS)], xbuf.at[s], gsem.at[s]).wait()

    def scatter_start(s):
        def body(r, c):
            dst = rd_cur[0, 0, r]
            pltpu.make_async_copy(obuf.at[s, pl.ds(r, 1)], y4_hbm.at[pl.ds(dst, 1)], ssem.at[s]).start()
            return c
        lax.fori_loop(0, MOE_ROWS, body, 0, unroll=8)

    def scatter_wait(s):
        pltpu.make_async_copy(obuf.at[s], y4_hbm.at[pl.ds(0, MOE_ROWS)], ssem.at[s]).wait()

    @pl.when(b == 0)
    def _():
        gather_start(rt_cur, 0)

    @pl.when(b + 1 < n_used)
    def _():
        gather_start(rt_nxt, 1 - slot)

    @pl.when(jnp.logical_and(b >= 2, b - 2 < n_used))
    def _():
        scatter_wait(slot)

    @pl.when(b < n_used)
    def _():
        e_cur = be_ref[b]
        e_prev = be_ref[jnp.maximum(b - 1, 0)]

        @pl.when(jnp.logical_or(b == 0, e_cur != e_prev))
        def _():
            wgu_bf[...] = wgu_ref[0].astype(BF16)
            wd_bf[...] = wd_ref[0].astype(BF16)

        gather_wait(slot)
        x = xbuf[slot].astype(BF16)
        gu = jnp.dot(x, wgu_bf[...], preferred_element_type=F32) + bgu_ref[0]
        g = jnp.minimum(gu[:, :D_FF], SWIGLU_LIMIT)
        up = jnp.clip(gu[:, D_FF:], -SWIGLU_LIMIT, SWIGLU_LIMIT)
        hdn = (up + 1.0) * (g * _sigmoid(SWIGLU_ALPHA * g))
        y = jnp.dot(hdn.astype(BF16), wd_bf[...], preferred_element_type=F32) + bd_ref[0]
        obuf[slot] = y
        scatter_start(slot)


def _moe(hn_all, block_e, n_used, row_tok, row_dst, w_gate_up, b_gate_up, w_down, b_down, n_rows_out):
    n_blocks = block_e.shape[0]
    steps = n_blocks + 2
    last = n_blocks - 1

    def e_map(b, be, nu):
        return (be[jnp.minimum(b, nu[0] - 1)], 0, 0)

    smem_blk = lambda f: pl.BlockSpec((1, 1, MOE_ROWS), f, memory_space=pltpu.SMEM)
    grid_spec = pltpu.PrefetchScalarGridSpec(
        num_scalar_prefetch=2,
        grid=(steps,),
        in_specs=[smem_blk(lambda b, be, nu: (jnp.minimum(b, last), 0, 0)),
                  smem_blk(lambda b, be, nu: (jnp.minimum(b + 1, last), 0, 0)),
                  smem_blk(lambda b, be, nu: (jnp.minimum(b, last), 0, 0)),
                  pl.BlockSpec(memory_space=pl.ANY),
                  pl.BlockSpec((1, D_MODEL, 2 * D_FF), e_map),
                  pl.BlockSpec((1, 1, 2 * D_FF), e_map),
                  pl.BlockSpec((1, D_FF, D_MODEL), e_map),
                  pl.BlockSpec((1, 1, D_MODEL), e_map)],
        out_specs=pl.BlockSpec(memory_space=pl.ANY),
        scratch_shapes=[pltpu.VMEM((2, MOE_ROWS, D_MODEL), F32),
                        pltpu.VMEM((2, MOE_ROWS, D_MODEL), F32),
                        pltpu.VMEM((D_MODEL, 2 * D_FF), BF16),
                        pltpu.VMEM((D_FF, D_MODEL), BF16),
                        pltpu.SemaphoreType.DMA((2,)),
                        pltpu.SemaphoreType.DMA((2,))],
    )
    return pl.pallas_call(
        _moe_kernel,
        grid_spec=grid_spec,
        out_shape=jax.ShapeDtypeStruct((n_rows_out, D_MODEL), F32),
        compiler_params=pltpu.CompilerParams(
            dimension_semantics=("arbitrary",), vmem_limit_bytes=VMEM_LIMIT_BYTES),
        name="moe_experts",
    )(block_e, n_used, row_tok, row_tok, row_dst, hn_all, w_gate_up,
      b_gate_up.reshape(N_EXPERTS, 1, 2 * D_FF), w_down, b_down.reshape(N_EXPERTS, 1, D_MODEL))


def _final_kernel(h_ref, y4_ref, gate_ref, g_ref, o_ref):
    acc = h_ref[...]
    gate = gate_ref[...]
    for k in range(TOP_K):
        acc = acc + gate[:, k:k + 1] * y4_ref[:, k * D_MODEL:(k + 1) * D_MODEL]
    o_ref[...] = _rmsnorm(acc, g_ref[...])


def _final(h_all, y4, gates, g_final, *, row0, n_rows, out_rows, out_cols, out_map):
    rows = out_rows
    blk0 = row0 // rows
    return pl.pallas_call(
        _final_kernel,
        grid=(n_rows // rows,),
        in_specs=[pl.BlockSpec((rows, D_MODEL), lambda i: (blk0 + i, 0)),
                  pl.BlockSpec((rows, TOP_K * D_MODEL), lambda i: (blk0 + i, 0)),
                  pl.BlockSpec((rows, TOP_K), lambda i: (blk0 + i, 0)),
                  _const_spec((1, D_MODEL))],
        out_specs=pl.BlockSpec((rows, D_MODEL), out_map),
        out_shape=jax.ShapeDtypeStruct((n_rows * D_MODEL // out_cols, out_cols), F32),
        compiler_params=pltpu.CompilerParams(
            dimension_semantics=("arbitrary",), vmem_limit_bytes=VMEM_LIMIT_BYTES),
        name="final_norm",
    )(h_all, y4, gates, g_final)


def _routing_tables(idx_t, n_tokens):
    m = n_tokens * TOP_K
    flat_e = idx_t.T.reshape(-1)
    order = jnp.argsort(flat_e).astype(jnp.int32)
    sorted_e = flat_e[order]
    counts = jnp.bincount(flat_e, length=N_EXPERTS).astype(jnp.int32)
    padded = (counts + MOE_ROWS - 1) // MOE_ROWS * MOE_ROWS
    offs = jnp.cumsum(counts) - counts
    pad_end = jnp.cumsum(padded)
    pad_offs = pad_end - padded
    dest = pad_offs[sorted_e] + (jnp.arange(m, dtype=jnp.int32) - offs[sorted_e])
    n_blocks = -(-m // MOE_ROWS) + N_EXPERTS
    p = n_blocks * MOE_ROWS
    row_asg = jnp.full((p,), -1, jnp.int32).at[dest].set(order)
    ridx = jnp.arange(p, dtype=jnp.int32)
    row_tok = jnp.where(row_asg >= 0, row_asg // TOP_K, 0)
    row_dst = jnp.where(row_asg >= 0, row_asg, m + ridx % DUMP_ROWS)
    block_start = jnp.arange(n_blocks, dtype=jnp.int32) * MOE_ROWS
    block_e = jnp.minimum(jnp.searchsorted(pad_end, block_start, side='right'),
                          N_EXPERTS - 1).astype(jnp.int32)
    n_used = (pad_end[-1:] // MOE_ROWS).astype(jnp.int32)
    return (block_e, n_used, row_tok.reshape(n_blocks, 1, MOE_ROWS),
            row_dst.reshape(n_blocks, 1, MOE_ROWS))


def kernel(x_prompt, x_sample, state_pool, state_conv, g_mix, w_in, pool_w, pool_scale, conv_w, conv_b,
           conv_ln_g, conv_ln_b, w_out, g_moe, w_router, b_router, w_gate_up, b_gate_up, w_down, b_down,
           g_final):
    nb_p, nt_p, _ = x_prompt.shape
    nb_s, nt_s, _ = x_sample.shape
    n_p, n_s = nb_p * nt_p, nb_s * nt_s
    n_all = n_p + n_s
    past_len = 16384
    assert nt_p % PROMPT_ROWS == 0 and nb_s % SAMPLE_SEQS == 0
    assert n_p % (SAMPLE_SEQS * nt_s) == 0 and n_all % FINAL_ROWS == 0

    weights = (g_mix[0][None], w_in[0].astype(BF16), pool_w[0].astype(BF16), pool_scale[0][None],
               conv_w[0], conv_b[0][None], conv_ln_g[0][None], conv_ln_b[0][None],
               w_out[0].astype(BF16), g_moe[0][None], w_router[0].T.astype(BF16),
               b_router[0][:, None])

    zero_pool = jnp.zeros((1, nb_p, POOL_BUF, POOL_CH), F32)
    zero_conv = jnp.zeros((1, nb_p, CONV_BUF, CONV_CH), F32)
    h_all, hn_all, idx_p, gate_p, pool_p, conv_p = _front_prompt(
        x_prompt, zero_pool, zero_conv, weights, pos0=0, n_all=n_all)
    h_all, hn_all, idx_s, gate_s, pool_s, conv_s = _front_sample(
        x_sample, state_pool[0], state_conv[0], weights, h_all, hn_all, pos0=past_len, row0=n_p)

    idx_t = jnp.concatenate([idx_p, idx_s], axis=1)
    gates = jnp.concatenate([gate_p, gate_s], axis=1).T
    block_e, n_used, row_tok, row_dst = _routing_tables(idx_t, n_all)

    m = n_all * TOP_K
    y4 = _moe(hn_all, block_e, n_used, row_tok, row_dst, w_gate_up[0], b_gate_up[0], w_down[0],
              b_down[0], m + DUMP_ROWS)
    y4 = y4.reshape((m + DUMP_ROWS) // TOP_K, TOP_K * D_MODEL)

    gf = g_final[None]
    y_prompt = _final(h_all, y4, gates, gf, row0=0, n_rows=n_p, out_rows=FINAL_ROWS,
                      out_cols=D_MODEL, out_map=lambda i: (i, 0))
    y_sample = _final(h_all, y4, gates, gf, row0=n_p, n_rows=n_s, out_rows=SAMPLE_SEQS,
                      out_cols=nt_s * D_MODEL, out_map=lambda i: (i // nt_s, i % nt_s))
    return (y_prompt.reshape(nb_p, nt_p, D_MODEL), y_sample.reshape(nb_s, nt_s, D_MODEL),
            pool_p, conv_p,
            pool_s.reshape(1, nb_s, POOL_BUF, POOL_CH), conv_s.reshape(1, nb_s, CONV_BUF, CONV_CH))
```

```python
import functools

import jax
import jax.numpy as jnp
from jax import lax
from jax.experimental import pallas as pl
from jax.experimental.pallas import tpu as pltpu

D_MODEL = 1024
POOL_CH = 512
CONV_CH = 512
POOL_WINDOWS = (2, 4, 8, 16)
POOL_GROUP = 128
POOL_BUF = 15
CONV_TAPS = 31
CONV_BUF = 30
IN_COLS = POOL_CH + 2 * CONV_CH
N_EXPERTS = 32
TOP_K = 4
D_FF = 1024
SWIGLU_LIMIT = 7.0
SWIGLU_ALPHA = 1.702
RMS_EPS = 1e-5
LN_EPS = 1e-5

LANES = 128
SUBLANES = 8
VMEM_LIMIT_BYTES = 60 * 1024 * 1024

TILE = 512
SAMPLE_SEQS = 64
CONV_CHUNK = 32
SORT_CHUNK = 256
SORTED_ROWS = -(-(TILE * TOP_K + N_EXPERTS * (SUBLANES - 1)) // SORT_CHUNK) * SORT_CHUNK
MOE_ROWS = 256
CHUNK_BITS = (256, 128, 64, 32, 16, 8)

POOL_HIST = 16
CONV_HIST = 32

BF16 = jnp.bfloat16
F32 = jnp.float32
I32 = jnp.int32


def _rmsnorm(x, g):
    ms = jnp.mean(x * x, axis=-1, keepdims=True)
    return x * lax.rsqrt(ms + RMS_EPS) * g


def _sigmoid(x):
    return 1.0 / (1.0 + jnp.exp(-x))


def _layernorm_silu(d, g, b):
    mu = jnp.mean(d, axis=-1, keepdims=True)
    c = d - mu
    var = jnp.mean(c * c, axis=-1, keepdims=True)
    n = c * lax.rsqrt(var + LN_EPS) * g + b
    return n * _sigmoid(n)


def _route_and_sort(hn_bf, wr_t_ref, br_ref, tri_ref, q_ref, gate_ref, tbl_ref, xs_ref):
    rows = hn_bf.shape[0]
    lt = lax.dot_general(wr_t_ref[...], hn_bf, (((1,), (1,)), ((), ())),
                         preferred_element_type=F32)
    lt = lt + br_ref[...]
    eidx = lax.broadcasted_iota(I32, (N_EXPERTS, rows), 0)
    vals, hots = [], []
    for _ in range(TOP_K):
        m = jnp.max(lt, axis=0, keepdims=True)
        sel = jnp.min(jnp.where(lt == m, eidx, N_EXPERTS), axis=0, keepdims=True)
        hit = eidx == sel
        vals.append(m)
        hots.append(hit)
        lt = jnp.where(hit, -jnp.inf, lt)
    v = jnp.concatenate(vals, axis=0)
    e = jnp.exp(v - vals[0])
    gate = e / jnp.sum(e, axis=0, keepdims=True)
    gate_ref[...] = jnp.concatenate([gate, jnp.zeros((SUBLANES - TOP_K, rows), F32)], axis=0)

    sel_cnt = sum(h.astype(F32) for h in hots)
    prefix = jnp.dot(sel_cnt.astype(BF16), tri_ref[...], preferred_element_type=F32)
    cnt = jnp.sum(sel_cnt, axis=1, keepdims=True)
    cnt_u = jnp.floor((cnt + (SUBLANES - 1)) * (1.0 / SUBLANES))
    lower = (lax.broadcasted_iota(I32, (N_EXPERTS, N_EXPERTS), 1)
             < lax.broadcasted_iota(I32, (N_EXPERTS, N_EXPERTS), 0)).astype(BF16)
    cnt_u_l = jnp.broadcast_to(cnt_u, (N_EXPERTS, LANES))
    off_u_l = jnp.dot(lower, cnt_u_l.astype(BF16), preferred_element_type=F32)
    off_l = off_u_l * float(SUBLANES)
    tbl_ref[0, 0] = (cnt_u_l * float(SUBLANES)).astype(I32)
    tbl_ref[0, 1] = off_l.astype(I32)

    pos = prefix + off_l[:, :1]
    qs = [jnp.sum(jnp.where(h, pos, 0.0), axis=0, keepdims=True).astype(I32) for h in hots]
    q_ref[...] = jnp.concatenate(qs + [jnp.zeros((SUBLANES - TOP_K, rows), I32)], axis=0)

    def sort_chunk(c, carry):
        r0 = pl.multiple_of(c * SORT_CHUNK, SORT_CHUNK)
        jidx = r0 + lax.broadcasted_iota(I32, (SORT_CHUNK, rows), 0)
        p = sum(jnp.where(jidx == qk, 1.0, 0.0) for qk in qs)
        xs_ref[pl.ds(r0, SORT_CHUNK), :] = jnp.dot(p.astype(BF16), hn_bf, preferred_element_type=F32)
        return carry

    lax.fori_loop(0, SORTED_ROWS // SORT_CHUNK, sort_chunk, 0)


def _mix_out_and_route(x, mixbuf, w_out_ref, g_moe_ref, wr_t_ref, br_ref, tri_ref,
                       h_ref, q_ref, gate_ref, tbl_ref, xs_ref):
    h = x + jnp.dot(mixbuf[...], w_out_ref[...], preferred_element_type=F32)
    h_ref[...] = h
    hn = _rmsnorm(h, g_moe_ref[...])
    _route_and_sort(hn.astype(BF16), wr_t_ref, br_ref, tri_ref, q_ref, gate_ref, tbl_ref, xs_ref)


def _front_prompt_kernel(x_ref, sp_ref, sc_ref, g_mix_ref, w_in_ref, pool_w_ref, pool_scale_ref,
                         conv_w_ref, conv_b_ref, ln_g_ref, ln_b_ref, w_out_ref, g_moe_ref,
                         wr_t_ref, br_ref, tri_ref,
                         h_ref, q_ref, gate_ref, tbl_ref, xs_ref, np_ref, nc_ref,
                         ubuf, gsh, dbuf, mixbuf, *, pos0, rows):
    t = pl.program_id(1)
    hist_rows = CONV_HIST + rows

    @pl.when(t == 0)
    def _():
        ubuf[pl.ds(POOL_HIST - POOL_BUF, POOL_BUF), :] = sp_ref[0, 0]
        gsh[0, pl.ds(CONV_HIST - CONV_BUF, CONV_BUF), :] = sc_ref[0, 0]

    x = x_ref[0]
    xn = _rmsnorm(x, g_mix_ref[...]).astype(BF16)
    proj = jnp.dot(xn, w_in_ref[...], preferred_element_type=F32)

    u = proj[:, :POOL_CH]
    ubuf[pl.ds(POOL_HIST, rows), :] = u
    pos = pos0 + t * rows + lax.broadcasted_iota(I32, (rows, 1), 0)
    for g, w in enumerate(POOL_WINDOWS):
        cs = slice(g * POOL_GROUP, (g + 1) * POOL_GROUP)
        ug = u[:, cs]
        s = ug
        for j in range(1, w):
            s = s + ubuf[pl.ds(POOL_HIST - j, rows), cs]
        cnt = jnp.minimum(w, pos + 1).astype(F32)
        z = (s / cnt - ug).astype(BF16)
        yg = jnp.dot(z, pool_w_ref[g], preferred_element_type=F32)
        mixbuf[:, cs] = (yg * pool_scale_ref[:, cs]).astype(BF16)

    glu = proj[:, POOL_CH:POOL_CH + CONV_CH] * _sigmoid(proj[:, POOL_CH + CONV_CH:])
    gsh[0, pl.ds(CONV_HIST, rows), :] = glu
    for s in range(1, SUBLANES):
        gsh[s, pl.ds(0, hist_rows - SUBLANES), :] = gsh[0, pl.ds(s, hist_rows - SUBLANES), :]

    def conv_chunk(i, carry):
        r0 = pl.multiple_of(i * CONV_CHUNK, CONV_CHUNK)
        for ct in range(CONV_CH // LANES):
            cs = slice(ct * LANES, (ct + 1) * LANES)
            acc = jnp.broadcast_to(conv_b_ref[:, cs], (CONV_CHUNK, LANES))
            for k in range(CONV_TAPS):
                first = CONV_HIST - CONV_BUF + k
                s = first % SUBLANES
                acc = acc + conv_w_ref[k:k + 1, cs] * gsh[s, pl.ds(r0 + (first - s), CONV_CHUNK), cs]
            dbuf[pl.ds(r0, CONV_CHUNK), cs] = acc
        return carry

    lax.fori_loop(0, rows // CONV_CHUNK, conv_chunk, 0)
    y_conv = _layernorm_silu(dbuf[...], ln_g_ref[...], ln_b_ref[...])
    mixbuf[:, POOL_CH:] = y_conv.astype(BF16)

    new_pool = ubuf[pl.ds(rows + POOL_HIST - POOL_BUF, POOL_BUF), :]
    new_conv = gsh[0, pl.ds(rows + CONV_HIST - CONV_BUF, CONV_BUF), :]
    np_ref[0, 0] = new_pool
    nc_ref[0, 0] = new_conv
    ubuf[pl.ds(POOL_HIST - POOL_BUF, POOL_BUF), :] = new_pool
    gsh[0, pl.ds(CONV_HIST - CONV_BUF, CONV_BUF), :] = new_conv

    _mix_out_and_route(x, mixbuf, w_out_ref, g_moe_ref, wr_t_ref, br_ref, tri_ref,
                       h_ref, q_ref, gate_ref, tbl_ref, xs_ref)


def _front_sample_kernel(x_ref, sp_ref, sc_ref, g_mix_ref, w_in_ref, pool_w_ref, pool_scale_ref,
                         conv_w_ref, conv_b_ref, ln_g_ref, ln_b_ref, w_out_ref, g_moe_ref,
                         wr_t_ref, br_ref, tri_ref, xs_in,
                         h_ref, q_ref, gate_ref, tbl_ref, xs_ref, np_ref, nc_ref,
                         xbuf, dbuf, mixbuf, *, pos0, seqs, steps):
    del xs_in
    for t in range(steps):
        xbuf[pl.ds(t * seqs, seqs), :] = x_ref[:, t * D_MODEL:(t + 1) * D_MODEL]
    x = xbuf[...]
    xn = _rmsnorm(x, g_mix_ref[...]).astype(BF16)
    proj = jnp.dot(xn, w_in_ref[...], preferred_element_type=F32)

    def slab(a, t):
        return a[t * seqs:(t + 1) * seqs]

    u = proj[:, :POOL_CH]
    cat = [sp_ref[:, j * POOL_CH:(j + 1) * POOL_CH] for j in range(POOL_BUF)]
    cat += [slab(u, t) for t in range(steps)]
    for t in range(steps):
        for g, w in enumerate(POOL_WINDOWS):
            cs = slice(g * POOL_GROUP, (g + 1) * POOL_GROUP)
            s = cat[POOL_BUF + t][:, cs]
            for j in range(1, w):
                s = s + cat[POOL_BUF + t - j][:, cs]
            cnt = float(min(w, pos0 + t + 1))
            z = (s / cnt - cat[POOL_BUF + t][:, cs]).astype(BF16)
            yg = jnp.dot(z, pool_w_ref[g], preferred_element_type=F32)
            mixbuf[pl.ds(t * seqs, seqs), cs] = (yg * pool_scale_ref[:, cs]).astype(BF16)
    for j in range(POOL_BUF):
        np_ref[:, j * POOL_CH:(j + 1) * POOL_CH] = cat[len(cat) - POOL_BUF + j]

    glu = proj[:, POOL_CH:POOL_CH + CONV_CH] * _sigmoid(proj[:, POOL_CH + CONV_CH:])
    ccat = [sc_ref[:, j * CONV_CH:(j + 1) * CONV_CH] for j in range(CONV_BUF)]
    ccat += [slab(glu, t) for t in range(steps)]
    for t in range(steps):
        acc = jnp.broadcast_to(conv_b_ref[...], (seqs, CONV_CH))
        for k in range(CONV_TAPS):
            acc = acc + conv_w_ref[k:k + 1, :] * ccat[t + k]
        dbuf[pl.ds(t * seqs, seqs), :] = acc
    for j in range(CONV_BUF):
        nc_ref[:, j * CONV_CH:(j + 1) * CONV_CH] = ccat[len(ccat) - CONV_BUF + j]
    y_conv = _layernorm_silu(dbuf[...], ln_g_ref[...], ln_b_ref[...])
    mixbuf[:, POOL_CH:] = y_conv.astype(BF16)

    _mix_out_and_route(x, mixbuf, w_out_ref, g_moe_ref, wr_t_ref, br_ref, tri_ref,
                       h_ref, q_ref, gate_ref, tbl_ref, xs_ref)


def _const_spec(shape, single_buffer=False):
    mode = pl.Buffered(1) if single_buffer else None
    return pl.BlockSpec(shape, lambda *_: (0,) * len(shape), pipeline_mode=mode)


def _front_weight_specs():
    return [
        _const_spec((1, D_MODEL)),
        _const_spec((D_MODEL, IN_COLS), True),
        _const_spec((len(POOL_WINDOWS), POOL_GROUP, POOL_GROUP)),
        _const_spec((1, POOL_CH)),
        _const_spec((CONV_TAPS, CONV_CH)),
        _const_spec((1, CONV_CH)),
        _const_spec((1, CONV_CH)),
        _const_spec((1, CONV_CH)),
        _const_spec((D_MODEL, D_MODEL), True),
        _const_spec((1, D_MODEL)),
        _const_spec((N_EXPERTS, D_MODEL)),
        _const_spec((N_EXPERTS, 1)),
        _const_spec((TILE, TILE), True),
    ]


def _front_out(n_tok, n_tiles_all, tile_map, tok_map, extra_specs, extra_shapes):
    specs = [pl.BlockSpec((TILE, D_MODEL), tok_map),
             pl.BlockSpec((SUBLANES, TILE), lambda *i: (0, tok_map(*i)[0])),
             pl.BlockSpec((SUBLANES, TILE), lambda *i: (0, tok_map(*i)[0])),
             pl.BlockSpec((1, 2, N_EXPERTS, LANES), lambda *i: (tok_map(*i)[0], 0, 0, 0)),
             pl.BlockSpec((SORTED_ROWS, D_MODEL), tile_map)]
    shapes = [jax.ShapeDtypeStruct((n_tok, D_MODEL), F32),
              jax.ShapeDtypeStruct((SUBLANES, n_tok), I32),
              jax.ShapeDtypeStruct((SUBLANES, n_tok), F32),
              jax.ShapeDtypeStruct((n_tok // TILE, 2, N_EXPERTS, LANES), I32),
              jax.ShapeDtypeStruct((n_tiles_all * SORTED_ROWS, D_MODEL), F32)]
    return specs + extra_specs, shapes + extra_shapes


def _front_prompt(x, state_pool, state_conv, weights, *, pos0, n_tiles_all):
    nb, nt, _ = x.shape
    rows = TILE
    tiles = nt // rows
    tok_map = lambda b, t: (b * tiles + t, 0)
    out_specs, out_shape = _front_out(
        nb * nt, n_tiles_all, tok_map, tok_map,
        [pl.BlockSpec((1, 1, POOL_BUF, POOL_CH), lambda b, t: (0, b, 0, 0)),
         pl.BlockSpec((1, 1, CONV_BUF, CONV_CH), lambda b, t: (0, b, 0, 0))],
        [jax.ShapeDtypeStruct((1, nb, POOL_BUF, POOL_CH), F32),
         jax.ShapeDtypeStruct((1, nb, CONV_BUF, CONV_CH), F32)])
    kern = functools.partial(_front_prompt_kernel, pos0=pos0, rows=rows)
    return pl.pallas_call(
        kern,
        grid=(nb, tiles),
        in_specs=[pl.BlockSpec((1, rows, D_MODEL), lambda b, t: (b, t, 0)),
                  pl.BlockSpec((1, 1, POOL_BUF, POOL_CH), lambda b, t: (0, b, 0, 0)),
                  pl.BlockSpec((1, 1, CONV_BUF, CONV_CH), lambda b, t: (0, b, 0, 0))]
                 + _front_weight_specs(),
        out_specs=out_specs,
        out_shape=out_shape,
        scratch_shapes=[pltpu.VMEM((POOL_HIST + rows, POOL_CH), F32),
                        pltpu.VMEM((SUBLANES, CONV_HIST + rows, CONV_CH), F32),
                        pltpu.VMEM((rows, CONV_CH), F32),
                        pltpu.VMEM((rows, D_MODEL), BF16)],
        compiler_params=pltpu.CompilerParams(
            dimension_semantics=("arbitrary", "arbitrary"), vmem_limit_bytes=VMEM_LIMIT_BYTES),
        name="front_prompt",
    )(x, state_pool, state_conv, *weights)


def _front_sample(x, state_pool, state_conv, weights, xs_all, *, pos0, tile0):
    nb, nt, _ = x.shape
    seqs = TILE // nt
    n_tiles_all = xs_all.shape[0] // SORTED_ROWS
    tok_map = lambda i: (i, 0)
    out_specs, out_shape = _front_out(
        nb * nt, n_tiles_all, lambda i: (tile0 + i, 0), tok_map,
        [pl.BlockSpec((seqs, POOL_BUF * POOL_CH), lambda i: (i, 0)),
         pl.BlockSpec((seqs, CONV_BUF * CONV_CH), lambda i: (i, 0))],
        [jax.ShapeDtypeStruct((nb, POOL_BUF * POOL_CH), F32),
         jax.ShapeDtypeStruct((nb, CONV_BUF * CONV_CH), F32)])
    kern = functools.partial(_front_sample_kernel, pos0=pos0, seqs=seqs, steps=nt)
    n_in = 3 + len(weights)
    return pl.pallas_call(
        kern,
        grid=(nb // seqs,),
        in_specs=[pl.BlockSpec((seqs, nt * D_MODEL), lambda i: (i, 0)),
                  pl.BlockSpec((seqs, POOL_BUF * POOL_CH), lambda i: (i, 0)),
                  pl.BlockSpec((seqs, CONV_BUF * CONV_CH), lambda i: (i, 0))]
                 + _front_weight_specs() + [pl.BlockSpec(memory_space=pl.ANY)],
        out_specs=out_specs,
        out_shape=out_shape,
        scratch_shapes=[pltpu.VMEM((TILE, D_MODEL), F32),
                        pltpu.VMEM((TILE, CONV_CH), F32),
                        pltpu.VMEM((TILE, D_MODEL), BF16)],
        input_output_aliases={n_in: 4},
        compiler_params=pltpu.CompilerParams(
            dimension_semantics=("arbitrary",), vmem_limit_bytes=VMEM_LIMIT_BYTES),
        name="front_sample",
    )(x.reshape(nb, nt * D_MODEL), state_pool.reshape(nb, POOL_BUF * POOL_CH),
      state_conv.reshape(nb, CONV_BUF * CONV_CH), *weights, xs_all)


def _moe_kernel(be_ref, row0_ref, valid_ref, t0_ref, nused_ref, cnt_ref, run_ref, off_ref,
                xs_hbm, wgu_ref, bgu_ref, wd_ref, bd_ref, ys_hbm,
                xbuf, obuf, wgu_bf, wd_bf, gsem, ssem, *, n_tiles):
    b = pl.program_id(0)
    n_used = nused_ref[0]
    slot = b % 2

    def walk(bb, issue):
        e = be_ref[bb]
        lo = row0_ref[bb]
        hi = lo + valid_ref[bb]

        def cond(t):
            return jnp.logical_and(t < n_tiles, run_ref[jnp.minimum(t, n_tiles - 1) * N_EXPERTS + e] < hi)

        def body(t):
            i = t * N_EXPERTS + e
            start = run_ref[i]
            c_lo = jnp.maximum(start, lo)
            n = jnp.minimum(start + cnt_ref[i], hi) - c_lo
            src = t * SORTED_ROWS + off_ref[i] + (c_lo - start)
            dst = c_lo - lo
            for bit in CHUNK_BITS:
                done = n & ~(2 * bit - 1)

                @pl.when((n & bit) != 0)
                def _():
                    issue(pl.multiple_of(src + done, SUBLANES), pl.multiple_of(dst + done, SUBLANES), bit)
            return t + 1

        lax.while_loop(cond, body, t0_ref[bb])

    def wait_rows(n, wait_bit):
        for bit in CHUNK_BITS:
            @pl.when((n & bit) != 0)
            def _():
                wait_bit(bit)

    def gather(bb, s):
        walk(bb, lambda hrow, brow, bit: pltpu.make_async_copy(
            xs_hbm.at[pl.ds(hrow, bit)], xbuf.at[s, pl.ds(brow, bit)], gsem.at[s]).start())

    def gather_wait(bb, s):
        wait_rows(valid_ref[bb], lambda bit: pltpu.make_async_copy(
            xs_hbm.at[pl.ds(0, bit)], xbuf.at[s, pl.ds(0, bit)], gsem.at[s]).wait())

    def scatter(bb, s):
        walk(bb, lambda hrow, brow, bit: pltpu.make_async_copy(
            obuf.at[s, pl.ds(brow, bit)], ys_hbm.at[pl.ds(hrow, bit)], ssem.at[s]).start())

    def scatter_wait(bb, s):
        wait_rows(valid_ref[bb], lambda bit: pltpu.make_async_copy(
            obuf.at[s, pl.ds(0, bit)], ys_hbm.at[pl.ds(0, bit)], ssem.at[s]).wait())

    @pl.when(b == 0)
    def _():
        xbuf[...] = jnp.zeros_like(xbuf)
        gather(0, 0)

    @pl.when(b + 1 < n_used)
    def _():
        gather(b + 1, 1 - slot)

    @pl.when(jnp.logical_and(b >= 2, b - 2 < n_used))
    def _():
        scatter_wait(b - 2, slot)

    @pl.when(b < n_used)
    def _():
        e_cur = be_ref[b]
        e_prev = be_ref[jnp.maximum(b - 1, 0)]

        @pl.when(jnp.logical_or(b == 0, e_cur != e_prev))
        def _():
            wgu_bf[...] = wgu_ref[0].astype(BF16)
            wd_bf[...] = wd_ref[0].astype(BF16)

        gather_wait(b, slot)
        x = xbuf[slot].astype(BF16)
        gu = jnp.dot(x, wgu_bf[...], preferred_element_type=F32) + bgu_ref[0]
        g = jnp.minimum(gu[:, :D_FF], SWIGLU_LIMIT)
        up = jnp.clip(gu[:, D_FF:], -SWIGLU_LIMIT, SWIGLU_LIMIT)
        hdn = (up + 1.0) * (g * _sigmoid(SWIGLU_ALPHA * g))
        y = jnp.dot(hdn.astype(BF16), wd_bf[...], preferred_element_type=F32) + bd_ref[0]
        obuf[slot] = y
        scatter(b, slot)


def _moe(xs_all, tables, w_gate_up, b_gate_up, w_down, b_down):
    block_e, row0, valid, t0, n_used, cnt, run, off = tables
    n_blocks = block_e.shape[0]
    n_tiles = xs_all.shape[0] // SORTED_ROWS
    steps = n_blocks + 2

    def e_map(b, be, r0, va, t0_, nu, *_):
        return (be[jnp.minimum(b, nu[0] - 1)], 0, 0)

    grid_spec = pltpu.PrefetchScalarGridSpec(
        num_scalar_prefetch=8,
        grid=(steps,),
        in_specs=[pl.BlockSpec(memory_space=pl.ANY),
                  pl.BlockSpec((1, D_MODEL, 2 * D_FF), e_map),
                  pl.BlockSpec((1, 1, 2 * D_FF), e_map),
                  pl.BlockSpec((1, D_FF, D_MODEL), e_map),
                  pl.BlockSpec((1, 1, D_MODEL), e_map)],
        out_specs=pl.BlockSpec(memory_space=pl.ANY),
        scratch_shapes=[pltpu.VMEM((2, MOE_ROWS, D_MODEL), F32),
                        pltpu.VMEM((2, MOE_ROWS, D_MODEL), F32),
                        pltpu.VMEM((D_MODEL, 2 * D_FF), BF16),
                        pltpu.VMEM((D_FF, D_MODEL), BF16),
                        pltpu.SemaphoreType.DMA((2,)),
                        pltpu.SemaphoreType.DMA((2,))],
    )
    pad2 = lambda a: jnp.concatenate([a, jnp.zeros((2,), I32)])
    return pl.pallas_call(
        functools.partial(_moe_kernel, n_tiles=n_tiles),
        grid_spec=grid_spec,
        out_shape=jax.ShapeDtypeStruct(xs_all.shape, F32),
        compiler_params=pltpu.CompilerParams(
            dimension_semantics=("arbitrary",), vmem_limit_bytes=VMEM_LIMIT_BYTES),
        name="moe_experts",
    )(pad2(block_e), pad2(row0), pad2(valid), pad2(t0), n_used, cnt, run, off,
      xs_all, w_gate_up, b_gate_up.reshape(N_EXPERTS, 1, 2 * D_FF), w_down,
      b_down.reshape(N_EXPERTS, 1, D_MODEL))


def _final_kernel(used_ref, h_ref, ys_ref, q_ref, gate_ref, g_ref, o_ref, *, tile0, slabs):
    i = pl.program_id(0)
    used = used_ref[tile0 + i]

    def to_cols(a):
        pad = jnp.zeros((LANES - SUBLANES, TILE), F32)
        return jnp.concatenate([a, pad], axis=0).T

    qc = to_cols(q_ref[...].astype(F32))
    gc = to_cols(gate_ref[...])
    acc = h_ref[...]
    for c in range(SORTED_ROWS // SORT_CHUNK):
        jl = (c * SORT_CHUNK + lax.broadcasted_iota(I32, (TILE, SORT_CHUNK), 1)).astype(F32)
        pg = sum(jnp.where(qc[:, k:k + 1] == jl, gc[:, k:k + 1], 0.0) for k in range(TOP_K))
        rid = c * SORT_CHUNK + lax.broadcasted_iota(I32, (SORT_CHUNK, 1), 0)
        ys = jnp.where(rid < used, ys_ref[pl.ds(c * SORT_CHUNK, SORT_CHUNK), :], 0.0)
        acc = acc + jnp.dot(pg.astype(BF16), ys.astype(BF16), preferred_element_type=F32)
    y = _rmsnorm(acc, g_ref[...])
    if slabs == 1:
        o_ref[...] = y
    else:
        seqs = TILE // slabs
        for t in range(slabs):
            o_ref[:, t * D_MODEL:(t + 1) * D_MODEL] = y[t * seqs:(t + 1) * seqs]


def _final(h, ys_all, q, gates, used, g_final, *, tile0, slabs):
    n_tok = h.shape[0]
    if slabs == 1:
        out_spec = pl.BlockSpec((TILE, D_MODEL), lambda i, u: (i, 0))
        out_shape = jax.ShapeDtypeStruct((n_tok, D_MODEL), F32)
    else:
        seqs = TILE // slabs
        out_spec = pl.BlockSpec((seqs, slabs * D_MODEL), lambda i, u: (i, 0))
        out_shape = jax.ShapeDtypeStruct((n_tok // slabs, slabs * D_MODEL), F32)
    grid_spec = pltpu.PrefetchScalarGridSpec(
        num_scalar_prefetch=1,
        grid=(n_tok // TILE,),
        in_specs=[pl.BlockSpec((TILE, D_MODEL), lambda i, u: (i, 0)),
                  pl.BlockSpec((SORTED_ROWS, D_MODEL), lambda i, u: (tile0 + i, 0)),
                  pl.BlockSpec((SUBLANES, TILE), lambda i, u: (0, i)),
                  pl.BlockSpec((SUBLANES, TILE), lambda i, u: (0, i)),
                  pl.BlockSpec((1, D_MODEL), lambda i, u: (0, 0))],
        out_specs=out_spec,
    )
    return pl.pallas_call(
        functools.partial(_final_kernel, tile0=tile0, slabs=slabs),
        grid_spec=grid_spec,
        out_shape=out_shape,
        compiler_params=pltpu.CompilerParams(
            dimension_semantics=("arbitrary",), vmem_limit_bytes=VMEM_LIMIT_BYTES),
        name="final_norm",
    )(used, h, ys_all, q, gates, g_final)


def _block_tables(tbl, n_blocks):
    cnt = tbl[:, 0, :, 0]
    off = tbl[:, 1, :, 0]
    run = jnp.cumsum(cnt, axis=0) - cnt
    total = jnp.sum(cnt, axis=0)
    nblk = (total + MOE_ROWS - 1) // MOE_ROWS
    blk_end = jnp.cumsum(nblk)
    blk_start = blk_end - nblk
    bidx = jnp.arange(n_blocks, dtype=I32)
    block_e = jnp.minimum(jnp.searchsorted(blk_end, bidx, side='right'), N_EXPERTS - 1).astype(I32)
    row0 = (bidx - blk_start[block_e]) * MOE_ROWS
    valid = jnp.clip(total[block_e] - row0, 0, MOE_ROWS)
    ends = (run + cnt)[:, block_e]
    t0 = jnp.sum(ends <= row0[None, :], axis=0)
    used = jnp.sum(cnt, axis=1)
    return ((block_e, row0.astype(I32), valid.astype(I32), t0.astype(I32), blk_end[-1:].astype(I32),
             cnt.reshape(-1), run.reshape(-1).astype(I32), off.reshape(-1)), used.astype(I32))


def kernel(x_prompt, x_sample, state_pool, state_conv, g_mix, w_in, pool_w, pool_scale, conv_w, conv_b,
           conv_ln_g, conv_ln_b, w_out, g_moe, w_router, b_router, w_gate_up, b_gate_up, w_down, b_down,
           g_final):
    nb_p, nt_p, _ = x_prompt.shape
    nb_s, nt_s, _ = x_sample.shape
    n_p, n_s = nb_p * nt_p, nb_s * nt_s
    past_len = 16384
    assert nt_p % TILE == 0 and TILE % nt_s == 0 and n_s % TILE == 0
    tiles_p, tiles_s = n_p // TILE, n_s // TILE
    n_tiles = tiles_p + tiles_s

    tri = (lax.broadcasted_iota(I32, (TILE, TILE), 0) < lax.broadcasted_iota(I32, (TILE, TILE), 1)).astype(BF16)
    weights = (g_mix[0][None], w_in[0].astype(BF16), pool_w[0].astype(BF16), pool_scale[0][None],
               conv_w[0], conv_b[0][None], conv_ln_g[0][None], conv_ln_b[0][None],
               w_out[0].astype(BF16), g_moe[0][None], w_router[0].T.astype(BF16),
               b_router[0][:, None], tri)

    zero_pool = jnp.zeros((1, nb_p, POOL_BUF, POOL_CH), F32)
    zero_conv = jnp.zeros((1, nb_p, CONV_BUF, CONV_CH), F32)
    h_p, q_p, gate_p, tbl_p, xs_all, pool_p, conv_p = _front_prompt(
        x_prompt, zero_pool, zero_conv, weights, pos0=0, n_tiles_all=n_tiles)
    h_s, q_s, gate_s, tbl_s, xs_all, pool_s, conv_s = _front_sample(
        x_sample, state_pool[0], state_conv[0], weights, xs_all, pos0=past_len, tile0=tiles_p)

    sorted_total = n_tiles * (TILE * TOP_K + N_EXPERTS * (SUBLANES - 1))
    n_blocks = -(-sorted_total // MOE_ROWS) + N_EXPERTS
    tables, used = _block_tables(jnp.concatenate([tbl_p, tbl_s], axis=0), n_blocks)
    ys_all = _moe(xs_all, tables, w_gate_up[0], b_gate_up[0], w_down[0], b_down[0])

    gf = g_final[None]
    y_prompt = _final(h_p, ys_all, q_p, gate_p, used, gf, tile0=0, slabs=1)
    y_sample = _final(h_s, ys_all, q_s, gate_s, used, gf, tile0=tiles_p, slabs=nt_s)
    return (y_prompt.reshape(nb_p, nt_p, D_MODEL), y_sample.reshape(nb_s, nt_s, D_MODEL),
            pool_p, conv_p,
            pool_s.reshape(1, nb_s, POOL_BUF, POOL_CH), conv_s.reshape(1, nb_s, CONV_BUF, CONV_CH))
```

```python
import functools

import jax
import jax.numpy as jnp
from jax import lax
from jax.experimental import pallas as pl
from jax.experimental.pallas import tpu as pltpu

D_MODEL = 1024
POOL_CH = 512
CONV_CH = 512
POOL_WINDOWS = (2, 4, 8, 16)
POOL_GROUP = 128
POOL_BUF = 15
CONV_TAPS = 31
CONV_BUF = 30
IN_COLS = POOL_CH + 2 * CONV_CH
N_EXPERTS = 32
TOP_K = 4
D_FF = 1024
SWIGLU_LIMIT = 7.0
SWIGLU_ALPHA = 1.702
RMS_EPS = 1e-5
LN_EPS = 1e-5

LANES = 128
SUBLANES = 8
VMEM_LIMIT_BYTES = 60 * 1024 * 1024

TILE = 512
SAMPLE_SEQS = 64
CONV_CHUNK = 32
SORT_CHUNK = 256
FRONT_SORT_CHUNK = 768
SORTED_ROWS = -(-(TILE * TOP_K + N_EXPERTS * (SUBLANES - 1)) // SORT_CHUNK) * SORT_CHUNK
MOE_ROWS = 256
MOE_GROUPS = MOE_ROWS // SUBLANES

POOL_HIST = 16
CONV_HIST = 32

BF16 = jnp.bfloat16
F32 = jnp.float32
I32 = jnp.int32


def _rmsnorm(x, g):
    ms = jnp.mean(x * x, axis=-1, keepdims=True)
    return x * lax.rsqrt(ms + RMS_EPS) * g


def _sigmoid(x):
    return 1.0 / (1.0 + jnp.exp(-x))


def _layernorm_silu(d, g, b):
    mu = jnp.mean(d, axis=-1, keepdims=True)
    c = d - mu
    var = jnp.mean(c * c, axis=-1, keepdims=True)
    n = c * lax.rsqrt(var + LN_EPS) * g + b
    return n * _sigmoid(n)


def _route_and_sort(hn_bf, wr_t_ref, br_ref, tri_ref, q_ref, gate_ref, tbl_ref, xs_ref):
    rows = hn_bf.shape[0]
    lt = lax.dot_general(wr_t_ref[...], hn_bf, (((1,), (1,)), ((), ())),
                         preferred_element_type=F32)
    lt = lt + br_ref[...]
    eidx = lax.broadcasted_iota(I32, (N_EXPERTS, rows), 0)
    vals, hots = [], []
    for _ in range(TOP_K):
        m = jnp.max(lt, axis=0, keepdims=True)
        sel = jnp.min(jnp.where(lt == m, eidx, N_EXPERTS), axis=0, keepdims=True)
        hit = eidx == sel
        vals.append(m)
        hots.append(hit)
        lt = jnp.where(hit, -jnp.inf, lt)
    v = jnp.concatenate(vals, axis=0)
    e = jnp.exp(v - vals[0])
    gate = e / jnp.sum(e, axis=0, keepdims=True)
    gate_ref[...] = jnp.concatenate([gate, jnp.zeros((SUBLANES - TOP_K, rows), F32)], axis=0)

    sel_cnt = sum(h.astype(F32) for h in hots)
    prefix = jnp.dot(sel_cnt.astype(BF16), tri_ref[...], preferred_element_type=F32)
    cnt = jnp.sum(sel_cnt, axis=1, keepdims=True)
    cnt_u = jnp.floor((cnt + (SUBLANES - 1)) * (1.0 / SUBLANES))
    lower = (lax.broadcasted_iota(I32, (N_EXPERTS, N_EXPERTS), 1)
             < lax.broadcasted_iota(I32, (N_EXPERTS, N_EXPERTS), 0)).astype(BF16)
    cnt_u_l = jnp.broadcast_to(cnt_u, (N_EXPERTS, LANES))
    off_u_l = jnp.dot(lower, cnt_u_l.astype(BF16), preferred_element_type=F32)
    off_l = off_u_l * float(SUBLANES)
    tbl_ref[0, 0] = (cnt_u_l * float(SUBLANES)).astype(I32)
    tbl_ref[0, 1] = off_l.astype(I32)

    pos = prefix + off_l[:, :1]
    qs = [jnp.sum(jnp.where(h, pos, 0.0), axis=0, keepdims=True).astype(I32) for h in hots]
    q_ref[...] = jnp.concatenate(qs + [jnp.zeros((SUBLANES - TOP_K, rows), I32)], axis=0)

    def sort_chunk(c, carry):
        r0 = pl.multiple_of(c * FRONT_SORT_CHUNK, FRONT_SORT_CHUNK)
        jidx = r0 + lax.broadcasted_iota(I32, (FRONT_SORT_CHUNK, rows), 0)
        p = sum(jnp.where(jidx == qk, 1.0, 0.0) for qk in qs)
        xs_ref[pl.ds(r0, FRONT_SORT_CHUNK), :] = jnp.dot(p.astype(BF16), hn_bf, preferred_element_type=F32)
        return carry

    lax.fori_loop(0, SORTED_ROWS // FRONT_SORT_CHUNK, sort_chunk, 0)


def _mix_out_and_route(x, mixbuf, w_out_ref, g_moe_ref, wr_t_ref, br_ref, tri_ref,
                       h_ref, q_ref, gate_ref, tbl_ref, xs_ref):
    h = x + jnp.dot(mixbuf[...], w_out_ref[...], preferred_element_type=F32)
    h_ref[...] = h
    hn = _rmsnorm(h, g_moe_ref[...])
    _route_and_sort(hn.astype(BF16), wr_t_ref, br_ref, tri_ref, q_ref, gate_ref, tbl_ref, xs_ref)


def _front_prompt_kernel(x_ref, sp_ref, sc_ref, g_mix_ref, w_in_ref, pool_w_ref, pool_scale_ref,
                         conv_w_ref, conv_b_ref, ln_g_ref, ln_b_ref, w_out_ref, g_moe_ref,
                         wr_t_ref, br_ref, tri_ref,
                         h_ref, q_ref, gate_ref, tbl_ref, xs_ref, np_ref, nc_ref,
                         ubuf, gsh, dbuf, mixbuf, *, pos0, rows):
    t = pl.program_id(1)
    hist_rows = CONV_HIST + rows

    @pl.when(t == 0)
    def _():
        ubuf[pl.ds(POOL_HIST - POOL_BUF, POOL_BUF), :] = sp_ref[0, 0]
        gsh[0, pl.ds(CONV_HIST - CONV_BUF, CONV_BUF), :] = sc_ref[0, 0]

    x = x_ref[0]
    xn = _rmsnorm(x, g_mix_ref[...]).astype(BF16)
    proj = jnp.dot(xn, w_in_ref[...], preferred_element_type=F32)

    u = proj[:, :POOL_CH]
    ubuf[pl.ds(POOL_HIST, rows), :] = u
    pos = pos0 + t * rows + lax.broadcasted_iota(I32, (rows, 1), 0)
    for g, w in enumerate(POOL_WINDOWS):
        cs = slice(g * POOL_GROUP, (g + 1) * POOL_GROUP)
        ug = u[:, cs]
        s = ug
        for j in range(1, w):
            s = s + ubuf[pl.ds(POOL_HIST - j, rows), cs]
        cnt = jnp.minimum(w, pos + 1).astype(F32)
        z = (s / cnt - ug).astype(BF16)
        yg = jnp.dot(z, pool_w_ref[g], preferred_element_type=F32)
        mixbuf[:, cs] = (yg * pool_scale_ref[:, cs]).astype(BF16)

    glu = proj[:, POOL_CH:POOL_CH + CONV_CH] * _sigmoid(proj[:, POOL_CH + CONV_CH:])
    gsh[0, pl.ds(CONV_HIST, rows), :] = glu
    for s in range(1, SUBLANES):
        gsh[s, pl.ds(0, hist_rows - SUBLANES), :] = gsh[0, pl.ds(s, hist_rows - SUBLANES), :]

    def conv_chunk(i, carry):
        r0 = pl.multiple_of(i * CONV_CHUNK, CONV_CHUNK)
        for ct in range(CONV_CH // LANES):
            cs = slice(ct * LANES, (ct + 1) * LANES)
            acc = jnp.broadcast_to(conv_b_ref[:, cs], (CONV_CHUNK, LANES))
            for k in range(CONV_TAPS):
                first = CONV_HIST - CONV_BUF + k
                s = first % SUBLANES
                acc = acc + conv_w_ref[k:k + 1, cs] * gsh[s, pl.ds(r0 + (first - s), CONV_CHUNK), cs]
            dbuf[pl.ds(r0, CONV_CHUNK), cs] = acc
        return carry

    lax.fori_loop(0, rows // CONV_CHUNK, conv_chunk, 0)
    y_conv = _layernorm_silu(dbuf[...], ln_g_ref[...], ln_b_ref[...])
    mixbuf[:, POOL_CH:] = y_conv.astype(BF16)

    new_pool = ubuf[pl.ds(rows + POOL_HIST - POOL_BUF, POOL_BUF), :]
    new_conv = gsh[0, pl.ds(rows + CONV_HIST - CONV_BUF, CONV_BUF), :]
    np_ref[0, 0] = new_pool
    nc_ref[0, 0] = new_conv
    ubuf[pl.ds(POOL_HIST - POOL_BUF, POOL_BUF), :] = new_pool
    gsh[0, pl.ds(CONV_HIST - CONV_BUF, CONV_BUF), :] = new_conv

    _mix_out_and_route(x, mixbuf, w_out_ref, g_moe_ref, wr_t_ref, br_ref, tri_ref,
                       h_ref, q_ref, gate_ref, tbl_ref, xs_ref)


def _front_sample_kernel(x_ref, sp_ref, sc_ref, g_mix_ref, w_in_ref, pool_w_ref, pool_scale_ref,
                         conv_w_ref, conv_b_ref, ln_g_ref, ln_b_ref, w_out_ref, g_moe_ref,
                         wr_t_ref, br_ref, tri_ref, xs_in,
                         h_ref, q_ref, gate_ref, tbl_ref, xs_ref, np_ref, nc_ref,
                         xbuf, dbuf, mixbuf, *, pos0, seqs, steps):
    del xs_in
    for t in range(steps):
        xbuf[pl.ds(t * seqs, seqs), :] = x_ref[:, t * D_MODEL:(t + 1) * D_MODEL]
    x = xbuf[...]
    xn = _rmsnorm(x, g_mix_ref[...]).astype(BF16)
    proj = jnp.dot(xn, w_in_ref[...], preferred_element_type=F32)

    def slab(a, t):
        return a[t * seqs:(t + 1) * seqs]

    u = proj[:, :POOL_CH]
    cat = [sp_ref[:, j * POOL_CH:(j + 1) * POOL_CH] for j in range(POOL_BUF)]
    cat += [slab(u, t) for t in range(steps)]
    for t in range(steps):
        for g, w in enumerate(POOL_WINDOWS):
            cs = slice(g * POOL_GROUP, (g + 1) * POOL_GROUP)
            s = cat[POOL_BUF + t][:, cs]
            for j in range(1, w):
                s = s + cat[POOL_BUF + t - j][:, cs]
            cnt = float(min(w, pos0 + t + 1))
            z = (s / cnt - cat[POOL_BUF + t][:, cs]).astype(BF16)
            yg = jnp.dot(z, pool_w_ref[g], preferred_element_type=F32)
            mixbuf[pl.ds(t * seqs, seqs), cs] = (yg * pool_scale_ref[:, cs]).astype(BF16)
    for j in range(POOL_BUF):
        np_ref[:, j * POOL_CH:(j + 1) * POOL_CH] = cat[len(cat) - POOL_BUF + j]

    glu = proj[:, POOL_CH:POOL_CH + CONV_CH] * _sigmoid(proj[:, POOL_CH + CONV_CH:])
    ccat = [sc_ref[:, j * CONV_CH:(j + 1) * CONV_CH] for j in range(CONV_BUF)]
    ccat += [slab(glu, t) for t in range(steps)]
    for t in range(steps):
        acc = jnp.broadcast_to(conv_b_ref[...], (seqs, CONV_CH))
        for k in range(CONV_TAPS):
            acc = acc + conv_w_ref[k:k + 1, :] * ccat[t + k]
        dbuf[pl.ds(t * seqs, seqs), :] = acc
    for j in range(CONV_BUF):
        nc_ref[:, j * CONV_CH:(j + 1) * CONV_CH] = ccat[len(ccat) - CONV_BUF + j]
    y_conv = _layernorm_silu(dbuf[...], ln_g_ref[...], ln_b_ref[...])
    mixbuf[:, POOL_CH:] = y_conv.astype(BF16)

    _mix_out_and_route(x, mixbuf, w_out_ref, g_moe_ref, wr_t_ref, br_ref, tri_ref,
                       h_ref, q_ref, gate_ref, tbl_ref, xs_ref)


def _const_spec(shape, single_buffer=False):
    mode = pl.Buffered(1) if single_buffer else None
    return pl.BlockSpec(shape, lambda *_: (0,) * len(shape), pipeline_mode=mode)


def _front_weight_specs():
    return [
        _const_spec((1, D_MODEL)),
        _const_spec((D_MODEL, IN_COLS), True),
        _const_spec((len(POOL_WINDOWS), POOL_GROUP, POOL_GROUP)),
        _const_spec((1, POOL_CH)),
        _const_spec((CONV_TAPS, CONV_CH)),
        _const_spec((1, CONV_CH)),
        _const_spec((1, CONV_CH)),
        _const_spec((1, CONV_CH)),
        _const_spec((D_MODEL, D_MODEL), True),
        _const_spec((1, D_MODEL)),
        _const_spec((N_EXPERTS, D_MODEL)),
        _const_spec((N_EXPERTS, 1)),
        _const_spec((TILE, TILE), True),
    ]


def _front_out(n_tok, n_tiles_all, tile_map, tok_map, extra_specs, extra_shapes):
    specs = [pl.BlockSpec((TILE, D_MODEL), tok_map),
             pl.BlockSpec((SUBLANES, TILE), lambda *i: (0, tok_map(*i)[0])),
             pl.BlockSpec((SUBLANES, TILE), lambda *i: (0, tok_map(*i)[0])),
             pl.BlockSpec((1, 2, N_EXPERTS, LANES), lambda *i: (tok_map(*i)[0], 0, 0, 0)),
             pl.BlockSpec((SORTED_ROWS, D_MODEL), tile_map)]
    shapes = [jax.ShapeDtypeStruct((n_tok, D_MODEL), F32),
              jax.ShapeDtypeStruct((SUBLANES, n_tok), I32),
              jax.ShapeDtypeStruct((SUBLANES, n_tok), F32),
              jax.ShapeDtypeStruct((n_tok // TILE, 2, N_EXPERTS, LANES), I32),
              jax.ShapeDtypeStruct((n_tiles_all * SORTED_ROWS, D_MODEL), F32)]
    return specs + extra_specs, shapes + extra_shapes


def _front_prompt(x, state_pool, state_conv, weights, *, pos0, n_tiles_all):
    nb, nt, _ = x.shape
    rows = TILE
    tiles = nt // rows
    tok_map = lambda b, t: (b * tiles + t, 0)
    out_specs, out_shape = _front_out(
        nb * nt, n_tiles_all, tok_map, tok_map,
        [pl.BlockSpec((1, 1, POOL_BUF, POOL_CH), lambda b, t: (0, b, 0, 0)),
         pl.BlockSpec((1, 1, CONV_BUF, CONV_CH), lambda b, t: (0, b, 0, 0))],
        [jax.ShapeDtypeStruct((1, nb, POOL_BUF, POOL_CH), F32),
         jax.ShapeDtypeStruct((1, nb, CONV_BUF, CONV_CH), F32)])
    kern = functools.partial(_front_prompt_kernel, pos0=pos0, rows=rows)
    return pl.pallas_call(
        kern,
        grid=(nb, tiles),
        in_specs=[pl.BlockSpec((1, rows, D_MODEL), lambda b, t: (b, t, 0)),
                  pl.BlockSpec((1, 1, POOL_BUF, POOL_CH), lambda b, t: (0, b, 0, 0)),
                  pl.BlockSpec((1, 1, CONV_BUF, CONV_CH), lambda b, t: (0, b, 0, 0))]
                 + _front_weight_specs(),
        out_specs=out_specs,
        out_shape=out_shape,
        scratch_shapes=[pltpu.VMEM((POOL_HIST + rows, POOL_CH), F32),
                        pltpu.VMEM((SUBLANES, CONV_HIST + rows, CONV_CH), F32),
                        pltpu.VMEM((rows, CONV_CH), F32),
                        pltpu.VMEM((rows, D_MODEL), BF16)],
        compiler_params=pltpu.CompilerParams(
            dimension_semantics=("arbitrary", "arbitrary"), vmem_limit_bytes=VMEM_LIMIT_BYTES),
        name="front_prompt",
    )(x, state_pool, state_conv, *weights)


def _front_sample(x, state_pool, state_conv, weights, xs_all, *, pos0, tile0):
    nb, nt, _ = x.shape
    seqs = TILE // nt
    n_tiles_all = xs_all.shape[0] // SORTED_ROWS
    tok_map = lambda i: (i, 0)
    out_specs, out_shape = _front_out(
        nb * nt, n_tiles_all, lambda i: (tile0 + i, 0), tok_map,
        [pl.BlockSpec((seqs, POOL_BUF * POOL_CH), lambda i: (i, 0)),
         pl.BlockSpec((seqs, CONV_BUF * CONV_CH), lambda i: (i, 0))],
        [jax.ShapeDtypeStruct((nb, POOL_BUF * POOL_CH), F32),
         jax.ShapeDtypeStruct((nb, CONV_BUF * CONV_CH), F32)])
    kern = functools.partial(_front_sample_kernel, pos0=pos0, seqs=seqs, steps=nt)
    n_in = 3 + len(weights)
    return pl.pallas_call(
        kern,
        grid=(nb // seqs,),
        in_specs=[pl.BlockSpec((seqs, nt * D_MODEL), lambda i: (i, 0)),
                  pl.BlockSpec((seqs, POOL_BUF * POOL_CH), lambda i: (i, 0)),
                  pl.BlockSpec((seqs, CONV_BUF * CONV_CH), lambda i: (i, 0))]
                 + _front_weight_specs() + [pl.BlockSpec(memory_space=pl.ANY)],
        out_specs=out_specs,
        out_shape=out_shape,
        scratch_shapes=[pltpu.VMEM((TILE, D_MODEL), F32),
                        pltpu.VMEM((TILE, CONV_CH), F32),
                        pltpu.VMEM((TILE, D_MODEL), BF16)],
        input_output_aliases={n_in: 4},
        compiler_params=pltpu.CompilerParams(
            dimension_semantics=("arbitrary",), vmem_limit_bytes=VMEM_LIMIT_BYTES),
        name="front_sample",
    )(x.reshape(nb, nt * D_MODEL), state_pool.reshape(nb, POOL_BUF * POOL_CH),
      state_conv.reshape(nb, CONV_BUF * CONV_CH), *weights, xs_all)


def _moe_kernel(be_ref, groups_ref, nused_ref, src_ref,
                xs_hbm, wgu_ref, bgu_ref, wd_ref, bd_ref, ys_hbm,
                xbuf, obuf, wgu_bf, wd_bf, gsem, ssem):
    b = pl.program_id(0)
    n_used = nused_ref[0]
    slot = b % 2

    def group_copy(bb, s, g, to_hbm):
        hrow = pl.multiple_of(src_ref[bb * MOE_GROUPS + g], SUBLANES)
        brow = pl.multiple_of(g * SUBLANES, SUBLANES)
        if to_hbm:
            return pltpu.make_async_copy(obuf.at[s, pl.ds(brow, SUBLANES)],
                                         ys_hbm.at[pl.ds(hrow, SUBLANES)], ssem.at[s])
        return pltpu.make_async_copy(xs_hbm.at[pl.ds(hrow, SUBLANES)],
                                     xbuf.at[s, pl.ds(brow, SUBLANES)], gsem.at[s])

    def for_groups(bb, fn):
        def body(g, carry):
            fn(g)
            return carry
        lax.fori_loop(0, groups_ref[bb], body, 0)

    def gather(bb, s):
        for_groups(bb, lambda g: group_copy(bb, s, g, False).start())

    def gather_wait(bb, s):
        for_groups(bb, lambda g: group_copy(bb, s, g, False).wait())

    def scatter(bb, s):
        for_groups(bb, lambda g: group_copy(bb, s, g, True).start())

    def scatter_wait(bb, s):
        for_groups(bb, lambda g: group_copy(bb, s, g, True).wait())

    @pl.when(b == 0)
    def _():
        xbuf[...] = jnp.zeros_like(xbuf)
        gather(0, 0)

    @pl.when(b + 1 < n_used)
    def _():
        gather(b + 1, 1 - slot)

    @pl.when(jnp.logical_and(b >= 2, b - 2 < n_used))
    def _():
        scatter_wait(b - 2, slot)

    @pl.when(b < n_used)
    def _():
        e_cur = be_ref[b]
        e_prev = be_ref[jnp.maximum(b - 1, 0)]

        @pl.when(jnp.logical_or(b == 0, e_cur != e_prev))
        def _():
            wgu_bf[...] = wgu_ref[0].astype(BF16)
            wd_bf[...] = wd_ref[0].astype(BF16)

        gather_wait(b, slot)
        x = xbuf[slot].astype(BF16)
        gu = jnp.dot(x, wgu_bf[...], preferred_element_type=F32) + bgu_ref[0]
        g = jnp.minimum(gu[:, :D_FF], SWIGLU_LIMIT)
        up = jnp.clip(gu[:, D_FF:], -SWIGLU_LIMIT, SWIGLU_LIMIT)
        hdn = (up + 1.0) * (g * _sigmoid(SWIGLU_ALPHA * g))
        y = jnp.dot(hdn.astype(BF16), wd_bf[...], preferred_element_type=F32) + bd_ref[0]
        obuf[slot] = y
        scatter(b, slot)


def _moe(xs_all, tables, w_gate_up, b_gate_up, w_down, b_down):
    block_e, groups, n_used, src = tables
    n_blocks = block_e.shape[0]
    steps = n_blocks + 2

    def e_map(b, be, gr, nu, sr):
        return (be[jnp.minimum(b, nu[0] - 1)], 0, 0)

    grid_spec = pltpu.PrefetchScalarGridSpec(
        num_scalar_prefetch=4,
        grid=(steps,),
        in_specs=[pl.BlockSpec(memory_space=pl.ANY),
                  pl.BlockSpec((1, D_MODEL, 2 * D_FF), e_map),
                  pl.BlockSpec((1, 1, 2 * D_FF), e_map),
                  pl.BlockSpec((1, D_FF, D_MODEL), e_map),
                  pl.BlockSpec((1, 1, D_MODEL), e_map)],
        out_specs=pl.BlockSpec(memory_space=pl.ANY),
        scratch_shapes=[pltpu.VMEM((2, MOE_ROWS, D_MODEL), F32),
                        pltpu.VMEM((2, MOE_ROWS, D_MODEL), F32),
                        pltpu.VMEM((D_MODEL, 2 * D_FF), BF16),
                        pltpu.VMEM((D_FF, D_MODEL), BF16),
                        pltpu.SemaphoreType.DMA((2,)),
                        pltpu.SemaphoreType.DMA((2,))],
    )
    return pl.pallas_call(
        _moe_kernel,
        grid_spec=grid_spec,
        out_shape=jax.ShapeDtypeStruct(xs_all.shape, F32),
        compiler_params=pltpu.CompilerParams(
            dimension_semantics=("arbitrary",), vmem_limit_bytes=VMEM_LIMIT_BYTES),
        name="moe_experts",
    )(block_e, groups, n_used, src,
      xs_all, w_gate_up, b_gate_up.reshape(N_EXPERTS, 1, 2 * D_FF), w_down,
      b_down.reshape(N_EXPERTS, 1, D_MODEL))


def _final_kernel(used_ref, h_ref, ys_ref, q_ref, gate_ref, g_ref, o_ref, *, tile0, slabs):
    i = pl.program_id(0)
    used = used_ref[tile0 + i]

    def to_cols(a):
        pad = jnp.zeros((LANES - SUBLANES, TILE), F32)
        return jnp.concatenate([a, pad], axis=0).T

    qc = to_cols(q_ref[...].astype(F32))
    gc = to_cols(gate_ref[...])
    acc = h_ref[...]
    for c in range(SORTED_ROWS // SORT_CHUNK):
        jl = (c * SORT_CHUNK + lax.broadcasted_iota(I32, (TILE, SORT_CHUNK), 1)).astype(F32)
        pg = sum(jnp.where(qc[:, k:k + 1] == jl, gc[:, k:k + 1], 0.0) for k in range(TOP_K))
        rid = c * SORT_CHUNK + lax.broadcasted_iota(I32, (SORT_CHUNK, 1), 0)
        ys = jnp.where(rid < used, ys_ref[pl.ds(c * SORT_CHUNK, SORT_CHUNK), :], 0.0)
        acc = acc + jnp.dot(pg.astype(BF16), ys.astype(BF16), preferred_element_type=F32)
    y = _rmsnorm(acc, g_ref[...])
    if slabs == 1:
        o_ref[...] = y
    else:
        seqs = TILE // slabs
        for t in range(slabs):
            o_ref[:, t * D_MODEL:(t + 1) * D_MODEL] = y[t * seqs:(t + 1) * seqs]


def _final(h, ys_all, q, gates, used, g_final, *, tile0, slabs):
    n_tok = h.shape[0]
    if slabs == 1:
        out_spec = pl.BlockSpec((TILE, D_MODEL), lambda i, u: (i, 0))
        out_shape = jax.ShapeDtypeStruct((n_tok, D_MODEL), F32)
    else:
        seqs = TILE // slabs
        out_spec = pl.BlockSpec((seqs, slabs * D_MODEL), lambda i, u: (i, 0))
        out_shape = jax.ShapeDtypeStruct((n_tok // slabs, slabs * D_MODEL), F32)
    grid_spec = pltpu.PrefetchScalarGridSpec(
        num_scalar_prefetch=1,
        grid=(n_tok // TILE,),
        in_specs=[pl.BlockSpec((TILE, D_MODEL), lambda i, u: (i, 0)),
                  pl.BlockSpec((SORTED_ROWS, D_MODEL), lambda i, u: (tile0 + i, 0)),
                  pl.BlockSpec((SUBLANES, TILE), lambda i, u: (0, i)),
                  pl.BlockSpec((SUBLANES, TILE), lambda i, u: (0, i)),
                  pl.BlockSpec((1, D_MODEL), lambda i, u: (0, 0))],
        out_specs=out_spec,
    )
    return pl.pallas_call(
        functools.partial(_final_kernel, tile0=tile0, slabs=slabs),
        grid_spec=grid_spec,
        out_shape=out_shape,
        compiler_params=pltpu.CompilerParams(
            dimension_semantics=("arbitrary",), vmem_limit_bytes=VMEM_LIMIT_BYTES),
        name="final_norm",
    )(used, h, ys_all, q, gates, g_final)


def _block_tables(tbl, n_blocks):
    cnt = tbl[:, 0, :, 0]
    off = tbl[:, 1, :, 0]
    run = jnp.cumsum(cnt, axis=0) - cnt
    total = jnp.sum(cnt, axis=0)
    nblk = (total + MOE_ROWS - 1) // MOE_ROWS
    blk_end = jnp.cumsum(nblk)
    blk_start = blk_end - nblk
    bidx = jnp.arange(n_blocks, dtype=I32)
    block_e = jnp.minimum(jnp.sum(blk_end[None, :] <= bidx[:, None], axis=1), N_EXPERTS - 1).astype(I32)
    row0 = (bidx - blk_start[block_e]) * MOE_ROWS
    valid = jnp.clip(total[block_e] - row0, 0, MOE_ROWS)
    erow = row0[:, None] + jnp.arange(MOE_GROUPS, dtype=I32)[None, :] * SUBLANES
    run_b, cnt_b, off_b = run[:, block_e], cnt[:, block_e], off[:, block_e]
    inside = jnp.logical_and(run_b[:, :, None] <= erow[None], erow[None] < (run_b + cnt_b)[:, :, None])
    tile_base = jnp.arange(cnt.shape[0], dtype=I32)[:, None] * SORTED_ROWS + off_b - run_b
    src = jnp.sum(jnp.where(inside, tile_base[:, :, None] + erow[None], 0), axis=0)
    used = jnp.sum(cnt, axis=1)
    return ((block_e, (valid // SUBLANES).astype(I32), blk_end[-1:].astype(I32),
             src.reshape(-1).astype(I32)), used.astype(I32))


def kernel(x_prompt, x_sample, state_pool, state_conv, g_mix, w_in, pool_w, pool_scale, conv_w, conv_b,
           conv_ln_g, conv_ln_b, w_out, g_moe, w_router, b_router, w_gate_up, b_gate_up, w_down, b_down,
           g_final):
    nb_p, nt_p, _ = x_prompt.shape
    nb_s, nt_s, _ = x_sample.shape
    n_p, n_s = nb_p * nt_p, nb_s * nt_s
    past_len = 16384
    assert nt_p % TILE == 0 and TILE % nt_s == 0 and n_s % TILE == 0
    tiles_p, tiles_s = n_p // TILE, n_s // TILE
    n_tiles = tiles_p + tiles_s

    tri = (lax.broadcasted_iota(I32, (TILE, TILE), 0) < lax.broadcasted_iota(I32, (TILE, TILE), 1)).astype(BF16)
    weights = (g_mix[0][None], w_in[0].astype(BF16), pool_w[0].astype(BF16), pool_scale[0][None],
               conv_w[0], conv_b[0][None], conv_ln_g[0][None], conv_ln_b[0][None],
               w_out[0].astype(BF16), g_moe[0][None], w_router[0].T.astype(BF16),
               b_router[0][:, None], tri)

    zero_pool = jnp.zeros((1, nb_p, POOL_BUF, POOL_CH), F32)
    zero_conv = jnp.zeros((1, nb_p, CONV_BUF, CONV_CH), F32)
    h_p, q_p, gate_p, tbl_p, xs_all, pool_p, conv_p = _front_prompt(
        x_prompt, zero_pool, zero_conv, weights, pos0=0, n_tiles_all=n_tiles)
    h_s, q_s, gate_s, tbl_s, xs_all, pool_s, conv_s = _front_sample(
        x_sample, state_pool[0], state_conv[0], weights, xs_all, pos0=past_len, tile0=tiles_p)

    sorted_total = n_tiles * (TILE * TOP_K + N_EXPERTS * (SUBLANES - 1))
    n_blocks = -(-sorted_total // MOE_ROWS) + N_EXPERTS
    tables, used = _block_tables(jnp.concatenate([tbl_p, tbl_s], axis=0), n_blocks)
    ys_all = _moe(xs_all, tables, w_gate_up[0], b_gate_up[0], w_down[0], b_down[0])

    gf = g_final[None]
    y_prompt = _final(h_p, ys_all, q_p, gate_p, used, gf, tile0=0, slabs=1)
    y_sample = _final(h_s, ys_all, q_s, gate_s, used, gf, tile0=tiles_p, slabs=nt_s)
    return (y_prompt.reshape(nb_p, nt_p, D_MODEL), y_sample.reshape(nb_s, nt_s, D_MODEL),
            pool_p, conv_p,
            pool_s.reshape(1, nb_s, POOL_BUF, POOL_CH), conv_s.reshape(1, nb_s, CONV_BUF, CONV_CH))
```

```python
import functools

import jax
import jax.numpy as jnp
from jax import lax
from jax.experimental import pallas as pl
from jax.experimental.pallas import tpu as pltpu

D_MODEL = 1024
POOL_CH = 512
CONV_CH = 512
POOL_WINDOWS = (2, 4, 8, 16)
POOL_GROUP = 128
POOL_BUF = 15
CONV_TAPS = 31
CONV_BUF = 30
IN_COLS = POOL_CH + 2 * CONV_CH
N_EXPERTS = 32
TOP_K = 4
D_FF = 1024
SWIGLU_LIMIT = 7.0
SWIGLU_ALPHA = 1.702
RMS_EPS = 1e-5
LN_EPS = 1e-5

LANES = 128
SUBLANES = 8
VMEM_LIMIT_BYTES = 60 * 1024 * 1024

TILE = 512
SAMPLE_SEQS = 64
CONV_CHUNK = 32
SORT_CHUNK = 256
FRONT_SORT_CHUNK = 768
SORTED_ROWS = -(-(TILE * TOP_K + N_EXPERTS * (SUBLANES - 1)) // SORT_CHUNK) * SORT_CHUNK
MOE_ROWS = 512
HALF_ROWS = MOE_ROWS // 2
MOE_GROUPS = MOE_ROWS // SUBLANES
HALF_GROUPS = MOE_GROUPS // 2

POOL_HIST = 16
CONV_HIST = 32

BF16 = jnp.bfloat16
F32 = jnp.float32
I32 = jnp.int32
U32 = jnp.uint32
PACKED = D_MODEL // 2


def _rmsnorm(x, g):
    ms = jnp.mean(x * x, axis=-1, keepdims=True)
    return x * lax.rsqrt(ms + RMS_EPS) * g


def _sigmoid(x):
    return 1.0 / (1.0 + jnp.exp(-x))


def _layernorm_silu(d, g, b):
    mu = jnp.mean(d, axis=-1, keepdims=True)
    c = d - mu
    var = jnp.mean(c * c, axis=-1, keepdims=True)
    n = c * lax.rsqrt(var + LN_EPS) * g + b
    return n * _sigmoid(n)


def _pack_halves(a):
    half = a.shape[1] // 2
    hi = lax.bitcast_convert_type(a[:, :half], U32)
    lo = lax.bitcast_convert_type(a[:, half:], U32)
    return (hi & jnp.uint32(0xFFFF0000)) | (lo >> 16)


def _unpack_halves(w):
    hi = lax.bitcast_convert_type(w & jnp.uint32(0xFFFF0000), F32)
    lo = lax.bitcast_convert_type(w << 16, F32)
    return jnp.concatenate([hi, lo], axis=1)


def _route_and_sort(hn_bf, wr_t_ref, br_ref, tri_ref, q_ref, gate_ref, tbl_ref, xs_ref):
    rows = hn_bf.shape[0]
    lt = lax.dot_general(wr_t_ref[...], hn_bf, (((1,), (1,)), ((), ())),
                         preferred_element_type=F32)
    lt = lt + br_ref[...]
    eidx = lax.broadcasted_iota(I32, (N_EXPERTS, rows), 0)
    vals, hots = [], []
    for _ in range(TOP_K):
        m = jnp.max(lt, axis=0, keepdims=True)
        sel = jnp.min(jnp.where(lt == m, eidx, N_EXPERTS), axis=0, keepdims=True)
        hit = eidx == sel
        vals.append(m)
        hots.append(hit)
        lt = jnp.where(hit, -jnp.inf, lt)
    v = jnp.concatenate(vals, axis=0)
    e = jnp.exp(v - vals[0])
    gate = e / jnp.sum(e, axis=0, keepdims=True)
    gate_ref[...] = jnp.concatenate([gate, jnp.zeros((SUBLANES - TOP_K, rows), F32)], axis=0)

    sel_cnt = sum(h.astype(F32) for h in hots)
    prefix = jnp.dot(sel_cnt.astype(BF16), tri_ref[...], preferred_element_type=F32)
    cnt = jnp.sum(sel_cnt, axis=1, keepdims=True)
    cnt_u = jnp.floor((cnt + (SUBLANES - 1)) * (1.0 / SUBLANES))
    lower = (lax.broadcasted_iota(I32, (N_EXPERTS, N_EXPERTS), 1)
             < lax.broadcasted_iota(I32, (N_EXPERTS, N_EXPERTS), 0)).astype(BF16)
    cnt_u_l = jnp.broadcast_to(cnt_u, (N_EXPERTS, LANES))
    off_u_l = jnp.dot(lower, cnt_u_l.astype(BF16), preferred_element_type=F32)
    off_l = off_u_l * float(SUBLANES)
    tbl_ref[0, 0] = (cnt_u_l * float(SUBLANES)).astype(I32)
    tbl_ref[0, 1] = off_l.astype(I32)

    pos = prefix + off_l[:, :1]
    qs = [jnp.sum(jnp.where(h, pos, 0.0), axis=0, keepdims=True).astype(I32) for h in hots]
    q_ref[...] = jnp.concatenate(qs + [jnp.zeros((SUBLANES - TOP_K, rows), I32)], axis=0)

    def sort_chunk(c, carry):
        r0 = pl.multiple_of(c * FRONT_SORT_CHUNK, FRONT_SORT_CHUNK)
        jidx = r0 + lax.broadcasted_iota(I32, (FRONT_SORT_CHUNK, rows), 0)
        p = sum(jnp.where(jidx == qk, 1.0, 0.0) for qk in qs)
        xs = jnp.dot(p.astype(BF16), hn_bf, preferred_element_type=F32)
        xs_ref[pl.ds(r0, FRONT_SORT_CHUNK), :] = _pack_halves(xs)
        return carry

    lax.fori_loop(0, SORTED_ROWS // FRONT_SORT_CHUNK, sort_chunk, 0)


def _mix_out_and_route(x, mixbuf, w_out_ref, g_moe_ref, wr_t_ref, br_ref, tri_ref,
                       h_ref, q_ref, gate_ref, tbl_ref, xs_ref):
    h = x + jnp.dot(mixbuf[...], w_out_ref[...], preferred_element_type=F32)
    h_ref[...] = h
    hn = _rmsnorm(h, g_moe_ref[...])
    _route_and_sort(hn.astype(BF16), wr_t_ref, br_ref, tri_ref, q_ref, gate_ref, tbl_ref, xs_ref)


def _front_prompt_kernel(x_ref, sp_ref, sc_ref, g_mix_ref, w_in_ref, pool_w_ref, pool_scale_ref,
                         conv_w_ref, conv_b_ref, ln_g_ref, ln_b_ref, w_out_ref, g_moe_ref,
                         wr_t_ref, br_ref, tri_ref,
                         h_ref, q_ref, gate_ref, tbl_ref, xs_ref, np_ref, nc_ref,
                         ubuf, gsh, dbuf, mixbuf, *, pos0, rows):
    t = pl.program_id(1)
    hist_rows = CONV_HIST + rows

    @pl.when(t == 0)
    def _():
        ubuf[pl.ds(POOL_HIST - POOL_BUF, POOL_BUF), :] = sp_ref[0, 0]
        gsh[0, pl.ds(CONV_HIST - CONV_BUF, CONV_BUF), :] = sc_ref[0, 0]

    x = x_ref[0]
    xn = _rmsnorm(x, g_mix_ref[...]).astype(BF16)
    proj = jnp.dot(xn, w_in_ref[...], preferred_element_type=F32)

    u = proj[:, :POOL_CH]
    ubuf[pl.ds(POOL_HIST, rows), :] = u
    pos = pos0 + t * rows + lax.broadcasted_iota(I32, (rows, 1), 0)
    for g, w in enumerate(POOL_WINDOWS):
        cs = slice(g * POOL_GROUP, (g + 1) * POOL_GROUP)
        ug = u[:, cs]
        s = ug
        for j in range(1, w):
            s = s + ubuf[pl.ds(POOL_HIST - j, rows), cs]
        cnt = jnp.minimum(w, pos + 1).astype(F32)
        z = (s / cnt - ug).astype(BF16)
        yg = jnp.dot(z, pool_w_ref[g], preferred_element_type=F32)
        mixbuf[:, cs] = (yg * pool_scale_ref[:, cs]).astype(BF16)

    glu = proj[:, POOL_CH:POOL_CH + CONV_CH] * _sigmoid(proj[:, POOL_CH + CONV_CH:])
    gsh[0, pl.ds(CONV_HIST, rows), :] = glu
    for s in range(1, SUBLANES):
        gsh[s, pl.ds(0, hist_rows - SUBLANES), :] = gsh[0, pl.ds(s, hist_rows - SUBLANES), :]

    def conv_chunk(i, carry):
        r0 = pl.multiple_of(i * CONV_CHUNK, CONV_CHUNK)
        for ct in range(CONV_CH // LANES):
            cs = slice(ct * LANES, (ct + 1) * LANES)
            acc = jnp.broadcast_to(conv_b_ref[:, cs], (CONV_CHUNK, LANES))
            for k in range(CONV_TAPS):
                first = CONV_HIST - CONV_BUF + k
                s = first % SUBLANES
                acc = acc + conv_w_ref[k:k + 1, cs] * gsh[s, pl.ds(r0 + (first - s), CONV_CHUNK), cs]
            dbuf[pl.ds(r0, CONV_CHUNK), cs] = acc
        return carry

    lax.fori_loop(0, rows // CONV_CHUNK, conv_chunk, 0)
    y_conv = _layernorm_silu(dbuf[...], ln_g_ref[...], ln_b_ref[...])
    mixbuf[:, POOL_CH:] = y_conv.astype(BF16)

    new_pool = ubuf[pl.ds(rows + POOL_HIST - POOL_BUF, POOL_BUF), :]
    new_conv = gsh[0, pl.ds(rows + CONV_HIST - CONV_BUF, CONV_BUF), :]
    np_ref[0, 0] = new_pool
    nc_ref[0, 0] = new_conv
    ubuf[pl.ds(POOL_HIST - POOL_BUF, POOL_BUF), :] = new_pool
    gsh[0, pl.ds(CONV_HIST - CONV_BUF, CONV_BUF), :] = new_conv

    _mix_out_and_route(x, mixbuf, w_out_ref, g_moe_ref, wr_t_ref, br_ref, tri_ref,
                       h_ref, q_ref, gate_ref, tbl_ref, xs_ref)


def _front_sample_kernel(x_ref, sp_ref, sc_ref, g_mix_ref, w_in_ref, pool_w_ref, pool_scale_ref,
                         conv_w_ref, conv_b_ref, ln_g_ref, ln_b_ref, w_out_ref, g_moe_ref,
                         wr_t_ref, br_ref, tri_ref, xs_in,
                         h_ref, q_ref, gate_ref, tbl_ref, xs_ref, np_ref, nc_ref,
                         xbuf, dbuf, mixbuf, *, pos0, seqs, steps):
    del xs_in
    for t in range(steps):
        xbuf[pl.ds(t * seqs, seqs), :] = x_ref[:, t * D_MODEL:(t + 1) * D_MODEL]
    x = xbuf[...]
    xn = _rmsnorm(x, g_mix_ref[...]).astype(BF16)
    proj = jnp.dot(xn, w_in_ref[...], preferred_element_type=F32)

    def slab(a, t):
        return a[t * seqs:(t + 1) * seqs]

    u = proj[:, :POOL_CH]
    cat = [sp_ref[:, j * POOL_CH:(j + 1) * POOL_CH] for j in range(POOL_BUF)]
    cat += [slab(u, t) for t in range(steps)]
    for t in range(steps):
        for g, w in enumerate(POOL_WINDOWS):
            cs = slice(g * POOL_GROUP, (g + 1) * POOL_GROUP)
            s = cat[POOL_BUF + t][:, cs]
            for j in range(1, w):
                s = s + cat[POOL_BUF + t - j][:, cs]
            cnt = float(min(w, pos0 + t + 1))
            z = (s / cnt - cat[POOL_BUF + t][:, cs]).astype(BF16)
            yg = jnp.dot(z, pool_w_ref[g], preferred_element_type=F32)
            mixbuf[pl.ds(t * seqs, seqs), cs] = (yg * pool_scale_ref[:, cs]).astype(BF16)
    for j in range(POOL_BUF):
        np_ref[:, j * POOL_CH:(j + 1) * POOL_CH] = cat[len(cat) - POOL_BUF + j]

    glu = proj[:, POOL_CH:POOL_CH + CONV_CH] * _sigmoid(proj[:, POOL_CH + CONV_CH:])
    ccat = [sc_ref[:, j * CONV_CH:(j + 1) * CONV_CH] for j in range(CONV_BUF)]
    ccat += [slab(glu, t) for t in range(steps)]
    for t in range(steps):
        acc = jnp.broadcast_to(conv_b_ref[...], (seqs, CONV_CH))
        for k in range(CONV_TAPS):
            acc = acc + conv_w_ref[k:k + 1, :] * ccat[t + k]
        dbuf[pl.ds(t * seqs, seqs), :] = acc
    for j in range(CONV_BUF):
        nc_ref[:, j * CONV_CH:(j + 1) * CONV_CH] = ccat[len(ccat) - CONV_BUF + j]
    y_conv = _layernorm_silu(dbuf[...], ln_g_ref[...], ln_b_ref[...])
    mixbuf[:, POOL_CH:] = y_conv.astype(BF16)

    _mix_out_and_route(x, mixbuf, w_out_ref, g_moe_ref, wr_t_ref, br_ref, tri_ref,
                       h_ref, q_ref, gate_ref, tbl_ref, xs_ref)


def _const_spec(shape, single_buffer=False):
    mode = pl.Buffered(1) if single_buffer else None
    return pl.BlockSpec(shape, lambda *_: (0,) * len(shape), pipeline_mode=mode)


def _front_weight_specs():
    return [
        _const_spec((1, D_MODEL)),
        _const_spec((D_MODEL, IN_COLS), True),
        _const_spec((len(POOL_WINDOWS), POOL_GROUP, POOL_GROUP)),
        _const_spec((1, POOL_CH)),
        _const_spec((CONV_TAPS, CONV_CH)),
        _const_spec((1, CONV_CH)),
        _const_spec((1, CONV_CH)),
        _const_spec((1, CONV_CH)),
        _const_spec((D_MODEL, D_MODEL), True),
        _const_spec((1, D_MODEL)),
        _const_spec((N_EXPERTS, D_MODEL)),
        _const_spec((N_EXPERTS, 1)),
        _const_spec((TILE, TILE), True),
    ]


def _front_out(n_tok, n_tiles_all, tile_map, tok_map, extra_specs, extra_shapes):
    specs = [pl.BlockSpec((TILE, D_MODEL), tok_map),
             pl.BlockSpec((SUBLANES, TILE), lambda *i: (0, tok_map(*i)[0])),
             pl.BlockSpec((SUBLANES, TILE), lambda *i: (0, tok_map(*i)[0])),
             pl.BlockSpec((1, 2, N_EXPERTS, LANES), lambda *i: (tok_map(*i)[0], 0, 0, 0)),
             pl.BlockSpec((SORTED_ROWS, PACKED), tile_map)]
    shapes = [jax.ShapeDtypeStruct((n_tok, D_MODEL), F32),
              jax.ShapeDtypeStruct((SUBLANES, n_tok), I32),
              jax.ShapeDtypeStruct((SUBLANES, n_tok), F32),
              jax.ShapeDtypeStruct((n_tok // TILE, 2, N_EXPERTS, LANES), I32),
              jax.ShapeDtypeStruct((n_tiles_all * SORTED_ROWS, PACKED), U32)]
    return specs + extra_specs, shapes + extra_shapes


def _front_prompt(x, state_pool, state_conv, weights, *, pos0, n_tiles_all):
    nb, nt, _ = x.shape
    rows = TILE
    tiles = nt // rows
    tok_map = lambda b, t: (b * tiles + t, 0)
    out_specs, out_shape = _front_out(
        nb * nt, n_tiles_all, tok_map, tok_map,
        [pl.BlockSpec((1, 1, POOL_BUF, POOL_CH), lambda b, t: (0, b, 0, 0)),
         pl.BlockSpec((1, 1, CONV_BUF, CONV_CH), lambda b, t: (0, b, 0, 0))],
        [jax.ShapeDtypeStruct((1, nb, POOL_BUF, POOL_CH), F32),
         jax.ShapeDtypeStruct((1, nb, CONV_BUF, CONV_CH), F32)])
    kern = functools.partial(_front_prompt_kernel, pos0=pos0, rows=rows)
    return pl.pallas_call(
        kern,
        grid=(nb, tiles),
        in_specs=[pl.BlockSpec((1, rows, D_MODEL), lambda b, t: (b, t, 0)),
                  pl.BlockSpec((1, 1, POOL_BUF, POOL_CH), lambda b, t: (0, b, 0, 0)),
                  pl.BlockSpec((1, 1, CONV_BUF, CONV_CH), lambda b, t: (0, b, 0, 0))]
                 + _front_weight_specs(),
        out_specs=out_specs,
        out_shape=out_shape,
        scratch_shapes=[pltpu.VMEM((POOL_HIST + rows, POOL_CH), F32),
                        pltpu.VMEM((SUBLANES, CONV_HIST + rows, CONV_CH), F32),
                        pltpu.VMEM((rows, CONV_CH), F32),
                        pltpu.VMEM((rows, D_MODEL), BF16)],
        compiler_params=pltpu.CompilerParams(
            dimension_semantics=("arbitrary", "arbitrary"), vmem_limit_bytes=VMEM_LIMIT_BYTES),
        name="front_prompt",
    )(x, state_pool, state_conv, *weights)


def _front_sample(x, state_pool, state_conv, weights, xs_all, *, pos0, tile0):
    nb, nt, _ = x.shape
    seqs = TILE // nt
    n_tiles_all = xs_all.shape[0] // SORTED_ROWS
    tok_map = lambda i: (i, 0)
    out_specs, out_shape = _front_out(
        nb * nt, n_tiles_all, lambda i: (tile0 + i, 0), tok_map,
        [pl.BlockSpec((seqs, POOL_BUF * POOL_CH), lambda i: (i, 0)),
         pl.BlockSpec((seqs, CONV_BUF * CONV_CH), lambda i: (i, 0))],
        [jax.ShapeDtypeStruct((nb, POOL_BUF * POOL_CH), F32),
         jax.ShapeDtypeStruct((nb, CONV_BUF * CONV_CH), F32)])
    kern = functools.partial(_front_sample_kernel, pos0=pos0, seqs=seqs, steps=nt)
    n_in = 3 + len(weights)
    return pl.pallas_call(
        kern,
        grid=(nb // seqs,),
        in_specs=[pl.BlockSpec((seqs, nt * D_MODEL), lambda i: (i, 0)),
                  pl.BlockSpec((seqs, POOL_BUF * POOL_CH), lambda i: (i, 0)),
                  pl.BlockSpec((seqs, CONV_BUF * CONV_CH), lambda i: (i, 0))]
                 + _front_weight_specs() + [pl.BlockSpec(memory_space=pl.ANY)],
        out_specs=out_specs,
        out_shape=out_shape,
        scratch_shapes=[pltpu.VMEM((TILE, D_MODEL), F32),
                        pltpu.VMEM((TILE, CONV_CH), F32),
                        pltpu.VMEM((TILE, D_MODEL), BF16)],
        input_output_aliases={n_in: 4},
        compiler_params=pltpu.CompilerParams(
            dimension_semantics=("arbitrary",), vmem_limit_bytes=VMEM_LIMIT_BYTES),
        name="front_sample",
    )(x.reshape(nb, nt * D_MODEL), state_pool.reshape(nb, POOL_BUF * POOL_CH),
      state_conv.reshape(nb, CONV_BUF * CONV_CH), *weights, xs_all)


def _moe_kernel(be_ref, nused_ref, src_ref, dst_ref,
                xs_hbm, wgu_ref, bgu_ref, wd_ref, bd_ref, ys_hbm,
                xa, xb, oa, ob, wgu_bf, wd_bf, gsem_a, gsem_b, ssem_a, ssem_b, *, dummy_half):
    i = pl.program_id(0)
    n_used = nused_ref[0]
    half_a = 2 * i
    half_b = 2 * i + 1

    def gather(half, buf, sem):
        for g in range(HALF_GROUPS):
            hrow = pl.multiple_of(src_ref[half * HALF_GROUPS + g], SUBLANES)
            pltpu.make_async_copy(xs_hbm.at[pl.ds(hrow, SUBLANES)],
                                  buf.at[pl.ds(g * SUBLANES, SUBLANES)], sem).start()

    def gather_wait(buf, sem):
        pltpu.make_async_copy(xs_hbm.at[pl.ds(0, HALF_ROWS)], buf, sem).wait()

    def scatter(half, buf, sem):
        for g in range(HALF_GROUPS):
            hrow = pl.multiple_of(dst_ref[half * HALF_GROUPS + g], SUBLANES)
            pltpu.make_async_copy(buf.at[pl.ds(g * SUBLANES, SUBLANES)],
                                  ys_hbm.at[pl.ds(hrow, SUBLANES)], sem).start()

    def scatter_wait(buf, sem):
        pltpu.make_async_copy(buf, ys_hbm.at[pl.ds(0, HALF_ROWS)], sem).wait()

    def experts(xbuf, obuf):
        x = _unpack_halves(xbuf[...]).astype(BF16)
        gu = jnp.dot(x, wgu_bf[...], preferred_element_type=F32) + bgu_ref[0]
        g = jnp.minimum(gu[:, :D_FF], SWIGLU_LIMIT)
        up = jnp.clip(gu[:, D_FF:], -SWIGLU_LIMIT, SWIGLU_LIMIT)
        hdn = (up + 1.0) * (g * _sigmoid(SWIGLU_ALPHA * g))
        y = jnp.dot(hdn.astype(BF16), wd_bf[...], preferred_element_type=F32) + bd_ref[0]
        obuf[...] = _pack_halves(y.astype(BF16).astype(F32))

    @pl.when(i == 0)
    def _():
        oa[...] = jnp.zeros_like(oa)
        ob[...] = jnp.zeros_like(ob)
        scatter(dummy_half, oa, ssem_a)
        gather(0, xa, gsem_a)

    @pl.when(i < n_used)
    def _():
        e_cur = be_ref[i]
        e_prev = be_ref[jnp.maximum(i - 1, 0)]

        @pl.when(jnp.logical_or(i == 0, e_cur != e_prev))
        def _():
            wgu_bf[...] = wgu_ref[0].astype(BF16)
            wd_bf[...] = wd_ref[0].astype(BF16)

        prev_b = jnp.where(i == 0, dummy_half, half_b - 2)
        next_a = 2 * jnp.minimum(i + 1, n_used - 1)
        scatter_wait(oa, ssem_a)
        scatter(prev_b, ob, ssem_b)
        gather(half_b, xb, gsem_b)
        gather_wait(xa, gsem_a)
        experts(xa, oa)
        scatter(half_a, oa, ssem_a)
        gather(next_a, xa, gsem_a)
        scatter_wait(ob, ssem_b)
        gather_wait(xb, gsem_b)
        experts(xb, ob)

        @pl.when(i == n_used - 1)
        def _():
            scatter(half_b, ob, ssem_b)
            scatter_wait(oa, ssem_a)
            scatter_wait(ob, ssem_b)
            gather_wait(xa, gsem_a)


def _moe(xs_all, tables, w_gate_up, b_gate_up, w_down, b_down):
    block_e, n_used, src, dst = tables
    n_blocks = block_e.shape[0]

    def e_map(i, be, nu, sr, ds_):
        return (be[jnp.minimum(i, nu[0] - 1)], 0, 0)

    half_buf = pltpu.VMEM((HALF_ROWS, PACKED), U32)
    grid_spec = pltpu.PrefetchScalarGridSpec(
        num_scalar_prefetch=4,
        grid=(n_blocks,),
        in_specs=[pl.BlockSpec(memory_space=pl.ANY),
                  pl.BlockSpec((1, D_MODEL, 2 * D_FF), e_map),
                  pl.BlockSpec((1, 1, 2 * D_FF), e_map),
                  pl.BlockSpec((1, D_FF, D_MODEL), e_map),
                  pl.BlockSpec((1, 1, D_MODEL), e_map)],
        out_specs=pl.BlockSpec(memory_space=pl.ANY),
        scratch_shapes=[half_buf, half_buf, half_buf, half_buf,
                        pltpu.VMEM((D_MODEL, 2 * D_FF), BF16),
                        pltpu.VMEM((D_FF, D_MODEL), BF16)]
                       + [pltpu.SemaphoreType.DMA(())] * 4,
    )
    return pl.pallas_call(
        functools.partial(_moe_kernel, dummy_half=2 * n_blocks),
        grid_spec=grid_spec,
        out_shape=jax.ShapeDtypeStruct((xs_all.shape[0] + MOE_ROWS, PACKED), U32),
        compiler_params=pltpu.CompilerParams(
            dimension_semantics=("arbitrary",), vmem_limit_bytes=VMEM_LIMIT_BYTES),
        name="moe_experts",
    )(block_e, n_used, src, dst,
      xs_all, w_gate_up, b_gate_up.reshape(N_EXPERTS, 1, 2 * D_FF), w_down,
      b_down.reshape(N_EXPERTS, 1, D_MODEL))


def _final_kernel(used_ref, h_ref, ys_ref, q_ref, gate_ref, g_ref, o_ref, *, tile0, slabs):
    i = pl.program_id(0)
    used = used_ref[tile0 + i]

    def to_cols(a):
        pad = jnp.zeros((LANES - SUBLANES, TILE), F32)
        return jnp.concatenate([a, pad], axis=0).T

    qc = to_cols(q_ref[...].astype(F32))
    gc = to_cols(gate_ref[...])
    acc = h_ref[...]
    for c in range(SORTED_ROWS // SORT_CHUNK):
        jl = (c * SORT_CHUNK + lax.broadcasted_iota(I32, (TILE, SORT_CHUNK), 1)).astype(F32)
        pg = sum(jnp.where(qc[:, k:k + 1] == jl, gc[:, k:k + 1], 0.0) for k in range(TOP_K))
        rid = c * SORT_CHUNK + lax.broadcasted_iota(I32, (SORT_CHUNK, 1), 0)
        ys = jnp.where(rid < used, _unpack_halves(ys_ref[pl.ds(c * SORT_CHUNK, SORT_CHUNK), :]), 0.0)
        acc = acc + jnp.dot(pg.astype(BF16), ys.astype(BF16), preferred_element_type=F32)
    y = _rmsnorm(acc, g_ref[...])
    if slabs == 1:
        o_ref[...] = y
    else:
        seqs = TILE // slabs
        for t in range(slabs):
            o_ref[:, t * D_MODEL:(t + 1) * D_MODEL] = y[t * seqs:(t + 1) * seqs]


def _final(h, ys_all, q, gates, used, g_final, *, tile0, slabs):
    n_tok = h.shape[0]
    if slabs == 1:
        out_spec = pl.BlockSpec((TILE, D_MODEL), lambda i, u: (i, 0))
        out_shape = jax.ShapeDtypeStruct((n_tok, D_MODEL), F32)
    else:
        seqs = TILE // slabs
        out_spec = pl.BlockSpec((seqs, slabs * D_MODEL), lambda i, u: (i, 0))
        out_shape = jax.ShapeDtypeStruct((n_tok // slabs, slabs * D_MODEL), F32)
    grid_spec = pltpu.PrefetchScalarGridSpec(
        num_scalar_prefetch=1,
        grid=(n_tok // TILE,),
        in_specs=[pl.BlockSpec((TILE, D_MODEL), lambda i, u: (i, 0)),
                  pl.BlockSpec((SORTED_ROWS, PACKED), lambda i, u: (tile0 + i, 0)),
                  pl.BlockSpec((SUBLANES, TILE), lambda i, u: (0, i)),
                  pl.BlockSpec((SUBLANES, TILE), lambda i, u: (0, i)),
                  pl.BlockSpec((1, D_MODEL), lambda i, u: (0, 0))],
        out_specs=out_spec,
    )
    return pl.pallas_call(
        functools.partial(_final_kernel, tile0=tile0, slabs=slabs),
        grid_spec=grid_spec,
        out_shape=out_shape,
        compiler_params=pltpu.CompilerParams(
            dimension_semantics=("arbitrary",), vmem_limit_bytes=VMEM_LIMIT_BYTES),
        name="final_norm",
    )(used, h, ys_all, q, gates, g_final)


def _block_tables(tbl, n_blocks):
    cnt = tbl[:, 0, :, 0]
    off = tbl[:, 1, :, 0]
    run = jnp.cumsum(cnt, axis=0) - cnt
    total = jnp.sum(cnt, axis=0)
    nblk = (total + MOE_ROWS - 1) // MOE_ROWS
    blk_end = jnp.cumsum(nblk)
    blk_start = blk_end - nblk
    bidx = jnp.arange(n_blocks, dtype=I32)
    block_e = jnp.minimum(jnp.sum(blk_end[None, :] <= bidx[:, None], axis=1), N_EXPERTS - 1).astype(I32)
    row0 = (bidx - blk_start[block_e]) * MOE_ROWS
    n_tiles = cnt.shape[0]
    groups = jnp.arange(MOE_GROUPS, dtype=I32)
    erow = row0[:, None] + groups[None, :] * SUBLANES
    run_b, cnt_b, off_b = run[:, block_e], cnt[:, block_e], off[:, block_e]
    inside = jnp.logical_and(run_b[:, :, None] <= erow[None], erow[None] < (run_b + cnt_b)[:, :, None])
    tile_base = jnp.arange(n_tiles, dtype=I32)[:, None] * SORTED_ROWS + off_b - run_b
    src = jnp.sum(jnp.where(inside, tile_base[:, :, None] + erow[None], 0), axis=0)
    dump = n_tiles * SORTED_ROWS + groups * SUBLANES
    dst = jnp.where(jnp.any(inside, axis=0), src, dump[None, :])
    src = jnp.concatenate([src.reshape(-1), jnp.zeros((HALF_GROUPS,), I32)])
    dst = jnp.concatenate([dst.reshape(-1), dump[:HALF_GROUPS]])
    used = jnp.sum(cnt, axis=1)
    return ((block_e, blk_end[-1:].astype(I32), src.astype(I32), dst.astype(I32)), used.astype(I32))


def kernel(x_prompt, x_sample, state_pool, state_conv, g_mix, w_in, pool_w, pool_scale, conv_w, conv_b,
           conv_ln_g, conv_ln_b, w_out, g_moe, w_router, b_router, w_gate_up, b_gate_up, w_down, b_down,
           g_final):
    nb_p, nt_p, _ = x_prompt.shape
    nb_s, nt_s, _ = x_sample.shape
    n_p, n_s = nb_p * nt_p, nb_s * nt_s
    past_len = 16384
    assert nt_p % TILE == 0 and TILE % nt_s == 0 and n_s % TILE == 0
    tiles_p, tiles_s = n_p // TILE, n_s // TILE
    n_tiles = tiles_p + tiles_s

    tri = (lax.broadcasted_iota(I32, (TILE, TILE), 0) < lax.broadcasted_iota(I32, (TILE, TILE), 1)).astype(BF16)
    weights = (g_mix[0][None], w_in[0].astype(BF16), pool_w[0].astype(BF16), pool_scale[0][None],
               conv_w[0], conv_b[0][None], conv_ln_g[0][None], conv_ln_b[0][None],
               w_out[0].astype(BF16), g_moe[0][None], w_router[0].T.astype(BF16),
               b_router[0][:, None], tri)

    zero_pool = jnp.zeros((1, nb_p, POOL_BUF, POOL_CH), F32)
    zero_conv = jnp.zeros((1, nb_p, CONV_BUF, CONV_CH), F32)
    h_p, q_p, gate_p, tbl_p, xs_all, pool_p, conv_p = _front_prompt(
        x_prompt, zero_pool, zero_conv, weights, pos0=0, n_tiles_all=n_tiles)
    h_s, q_s, gate_s, tbl_s, xs_all, pool_s, conv_s = _front_sample(
        x_sample, state_pool[0], state_conv[0], weights, xs_all, pos0=past_len, tile0=tiles_p)

    sorted_total = n_tiles * (TILE * TOP_K + N_EXPERTS * (SUBLANES - 1))
    n_blocks = -(-sorted_total // MOE_ROWS) + N_EXPERTS
    tables, used = _block_tables(jnp.concatenate([tbl_p, tbl_s], axis=0), n_blocks)
    ys_all = _moe(xs_all, tables, w_gate_up[0], b_gate_up[0], w_down[0], b_down[0])

    gf = g_final[None]
    y_prompt = _final(h_p, ys_all, q_p, gate_p, used, gf, tile0=0, slabs=1)
    y_sample = _final(h_s, ys_all, q_s, gate_s, used, gf, tile0=tiles_p, slabs=nt_s)
    return (y_prompt.reshape(nb_p, nt_p, D_MODEL), y_sample.reshape(nb_s, nt_s, D_MODEL),
            pool_p, conv_p,
            pool_s.reshape(1, nb_s, POOL_BUF, POOL_CH), conv_s.reshape(1, nb_s, CONV_BUF, CONV_CH))
```

```python
import functools

import jax
import jax.numpy as jnp
from jax import lax
from jax.experimental import pallas as pl
from jax.experimental.pallas import tpu as pltpu

D_MODEL = 1024
POOL_CH = 512
CONV_CH = 512
POOL_WINDOWS = (2, 4, 8, 16)
POOL_GROUP = 128
POOL_BUF = 15
CONV_TAPS = 31
CONV_BUF = 30
IN_COLS = POOL_CH + 2 * CONV_CH
N_EXPERTS = 32
TOP_K = 4
D_FF = 1024
SWIGLU_LIMIT = 7.0
SWIGLU_ALPHA = 1.702
RMS_EPS = 1e-5
LN_EPS = 1e-5

LANES = 128
SUBLANES = 8
VMEM_LIMIT_BYTES = 60 * 1024 * 1024

TILE = 512
SAMPLE_SEQS = 64
CONV_CHUNK = 32
SORT_CHUNK = 256
FRONT_SORT_CHUNK = 768
SORTED_ROWS = -(-(TILE * TOP_K + N_EXPERTS * (SUBLANES - 1)) // SORT_CHUNK) * SORT_CHUNK
MOE_ROWS = 512
HALF_ROWS = MOE_ROWS // 2
MOE_GROUPS = MOE_ROWS // SUBLANES
HALF_GROUPS = MOE_GROUPS // 2

POOL_HIST = 24
CONV_HIST = 32

BF16 = jnp.bfloat16
F32 = jnp.float32
I32 = jnp.int32
U32 = jnp.uint32
PACKED = D_MODEL // 2


def _rmsnorm(x, g):
    ms = jnp.mean(x * x, axis=-1, keepdims=True)
    return x * lax.rsqrt(ms + RMS_EPS) * g


def _sigmoid(x):
    return 1.0 / (1.0 + jnp.exp(-x))


def _layernorm_silu(d, g, b):
    mu = jnp.mean(d, axis=-1, keepdims=True)
    c = d - mu
    var = jnp.mean(c * c, axis=-1, keepdims=True)
    n = c * lax.rsqrt(var + LN_EPS) * g + b
    return n * _sigmoid(n)


def _pack_halves(a):
    half = a.shape[1] // 2
    hi = lax.bitcast_convert_type(a[:, :half], U32)
    lo = lax.bitcast_convert_type(a[:, half:], U32)
    return hi | (lo >> 16)


def _unpack_halves(w):
    hi = lax.bitcast_convert_type(w & jnp.uint32(0xFFFF0000), F32)
    lo = lax.bitcast_convert_type(w << 16, F32)
    return jnp.concatenate([hi, lo], axis=1)


def _route_and_sort(hn_bf, wr_t_ref, br_ref, tri_ref, q_ref, gate_ref, tbl_ref, xs_ref):
    rows = hn_bf.shape[0]
    lt = lax.dot_general(wr_t_ref[...], hn_bf, (((1,), (1,)), ((), ())),
                         preferred_element_type=F32)
    lt = lt + br_ref[...]
    eidx = lax.broadcasted_iota(I32, (N_EXPERTS, rows), 0)
    vals, hots = [], []
    for _ in range(TOP_K):
        m = jnp.max(lt, axis=0, keepdims=True)
        sel = jnp.min(jnp.where(lt == m, eidx, N_EXPERTS), axis=0, keepdims=True)
        hit = eidx == sel
        vals.append(m)
        hots.append(hit)
        lt = jnp.where(hit, -jnp.inf, lt)
    v = jnp.concatenate(vals, axis=0)
    e = jnp.exp(v - vals[0])
    gate = e / jnp.sum(e, axis=0, keepdims=True)
    gate_ref[...] = jnp.concatenate([gate, jnp.zeros((SUBLANES - TOP_K, rows), F32)], axis=0)

    sel_cnt = sum(h.astype(F32) for h in hots)
    prefix = jnp.dot(sel_cnt.astype(BF16), tri_ref[...], preferred_element_type=F32)
    cnt = jnp.sum(sel_cnt, axis=1, keepdims=True)
    cnt_u = jnp.floor((cnt + (SUBLANES - 1)) * (1.0 / SUBLANES))
    lower = (lax.broadcasted_iota(I32, (N_EXPERTS, N_EXPERTS), 1)
             < lax.broadcasted_iota(I32, (N_EXPERTS, N_EXPERTS), 0)).astype(BF16)
    cnt_u_l = jnp.broadcast_to(cnt_u, (N_EXPERTS, LANES))
    off_u_l = jnp.dot(lower, cnt_u_l.astype(BF16), preferred_element_type=F32)
    off_l = off_u_l * float(SUBLANES)
    tbl_ref[0, 0] = (cnt_u_l * float(SUBLANES)).astype(I32)
    tbl_ref[0, 1] = off_l.astype(I32)

    pos = prefix + off_l[:, :1]
    qs = [jnp.sum(jnp.where(h, pos, 0.0), axis=0, keepdims=True).astype(I32) for h in hots]
    q_ref[...] = jnp.concatenate(qs + [jnp.zeros((SUBLANES - TOP_K, rows), I32)], axis=0)

    def sort_chunk(c, carry):
        r0 = pl.multiple_of(c * FRONT_SORT_CHUNK, FRONT_SORT_CHUNK)
        jidx = r0 + lax.broadcasted_iota(I32, (FRONT_SORT_CHUNK, rows), 0)
        p = sum(jnp.where(jidx == qk, 1.0, 0.0) for qk in qs)
        xs = jnp.dot(p.astype(BF16), hn_bf, preferred_element_type=F32)
        xs_ref[pl.ds(r0, FRONT_SORT_CHUNK), :] = _pack_halves(xs)
        return carry

    lax.fori_loop(0, SORTED_ROWS // FRONT_SORT_CHUNK, sort_chunk, 0)


def _mix_out_and_route(x, mixbuf, w_out_ref, g_moe_ref, wr_t_ref, br_ref, tri_ref,
                       h_ref, q_ref, gate_ref, tbl_ref, xs_ref):
    h = x + jnp.dot(mixbuf[...], w_out_ref[...], preferred_element_type=F32)
    h_ref[...] = h
    hn = _rmsnorm(h, g_moe_ref[...])
    _route_and_sort(hn.astype(BF16), wr_t_ref, br_ref, tri_ref, q_ref, gate_ref, tbl_ref, xs_ref)


def _front_prompt_kernel(x_ref, sp_ref, sc_ref, g_mix_ref, w_in_ref, pool_w_ref, pool_scale_ref,
                         conv_w_ref, conv_b_ref, ln_g_ref, ln_b_ref, w_out_ref, g_moe_ref,
                         wr_t_ref, br_ref, tri_ref,
                         h_ref, q_ref, gate_ref, tbl_ref, xs_ref, np_ref, nc_ref,
                         ubuf, gsh, dbuf, mixbuf, *, pos0, rows):
    t = pl.program_id(1)

    @pl.when(t == 0)
    def _():
        ubuf[:, pl.ds(0, POOL_HIST), :] = jnp.zeros((len(POOL_WINDOWS), POOL_HIST, POOL_CH), F32)
        ubuf[0, pl.ds(POOL_HIST - POOL_BUF, POOL_BUF), :] = sp_ref[0, 0]
        gsh[0, pl.ds(CONV_HIST - CONV_BUF, CONV_BUF), :] = sc_ref[0, 0]

    x = x_ref[0]
    xn = _rmsnorm(x, g_mix_ref[...]).astype(BF16)
    proj = jnp.dot(xn, w_in_ref[...], preferred_element_type=F32)

    u = proj[:, :POOL_CH]
    ubuf[0, pl.ds(POOL_HIST, rows), :] = u
    pos = pos0 + t * rows + lax.broadcasted_iota(I32, (rows, 1), 0)
    span = rows + POOL_HIST - SUBLANES
    for g, w in enumerate(POOL_WINDOWS):
        assert w == 2 ** (g + 1) and w // 2 <= SUBLANES
        c0 = g * POOL_GROUP
        cs = slice(c0, c0 + POOL_GROUP)
        a = ubuf[g, pl.ds(SUBLANES, span), c0:] + ubuf[g, pl.ds(SUBLANES - w // 2, span), c0:]
        if g + 1 < len(POOL_WINDOWS):
            ubuf[g + 1, pl.ds(SUBLANES, span), c0 + POOL_GROUP:] = a[:, POOL_GROUP:]
        s = a[span - rows:, :POOL_GROUP]
        cnt = jnp.minimum(w, pos + 1).astype(F32)
        z = (s / cnt - u[:, cs]).astype(BF16)
        yg = jnp.dot(z, pool_w_ref[g], preferred_element_type=F32)
        mixbuf[:, cs] = (yg * pool_scale_ref[:, cs]).astype(BF16)

    glu = proj[:, POOL_CH:POOL_CH + CONV_CH] * _sigmoid(proj[:, POOL_CH + CONV_CH:])
    gsh[0, pl.ds(CONV_HIST, rows), :] = glu
    for s in range(1, SUBLANES):
        gsh[s, pl.ds(0, CONV_HIST - s), :] = gsh[0, pl.ds(s, CONV_HIST - s), :]
        gsh[s, pl.ds(CONV_HIST - s, rows), :] = glu

    def conv_chunk(i, carry):
        r0 = pl.multiple_of(i * CONV_CHUNK, CONV_CHUNK)
        for ct in range(CONV_CH // LANES):
            cs = slice(ct * LANES, (ct + 1) * LANES)
            acc = jnp.broadcast_to(conv_b_ref[:, cs], (CONV_CHUNK, LANES))
            for k in range(CONV_TAPS):
                first = CONV_HIST - CONV_BUF + k
                s = first % SUBLANES
                acc = acc + conv_w_ref[k:k + 1, cs] * gsh[s, pl.ds(r0 + (first - s), CONV_CHUNK), cs]
            dbuf[pl.ds(r0, CONV_CHUNK), cs] = acc
        return carry

    lax.fori_loop(0, rows // CONV_CHUNK, conv_chunk, 0)
    y_conv = _layernorm_silu(dbuf[...], ln_g_ref[...], ln_b_ref[...])
    mixbuf[:, POOL_CH:] = y_conv.astype(BF16)

    new_pool = ubuf[0, pl.ds(rows + POOL_HIST - POOL_BUF, POOL_BUF), :]
    new_conv = gsh[0, pl.ds(rows + CONV_HIST - CONV_BUF, CONV_BUF), :]
    np_ref[0, 0] = new_pool
    nc_ref[0, 0] = new_conv
    ubuf[0, pl.ds(POOL_HIST - POOL_BUF, POOL_BUF), :] = new_pool
    gsh[0, pl.ds(CONV_HIST - CONV_BUF, CONV_BUF), :] = new_conv

    _mix_out_and_route(x, mixbuf, w_out_ref, g_moe_ref, wr_t_ref, br_ref, tri_ref,
                       h_ref, q_ref, gate_ref, tbl_ref, xs_ref)


def _front_sample_kernel(x_ref, sp_ref, sc_ref, g_mix_ref, w_in_ref, pool_w_ref, pool_scale_ref,
                         conv_w_ref, conv_b_ref, ln_g_ref, ln_b_ref, w_out_ref, g_moe_ref,
                         wr_t_ref, br_ref, tri_ref, xs_in,
                         h_ref, q_ref, gate_ref, tbl_ref, xs_ref, np_ref, nc_ref,
                         xbuf, dbuf, mixbuf, *, pos0, seqs, steps):
    del xs_in
    for t in range(steps):
        xbuf[pl.ds(t * seqs, seqs), :] = x_ref[:, t * D_MODEL:(t + 1) * D_MODEL]
    x = xbuf[...]
    xn = _rmsnorm(x, g_mix_ref[...]).astype(BF16)
    proj = jnp.dot(xn, w_in_ref[...], preferred_element_type=F32)

    def slab(a, t):
        return a[t * seqs:(t + 1) * seqs]

    u = proj[:, :POOL_CH]
    cat = [sp_ref[:, j * POOL_CH:(j + 1) * POOL_CH] for j in range(POOL_BUF)]
    cat += [slab(u, t) for t in range(steps)]
    for t in range(steps):
        for g, w in enumerate(POOL_WINDOWS):
            cs = slice(g * POOL_GROUP, (g + 1) * POOL_GROUP)
            s = cat[POOL_BUF + t][:, cs]
            for j in range(1, w):
                s = s + cat[POOL_BUF + t - j][:, cs]
            cnt = float(min(w, pos0 + t + 1))
            z = (s / cnt - cat[POOL_BUF + t][:, cs]).astype(BF16)
            yg = jnp.dot(z, pool_w_ref[g], preferred_element_type=F32)
            mixbuf[pl.ds(t * seqs, seqs), cs] = (yg * pool_scale_ref[:, cs]).astype(BF16)
    for j in range(POOL_BUF):
        np_ref[:, j * POOL_CH:(j + 1) * POOL_CH] = cat[len(cat) - POOL_BUF + j]

    glu = proj[:, POOL_CH:POOL_CH + CONV_CH] * _sigmoid(proj[:, POOL_CH + CONV_CH:])
    ccat = [sc_ref[:, j * CONV_CH:(j + 1) * CONV_CH] for j in range(CONV_BUF)]
    ccat += [slab(glu, t) for t in range(steps)]
    for t in range(steps):
        acc = jnp.broadcast_to(conv_b_ref[...], (seqs, CONV_CH))
        for k in range(CONV_TAPS):
            acc = acc + conv_w_ref[k:k + 1, :] * ccat[t + k]
        dbuf[pl.ds(t * seqs, seqs), :] = acc
    for j in range(CONV_BUF):
        nc_ref[:, j * CONV_CH:(j + 1) * CONV_CH] = ccat[len(ccat) - CONV_BUF + j]
    y_conv = _layernorm_silu(dbuf[...], ln_g_ref[...], ln_b_ref[...])
    mixbuf[:, POOL_CH:] = y_conv.astype(BF16)

    _mix_out_and_route(x, mixbuf, w_out_ref, g_moe_ref, wr_t_ref, br_ref, tri_ref,
                       h_ref, q_ref, gate_ref, tbl_ref, xs_ref)


def _const_spec(shape, single_buffer=False):
    mode = pl.Buffered(1) if single_buffer else None
    return pl.BlockSpec(shape, lambda *_: (0,) * len(shape), pipeline_mode=mode)


def _front_weight_specs():
    return [
        _const_spec((1, D_MODEL)),
        _const_spec((D_MODEL, IN_COLS), True),
        _const_spec((len(POOL_WINDOWS), POOL_GROUP, POOL_GROUP)),
        _const_spec((1, POOL_CH)),
        _const_spec((CONV_TAPS, CONV_CH)),
        _const_spec((1, CONV_CH)),
        _const_spec((1, CONV_CH)),
        _const_spec((1, CONV_CH)),
        _const_spec((D_MODEL, D_MODEL), True),
        _const_spec((1, D_MODEL)),
        _const_spec((N_EXPERTS, D_MODEL)),
        _const_spec((N_EXPERTS, 1)),
        _const_spec((TILE, TILE), True),
    ]


def _front_out(n_tok, n_tiles_all, tile_map, tok_map, extra_specs, extra_shapes):
    specs = [pl.BlockSpec((TILE, D_MODEL), tok_map),
             pl.BlockSpec((SUBLANES, TILE), lambda *i: (0, tok_map(*i)[0])),
             pl.BlockSpec((SUBLANES, TILE), lambda *i: (0, tok_map(*i)[0])),
             pl.BlockSpec((1, 2, N_EXPERTS, LANES), lambda *i: (tok_map(*i)[0], 0, 0, 0)),
             pl.BlockSpec((SORTED_ROWS, PACKED), tile_map)]
    shapes = [jax.ShapeDtypeStruct((n_tok, D_MODEL), F32),
              jax.ShapeDtypeStruct((SUBLANES, n_tok), I32),
              jax.ShapeDtypeStruct((SUBLANES, n_tok), F32),
              jax.ShapeDtypeStruct((n_tok // TILE, 2, N_EXPERTS, LANES), I32),
              jax.ShapeDtypeStruct((n_tiles_all * SORTED_ROWS, PACKED), U32)]
    return specs + extra_specs, shapes + extra_shapes


def _front_prompt(x, state_pool, state_conv, weights, *, pos0, n_tiles_all):
    nb, nt, _ = x.shape
    rows = TILE
    tiles = nt // rows
    tok_map = lambda b, t: (b * tiles + t, 0)
    out_specs, out_shape = _front_out(
        nb * nt, n_tiles_all, tok_map, tok_map,
        [pl.BlockSpec((1, 1, POOL_BUF, POOL_CH), lambda b, t: (0, b, 0, 0)),
         pl.BlockSpec((1, 1, CONV_BUF, CONV_CH), lambda b, t: (0, b, 0, 0))],
        [jax.ShapeDtypeStruct((1, nb, POOL_BUF, POOL_CH), F32),
         jax.ShapeDtypeStruct((1, nb, CONV_BUF, CONV_CH), F32)])
    kern = functools.partial(_front_prompt_kernel, pos0=pos0, rows=rows)
    return pl.pallas_call(
        kern,
        grid=(nb, tiles),
        in_specs=[pl.BlockSpec((1, rows, D_MODEL), lambda b, t: (b, t, 0)),
                  pl.BlockSpec((1, 1, POOL_BUF, POOL_CH), lambda b, t: (0, b, 0, 0)),
                  pl.BlockSpec((1, 1, CONV_BUF, CONV_CH), lambda b, t: (0, b, 0, 0))]
                 + _front_weight_specs(),
        out_specs=out_specs,
        out_shape=out_shape,
        scratch_shapes=[pltpu.VMEM((len(POOL_WINDOWS), POOL_HIST + rows, POOL_CH), F32),
                        pltpu.VMEM((SUBLANES, CONV_HIST + rows, CONV_CH), F32),
                        pltpu.VMEM((rows, CONV_CH), F32),
                        pltpu.VMEM((rows, D_MODEL), BF16)],
        compiler_params=pltpu.CompilerParams(
            dimension_semantics=("arbitrary", "arbitrary"), vmem_limit_bytes=VMEM_LIMIT_BYTES),
        name="front_prompt",
    )(x, state_pool, state_conv, *weights)


def _front_sample(x, state_pool, state_conv, weights, xs_all, *, pos0, tile0):
    nb, nt, _ = x.shape
    seqs = TILE // nt
    n_tiles_all = xs_all.shape[0] // SORTED_ROWS
    tok_map = lambda i: (i, 0)
    out_specs, out_shape = _front_out(
        nb * nt, n_tiles_all, lambda i: (tile0 + i, 0), tok_map,
        [pl.BlockSpec((seqs, POOL_BUF * POOL_CH), lambda i: (i, 0)),
         pl.BlockSpec((seqs, CONV_BUF * CONV_CH), lambda i: (i, 0))],
        [jax.ShapeDtypeStruct((nb, POOL_BUF * POOL_CH), F32),
         jax.ShapeDtypeStruct((nb, CONV_BUF * CONV_CH), F32)])
    kern = functools.partial(_front_sample_kernel, pos0=pos0, seqs=seqs, steps=nt)
    n_in = 3 + len(weights)
    return pl.pallas_call(
        kern,
        grid=(nb // seqs,),
        in_specs=[pl.BlockSpec((seqs, nt * D_MODEL), lambda i: (i, 0)),
                  pl.BlockSpec((seqs, POOL_BUF * POOL_CH), lambda i: (i, 0)),
                  pl.BlockSpec((seqs, CONV_BUF * CONV_CH), lambda i: (i, 0))]
                 + _front_weight_specs() + [pl.BlockSpec(memory_space=pl.ANY)],
        out_specs=out_specs,
        out_shape=out_shape,
        scratch_shapes=[pltpu.VMEM((TILE, D_MODEL), F32),
                        pltpu.VMEM((TILE, CONV_CH), F32),
                        pltpu.VMEM((TILE, D_MODEL), BF16)],
        input_output_aliases={n_in: 4},
        compiler_params=pltpu.CompilerParams(
            dimension_semantics=("arbitrary",), vmem_limit_bytes=VMEM_LIMIT_BYTES),
        name="front_sample",
    )(x.reshape(nb, nt * D_MODEL), state_pool.reshape(nb, POOL_BUF * POOL_CH),
      state_conv.reshape(nb, CONV_BUF * CONV_CH), *weights, xs_all)


def _moe_kernel(be_ref, nused_ref, src_ref, dst_ref,
                xs_hbm, wgu_ref, bgu_ref, wd_ref, bd_ref, ys_hbm,
                xa, xb, oa, ob, wgu_bf, wd_bf, gsem_a, gsem_b, ssem_a, ssem_b, *, dummy_half):
    i = pl.program_id(0)
    n_used = nused_ref[0]
    half_a = 2 * i
    half_b = 2 * i + 1

    def gather(half, buf, sem):
        for g in range(HALF_GROUPS):
            hrow = pl.multiple_of(src_ref[half * HALF_GROUPS + g], SUBLANES)
            pltpu.make_async_copy(xs_hbm.at[pl.ds(hrow, SUBLANES)],
                                  buf.at[pl.ds(g * SUBLANES, SUBLANES)], sem).start(priority=1)

    def gather_wait(buf, sem):
        pltpu.make_async_copy(xs_hbm.at[pl.ds(0, HALF_ROWS)], buf, sem).wait()

    def scatter(half, buf, sem):
        for g in range(HALF_GROUPS):
            hrow = pl.multiple_of(dst_ref[half * HALF_GROUPS + g], SUBLANES)
            pltpu.make_async_copy(buf.at[pl.ds(g * SUBLANES, SUBLANES)],
                                  ys_hbm.at[pl.ds(hrow, SUBLANES)], sem).start()

    def scatter_wait(buf, sem):
        pltpu.make_async_copy(buf, ys_hbm.at[pl.ds(0, HALF_ROWS)], sem).wait()

    def experts(xbuf, obuf):
        x = _unpack_halves(xbuf[...]).astype(BF16)
        gu = jnp.dot(x, wgu_bf[...], preferred_element_type=F32) + bgu_ref[0]
        g = jnp.minimum(gu[:, :D_FF], SWIGLU_LIMIT)
        up = jnp.clip(gu[:, D_FF:], -SWIGLU_LIMIT, SWIGLU_LIMIT)
        hdn = (up + 1.0) * (g * _sigmoid(SWIGLU_ALPHA * g))
        y = jnp.dot(hdn.astype(BF16), wd_bf[...], preferred_element_type=F32) + bd_ref[0]
        obuf[...] = _pack_halves(y.astype(BF16).astype(F32))

    @pl.when(i == 0)
    def _():
        oa[...] = jnp.zeros_like(oa)
        ob[...] = jnp.zeros_like(ob)
        scatter(dummy_half, oa, ssem_a)
        gather(0, xa, gsem_a)

    @pl.when(i < n_used)
    def _():
        e_cur = be_ref[i]
        e_prev = be_ref[jnp.maximum(i - 1, 0)]

        @pl.when(jnp.logical_or(i == 0, e_cur != e_prev))
        def _():
            wgu_bf[...] = wgu_ref[0].astype(BF16)
            wd_bf[...] = wd_ref[0].astype(BF16)

        prev_b = jnp.where(i == 0, dummy_half, half_b - 2)
        next_a = 2 * jnp.minimum(i + 1, n_used - 1)
        scatter_wait(oa, ssem_a)
        scatter(prev_b, ob, ssem_b)
        gather(half_b, xb, gsem_b)
        gather_wait(xa, gsem_a)
        experts(xa, oa)
        scatter(half_a, oa, ssem_a)
        gather(next_a, xa, gsem_a)
        scatter_wait(ob, ssem_b)
        gather_wait(xb, gsem_b)
        experts(xb, ob)

        @pl.when(i == n_used - 1)
        def _():
            scatter(half_b, ob, ssem_b)
            scatter_wait(oa, ssem_a)
            scatter_wait(ob, ssem_b)
            gather_wait(xa, gsem_a)


def _moe(xs_all, tables, w_gate_up, b_gate_up, w_down, b_down):
    block_e, n_used, src, dst = tables
    n_blocks = block_e.shape[0]

    def e_map(i, be, nu, sr, ds_):
        return (be[jnp.minimum(i, nu[0] - 1)], 0, 0)

    half_buf = pltpu.VMEM((HALF_ROWS, PACKED), U32)
    grid_spec = pltpu.PrefetchScalarGridSpec(
        num_scalar_prefetch=4,
        grid=(n_blocks,),
        in_specs=[pl.BlockSpec(memory_space=pl.ANY),
                  pl.BlockSpec((1, D_MODEL, 2 * D_FF), e_map),
                  pl.BlockSpec((1, 1, 2 * D_FF), e_map),
                  pl.BlockSpec((1, D_FF, D_MODEL), e_map),
                  pl.BlockSpec((1, 1, D_MODEL), e_map)],
        out_specs=pl.BlockSpec(memory_space=pl.ANY),
        scratch_shapes=[half_buf, half_buf, half_buf, half_buf,
                        pltpu.VMEM((D_MODEL, 2 * D_FF), BF16),
                        pltpu.VMEM((D_FF, D_MODEL), BF16)]
                       + [pltpu.SemaphoreType.DMA(())] * 4,
    )
    return pl.pallas_call(
        functools.partial(_moe_kernel, dummy_half=2 * n_blocks),
        grid_spec=grid_spec,
        out_shape=jax.ShapeDtypeStruct((xs_all.shape[0] + MOE_ROWS, PACKED), U32),
        compiler_params=pltpu.CompilerParams(
            dimension_semantics=("arbitrary",), vmem_limit_bytes=VMEM_LIMIT_BYTES),
        name="moe_experts",
    )(block_e, n_used, src, dst,
      xs_all, w_gate_up, b_gate_up.reshape(N_EXPERTS, 1, 2 * D_FF), w_down,
      b_down.reshape(N_EXPERTS, 1, D_MODEL))


def _final_kernel(used_ref, h_ref, ys_ref, q_ref, gate_ref, g_ref, o_ref, *, tile0, slabs):
    i = pl.program_id(0)
    used = used_ref[tile0 + i]

    def to_cols(a):
        pad = jnp.zeros((LANES - SUBLANES, TILE), F32)
        return jnp.concatenate([a, pad], axis=0).T

    qc = to_cols(q_ref[...].astype(F32))
    gc = to_cols(gate_ref[...])
    acc = h_ref[...]
    for c in range(SORTED_ROWS // SORT_CHUNK):
        jl = (c * SORT_CHUNK + lax.broadcasted_iota(I32, (TILE, SORT_CHUNK), 1)).astype(F32)
        pg = sum(jnp.where(qc[:, k:k + 1] == jl, gc[:, k:k + 1], 0.0) for k in range(TOP_K))
        rid = c * SORT_CHUNK + lax.broadcasted_iota(I32, (SORT_CHUNK, 1), 0)
        ys = jnp.where(rid < used, _unpack_halves(ys_ref[pl.ds(c * SORT_CHUNK, SORT_CHUNK), :]), 0.0)
        acc = acc + jnp.dot(pg.astype(BF16), ys.astype(BF16), preferred_element_type=F32)
    y = _rmsnorm(acc, g_ref[...])
    if slabs == 1:
        o_ref[...] = y
    else:
        seqs = TILE // slabs
        for t in range(slabs):
            o_ref[:, t * D_MODEL:(t + 1) * D_MODEL] = y[t * seqs:(t + 1) * seqs]


def _final(h, ys_all, q, gates, used, g_final, *, tile0, slabs):
    n_tok = h.shape[0]
    if slabs == 1:
        out_spec = pl.BlockSpec((TILE, D_MODEL), lambda i, u: (i, 0))
        out_shape = jax.ShapeDtypeStruct((n_tok, D_MODEL), F32)
    else:
        seqs = TILE // slabs
        out_spec = pl.BlockSpec((seqs, slabs * D_MODEL), lambda i, u: (i, 0))
        out_shape = jax.ShapeDtypeStruct((n_tok // slabs, slabs * D_MODEL), F32)
    grid_spec = pltpu.PrefetchScalarGridSpec(
        num_scalar_prefetch=1,
        grid=(n_tok // TILE,),
        in_specs=[pl.BlockSpec((TILE, D_MODEL), lambda i, u: (i, 0)),
                  pl.BlockSpec((SORTED_ROWS, PACKED), lambda i, u: (tile0 + i, 0)),
                  pl.BlockSpec((SUBLANES, TILE), lambda i, u: (0, i)),
                  pl.BlockSpec((SUBLANES, TILE), lambda i, u: (0, i)),
                  pl.BlockSpec((1, D_MODEL), lambda i, u: (0, 0))],
        out_specs=out_spec,
    )
    return pl.pallas_call(
        functools.partial(_final_kernel, tile0=tile0, slabs=slabs),
        grid_spec=grid_spec,
        out_shape=out_shape,
        compiler_params=pltpu.CompilerParams(
            dimension_semantics=("arbitrary",), vmem_limit_bytes=VMEM_LIMIT_BYTES),
        name="final_norm",
    )(used, h, ys_all, q, gates, g_final)


def _block_tables(tbl, n_blocks):
    cnt = tbl[:, 0, :, 0]
    off = tbl[:, 1, :, 0]
    run = jnp.cumsum(cnt, axis=0) - cnt
    total = jnp.sum(cnt, axis=0)
    nblk = (total + MOE_ROWS - 1) // MOE_ROWS
    blk_end = jnp.cumsum(nblk)
    blk_start = blk_end - nblk
    bidx = jnp.arange(n_blocks, dtype=I32)
    block_e = jnp.minimum(jnp.sum(blk_end[None, :] <= bidx[:, None], axis=1), N_EXPERTS - 1).astype(I32)
    row0 = (bidx - blk_start[block_e]) * MOE_ROWS
    n_tiles = cnt.shape[0]
    groups = jnp.arange(MOE_GROUPS, dtype=I32)
    erow = row0[:, None] + groups[None, :] * SUBLANES
    run_b, cnt_b, off_b = run[:, block_e], cnt[:, block_e], off[:, block_e]
    inside = jnp.logical_and(run_b[:, :, None] <= erow[None], erow[None] < (run_b + cnt_b)[:, :, None])
    tile_base = jnp.arange(n_tiles, dtype=I32)[:, None] * SORTED_ROWS + off_b - run_b
    src = jnp.sum(jnp.where(inside, tile_base[:, :, None] + erow[None], 0), axis=0)
    dump = n_tiles * SORTED_ROWS + groups * SUBLANES
    dst = jnp.where(jnp.any(inside, axis=0), src, dump[None, :])
    src = jnp.concatenate([src.reshape(-1), jnp.zeros((HALF_GROUPS,), I32)])
    dst = jnp.concatenate([dst.reshape(-1), dump[:HALF_GROUPS]])
    used = jnp.sum(cnt, axis=1)
    return ((block_e, blk_end[-1:].astype(I32), src.astype(I32), dst.astype(I32)), used.astype(I32))


def kernel(x_prompt, x_sample, state_pool, state_conv, g_mix, w_in, pool_w, pool_scale, conv_w, conv_b,
           conv_ln_g, conv_ln_b, w_out, g_moe, w_router, b_router, w_gate_up, b_gate_up, w_down, b_down,
           g_final):
    nb_p, nt_p, _ = x_prompt.shape
    nb_s, nt_s, _ = x_sample.shape
    n_p, n_s = nb_p * nt_p, nb_s * nt_s
    past_len = 16384
    assert nt_p % TILE == 0 and TILE % nt_s == 0 and n_s % TILE == 0
    tiles_p, tiles_s = n_p // TILE, n_s // TILE
    n_tiles = tiles_p + tiles_s

    tri = (lax.broadcasted_iota(I32, (TILE, TILE), 0) < lax.broadcasted_iota(I32, (TILE, TILE), 1)).astype(BF16)
    weights = (g_mix[0][None], w_in[0].astype(BF16), pool_w[0].astype(BF16), pool_scale[0][None],
               conv_w[0], conv_b[0][None], conv_ln_g[0][None], conv_ln_b[0][None],
               w_out[0].astype(BF16), g_moe[0][None], w_router[0].T.astype(BF16),
               b_router[0][:, None], tri)

    zero_pool = jnp.zeros((1, nb_p, POOL_BUF, POOL_CH), F32)
    zero_conv = jnp.zeros((1, nb_p, CONV_BUF, CONV_CH), F32)
    h_p, q_p, gate_p, tbl_p, xs_all, pool_p, conv_p = _front_prompt(
        x_prompt, zero_pool, zero_conv, weights, pos0=0, n_tiles_all=n_tiles)
    h_s, q_s, gate_s, tbl_s, xs_all, pool_s, conv_s = _front_sample(
        x_sample, state_pool[0], state_conv[0], weights, xs_all, pos0=past_len, tile0=tiles_p)

    sorted_total = n_tiles * (TILE * TOP_K + N_EXPERTS * (SUBLANES - 1))
    n_blocks = -(-sorted_total // MOE_ROWS) + N_EXPERTS
    tables, used = _block_tables(jnp.concatenate([tbl_p, tbl_s], axis=0), n_blocks)
    ys_all = _moe(xs_all, tables, w_gate_up[0], b_gate_up[0], w_down[0], b_down[0])

    gf = g_final[None]
    y_prompt = _final(h_p, ys_all, q_p, gate_p, used, gf, tile0=0, slabs=1)
    y_sample = _final(h_s, ys_all, q_s, gate_s, used, gf, tile0=tiles_p, slabs=nt_s)
    return (y_prompt.reshape(nb_p, nt_p, D_MODEL), y_sample.reshape(nb_s, nt_s, D_MODEL),
            pool_p, conv_p,
            pool_s.reshape(1, nb_s, POOL_BUF, POOL_CH), conv_s.reshape(1, nb_s, CONV_BUF, CONV_CH))
```

```python
import functools

import jax
import jax.numpy as jnp
from jax import lax
from jax.experimental import pallas as pl
from jax.experimental.pallas import tpu as pltpu

D_MODEL = 1024
POOL_CH = 512
CONV_CH = 512
POOL_WINDOWS = (2, 4, 8, 16)
POOL_GROUP = 128
POOL_BUF = 15
CONV_TAPS = 31
CONV_BUF = 30
IN_COLS = POOL_CH + 2 * CONV_CH
N_EXPERTS = 32
TOP_K = 4
D_FF = 1024
SWIGLU_LIMIT = 7.0
SWIGLU_ALPHA = 1.702
RMS_EPS = 1e-5
LN_EPS = 1e-5

LANES = 128
SUBLANES = 8
VMEM_LIMIT_BYTES = 60 * 1024 * 1024

TILE = 512
SAMPLE_SEQS = 64
CONV_CHUNK = 32
SORT_CHUNK = 256
FRONT_SORT_CHUNK = 768
SORTED_ROWS = -(-(TILE * TOP_K + N_EXPERTS * (SUBLANES - 1)) // SORT_CHUNK) * SORT_CHUNK
MOE_ROWS = 512
HALF_ROWS = MOE_ROWS // 2
MOE_GROUPS = MOE_ROWS // SUBLANES
HALF_GROUPS = MOE_GROUPS // 2

POOL_HIST = 24
CONV_HIST = 32

BF16 = jnp.bfloat16
F32 = jnp.float32
I32 = jnp.int32
U32 = jnp.uint32
PACKED = D_MODEL // 2


def _rmsnorm(x, g):
    ms = jnp.mean(x * x, axis=-1, keepdims=True)
    return x * lax.rsqrt(ms + RMS_EPS) * g


def _sigmoid(x):
    return 1.0 / (1.0 + jnp.exp(-x))


def _layernorm_silu(d, g, b):
    mu = jnp.mean(d, axis=-1, keepdims=True)
    c = d - mu
    var = jnp.mean(c * c, axis=-1, keepdims=True)
    n = c * lax.rsqrt(var + LN_EPS) * g + b
    return n * _sigmoid(n)


def _pack_halves(a):
    half = a.shape[1] // 2
    hi = lax.bitcast_convert_type(a[:, :half], U32)
    lo = lax.bitcast_convert_type(a[:, half:], U32)
    return hi | (lo >> 16)


def _unpack_halves(w):
    hi = lax.bitcast_convert_type(w & jnp.uint32(0xFFFF0000), F32)
    lo = lax.bitcast_convert_type(w << 16, F32)
    return jnp.concatenate([hi, lo], axis=1)


def _route_and_sort(hn_bf, wr_t_ref, br_ref, tri_ref, q_ref, gate_ref, tbl_ref, xs_ref):
    rows = hn_bf.shape[0]
    lt = lax.dot_general(wr_t_ref[...], hn_bf, (((1,), (1,)), ((), ())),
                         preferred_element_type=F32)
    lt = lt + br_ref[...]
    eidx = lax.broadcasted_iota(I32, (N_EXPERTS, rows), 0)
    vals, hots = [], []
    for _ in range(TOP_K):
        m = jnp.max(lt, axis=0, keepdims=True)
        sel = jnp.min(jnp.where(lt == m, eidx, N_EXPERTS), axis=0, keepdims=True)
        hit = eidx == sel
        vals.append(m)
        hots.append(hit)
        lt = jnp.where(hit, -jnp.inf, lt)
    v = jnp.concatenate(vals, axis=0)
    e = jnp.exp(v - vals[0])
    gate = e / jnp.sum(e, axis=0, keepdims=True)
    gate_ref[...] = jnp.concatenate([gate, jnp.zeros((SUBLANES - TOP_K, rows), F32)], axis=0)

    sel_cnt = sum(h.astype(F32) for h in hots)
    prefix = jnp.dot(sel_cnt.astype(BF16), tri_ref[...], preferred_element_type=F32)
    cnt = jnp.sum(sel_cnt, axis=1, keepdims=True)
    cnt_u = jnp.floor((cnt + (SUBLANES - 1)) * (1.0 / SUBLANES))
    lower = (lax.broadcasted_iota(I32, (N_EXPERTS, N_EXPERTS), 1)
             < lax.broadcasted_iota(I32, (N_EXPERTS, N_EXPERTS), 0)).astype(BF16)
    cnt_u_l = jnp.broadcast_to(cnt_u, (N_EXPERTS, LANES))
    off_u_l = jnp.dot(lower, cnt_u_l.astype(BF16), preferred_element_type=F32)
    off_l = off_u_l * float(SUBLANES)
    tbl_ref[0, 0] = (cnt_u_l * float(SUBLANES)).astype(I32)
    tbl_ref[0, 1] = off_l.astype(I32)

    pos = prefix + off_l[:, :1]
    qs = [jnp.sum(jnp.where(h, pos, 0.0), axis=0, keepdims=True).astype(I32) for h in hots]
    q_ref[...] = jnp.concatenate(qs + [jnp.zeros((SUBLANES - TOP_K, rows), I32)], axis=0)

    def sort_chunk(c, carry):
        r0 = pl.multiple_of(c * FRONT_SORT_CHUNK, FRONT_SORT_CHUNK)
        jidx = r0 + lax.broadcasted_iota(I32, (FRONT_SORT_CHUNK, rows), 0)
        p = sum(jnp.where(jidx == qk, 1.0, 0.0) for qk in qs)
        xs = jnp.dot(p.astype(BF16), hn_bf, preferred_element_type=F32)
        xs_ref[pl.ds(r0, FRONT_SORT_CHUNK), :] = _pack_halves(xs)
        return carry

    lax.fori_loop(0, SORTED_ROWS // FRONT_SORT_CHUNK, sort_chunk, 0)


def _mix_out_and_route(x, mixbuf, w_out_ref, g_moe_ref, wr_t_ref, br_ref, tri_ref,
                       h_ref, q_ref, gate_ref, tbl_ref, xs_ref):
    h = x + jnp.dot(mixbuf[...], w_out_ref[...], preferred_element_type=F32)
    h_ref[...] = h
    hn = _rmsnorm(h, g_moe_ref[...])
    _route_and_sort(hn.astype(BF16), wr_t_ref, br_ref, tri_ref, q_ref, gate_ref, tbl_ref, xs_ref)


def _front_prompt_kernel(x_ref, sp_ref, sc_ref, g_mix_ref, w_in_ref, pool_w_ref, pool_scale_ref,
                         conv_w_ref, conv_b_ref, ln_g_ref, ln_b_ref, w_out_ref, g_moe_ref,
                         wr_t_ref, br_ref, tri_ref,
                         h_ref, q_ref, gate_ref, tbl_ref, xs_ref, np_ref, nc_ref,
                         ubuf, gsh, dbuf, mixbuf, *, pos0, rows):
    t = pl.program_id(1)

    @pl.when(t == 0)
    def _():
        ubuf[:, pl.ds(0, POOL_HIST), :] = jnp.zeros((len(POOL_WINDOWS), POOL_HIST, POOL_CH), F32)
        ubuf[0, pl.ds(POOL_HIST - POOL_BUF, POOL_BUF), :] = sp_ref[0, 0]
        gsh[0, pl.ds(CONV_HIST - CONV_BUF, CONV_BUF), :] = sc_ref[0, 0]

    x = x_ref[0]
    xn = _rmsnorm(x, g_mix_ref[...]).astype(BF16)
    proj = jnp.dot(xn, w_in_ref[...], preferred_element_type=F32)

    u = proj[:, :POOL_CH]
    ubuf[0, pl.ds(POOL_HIST, rows), :] = u
    pos = pos0 + t * rows + lax.broadcasted_iota(I32, (rows, 1), 0)
    span = rows + POOL_HIST - SUBLANES
    for g, w in enumerate(POOL_WINDOWS):
        assert w == 2 ** (g + 1) and w // 2 <= SUBLANES
        c0 = g * POOL_GROUP
        cs = slice(c0, c0 + POOL_GROUP)
        a = ubuf[g, pl.ds(SUBLANES, span), c0:] + ubuf[g, pl.ds(SUBLANES - w // 2, span), c0:]
        if g + 1 < len(POOL_WINDOWS):
            ubuf[g + 1, pl.ds(SUBLANES, span), c0 + POOL_GROUP:] = a[:, POOL_GROUP:]
        s = a[span - rows:, :POOL_GROUP]
        cnt = jnp.minimum(w, pos + 1).astype(F32)
        z = (s / cnt - u[:, cs]).astype(BF16)
        yg = jnp.dot(z, pool_w_ref[g], preferred_element_type=F32)
        mixbuf[:, cs] = (yg * pool_scale_ref[:, cs]).astype(BF16)

    glu = proj[:, POOL_CH:POOL_CH + CONV_CH] * _sigmoid(proj[:, POOL_CH + CONV_CH:])
    gsh[0, pl.ds(CONV_HIST, rows), :] = glu
    for s in range(1, SUBLANES):
        gsh[s, pl.ds(0, CONV_HIST - s), :] = gsh[0, pl.ds(s, CONV_HIST - s), :]
        gsh[s, pl.ds(CONV_HIST - s, rows), :] = glu

    def conv_chunk(i, carry):
        r0 = pl.multiple_of(i * CONV_CHUNK, CONV_CHUNK)
        for ct in range(CONV_CH // LANES):
            cs = slice(ct * LANES, (ct + 1) * LANES)
            acc = jnp.broadcast_to(conv_b_ref[:, cs], (CONV_CHUNK, LANES))
            for k in range(CONV_TAPS):
                first = CONV_HIST - CONV_BUF + k
                s = first % SUBLANES
                acc = acc + conv_w_ref[k:k + 1, cs] * gsh[s, pl.ds(r0 + (first - s), CONV_CHUNK), cs]
            dbuf[pl.ds(r0, CONV_CHUNK), cs] = acc
        return carry

    lax.fori_loop(0, rows // CONV_CHUNK, conv_chunk, 0)
    y_conv = _layernorm_silu(dbuf[...], ln_g_ref[...], ln_b_ref[...])
    mixbuf[:, POOL_CH:] = y_conv.astype(BF16)

    new_pool = ubuf[0, pl.ds(rows + POOL_HIST - POOL_BUF, POOL_BUF), :]
    new_conv = gsh[0, pl.ds(rows + CONV_HIST - CONV_BUF, CONV_BUF), :]
    np_ref[0, 0] = new_pool
    nc_ref[0, 0] = new_conv
    ubuf[0, pl.ds(POOL_HIST - POOL_BUF, POOL_BUF), :] = new_pool
    gsh[0, pl.ds(CONV_HIST - CONV_BUF, CONV_BUF), :] = new_conv

    _mix_out_and_route(x, mixbuf, w_out_ref, g_moe_ref, wr_t_ref, br_ref, tri_ref,
                       h_ref, q_ref, gate_ref, tbl_ref, xs_ref)


def _front_sample_kernel(x_ref, sp_ref, sc_ref, g_mix_ref, w_in_ref, pool_w_ref, pool_scale_ref,
                         conv_w_ref, conv_b_ref, ln_g_ref, ln_b_ref, w_out_ref, g_moe_ref,
                         wr_t_ref, br_ref, tri_ref, xs_in,
                         h_ref, q_ref, gate_ref, tbl_ref, xs_ref, np_ref, nc_ref,
                         xbuf, dbuf, mixbuf, *, pos0, seqs, steps):
    del xs_in
    for t in range(steps):
        xbuf[pl.ds(t * seqs, seqs), :] = x_ref[:, t * D_MODEL:(t + 1) * D_MODEL]
    x = xbuf[...]
    xn = _rmsnorm(x, g_mix_ref[...]).astype(BF16)
    proj = jnp.dot(xn, w_in_ref[...], preferred_element_type=F32)

    def slab(a, t):
        return a[t * seqs:(t + 1) * seqs]

    u = proj[:, :POOL_CH]
    cat = [sp_ref[:, j * POOL_CH:(j + 1) * POOL_CH] for j in range(POOL_BUF)]
    cat += [slab(u, t) for t in range(steps)]
    for t in range(steps):
        for g, w in enumerate(POOL_WINDOWS):
            cs = slice(g * POOL_GROUP, (g + 1) * POOL_GROUP)
            s = cat[POOL_BUF + t][:, cs]
            for j in range(1, w):
                s = s + cat[POOL_BUF + t - j][:, cs]
            cnt = float(min(w, pos0 + t + 1))
            z = (s / cnt - cat[POOL_BUF + t][:, cs]).astype(BF16)
            yg = jnp.dot(z, pool_w_ref[g], preferred_element_type=F32)
            mixbuf[pl.ds(t * seqs, seqs), cs] = (yg * pool_scale_ref[:, cs]).astype(BF16)
    for j in range(POOL_BUF):
        np_ref[:, j * POOL_CH:(j + 1) * POOL_CH] = cat[len(cat) - POOL_BUF + j]

    glu = proj[:, POOL_CH:POOL_CH + CONV_CH] * _sigmoid(proj[:, POOL_CH + CONV_CH:])
    ccat = [sc_ref[:, j * CONV_CH:(j + 1) * CONV_CH] for j in range(CONV_BUF)]
    ccat += [slab(glu, t) for t in range(steps)]
    for t in range(steps):
        acc = jnp.broadcast_to(conv_b_ref[...], (seqs, CONV_CH))
        for k in range(CONV_TAPS):
            acc = acc + conv_w_ref[k:k + 1, :] * ccat[t + k]
        dbuf[pl.ds(t * seqs, seqs), :] = acc
    for j in range(CONV_BUF):
        nc_ref[:, j * CONV_CH:(j + 1) * CONV_CH] = ccat[len(ccat) - CONV_BUF + j]
    y_conv = _layernorm_silu(dbuf[...], ln_g_ref[...], ln_b_ref[...])
    mixbuf[:, POOL_CH:] = y_conv.astype(BF16)

    _mix_out_and_route(x, mixbuf, w_out_ref, g_moe_ref, wr_t_ref, br_ref, tri_ref,
                       h_ref, q_ref, gate_ref, tbl_ref, xs_ref)


def _const_spec(shape, single_buffer=False):
    mode = pl.Buffered(1) if single_buffer else None
    return pl.BlockSpec(shape, lambda *_: (0,) * len(shape), pipeline_mode=mode)


def _front_weight_specs():
    return [
        _const_spec((1, D_MODEL)),
        _const_spec((D_MODEL, IN_COLS), True),
        _const_spec((len(POOL_WINDOWS), POOL_GROUP, POOL_GROUP)),
        _const_spec((1, POOL_CH)),
        _const_spec((CONV_TAPS, CONV_CH)),
        _const_spec((1, CONV_CH)),
        _const_spec((1, CONV_CH)),
        _const_spec((1, CONV_CH)),
        _const_spec((D_MODEL, D_MODEL), True),
        _const_spec((1, D_MODEL)),
        _const_spec((N_EXPERTS, D_MODEL)),
        _const_spec((N_EXPERTS, 1)),
        _const_spec((TILE, TILE), True),
    ]


def _front_out(n_tok, n_tiles_all, tile_map, tok_map, extra_specs, extra_shapes):
    specs = [pl.BlockSpec((TILE, D_MODEL), tok_map),
             pl.BlockSpec((SUBLANES, TILE), lambda *i: (0, tok_map(*i)[0])),
             pl.BlockSpec((SUBLANES, TILE), lambda *i: (0, tok_map(*i)[0])),
             pl.BlockSpec((1, 2, N_EXPERTS, LANES), lambda *i: (tok_map(*i)[0], 0, 0, 0)),
             pl.BlockSpec((SORTED_ROWS, PACKED), tile_map)]
    shapes = [jax.ShapeDtypeStruct((n_tok, D_MODEL), F32),
              jax.ShapeDtypeStruct((SUBLANES, n_tok), I32),
              jax.ShapeDtypeStruct((SUBLANES, n_tok), F32),
              jax.ShapeDtypeStruct((n_tok // TILE, 2, N_EXPERTS, LANES), I32),
              jax.ShapeDtypeStruct((n_tiles_all * SORTED_ROWS, PACKED), U32)]
    return specs + extra_specs, shapes + extra_shapes


def _front_prompt(x, state_pool, state_conv, weights, *, pos0, n_tiles_all):
    nb, nt, _ = x.shape
    rows = TILE
    tiles = nt // rows
    tok_map = lambda b, t: (b * tiles + t, 0)
    out_specs, out_shape = _front_out(
        nb * nt, n_tiles_all, tok_map, tok_map,
        [pl.BlockSpec((1, 1, POOL_BUF, POOL_CH), lambda b, t: (0, b, 0, 0)),
         pl.BlockSpec((1, 1, CONV_BUF, CONV_CH), lambda b, t: (0, b, 0, 0))],
        [jax.ShapeDtypeStruct((1, nb, POOL_BUF, POOL_CH), F32),
         jax.ShapeDtypeStruct((1, nb, CONV_BUF, CONV_CH), F32)])
    kern = functools.partial(_front_prompt_kernel, pos0=pos0, rows=rows)
    return pl.pallas_call(
        kern,
        grid=(nb, tiles),
        in_specs=[pl.BlockSpec((1, rows, D_MODEL), lambda b, t: (b, t, 0)),
                  pl.BlockSpec((1, 1, POOL_BUF, POOL_CH), lambda b, t: (0, b, 0, 0)),
                  pl.BlockSpec((1, 1, CONV_BUF, CONV_CH), lambda b, t: (0, b, 0, 0))]
                 + _front_weight_specs(),
        out_specs=out_specs,
        out_shape=out_shape,
        scratch_shapes=[pltpu.VMEM((len(POOL_WINDOWS), POOL_HIST + rows, POOL_CH), F32),
                        pltpu.VMEM((SUBLANES, CONV_HIST + rows, CONV_CH), F32),
                        pltpu.VMEM((rows, CONV_CH), F32),
                        pltpu.VMEM((rows, D_MODEL), BF16)],
        compiler_params=pltpu.CompilerParams(
            dimension_semantics=("arbitrary", "arbitrary"), vmem_limit_bytes=VMEM_LIMIT_BYTES),
        name="front_prompt",
    )(x, state_pool, state_conv, *weights)


def _front_sample(x, state_pool, state_conv, weights, xs_all, *, pos0, tile0):
    nb, nt, _ = x.shape
    seqs = TILE // nt
    n_tiles_all = xs_all.shape[0] // SORTED_ROWS
    tok_map = lambda i: (i, 0)
    out_specs, out_shape = _front_out(
        nb * nt, n_tiles_all, lambda i: (tile0 + i, 0), tok_map,
        [pl.BlockSpec((seqs, POOL_BUF * POOL_CH), lambda i: (i, 0)),
         pl.BlockSpec((seqs, CONV_BUF * CONV_CH), lambda i: (i, 0))],
        [jax.ShapeDtypeStruct((nb, POOL_BUF * POOL_CH), F32),
         jax.ShapeDtypeStruct((nb, CONV_BUF * CONV_CH), F32)])
    kern = functools.partial(_front_sample_kernel, pos0=pos0, seqs=seqs, steps=nt)
    n_in = 3 + len(weights)
    return pl.pallas_call(
        kern,
        grid=(nb // seqs,),
        in_specs=[pl.BlockSpec((seqs, nt * D_MODEL), lambda i: (i, 0)),
                  pl.BlockSpec((seqs, POOL_BUF * POOL_CH), lambda i: (i, 0)),
                  pl.BlockSpec((seqs, CONV_BUF * CONV_CH), lambda i: (i, 0))]
                 + _front_weight_specs() + [pl.BlockSpec(memory_space=pl.ANY)],
        out_specs=out_specs,
        out_shape=out_shape,
        scratch_shapes=[pltpu.VMEM((TILE, D_MODEL), F32),
                        pltpu.VMEM((TILE, CONV_CH), F32),
                        pltpu.VMEM((TILE, D_MODEL), BF16)],
        input_output_aliases={n_in: 4},
        compiler_params=pltpu.CompilerParams(
            dimension_semantics=("arbitrary",), vmem_limit_bytes=VMEM_LIMIT_BYTES),
        name="front_sample",
    )(x.reshape(nb, nt * D_MODEL), state_pool.reshape(nb, POOL_BUF * POOL_CH),
      state_conv.reshape(nb, CONV_BUF * CONV_CH), *weights, xs_all)


def _moe_kernel(be_ref, nused_ref, src_ref, dst_ref, first_ref, wslot_ref, nxt_ref,
                xs_hbm, wgu_hbm, bgu_ref, wd_hbm, bd_ref, ys_hbm,
                xa, xb, oa, ob, wgu_f32, wd_f32, wgu_bf, wd_bf,
                gsem_a, gsem_b, ssem_a, ssem_b, wsem, *, dummy_half):
    i = pl.program_id(0)
    n_used = nused_ref[0]
    half_a = 2 * i
    half_b = 2 * i + 1

    def weight_copies(e, s):
        return (pltpu.make_async_copy(wgu_hbm.at[e], wgu_f32.at[s], wsem.at[s]),
                pltpu.make_async_copy(wd_hbm.at[e], wd_f32.at[s], wsem.at[s]))

    def gather(half, buf, sem):
        for g in range(HALF_GROUPS):
            hrow = pl.multiple_of(src_ref[half * HALF_GROUPS + g], SUBLANES)
            pltpu.make_async_copy(xs_hbm.at[pl.ds(hrow, SUBLANES)],
                                  buf.at[pl.ds(g * SUBLANES, SUBLANES)], sem).start(priority=1)

    def gather_wait(buf, sem):
        pltpu.make_async_copy(xs_hbm.at[pl.ds(0, HALF_ROWS)], buf, sem).wait()

    def scatter(half, buf, sem):
        for g in range(HALF_GROUPS):
            hrow = pl.multiple_of(dst_ref[half * HALF_GROUPS + g], SUBLANES)
            pltpu.make_async_copy(buf.at[pl.ds(g * SUBLANES, SUBLANES)],
                                  ys_hbm.at[pl.ds(hrow, SUBLANES)], sem).start()

    def scatter_wait(buf, sem):
        pltpu.make_async_copy(buf, ys_hbm.at[pl.ds(0, HALF_ROWS)], sem).wait()

    def experts(xbuf, obuf):
        x = _unpack_halves(xbuf[...]).astype(BF16)
        gu = jnp.dot(x, wgu_bf[...], preferred_element_type=F32) + bgu_ref[0]
        g = jnp.minimum(gu[:, :D_FF], SWIGLU_LIMIT)
        up = jnp.clip(gu[:, D_FF:], -SWIGLU_LIMIT, SWIGLU_LIMIT)
        hdn = (up + 1.0) * (g * _sigmoid(SWIGLU_ALPHA * g))
        y = jnp.dot(hdn.astype(BF16), wd_bf[...], preferred_element_type=F32) + bd_ref[0]
        obuf[...] = _pack_halves(y.astype(BF16).astype(F32))

    @pl.when(i == 0)
    def _():
        oa[...] = jnp.zeros_like(oa)
        ob[...] = jnp.zeros_like(ob)
        scatter(dummy_half, oa, ssem_a)
        gather(0, xa, gsem_a)
        for c in weight_copies(be_ref[0], 0):
            c.start()

    @pl.when(i < n_used)
    def _():
        @pl.when(first_ref[i] == 1)
        def _():
            s = wslot_ref[i]
            for c in weight_copies(be_ref[i], s):
                c.wait()
            wgu_bf[...] = wgu_f32[s].astype(BF16)
            wd_bf[...] = wd_f32[s].astype(BF16)

            @pl.when(nxt_ref[i] < N_EXPERTS)
            def _():
                for c in weight_copies(nxt_ref[i], 1 - s):
                    c.start()

        prev_b = jnp.where(i == 0, dummy_half, half_b - 2)
        next_a = 2 * jnp.minimum(i + 1, n_used - 1)
        scatter_wait(oa, ssem_a)
        scatter(prev_b, ob, ssem_b)
        gather(half_b, xb, gsem_b)
        gather_wait(xa, gsem_a)
        experts(xa, oa)
        scatter(half_a, oa, ssem_a)
        gather(next_a, xa, gsem_a)
        scatter_wait(ob, ssem_b)
        gather_wait(xb, gsem_b)
        experts(xb, ob)

        @pl.when(i == n_used - 1)
        def _():
            scatter(half_b, ob, ssem_b)
            scatter_wait(oa, ssem_a)
            scatter_wait(ob, ssem_b)
            gather_wait(xa, gsem_a)


def _moe(xs_all, tables, w_gate_up, b_gate_up, w_down, b_down):
    block_e, n_used, src, dst, first, wslot, nxt = tables
    n_blocks = block_e.shape[0]

    def e_map(i, be, nu, *_):
        return (be[jnp.minimum(i, nu[0] - 1)], 0, 0)

    half_buf = pltpu.VMEM((HALF_ROWS, PACKED), U32)
    any_spec = pl.BlockSpec(memory_space=pl.ANY)
    grid_spec = pltpu.PrefetchScalarGridSpec(
        num_scalar_prefetch=7,
        grid=(n_blocks,),
        in_specs=[any_spec,
                  any_spec,
                  pl.BlockSpec((1, 1, 2 * D_FF), e_map),
                  any_spec,
                  pl.BlockSpec((1, 1, D_MODEL), e_map)],
        out_specs=any_spec,
        scratch_shapes=[half_buf, half_buf, half_buf, half_buf,
                        pltpu.VMEM((2, D_MODEL, 2 * D_FF), F32),
                        pltpu.VMEM((2, D_FF, D_MODEL), F32),
                        pltpu.VMEM((D_MODEL, 2 * D_FF), BF16),
                        pltpu.VMEM((D_FF, D_MODEL), BF16)]
                       + [pltpu.SemaphoreType.DMA(())] * 4 + [pltpu.SemaphoreType.DMA((2,))],
    )
    return pl.pallas_call(
        functools.partial(_moe_kernel, dummy_half=2 * n_blocks),
        grid_spec=grid_spec,
        out_shape=jax.ShapeDtypeStruct((xs_all.shape[0] + MOE_ROWS, PACKED), U32),
        compiler_params=pltpu.CompilerParams(
            dimension_semantics=("arbitrary",), vmem_limit_bytes=VMEM_LIMIT_BYTES),
        name="moe_experts",
    )(block_e, n_used, src, dst, first, wslot, nxt,
      xs_all, w_gate_up, b_gate_up.reshape(N_EXPERTS, 1, 2 * D_FF), w_down,
      b_down.reshape(N_EXPERTS, 1, D_MODEL))


def _final_kernel(used_ref, h_ref, ys_ref, q_ref, gate_ref, g_ref, o_ref, *, tile0, slabs):
    i = pl.program_id(0)
    used = used_ref[tile0 + i]

    def to_cols(a):
        pad = jnp.zeros((LANES - SUBLANES, TILE), F32)
        return jnp.concatenate([a, pad], axis=0).T

    qc = to_cols(q_ref[...].astype(F32))
    gc = to_cols(gate_ref[...])
    acc = h_ref[...]
    for c in range(SORTED_ROWS // SORT_CHUNK):
        jl = (c * SORT_CHUNK + lax.broadcasted_iota(I32, (TILE, SORT_CHUNK), 1)).astype(F32)
        pg = sum(jnp.where(qc[:, k:k + 1] == jl, gc[:, k:k + 1], 0.0) for k in range(TOP_K))
        rid = c * SORT_CHUNK + lax.broadcasted_iota(I32, (SORT_CHUNK, 1), 0)
        ys = jnp.where(rid < used, _unpack_halves(ys_ref[pl.ds(c * SORT_CHUNK, SORT_CHUNK), :]), 0.0)
        acc = acc + jnp.dot(pg.astype(BF16), ys.astype(BF16), preferred_element_type=F32)
    y = _rmsnorm(acc, g_ref[...])
    if slabs == 1:
        o_ref[...] = y
    else:
        seqs = TILE // slabs
        for t in range(slabs):
            o_ref[:, t * D_MODEL:(t + 1) * D_MODEL] = y[t * seqs:(t + 1) * seqs]


def _final(h, ys_all, q, gates, used, g_final, *, tile0, slabs):
    n_tok = h.shape[0]
    if slabs == 1:
        out_spec = pl.BlockSpec((TILE, D_MODEL), lambda i, u: (i, 0))
        out_shape = jax.ShapeDtypeStruct((n_tok, D_MODEL), F32)
    else:
        seqs = TILE // slabs
        out_spec = pl.BlockSpec((seqs, slabs * D_MODEL), lambda i, u: (i, 0))
        out_shape = jax.ShapeDtypeStruct((n_tok // slabs, slabs * D_MODEL), F32)
    grid_spec = pltpu.PrefetchScalarGridSpec(
        num_scalar_prefetch=1,
        grid=(n_tok // TILE,),
        in_specs=[pl.BlockSpec((TILE, D_MODEL), lambda i, u: (i, 0)),
                  pl.BlockSpec((SORTED_ROWS, PACKED), lambda i, u: (tile0 + i, 0)),
                  pl.BlockSpec((SUBLANES, TILE), lambda i, u: (0, i)),
                  pl.BlockSpec((SUBLANES, TILE), lambda i, u: (0, i)),
                  pl.BlockSpec((1, D_MODEL), lambda i, u: (0, 0))],
        out_specs=out_spec,
    )
    return pl.pallas_call(
        functools.partial(_final_kernel, tile0=tile0, slabs=slabs),
        grid_spec=grid_spec,
        out_shape=out_shape,
        compiler_params=pltpu.CompilerParams(
            dimension_semantics=("arbitrary",), vmem_limit_bytes=VMEM_LIMIT_BYTES),
        name="final_norm",
    )(used, h, ys_all, q, gates, g_final)


def _block_tables(tbl, n_blocks):
    cnt = tbl[:, 0, :, 0]
    off = tbl[:, 1, :, 0]
    run = jnp.cumsum(cnt, axis=0) - cnt
    total = jnp.sum(cnt, axis=0)
    nblk = (total + MOE_ROWS - 1) // MOE_ROWS
    blk_end = jnp.cumsum(nblk)
    blk_start = blk_end - nblk
    bidx = jnp.arange(n_blocks, dtype=I32)
    block_e = jnp.minimum(jnp.sum(blk_end[None, :] <= bidx[:, None], axis=1), N_EXPERTS - 1).astype(I32)
    row0 = (bidx - blk_start[block_e]) * MOE_ROWS
    n_tiles = cnt.shape[0]
    groups = jnp.arange(MOE_GROUPS, dtype=I32)
    erow = row0[:, None] + groups[None, :] * SUBLANES
    run_b, cnt_b, off_b = run[:, block_e], cnt[:, block_e], off[:, block_e]
    inside = jnp.logical_and(run_b[:, :, None] <= erow[None], erow[None] < (run_b + cnt_b)[:, :, None])
    tile_base = jnp.arange(n_tiles, dtype=I32)[:, None] * SORTED_ROWS + off_b - run_b
    src = jnp.sum(jnp.where(inside, tile_base[:, :, None] + erow[None], 0), axis=0)
    dump = n_tiles * SORTED_ROWS + groups * SUBLANES
    dst = jnp.where(jnp.any(inside, axis=0), src, dump[None, :])
    src = jnp.concatenate([src.reshape(-1), jnp.zeros((HALF_GROUPS,), I32)])
    dst = jnp.concatenate([dst.reshape(-1), dump[:HALF_GROUPS]])
    used = jnp.sum(cnt, axis=1)
    has_rows = nblk > 0
    first = (bidx == blk_start[block_e]).astype(I32)
    wslot = ((jnp.cumsum(has_rows) - 1) % 2)[block_e]
    eidx = jnp.arange(N_EXPERTS, dtype=I32)
    later = jnp.where(jnp.logical_and(has_rows[None, :], eidx[None, :] > eidx[:, None]), eidx[None, :], N_EXPERTS)
    nxt = jnp.min(later, axis=1)[block_e]
    return ((block_e, blk_end[-1:].astype(I32), src.astype(I32), dst.astype(I32), first,
             wslot.astype(I32), nxt.astype(I32)), used.astype(I32))


def kernel(x_prompt, x_sample, state_pool, state_conv, g_mix, w_in, pool_w, pool_scale, conv_w, conv_b,
           conv_ln_g, conv_ln_b, w_out, g_moe, w_router, b_router, w_gate_up, b_gate_up, w_down, b_down,
           g_final):
    nb_p, nt_p, _ = x_prompt.shape
    nb_s, nt_s, _ = x_sample.shape
    n_p, n_s = nb_p * nt_p, nb_s * nt_s
    past_len = 16384
    assert nt_p % TILE == 0 and TILE % nt_s == 0 and n_s % TILE == 0
    tiles_p, tiles_s = n_p // TILE, n_s // TILE
    n_tiles = tiles_p + tiles_s

    tri = (lax.broadcasted_iota(I32, (TILE, TILE), 0) < lax.broadcasted_iota(I32, (TILE, TILE), 1)).astype(BF16)
    weights = (g_mix[0][None], w_in[0].astype(BF16), pool_w[0].astype(BF16), pool_scale[0][None],
               conv_w[0], conv_b[0][None], conv_ln_g[0][None], conv_ln_b[0][None],
               w_out[0].astype(BF16), g_moe[0][None], w_router[0].T.astype(BF16),
               b_router[0][:, None], tri)

    zero_pool = jnp.zeros((1, nb_p, POOL_BUF, POOL_CH), F32)
    zero_conv = jnp.zeros((1, nb_p, CONV_BUF, CONV_CH), F32)
    h_p, q_p, gate_p, tbl_p, xs_all, pool_p, conv_p = _front_prompt(
        x_prompt, zero_pool, zero_conv, weights, pos0=0, n_tiles_all=n_tiles)
    h_s, q_s, gate_s, tbl_s, xs_all, pool_s, conv_s = _front_sample(
        x_sample, state_pool[0], state_conv[0], weights, xs_all, pos0=past_len, tile0=tiles_p)

    sorted_total = n_tiles * (TILE * TOP_K + N_EXPERTS * (SUBLANES - 1))
    n_blocks = -(-sorted_total // MOE_ROWS) + N_EXPERTS
    tables, used = _block_tables(jnp.concatenate([tbl_p, tbl_s], axis=0), n_blocks)
    ys_all = _moe(xs_all, tables, w_gate_up[0], b_gate_up[0], w_down[0], b_down[0])

    gf = g_final[None]
    y_prompt = _final(h_p, ys_all, q_p, gate_p, used, gf, tile0=0, slabs=1)
    y_sample = _final(h_s, ys_all, q_s, gate_s, used, gf, tile0=tiles_p, slabs=nt_s)
    return (y_prompt.reshape(nb_p, nt_p, D_MODEL), y_sample.reshape(nb_s, nt_s, D_MODEL),
            pool_p, conv_p,
            pool_s.reshape(1, nb_s, POOL_BUF, POOL_CH), conv_s.reshape(1, nb_s, CONV_BUF, CONV_CH))
```

```python
import functools

import jax
import jax.numpy as jnp
from jax import lax
from jax.experimental import pallas as pl
from jax.experimental.pallas import tpu as pltpu

D_MODEL = 1024
POOL_CH = 512
CONV_CH = 512
POOL_WINDOWS = (2, 4, 8, 16)
POOL_GROUP = 128
POOL_BUF = 15
CONV_TAPS = 31
CONV_BUF = 30
IN_COLS = POOL_CH + 2 * CONV_CH
N_EXPERTS = 32
TOP_K = 4
D_FF = 1024
SWIGLU_LIMIT = 7.0
SWIGLU_ALPHA = 1.702
RMS_EPS = 1e-5
LN_EPS = 1e-5

LANES = 128
SUBLANES = 8
VMEM_LIMIT_BYTES = 60 * 1024 * 1024

TILE = 512
SAMPLE_SEQS = 64
CONV_CHUNK = 32
SORT_CHUNK = 256
FRONT_SORT_CHUNK = 768
SORTED_ROWS = -(-(TILE * TOP_K + N_EXPERTS * (SUBLANES - 1)) // SORT_CHUNK) * SORT_CHUNK
MOE_ROWS = 512
HALF_ROWS = MOE_ROWS // 2
MOE_GROUPS = MOE_ROWS // SUBLANES
HALF_GROUPS = MOE_GROUPS // 2
WEIGHT_DMA_PRIORITY = 1

POOL_HIST = 24
CONV_HIST = 32

BF16 = jnp.bfloat16
F32 = jnp.float32
I32 = jnp.int32
U32 = jnp.uint32
PACKED = D_MODEL // 2


def _rmsnorm(x, g):
    ms = jnp.mean(x * x, axis=-1, keepdims=True)
    return x * lax.rsqrt(ms + RMS_EPS) * g


def _sigmoid(x):
    return 1.0 / (1.0 + jnp.exp(-x))


def _layernorm_silu(d, g, b):
    mu = jnp.mean(d, axis=-1, keepdims=True)
    c = d - mu
    var = jnp.mean(c * c, axis=-1, keepdims=True)
    n = c * lax.rsqrt(var + LN_EPS) * g + b
    return n * _sigmoid(n)


def _pack_halves(a):
    half = a.shape[1] // 2
    hi = lax.bitcast_convert_type(a[:, :half], U32)
    lo = lax.bitcast_convert_type(a[:, half:], U32)
    return hi | (lo >> 16)


def _unpack_halves(w):
    hi = lax.bitcast_convert_type(w & jnp.uint32(0xFFFF0000), F32)
    lo = lax.bitcast_convert_type(w << 16, F32)
    return jnp.concatenate([hi, lo], axis=1)


def _route_and_sort(hn_bf, wr_t_ref, br_ref, tri_ref, q_ref, gate_ref, tbl_ref, xs_ref):
    rows = hn_bf.shape[0]
    lt = lax.dot_general(wr_t_ref[...], hn_bf, (((1,), (1,)), ((), ())),
                         preferred_element_type=F32)
    lt = lt + br_ref[...]
    eidx = lax.broadcasted_iota(I32, (N_EXPERTS, rows), 0)
    vals, hots = [], []
    for _ in range(TOP_K):
        m = jnp.max(lt, axis=0, keepdims=True)
        sel = jnp.min(jnp.where(lt == m, eidx, N_EXPERTS), axis=0, keepdims=True)
        hit = eidx == sel
        vals.append(m)
        hots.append(hit)
        lt = jnp.where(hit, -jnp.inf, lt)
    v = jnp.concatenate(vals, axis=0)
    e = jnp.exp(v - vals[0])
    gate = e / jnp.sum(e, axis=0, keepdims=True)
    gate_ref[...] = jnp.concatenate([gate, jnp.zeros((SUBLANES - TOP_K, rows), F32)], axis=0)

    sel_cnt = sum(h.astype(F32) for h in hots)
    prefix = jnp.dot(sel_cnt.astype(BF16), tri_ref[...], preferred_element_type=F32)
    cnt = jnp.sum(sel_cnt, axis=1, keepdims=True)
    cnt_u = jnp.floor((cnt + (SUBLANES - 1)) * (1.0 / SUBLANES))
    lower = (lax.broadcasted_iota(I32, (N_EXPERTS, N_EXPERTS), 1)
             < lax.broadcasted_iota(I32, (N_EXPERTS, N_EXPERTS), 0)).astype(BF16)
    cnt_u_l = jnp.broadcast_to(cnt_u, (N_EXPERTS, LANES))
    off_u_l = jnp.dot(lower, cnt_u_l.astype(BF16), preferred_element_type=F32)
    off_l = off_u_l * float(SUBLANES)
    tbl_ref[0, 0] = (cnt_u_l * float(SUBLANES)).astype(I32)
    tbl_ref[0, 1] = off_l.astype(I32)

    pos = prefix + off_l[:, :1]
    qs = [jnp.sum(jnp.where(h, pos, 0.0), axis=0, keepdims=True).astype(I32) for h in hots]
    q_ref[...] = jnp.concatenate(qs + [jnp.zeros((SUBLANES - TOP_K, rows), I32)], axis=0)

    def sort_chunk(c, carry):
        r0 = pl.multiple_of(c * FRONT_SORT_CHUNK, FRONT_SORT_CHUNK)
        jidx = r0 + lax.broadcasted_iota(I32, (FRONT_SORT_CHUNK, rows), 0)
        p = sum(jnp.where(jidx == qk, 1.0, 0.0) for qk in qs)
        xs = jnp.dot(p.astype(BF16), hn_bf, preferred_element_type=F32)
        xs_ref[pl.ds(r0, FRONT_SORT_CHUNK), :] = _pack_halves(xs)
        return carry

    lax.fori_loop(0, SORTED_ROWS // FRONT_SORT_CHUNK, sort_chunk, 0)


def _mix_out_and_route(x, mixbuf, w_out_ref, g_moe_ref, wr_t_ref, br_ref, tri_ref,
                       h_ref, q_ref, gate_ref, tbl_ref, xs_ref):
    h = x + jnp.dot(mixbuf[...], w_out_ref[...], preferred_element_type=F32)
    h_ref[...] = h
    hn = _rmsnorm(h, g_moe_ref[...])
    _route_and_sort(hn.astype(BF16), wr_t_ref, br_ref, tri_ref, q_ref, gate_ref, tbl_ref, xs_ref)


def _front_prompt_kernel(x_ref, sp_ref, sc_ref, g_mix_ref, w_in_ref, pool_w_ref, pool_scale_ref,
                         conv_w_ref, conv_b_ref, ln_g_ref, ln_b_ref, w_out_ref, g_moe_ref,
                         wr_t_ref, br_ref, tri_ref,
                         h_ref, q_ref, gate_ref, tbl_ref, xs_ref, np_ref, nc_ref,
                         ubuf, gsh, dbuf, mixbuf, *, pos0, rows):
    t = pl.program_id(1)

    @pl.when(t == 0)
    def _():
        ubuf[:, pl.ds(0, POOL_HIST), :] = jnp.zeros((len(POOL_WINDOWS), POOL_HIST, POOL_CH), F32)
        ubuf[0, pl.ds(POOL_HIST - POOL_BUF, POOL_BUF), :] = sp_ref[0, 0]
        gsh[0, pl.ds(CONV_HIST - CONV_BUF, CONV_BUF), :] = sc_ref[0, 0]

    x = x_ref[0]
    xn = _rmsnorm(x, g_mix_ref[...]).astype(BF16)
    proj = jnp.dot(xn, w_in_ref[...], preferred_element_type=F32)

    u = proj[:, :POOL_CH]
    ubuf[0, pl.ds(POOL_HIST, rows), :] = u
    pos = pos0 + t * rows + lax.broadcasted_iota(I32, (rows, 1), 0)
    span = rows + POOL_HIST - SUBLANES
    for g, w in enumerate(POOL_WINDOWS):
        assert w == 2 ** (g + 1) and w // 2 <= SUBLANES
        c0 = g * POOL_GROUP
        cs = slice(c0, c0 + POOL_GROUP)
        a = ubuf[g, pl.ds(SUBLANES, span), c0:] + ubuf[g, pl.ds(SUBLANES - w // 2, span), c0:]
        if g + 1 < len(POOL_WINDOWS):
            ubuf[g + 1, pl.ds(SUBLANES, span), c0 + POOL_GROUP:] = a[:, POOL_GROUP:]
        s = a[span - rows:, :POOL_GROUP]
        cnt = jnp.minimum(w, pos + 1).astype(F32)
        z = (s / cnt - u[:, cs]).astype(BF16)
        yg = jnp.dot(z, pool_w_ref[g], preferred_element_type=F32)
        mixbuf[:, cs] = (yg * pool_scale_ref[:, cs]).astype(BF16)

    glu = proj[:, POOL_CH:POOL_CH + CONV_CH] * _sigmoid(proj[:, POOL_CH + CONV_CH:])
    gsh[0, pl.ds(CONV_HIST, rows), :] = glu
    for s in range(1, SUBLANES):
        gsh[s, pl.ds(0, CONV_HIST - s), :] = gsh[0, pl.ds(s, CONV_HIST - s), :]
        gsh[s, pl.ds(CONV_HIST - s, rows), :] = glu

    def conv_chunk(i, carry):
        r0 = pl.multiple_of(i * CONV_CHUNK, CONV_CHUNK)
        for ct in range(CONV_CH // LANES):
            cs = slice(ct * LANES, (ct + 1) * LANES)
            acc = jnp.broadcast_to(conv_b_ref[:, cs], (CONV_CHUNK, LANES))
            for k in range(CONV_TAPS):
                first = CONV_HIST - CONV_BUF + k
                s = first % SUBLANES
                acc = acc + conv_w_ref[k:k + 1, cs] * gsh[s, pl.ds(r0 + (first - s), CONV_CHUNK), cs]
            dbuf[pl.ds(r0, CONV_CHUNK), cs] = acc
        return carry

    lax.fori_loop(0, rows // CONV_CHUNK, conv_chunk, 0)
    y_conv = _layernorm_silu(dbuf[...], ln_g_ref[...], ln_b_ref[...])
    mixbuf[:, POOL_CH:] = y_conv.astype(BF16)

    new_pool = ubuf[0, pl.ds(rows + POOL_HIST - POOL_BUF, POOL_BUF), :]
    new_conv = gsh[0, pl.ds(rows + CONV_HIST - CONV_BUF, CONV_BUF), :]
    np_ref[0, 0] = new_pool
    nc_ref[0, 0] = new_conv
    ubuf[0, pl.ds(POOL_HIST - POOL_BUF, POOL_BUF), :] = new_pool
    gsh[0, pl.ds(CONV_HIST - CONV_BUF, CONV_BUF), :] = new_conv

    _mix_out_and_route(x, mixbuf, w_out_ref, g_moe_ref, wr_t_ref, br_ref, tri_ref,
                       h_ref, q_ref, gate_ref, tbl_ref, xs_ref)


def _front_sample_kernel(x_ref, sp_ref, sc_ref, g_mix_ref, w_in_ref, pool_w_ref, pool_scale_ref,
                         conv_w_ref, conv_b_ref, ln_g_ref, ln_b_ref, w_out_ref, g_moe_ref,
                         wr_t_ref, br_ref, tri_ref, xs_in,
                         h_ref, q_ref, gate_ref, tbl_ref, xs_ref, np_ref, nc_ref,
                         xbuf, dbuf, mixbuf, *, pos0, seqs, steps):
    del xs_in
    for t in range(steps):
        xbuf[pl.ds(t * seqs, seqs), :] = x_ref[:, t * D_MODEL:(t + 1) * D_MODEL]
    x = xbuf[...]
    xn = _rmsnorm(x, g_mix_ref[...]).astype(BF16)
    proj = jnp.dot(xn, w_in_ref[...], preferred_element_type=F32)

    def slab(a, t):
        return a[t * seqs:(t + 1) * seqs]

    u = proj[:, :POOL_CH]
    cat = [sp_ref[:, j * POOL_CH:(j + 1) * POOL_CH] for j in range(POOL_BUF)]
    cat += [slab(u, t) for t in range(steps)]
    for t in range(steps):
        for g, w in enumerate(POOL_WINDOWS):
            cs = slice(g * POOL_GROUP, (g + 1) * POOL_GROUP)
            s = cat[POOL_BUF + t][:, cs]
            for j in range(1, w):
                s = s + cat[POOL_BUF + t - j][:, cs]
            cnt = float(min(w, pos0 + t + 1))
            z = (s / cnt - cat[POOL_BUF + t][:, cs]).astype(BF16)
            yg = jnp.dot(z, pool_w_ref[g], preferred_element_type=F32)
            mixbuf[pl.ds(t * seqs, seqs), cs] = (yg * pool_scale_ref[:, cs]).astype(BF16)
    for j in range(POOL_BUF):
        np_ref[:, j * POOL_CH:(j + 1) * POOL_CH] = cat[len(cat) - POOL_BUF + j]

    glu = proj[:, POOL_CH:POOL_CH + CONV_CH] * _sigmoid(proj[:, POOL_CH + CONV_CH:])
    ccat = [sc_ref[:, j * CONV_CH:(j + 1) * CONV_CH] for j in range(CONV_BUF)]
    ccat += [slab(glu, t) for t in range(steps)]
    for t in range(steps):
        acc = jnp.broadcast_to(conv_b_ref[...], (seqs, CONV_CH))
        for k in range(CONV_TAPS):
            acc = acc + conv_w_ref[k:k + 1, :] * ccat[t + k]
        dbuf[pl.ds(t * seqs, seqs), :] = acc
    for j in range(CONV_BUF):
        nc_ref[:, j * CONV_CH:(j + 1) * CONV_CH] = ccat[len(ccat) - CONV_BUF + j]
    y_conv = _layernorm_silu(dbuf[...], ln_g_ref[...], ln_b_ref[...])
    mixbuf[:, POOL_CH:] = y_conv.astype(BF16)

    _mix_out_and_route(x, mixbuf, w_out_ref, g_moe_ref, wr_t_ref, br_ref, tri_ref,
                       h_ref, q_ref, gate_ref, tbl_ref, xs_ref)


def _const_spec(shape, single_buffer=False):
    mode = pl.Buffered(1) if single_buffer else None
    return pl.BlockSpec(shape, lambda *_: (0,) * len(shape), pipeline_mode=mode)


def _front_weight_specs():
    return [
        _const_spec((1, D_MODEL)),
        _const_spec((D_MODEL, IN_COLS), True),
        _const_spec((len(POOL_WINDOWS), POOL_GROUP, POOL_GROUP)),
        _const_spec((1, POOL_CH)),
        _const_spec((CONV_TAPS, CONV_CH)),
        _const_spec((1, CONV_CH)),
        _const_spec((1, CONV_CH)),
        _const_spec((1, CONV_CH)),
        _const_spec((D_MODEL, D_MODEL), True),
        _const_spec((1, D_MODEL)),
        _const_spec((N_EXPERTS, D_MODEL)),
        _const_spec((N_EXPERTS, 1)),
        _const_spec((TILE, TILE), True),
    ]


def _front_out(n_tok, n_tiles_all, tile_map, tok_map, extra_specs, extra_shapes):
    specs = [pl.BlockSpec((TILE, D_MODEL), tok_map),
             pl.BlockSpec((SUBLANES, TILE), lambda *i: (0, tok_map(*i)[0])),
             pl.BlockSpec((SUBLANES, TILE), lambda *i: (0, tok_map(*i)[0])),
             pl.BlockSpec((1, 2, N_EXPERTS, LANES), lambda *i: (tok_map(*i)[0], 0, 0, 0)),
             pl.BlockSpec((SORTED_ROWS, PACKED), tile_map)]
    shapes = [jax.ShapeDtypeStruct((n_tok, D_MODEL), F32),
              jax.ShapeDtypeStruct((SUBLANES, n_tok), I32),
              jax.ShapeDtypeStruct((SUBLANES, n_tok), F32),
              jax.ShapeDtypeStruct((n_tok // TILE, 2, N_EXPERTS, LANES), I32),
              jax.ShapeDtypeStruct((n_tiles_all * SORTED_ROWS, PACKED), U32)]
    return specs + extra_specs, shapes + extra_shapes


def _front_prompt(x, state_pool, state_conv, weights, *, pos0, n_tiles_all):
    nb, nt, _ = x.shape
    rows = TILE
    tiles = nt // rows
    tok_map = lambda b, t: (b * tiles + t, 0)
    out_specs, out_shape = _front_out(
        nb * nt, n_tiles_all, tok_map, tok_map,
        [pl.BlockSpec((1, 1, POOL_BUF, POOL_CH), lambda b, t: (0, b, 0, 0)),
         pl.BlockSpec((1, 1, CONV_BUF, CONV_CH), lambda b, t: (0, b, 0, 0))],
        [jax.ShapeDtypeStruct((1, nb, POOL_BUF, POOL_CH), F32),
         jax.ShapeDtypeStruct((1, nb, CONV_BUF, CONV_CH), F32)])
    kern = functools.partial(_front_prompt_kernel, pos0=pos0, rows=rows)
    return pl.pallas_call(
        kern,
        grid=(nb, tiles),
        in_specs=[pl.BlockSpec((1, rows, D_MODEL), lambda b, t: (b, t, 0)),
                  pl.BlockSpec((1, 1, POOL_BUF, POOL_CH), lambda b, t: (0, b, 0, 0)),
                  pl.BlockSpec((1, 1, CONV_BUF, CONV_CH), lambda b, t: (0, b, 0, 0))]
                 + _front_weight_specs(),
        out_specs=out_specs,
        out_shape=out_shape,
        scratch_shapes=[pltpu.VMEM((len(POOL_WINDOWS), POOL_HIST + rows, POOL_CH), F32),
                        pltpu.VMEM((SUBLANES, CONV_HIST + rows, CONV_CH), F32),
                        pltpu.VMEM((rows, CONV_CH), F32),
                        pltpu.VMEM((rows, D_MODEL), BF16)],
        compiler_params=pltpu.CompilerParams(
            dimension_semantics=("arbitrary", "arbitrary"), vmem_limit_bytes=VMEM_LIMIT_BYTES),
        name="front_prompt",
    )(x, state_pool, state_conv, *weights)


def _front_sample(x, state_pool, state_conv, weights, xs_all, *, pos0, tile0):
    nb, nt, _ = x.shape
    seqs = TILE // nt
    n_tiles_all = xs_all.shape[0] // SORTED_ROWS
    tok_map = lambda i: (i, 0)
    out_specs, out_shape = _front_out(
        nb * nt, n_tiles_all, lambda i: (tile0 + i, 0), tok_map,
        [pl.BlockSpec((seqs, POOL_BUF * POOL_CH), lambda i: (i, 0)),
         pl.BlockSpec((seqs, CONV_BUF * CONV_CH), lambda i: (i, 0))],
        [jax.ShapeDtypeStruct((nb, POOL_BUF * POOL_CH), F32),
         jax.ShapeDtypeStruct((nb, CONV_BUF * CONV_CH), F32)])
    kern = functools.partial(_front_sample_kernel, pos0=pos0, seqs=seqs, steps=nt)
    n_in = 3 + len(weights)
    return pl.pallas_call(
        kern,
        grid=(nb // seqs,),
        in_specs=[pl.BlockSpec((seqs, nt * D_MODEL), lambda i: (i, 0)),
                  pl.BlockSpec((seqs, POOL_BUF * POOL_CH), lambda i: (i, 0)),
                  pl.BlockSpec((seqs, CONV_BUF * CONV_CH), lambda i: (i, 0))]
                 + _front_weight_specs() + [pl.BlockSpec(memory_space=pl.ANY)],
        out_specs=out_specs,
        out_shape=out_shape,
        scratch_shapes=[pltpu.VMEM((TILE, D_MODEL), F32),
                        pltpu.VMEM((TILE, CONV_CH), F32),
                        pltpu.VMEM((TILE, D_MODEL), BF16)],
        input_output_aliases={n_in: 4},
        compiler_params=pltpu.CompilerParams(
            dimension_semantics=("arbitrary",), vmem_limit_bytes=VMEM_LIMIT_BYTES),
        name="front_sample",
    )(x.reshape(nb, nt * D_MODEL), state_pool.reshape(nb, POOL_BUF * POOL_CH),
      state_conv.reshape(nb, CONV_BUF * CONV_CH), *weights, xs_all)


def _moe_kernel(be_ref, nused_ref, src_ref, dst_ref, first_ref, wslot_ref, nxt_ref,
                xs_hbm, wgu_hbm, bgu_ref, wd_hbm, bd_ref, ys_hbm,
                xa, xb, oa, ob, wgu_f32, wd_f32, wgu_bf, wd_bf,
                gsem_a, gsem_b, ssem_a, ssem_b, wsem, *, dummy_half):
    i = pl.program_id(0)
    n_used = nused_ref[0]
    half_a = 2 * i
    half_b = 2 * i + 1

    def weight_copies(e, s):
        return (pltpu.make_async_copy(wgu_hbm.at[e], wgu_f32.at[s], wsem.at[s]),
                pltpu.make_async_copy(wd_hbm.at[e], wd_f32.at[s], wsem.at[s]))

    def gather(half, buf, sem):
        for g in range(HALF_GROUPS):
            hrow = pl.multiple_of(src_ref[half * HALF_GROUPS + g], SUBLANES)
            pltpu.make_async_copy(xs_hbm.at[pl.ds(hrow, SUBLANES)],
                                  buf.at[pl.ds(g * SUBLANES, SUBLANES)], sem).start()

    def gather_wait(buf, sem):
        pltpu.make_async_copy(xs_hbm.at[pl.ds(0, HALF_ROWS)], buf, sem).wait()

    def scatter(half, buf, sem):
        for g in range(HALF_GROUPS):
            hrow = pl.multiple_of(dst_ref[half * HALF_GROUPS + g], SUBLANES)
            pltpu.make_async_copy(buf.at[pl.ds(g * SUBLANES, SUBLANES)],
                                  ys_hbm.at[pl.ds(hrow, SUBLANES)], sem).start()

    def scatter_wait(buf, sem):
        pltpu.make_async_copy(buf, ys_hbm.at[pl.ds(0, HALF_ROWS)], sem).wait()

    def experts(xbuf, obuf):
        x = _unpack_halves(xbuf[...]).astype(BF16)
        gu = jnp.dot(x, wgu_bf[...], preferred_element_type=F32) + bgu_ref[0]
        g = jnp.minimum(gu[:, :D_FF], SWIGLU_LIMIT)
        up = jnp.clip(gu[:, D_FF:], -SWIGLU_LIMIT, SWIGLU_LIMIT)
        hdn = (up + 1.0) * (g * _sigmoid(SWIGLU_ALPHA * g))
        y = jnp.dot(hdn.astype(BF16), wd_bf[...], preferred_element_type=F32) + bd_ref[0]
        obuf[...] = _pack_halves(y.astype(BF16).astype(F32))

    @pl.when(i == 0)
    def _():
        oa[...] = jnp.zeros_like(oa)
        ob[...] = jnp.zeros_like(ob)
        scatter(dummy_half, oa, ssem_a)
        gather(0, xa, gsem_a)
        for c in weight_copies(be_ref[0], 0):
            c.start(priority=WEIGHT_DMA_PRIORITY)

    @pl.when(i < n_used)
    def _():
        @pl.when(first_ref[i] == 1)
        def _():
            s = wslot_ref[i]
            for c in weight_copies(be_ref[i], s):
                c.wait()
            wgu_bf[...] = wgu_f32[s].astype(BF16)
            wd_bf[...] = wd_f32[s].astype(BF16)

            @pl.when(nxt_ref[i] < N_EXPERTS)
            def _():
                for c in weight_copies(nxt_ref[i], 1 - s):
                    c.start(priority=WEIGHT_DMA_PRIORITY)

        prev_b = jnp.where(i == 0, dummy_half, half_b - 2)
        next_a = 2 * jnp.minimum(i + 1, n_used - 1)
        scatter_wait(oa, ssem_a)
        scatter(prev_b, ob, ssem_b)
        gather(half_b, xb, gsem_b)
        gather_wait(xa, gsem_a)
        experts(xa, oa)
        scatter(half_a, oa, ssem_a)
        gather(next_a, xa, gsem_a)
        scatter_wait(ob, ssem_b)
        gather_wait(xb, gsem_b)
        experts(xb, ob)

        @pl.when(i == n_used - 1)
        def _():
            scatter(half_b, ob, ssem_b)
            scatter_wait(oa, ssem_a)
            scatter_wait(ob, ssem_b)
            gather_wait(xa, gsem_a)


def _moe(xs_all, tables, w_gate_up, b_gate_up, w_down, b_down):
    block_e, n_used, src, dst, first, wslot, nxt = tables
    n_blocks = block_e.shape[0]

    def e_map(i, be, nu, *_):
        return (be[jnp.minimum(i, nu[0] - 1)], 0, 0)

    half_buf = pltpu.VMEM((HALF_ROWS, PACKED), U32)
    any_spec = pl.BlockSpec(memory_space=pl.ANY)
    grid_spec = pltpu.PrefetchScalarGridSpec(
        num_scalar_prefetch=7,
        grid=(n_blocks,),
        in_specs=[any_spec,
                  any_spec,
                  pl.BlockSpec((1, 1, 2 * D_FF), e_map),
                  any_spec,
                  pl.BlockSpec((1, 1, D_MODEL), e_map)],
        out_specs=any_spec,
        scratch_shapes=[half_buf, half_buf, half_buf, half_buf,
                        pltpu.VMEM((2, D_MODEL, 2 * D_FF), F32),
                        pltpu.VMEM((2, D_FF, D_MODEL), F32),
                        pltpu.VMEM((D_MODEL, 2 * D_FF), BF16),
                        pltpu.VMEM((D_FF, D_MODEL), BF16)]
                       + [pltpu.SemaphoreType.DMA(())] * 4 + [pltpu.SemaphoreType.DMA((2,))],
    )
    return pl.pallas_call(
        functools.partial(_moe_kernel, dummy_half=2 * n_blocks),
        grid_spec=grid_spec,
        out_shape=jax.ShapeDtypeStruct((xs_all.shape[0] + MOE_ROWS, PACKED), U32),
        compiler_params=pltpu.CompilerParams(
            dimension_semantics=("arbitrary",), vmem_limit_bytes=VMEM_LIMIT_BYTES),
        name="moe_experts",
    )(block_e, n_used, src, dst, first, wslot, nxt,
      xs_all, w_gate_up, b_gate_up.reshape(N_EXPERTS, 1, 2 * D_FF), w_down,
      b_down.reshape(N_EXPERTS, 1, D_MODEL))


def _final_kernel(used_ref, h_ref, ys_ref, q_ref, gate_ref, g_ref, o_ref, *, tile0, slabs):
    i = pl.program_id(0)
    used = used_ref[tile0 + i]

    def to_cols(a):
        pad = jnp.zeros((LANES - SUBLANES, TILE), F32)
        return jnp.concatenate([a, pad], axis=0).T

    qc = to_cols(q_ref[...].astype(F32))
    gc = to_cols(gate_ref[...])
    acc = h_ref[...]
    for c in range(SORTED_ROWS // SORT_CHUNK):
        jl = (c * SORT_CHUNK + lax.broadcasted_iota(I32, (TILE, SORT_CHUNK), 1)).astype(F32)
        pg = sum(jnp.where(qc[:, k:k + 1] == jl, gc[:, k:k + 1], 0.0) for k in range(TOP_K))
        rid = c * SORT_CHUNK + lax.broadcasted_iota(I32, (SORT_CHUNK, 1), 0)
        ys = jnp.where(rid < used, _unpack_halves(ys_ref[pl.ds(c * SORT_CHUNK, SORT_CHUNK), :]), 0.0)
        acc = acc + jnp.dot(pg.astype(BF16), ys.astype(BF16), preferred_element_type=F32)
    y = _rmsnorm(acc, g_ref[...])
    if slabs == 1:
        o_ref[...] = y
    else:
        seqs = TILE // slabs
        for t in range(slabs):
            o_ref[:, t * D_MODEL:(t + 1) * D_MODEL] = y[t * seqs:(t + 1) * seqs]


def _final(h, ys_all, q, gates, used, g_final, *, tile0, slabs):
    n_tok = h.shape[0]
    if slabs == 1:
        out_spec = pl.BlockSpec((TILE, D_MODEL), lambda i, u: (i, 0))
        out_shape = jax.ShapeDtypeStruct((n_tok, D_MODEL), F32)
    else:
        seqs = TILE // slabs
        out_spec = pl.BlockSpec((seqs, slabs * D_MODEL), lambda i, u: (i, 0))
        out_shape = jax.ShapeDtypeStruct((n_tok // slabs, slabs * D_MODEL), F32)
    grid_spec = pltpu.PrefetchScalarGridSpec(
        num_scalar_prefetch=1,
        grid=(n_tok // TILE,),
        in_specs=[pl.BlockSpec((TILE, D_MODEL), lambda i, u: (i, 0)),
                  pl.BlockSpec((SORTED_ROWS, PACKED), lambda i, u: (tile0 + i, 0)),
                  pl.BlockSpec((SUBLANES, TILE), lambda i, u: (0, i)),
                  pl.BlockSpec((SUBLANES, TILE), lambda i, u: (0, i)),
                  pl.BlockSpec((1, D_MODEL), lambda i, u: (0, 0))],
        out_specs=out_spec,
    )
    return pl.pallas_call(
        functools.partial(_final_kernel, tile0=tile0, slabs=slabs),
        grid_spec=grid_spec,
        out_shape=out_shape,
        compiler_params=pltpu.CompilerParams(
            dimension_semantics=("arbitrary",), vmem_limit_bytes=VMEM_LIMIT_BYTES),
        name="final_norm",
    )(used, h, ys_all, q, gates, g_final)


def _block_tables(tbl, n_blocks):
    cnt = tbl[:, 0, :, 0]
    off = tbl[:, 1, :, 0]
    run = jnp.cumsum(cnt, axis=0) - cnt
    total = jnp.sum(cnt, axis=0)
    nblk = (total + MOE_ROWS - 1) // MOE_ROWS
    blk_end = jnp.cumsum(nblk)
    blk_start = blk_end - nblk
    bidx = jnp.arange(n_blocks, dtype=I32)
    eidx = jnp.arange(N_EXPERTS, dtype=I32)
    block_e = jnp.minimum(jnp.sum(blk_end[None, :] <= bidx[:, None], axis=1), N_EXPERTS - 1).astype(I32)
    hot = block_e[:, None] == eidx[None, :]

    def per_block(v):
        return jnp.sum(jnp.where(hot, v[..., None, :], 0), axis=-1)

    row0 = (bidx - per_block(blk_start)) * MOE_ROWS
    n_tiles = cnt.shape[0]
    groups = jnp.arange(MOE_GROUPS, dtype=I32)
    erow = row0[:, None] + groups[None, :] * SUBLANES
    run_b, cnt_b, off_b = per_block(run), per_block(cnt), per_block(off)
    inside = jnp.logical_and(run_b[:, :, None] <= erow[None], erow[None] < (run_b + cnt_b)[:, :, None])
    tile_base = jnp.arange(n_tiles, dtype=I32)[:, None] * SORTED_ROWS + off_b - run_b
    src = jnp.sum(jnp.where(inside, tile_base[:, :, None] + erow[None], 0), axis=0)
    dump = n_tiles * SORTED_ROWS + groups * SUBLANES
    dst = jnp.where(erow < per_block(total)[:, None], src, dump[None, :])
    src = jnp.concatenate([src.reshape(-1), jnp.zeros((HALF_GROUPS,), I32)])
    dst = jnp.concatenate([dst.reshape(-1), dump[:HALF_GROUPS]])
    used = jnp.sum(cnt, axis=1)
    has_rows = nblk > 0
    first = (bidx == per_block(blk_start)).astype(I32)
    wslot = per_block((jnp.cumsum(has_rows) - 1) % 2)
    later = jnp.where(jnp.logical_and(has_rows[None, :], eidx[None, :] > eidx[:, None]), eidx[None, :], N_EXPERTS)
    nxt = per_block(jnp.min(later, axis=1))
    return ((block_e, blk_end[-1:].astype(I32), src.astype(I32), dst.astype(I32), first,
             wslot.astype(I32), nxt.astype(I32)), used.astype(I32))


def kernel(x_prompt, x_sample, state_pool, state_conv, g_mix, w_in, pool_w, pool_scale, conv_w, conv_b,
           conv_ln_g, conv_ln_b, w_out, g_moe, w_router, b_router, w_gate_up, b_gate_up, w_down, b_down,
           g_final):
    nb_p, nt_p, _ = x_prompt.shape
    nb_s, nt_s, _ = x_sample.shape
    n_p, n_s = nb_p * nt_p, nb_s * nt_s
    past_len = 16384
    assert nt_p % TILE == 0 and TILE % nt_s == 0 and n_s % TILE == 0
    tiles_p, tiles_s = n_p // TILE, n_s // TILE
    n_tiles = tiles_p + tiles_s

    tri = (lax.broadcasted_iota(I32, (TILE, TILE), 0) < lax.broadcasted_iota(I32, (TILE, TILE), 1)).astype(BF16)
    weights = (g_mix[0][None], w_in[0].astype(BF16), pool_w[0].astype(BF16), pool_scale[0][None],
               conv_w[0], conv_b[0][None], conv_ln_g[0][None], conv_ln_b[0][None],
               w_out[0].astype(BF16), g_moe[0][None], w_router[0].T.astype(BF16),
               b_router[0][:, None], tri)

    zero_pool = jnp.zeros((1, nb_p, POOL_BUF, POOL_CH), F32)
    zero_conv = jnp.zeros((1, nb_p, CONV_BUF, CONV_CH), F32)
    h_p, q_p, gate_p, tbl_p, xs_all, pool_p, conv_p = _front_prompt(
        x_prompt, zero_pool, zero_conv, weights, pos0=0, n_tiles_all=n_tiles)
    h_s, q_s, gate_s, tbl_s, xs_all, pool_s, conv_s = _front_sample(
        x_sample, state_pool[0], state_conv[0], weights, xs_all, pos0=past_len, tile0=tiles_p)

    sorted_total = n_tiles * (TILE * TOP_K + N_EXPERTS * (SUBLANES - 1))
    n_blocks = -(-sorted_total // MOE_ROWS) + N_EXPERTS
    tables, used = _block_tables(jnp.concatenate([tbl_p, tbl_s], axis=0), n_blocks)
    ys_all = _moe(xs_all, tables, w_gate_up[0], b_gate_up[0], w_down[0], b_down[0])

    gf = g_final[None]
    y_prompt = _final(h_p, ys_all, q_p, gate_p, used, gf, tile0=0, slabs=1)
    y_sample = _final(h_s, ys_all, q_s, gate_s, used, gf, tile0=tiles_p, slabs=nt_s)
    return (y_prompt.reshape(nb_p, nt_p, D_MODEL), y_sample.reshape(nb_s, nt_s, D_MODEL),
            pool_p, conv_p,
            pool_s.reshape(1, nb_s, POOL_BUF, POOL_CH), conv_s.reshape(1, nb_s, CONV_BUF, CONV_CH))
```

```python
import functools

import jax
import jax.numpy as jnp
from jax import lax
from jax.experimental import pallas as pl
from jax.experimental.pallas import tpu as pltpu

D_MODEL = 1024
POOL_CH = 512
CONV_CH = 512
POOL_WINDOWS = (2, 4, 8, 16)
POOL_GROUP = 128
POOL_BUF = 15
CONV_TAPS = 31
CONV_BUF = 30
IN_COLS = POOL_CH + 2 * CONV_CH
N_EXPERTS = 32
TOP_K = 4
D_FF = 1024
SWIGLU_LIMIT = 7.0
SWIGLU_ALPHA = 1.702
RMS_EPS = 1e-5
LN_EPS = 1e-5

LANES = 128
SUBLANES = 8
VMEM_LIMIT_BYTES = 60 * 1024 * 1024

TILE = 512
SAMPLE_SEQS = 64
CONV_CHUNK = 32
SORT_CHUNK = 256
FRONT_SORT_CHUNK = 768
SORTED_ROWS = -(-(TILE * TOP_K + N_EXPERTS * (SUBLANES - 1)) // SORT_CHUNK) * SORT_CHUNK
MOE_ROWS = 512
HALF_ROWS = MOE_ROWS // 2
MOE_GROUPS = MOE_ROWS // SUBLANES
HALF_GROUPS = MOE_GROUPS // 2
WEIGHT_DMA_PRIORITY = 1

POOL_HIST = 24
CONV_HIST = 32

BF16 = jnp.bfloat16
F32 = jnp.float32
I32 = jnp.int32
U32 = jnp.uint32
PACKED = D_MODEL // 2


def _rmsnorm(x, g):
    ms = jnp.mean(x * x, axis=-1, keepdims=True)
    return x * lax.rsqrt(ms + RMS_EPS) * g


def _sigmoid(x):
    return 1.0 / (1.0 + jnp.exp(-x))


def _layernorm_silu(d, g, b):
    mu = jnp.mean(d, axis=-1, keepdims=True)
    c = d - mu
    var = jnp.mean(c * c, axis=-1, keepdims=True)
    n = c * lax.rsqrt(var + LN_EPS) * g + b
    return n * _sigmoid(n)


def _pack_halves(a):
    half = a.shape[1] // 2
    hi = lax.bitcast_convert_type(a[:, :half], U32)
    lo = lax.bitcast_convert_type(a[:, half:], U32)
    return hi | (lo >> 16)


def _unpack_halves(w):
    hi = lax.bitcast_convert_type(w & jnp.uint32(0xFFFF0000), F32)
    lo = lax.bitcast_convert_type(w << 16, F32)
    return jnp.concatenate([hi, lo], axis=1)


def _route_and_sort(hn_bf, wr_t_ref, br_ref, tri_ref, q_ref, gate_ref, tbl_ref, xs_ref):
    rows = hn_bf.shape[0]
    lt = lax.dot_general(wr_t_ref[...], hn_bf, (((1,), (1,)), ((), ())),
                         preferred_element_type=F32)
    lt = lt + br_ref[...]
    eidx = lax.broadcasted_iota(I32, (N_EXPERTS, rows), 0)
    vals, hots = [], []
    for _ in range(TOP_K):
        m = jnp.max(lt, axis=0, keepdims=True)
        sel = jnp.min(jnp.where(lt == m, eidx, N_EXPERTS), axis=0, keepdims=True)
        hit = eidx == sel
        vals.append(m)
        hots.append(hit)
        lt = jnp.where(hit, -jnp.inf, lt)
    v = jnp.concatenate(vals, axis=0)
    e = jnp.exp(v - vals[0])
    gate = e / jnp.sum(e, axis=0, keepdims=True)
    gate_ref[...] = jnp.concatenate([gate, jnp.zeros((SUBLANES - TOP_K, rows), F32)], axis=0)

    sel_cnt = sum(h.astype(F32) for h in hots)
    prefix = jnp.dot(sel_cnt.astype(BF16), tri_ref[...], preferred_element_type=F32)
    cnt = jnp.sum(sel_cnt, axis=1, keepdims=True)
    cnt_u = jnp.floor((cnt + (SUBLANES - 1)) * (1.0 / SUBLANES))
    lower = (lax.broadcasted_iota(I32, (N_EXPERTS, N_EXPERTS), 1)
             < lax.broadcasted_iota(I32, (N_EXPERTS, N_EXPERTS), 0)).astype(BF16)
    cnt_u_l = jnp.broadcast_to(cnt_u, (N_EXPERTS, LANES))
    off_u_l = jnp.dot(lower, cnt_u_l.astype(BF16), preferred_element_type=F32)
    off_l = off_u_l * float(SUBLANES)
    tbl_ref[0, 0] = (cnt_u_l * float(SUBLANES)).astype(I32)
    tbl_ref[0, 1] = off_l.astype(I32)

    pos = prefix + off_l[:, :1]
    qs = [jnp.sum(jnp.where(h, pos, 0.0), axis=0, keepdims=True).astype(I32) for h in hots]
    q_ref[...] = jnp.concatenate(qs + [jnp.zeros((SUBLANES - TOP_K, rows), I32)], axis=0)

    def sort_chunk(c, carry):
        r0 = pl.multiple_of(c * FRONT_SORT_CHUNK, FRONT_SORT_CHUNK)
        jidx = r0 + lax.broadcasted_iota(I32, (FRONT_SORT_CHUNK, rows), 0)
        p = sum(jnp.where(jidx == qk, 1.0, 0.0) for qk in qs)
        xs = jnp.dot(p.astype(BF16), hn_bf, preferred_element_type=F32)
        xs_ref[pl.ds(r0, FRONT_SORT_CHUNK), :] = _pack_halves(xs)
        return carry

    lax.fori_loop(0, SORTED_ROWS // FRONT_SORT_CHUNK, sort_chunk, 0)


def _mix_out_and_route(x, mixbuf, w_out_ref, g_moe_ref, wr_t_ref, br_ref, tri_ref,
                       h_ref, q_ref, gate_ref, tbl_ref, xs_ref):
    h = x + jnp.dot(mixbuf[...], w_out_ref[...], preferred_element_type=F32)
    h_ref[...] = h
    hn = _rmsnorm(h, g_moe_ref[...])
    _route_and_sort(hn.astype(BF16), wr_t_ref, br_ref, tri_ref, q_ref, gate_ref, tbl_ref, xs_ref)


def _front_prompt_kernel(x_ref, sp_ref, sc_ref, g_mix_ref, w_in_ref, pool_w_ref, pool_scale_ref,
                         conv_w_ref, conv_b_ref, ln_g_ref, ln_b_ref, w_out_ref, g_moe_ref,
                         wr_t_ref, br_ref, tri_ref,
                         h_ref, q_ref, gate_ref, tbl_ref, xs_ref, np_ref, nc_ref,
                         ubuf, gsh, dbuf, mixbuf, *, pos0, rows):
    t = pl.program_id(1)

    @pl.when(t == 0)
    def _():
        ubuf[:, pl.ds(0, POOL_HIST), :] = jnp.zeros((len(POOL_WINDOWS), POOL_HIST, POOL_CH), F32)
        ubuf[0, pl.ds(POOL_HIST - POOL_BUF, POOL_BUF), :] = sp_ref[0, 0]
        gsh[0, pl.ds(CONV_HIST - CONV_BUF, CONV_BUF), :] = sc_ref[0, 0]

    x = x_ref[0]
    xn = _rmsnorm(x, g_mix_ref[...]).astype(BF16)
    proj = jnp.dot(xn, w_in_ref[...], preferred_element_type=F32)

    u = proj[:, :POOL_CH]
    ubuf[0, pl.ds(POOL_HIST, rows), :] = u
    pos = pos0 + t * rows + lax.broadcasted_iota(I32, (rows, 1), 0)
    span = rows + POOL_HIST - SUBLANES
    for g, w in enumerate(POOL_WINDOWS):
        assert w == 2 ** (g + 1) and w // 2 <= SUBLANES
        c0 = g * POOL_GROUP
        cs = slice(c0, c0 + POOL_GROUP)
        a = ubuf[g, pl.ds(SUBLANES, span), c0:] + ubuf[g, pl.ds(SUBLANES - w // 2, span), c0:]
        if g + 1 < len(POOL_WINDOWS):
            ubuf[g + 1, pl.ds(SUBLANES, span), c0 + POOL_GROUP:] = a[:, POOL_GROUP:]
        s = a[span - rows:, :POOL_GROUP]
        cnt = jnp.minimum(w, pos + 1).astype(F32)
        z = (s / cnt - u[:, cs]).astype(BF16)
        yg = jnp.dot(z, pool_w_ref[g], preferred_element_type=F32)
        mixbuf[:, cs] = (yg * pool_scale_ref[:, cs]).astype(BF16)

    glu = proj[:, POOL_CH:POOL_CH + CONV_CH] * _sigmoid(proj[:, POOL_CH + CONV_CH:])
    gsh[0, pl.ds(CONV_HIST, rows), :] = glu
    for s in range(1, SUBLANES):
        gsh[s, pl.ds(0, CONV_HIST - s), :] = gsh[0, pl.ds(s, CONV_HIST - s), :]
        gsh[s, pl.ds(CONV_HIST - s, rows), :] = glu

    def conv_chunk(i, carry):
        r0 = pl.multiple_of(i * CONV_CHUNK, CONV_CHUNK)
        for ct in range(CONV_CH // LANES):
            cs = slice(ct * LANES, (ct + 1) * LANES)
            acc = jnp.broadcast_to(conv_b_ref[:, cs], (CONV_CHUNK, LANES))
            for k in range(CONV_TAPS):
                first = CONV_HIST - CONV_BUF + k
                s = first % SUBLANES
                acc = acc + conv_w_ref[k:k + 1, cs] * gsh[s, pl.ds(r0 + (first - s), CONV_CHUNK), cs]
            dbuf[pl.ds(r0, CONV_CHUNK), cs] = acc
        return carry

    lax.fori_loop(0, rows // CONV_CHUNK, conv_chunk, 0)
    y_conv = _layernorm_silu(dbuf[...], ln_g_ref[...], ln_b_ref[...])
    mixbuf[:, POOL_CH:] = y_conv.astype(BF16)

    new_pool = ubuf[0, pl.ds(rows + POOL_HIST - POOL_BUF, POOL_BUF), :]
    new_conv = gsh[0, pl.ds(rows + CONV_HIST - CONV_BUF, CONV_BUF), :]
    np_ref[0, 0] = new_pool
    nc_ref[0, 0] = new_conv
    ubuf[0, pl.ds(POOL_HIST - POOL_BUF, POOL_BUF), :] = new_pool
    gsh[0, pl.ds(CONV_HIST - CONV_BUF, CONV_BUF), :] = new_conv

    _mix_out_and_route(x, mixbuf, w_out_ref, g_moe_ref, wr_t_ref, br_ref, tri_ref,
                       h_ref, q_ref, gate_ref, tbl_ref, xs_ref)


def _front_sample_kernel(x_ref, sp_ref, sc_ref, g_mix_ref, w_in_ref, pool_w_ref, pool_scale_ref,
                         conv_w_ref, conv_b_ref, ln_g_ref, ln_b_ref, w_out_ref, g_moe_ref,
                         wr_t_ref, br_ref, tri_ref, xs_in,
                         h_ref, q_ref, gate_ref, tbl_ref, xs_ref, np_ref, nc_ref,
                         xbuf, dbuf, mixbuf, *, pos0, seqs, steps):
    del xs_in
    for t in range(steps):
        xbuf[pl.ds(t * seqs, seqs), :] = x_ref[:, t * D_MODEL:(t + 1) * D_MODEL]
    x = xbuf[...]
    xn = _rmsnorm(x, g_mix_ref[...]).astype(BF16)
    proj = jnp.dot(xn, w_in_ref[...], preferred_element_type=F32)

    def slab(a, t):
        return a[t * seqs:(t + 1) * seqs]

    u = proj[:, :POOL_CH]
    cat = [sp_ref[:, j * POOL_CH:(j + 1) * POOL_CH] for j in range(POOL_BUF)]
    cat += [slab(u, t) for t in range(steps)]
    for t in range(steps):
        for g, w in enumerate(POOL_WINDOWS):
            cs = slice(g * POOL_GROUP, (g + 1) * POOL_GROUP)
            s = cat[POOL_BUF + t][:, cs]
            for j in range(1, w):
                s = s + cat[POOL_BUF + t - j][:, cs]
            cnt = float(min(w, pos0 + t + 1))
            z = (s / cnt - cat[POOL_BUF + t][:, cs]).astype(BF16)
            yg = jnp.dot(z, pool_w_ref[g], preferred_element_type=F32)
            mixbuf[pl.ds(t * seqs, seqs), cs] = (yg * pool_scale_ref[:, cs]).astype(BF16)
    for j in range(POOL_BUF):
        np_ref[:, j * POOL_CH:(j + 1) * POOL_CH] = cat[len(cat) - POOL_BUF + j]

    glu = proj[:, POOL_CH:POOL_CH + CONV_CH] * _sigmoid(proj[:, POOL_CH + CONV_CH:])
    ccat = [sc_ref[:, j * CONV_CH:(j + 1) * CONV_CH] for j in range(CONV_BUF)]
    ccat += [slab(glu, t) for t in range(steps)]
    for t in range(steps):
        acc = jnp.broadcast_to(conv_b_ref[...], (seqs, CONV_CH))
        for k in range(CONV_TAPS):
            acc = acc + conv_w_ref[k:k + 1, :] * ccat[t + k]
        dbuf[pl.ds(t * seqs, seqs), :] = acc
    for j in range(CONV_BUF):
        nc_ref[:, j * CONV_CH:(j + 1) * CONV_CH] = ccat[len(ccat) - CONV_BUF + j]
    y_conv = _layernorm_silu(dbuf[...], ln_g_ref[...], ln_b_ref[...])
    mixbuf[:, POOL_CH:] = y_conv.astype(BF16)

    _mix_out_and_route(x, mixbuf, w_out_ref, g_moe_ref, wr_t_ref, br_ref, tri_ref,
                       h_ref, q_ref, gate_ref, tbl_ref, xs_ref)


def _const_spec(shape, single_buffer=False):
    mode = pl.Buffered(1) if single_buffer else None
    return pl.BlockSpec(shape, lambda *_: (0,) * len(shape), pipeline_mode=mode)


def _front_weight_specs():
    return [
        _const_spec((1, D_MODEL)),
        _const_spec((D_MODEL, IN_COLS), True),
        _const_spec((len(POOL_WINDOWS), POOL_GROUP, POOL_GROUP)),
        _const_spec((1, POOL_CH)),
        _const_spec((CONV_TAPS, CONV_CH)),
        _const_spec((1, CONV_CH)),
        _const_spec((1, CONV_CH)),
        _const_spec((1, CONV_CH)),
        _const_spec((D_MODEL, D_MODEL), True),
        _const_spec((1, D_MODEL)),
        _const_spec((N_EXPERTS, D_MODEL)),
        _const_spec((N_EXPERTS, 1)),
        _const_spec((TILE, TILE), True),
    ]


def _front_out(n_tok, n_tiles_all, tile_map, tok_map, extra_specs, extra_shapes):
    specs = [pl.BlockSpec((TILE, D_MODEL), tok_map),
             pl.BlockSpec((SUBLANES, TILE), lambda *i: (0, tok_map(*i)[0])),
             pl.BlockSpec((SUBLANES, TILE), lambda *i: (0, tok_map(*i)[0])),
             pl.BlockSpec((1, 2, N_EXPERTS, LANES), lambda *i: (tok_map(*i)[0], 0, 0, 0)),
             pl.BlockSpec((SORTED_ROWS, PACKED), tile_map)]
    shapes = [jax.ShapeDtypeStruct((n_tok, D_MODEL), F32),
              jax.ShapeDtypeStruct((SUBLANES, n_tok), I32),
              jax.ShapeDtypeStruct((SUBLANES, n_tok), F32),
              jax.ShapeDtypeStruct((n_tok // TILE, 2, N_EXPERTS, LANES), I32),
              jax.ShapeDtypeStruct((n_tiles_all * SORTED_ROWS, PACKED), U32)]
    return specs + extra_specs, shapes + extra_shapes


def _front_prompt(x, state_pool, state_conv, weights, *, pos0, n_tiles_all):
    nb, nt, _ = x.shape
    rows = TILE
    tiles = nt // rows
    tok_map = lambda b, t: (b * tiles + t, 0)
    out_specs, out_shape = _front_out(
        nb * nt, n_tiles_all, tok_map, tok_map,
        [pl.BlockSpec((1, 1, POOL_BUF, POOL_CH), lambda b, t: (0, b, 0, 0)),
         pl.BlockSpec((1, 1, CONV_BUF, CONV_CH), lambda b, t: (0, b, 0, 0))],
        [jax.ShapeDtypeStruct((1, nb, POOL_BUF, POOL_CH), F32),
         jax.ShapeDtypeStruct((1, nb, CONV_BUF, CONV_CH), F32)])
    kern = functools.partial(_front_prompt_kernel, pos0=pos0, rows=rows)
    return pl.pallas_call(
        kern,
        grid=(nb, tiles),
        in_specs=[pl.BlockSpec((1, rows, D_MODEL), lambda b, t: (b, t, 0)),
                  pl.BlockSpec((1, 1, POOL_BUF, POOL_CH), lambda b, t: (0, b, 0, 0)),
                  pl.BlockSpec((1, 1, CONV_BUF, CONV_CH), lambda b, t: (0, b, 0, 0))]
                 + _front_weight_specs(),
        out_specs=out_specs,
        out_shape=out_shape,
        scratch_shapes=[pltpu.VMEM((len(POOL_WINDOWS), POOL_HIST + rows, POOL_CH), F32),
                        pltpu.VMEM((SUBLANES, CONV_HIST + rows, CONV_CH), F32),
                        pltpu.VMEM((rows, CONV_CH), F32),
                        pltpu.VMEM((rows, D_MODEL), BF16)],
        compiler_params=pltpu.CompilerParams(
            dimension_semantics=("arbitrary", "arbitrary"), vmem_limit_bytes=VMEM_LIMIT_BYTES),
        name="front_prompt",
    )(x, state_pool, state_conv, *weights)


def _front_sample(x, state_pool, state_conv, weights, xs_all, *, pos0, tile0):
    nb, nt, _ = x.shape
    seqs = TILE // nt
    n_tiles_all = xs_all.shape[0] // SORTED_ROWS
    tok_map = lambda i: (i, 0)
    out_specs, out_shape = _front_out(
        nb * nt, n_tiles_all, lambda i: (tile0 + i, 0), tok_map,
        [pl.BlockSpec((seqs, POOL_BUF * POOL_CH), lambda i: (i, 0)),
         pl.BlockSpec((seqs, CONV_BUF * CONV_CH), lambda i: (i, 0))],
        [jax.ShapeDtypeStruct((nb, POOL_BUF * POOL_CH), F32),
         jax.ShapeDtypeStruct((nb, CONV_BUF * CONV_CH), F32)])
    kern = functools.partial(_front_sample_kernel, pos0=pos0, seqs=seqs, steps=nt)
    n_in = 3 + len(weights)
    return pl.pallas_call(
        kern,
        grid=(nb // seqs,),
        in_specs=[pl.BlockSpec((seqs, nt * D_MODEL), lambda i: (i, 0)),
                  pl.BlockSpec((seqs, POOL_BUF * POOL_CH), lambda i: (i, 0)),
                  pl.BlockSpec((seqs, CONV_BUF * CONV_CH), lambda i: (i, 0))]
                 + _front_weight_specs() + [pl.BlockSpec(memory_space=pl.ANY)],
        out_specs=out_specs,
        out_shape=out_shape,
        scratch_shapes=[pltpu.VMEM((TILE, D_MODEL), F32),
                        pltpu.VMEM((TILE, CONV_CH), F32),
                        pltpu.VMEM((TILE, D_MODEL), BF16)],
        input_output_aliases={n_in: 4},
        compiler_params=pltpu.CompilerParams(
            dimension_semantics=("arbitrary",), vmem_limit_bytes=VMEM_LIMIT_BYTES),
        name="front_sample",
    )(x.reshape(nb, nt * D_MODEL), state_pool.reshape(nb, POOL_BUF * POOL_CH),
      state_conv.reshape(nb, CONV_BUF * CONV_CH), *weights, xs_all)


def _moe_kernel(be_ref, nused_ref, src_ref, dst_ref, first_ref, wslot_ref, nxt_ref,
                xs_hbm, wgu_hbm, bgu_ref, wd_hbm, bd_ref, ys_hbm,
                xa, xb, oa, ob, wgu_f32, wd_f32, wgu_bf, wd_bf,
                gsem_a, gsem_b, ssem_a, ssem_b, wsem, *, dummy_half):
    i = pl.program_id(0)
    n_used = nused_ref[0]
    half_a = 2 * i
    half_b = 2 * i + 1

    def weight_copies(e, s):
        return (pltpu.make_async_copy(wgu_hbm.at[e], wgu_f32.at[s], wsem.at[s]),
                pltpu.make_async_copy(wd_hbm.at[e], wd_f32.at[s], wsem.at[s]))

    def gather(half, buf, sem):
        for g in range(HALF_GROUPS):
            hrow = pl.multiple_of(src_ref[half * HALF_GROUPS + g], SUBLANES)
            pltpu.make_async_copy(xs_hbm.at[pl.ds(hrow, SUBLANES)],
                                  buf.at[pl.ds(g * SUBLANES, SUBLANES)], sem).start()

    def gather_wait(buf, sem):
        pltpu.make_async_copy(xs_hbm.at[pl.ds(0, HALF_ROWS)], buf, sem).wait()

    def scatter(half, buf, sem):
        for g in range(HALF_GROUPS):
            hrow = pl.multiple_of(dst_ref[half * HALF_GROUPS + g], SUBLANES)
            pltpu.make_async_copy(buf.at[pl.ds(g * SUBLANES, SUBLANES)],
                                  ys_hbm.at[pl.ds(hrow, SUBLANES)], sem).start()

    def scatter_wait(buf, sem):
        pltpu.make_async_copy(buf, ys_hbm.at[pl.ds(0, HALF_ROWS)], sem).wait()

    def experts(xbuf, obuf):
        x = _unpack_halves(xbuf[...]).astype(BF16)
        gu = jnp.dot(x, wgu_bf[...], preferred_element_type=F32) + bgu_ref[0]
        g = jnp.minimum(gu[:, :D_FF], SWIGLU_LIMIT)
        up = jnp.clip(gu[:, D_FF:], -SWIGLU_LIMIT, SWIGLU_LIMIT)
        hdn = (up + 1.0) * (g * _sigmoid(SWIGLU_ALPHA * g))
        y = jnp.dot(hdn.astype(BF16), wd_bf[...], preferred_element_type=F32) + bd_ref[0]
        obuf[...] = _pack_halves(y.astype(BF16).astype(F32))

    @pl.when(i == 0)
    def _():
        oa[...] = jnp.zeros_like(oa)
        ob[...] = jnp.zeros_like(ob)
        scatter(dummy_half, oa, ssem_a)
        gather(0, xa, gsem_a)
        for c in weight_copies(be_ref[0], 0):
            c.start(priority=WEIGHT_DMA_PRIORITY)

    @pl.when(i < n_used)
    def _():
        @pl.when(first_ref[i] == 1)
        def _():
            s = wslot_ref[i]
            for c in weight_copies(be_ref[i], s):
                c.wait()
            wgu_bf[...] = wgu_f32[s].astype(BF16)
            wd_bf[...] = wd_f32[s].astype(BF16)

            @pl.when(nxt_ref[i] < N_EXPERTS)
            def _():
                for c in weight_copies(nxt_ref[i], 1 - s):
                    c.start(priority=WEIGHT_DMA_PRIORITY)

        prev_b = jnp.where(i == 0, dummy_half, half_b - 2)
        next_a = 2 * jnp.minimum(i + 1, n_used - 1)
        scatter_wait(oa, ssem_a)
        gather_wait(xa, gsem_a)
        scatter(prev_b, ob, ssem_b)
        gather(half_b, xb, gsem_b)
        experts(xa, oa)
        scatter_wait(ob, ssem_b)
        gather_wait(xb, gsem_b)
        scatter(half_a, oa, ssem_a)
        gather(next_a, xa, gsem_a)
        experts(xb, ob)

        @pl.when(i == n_used - 1)
        def _():
            scatter(half_b, ob, ssem_b)
            scatter_wait(oa, ssem_a)
            scatter_wait(ob, ssem_b)
            gather_wait(xa, gsem_a)


def _moe(xs_all, tables, w_gate_up, b_gate_up, w_down, b_down):
    block_e, n_used, src, dst, first, wslot, nxt = tables
    n_blocks = block_e.shape[0]

    def e_map(i, be, nu, *_):
        return (be[jnp.minimum(i, nu[0] - 1)], 0, 0)

    half_buf = pltpu.VMEM((HALF_ROWS, PACKED), U32)
    any_spec = pl.BlockSpec(memory_space=pl.ANY)
    grid_spec = pltpu.PrefetchScalarGridSpec(
        num_scalar_prefetch=7,
        grid=(n_blocks,),
        in_specs=[any_spec,
                  any_spec,
                  pl.BlockSpec((1, 1, 2 * D_FF), e_map),
                  any_spec,
                  pl.BlockSpec((1, 1, D_MODEL), e_map)],
        out_specs=any_spec,
        scratch_shapes=[half_buf, half_buf, half_buf, half_buf,
                        pltpu.VMEM((2, D_MODEL, 2 * D_FF), F32),
                        pltpu.VMEM((2, D_FF, D_MODEL), F32),
                        pltpu.VMEM((D_MODEL, 2 * D_FF), BF16),
                        pltpu.VMEM((D_FF, D_MODEL), BF16)]
                       + [pltpu.SemaphoreType.DMA(())] * 4 + [pltpu.SemaphoreType.DMA((2,))],
    )
    return pl.pallas_call(
        functools.partial(_moe_kernel, dummy_half=2 * n_blocks),
        grid_spec=grid_spec,
        out_shape=jax.ShapeDtypeStruct((xs_all.shape[0] + MOE_ROWS, PACKED), U32),
        compiler_params=pltpu.CompilerParams(
            dimension_semantics=("arbitrary",), vmem_limit_bytes=VMEM_LIMIT_BYTES),
        name="moe_experts",
    )(block_e, n_used, src, dst, first, wslot, nxt,
      xs_all, w_gate_up, b_gate_up.reshape(N_EXPERTS, 1, 2 * D_FF), w_down,
      b_down.reshape(N_EXPERTS, 1, D_MODEL))


def _final_kernel(used_ref, h_ref, ys_ref, q_ref, gate_ref, g_ref, o_ref, *, tile0, slabs):
    i = pl.program_id(0)
    used = used_ref[tile0 + i]

    def to_cols(a):
        pad = jnp.zeros((LANES - SUBLANES, TILE), F32)
        return jnp.concatenate([a, pad], axis=0).T

    qc = to_cols(q_ref[...].astype(F32))
    gc = to_cols(gate_ref[...])
    acc = h_ref[...]
    for c in range(SORTED_ROWS // SORT_CHUNK):
        jl = (c * SORT_CHUNK + lax.broadcasted_iota(I32, (TILE, SORT_CHUNK), 1)).astype(F32)
        pg = sum(jnp.where(qc[:, k:k + 1] == jl, gc[:, k:k + 1], 0.0) for k in range(TOP_K))
        rid = c * SORT_CHUNK + lax.broadcasted_iota(I32, (SORT_CHUNK, 1), 0)
        ys = jnp.where(rid < used, _unpack_halves(ys_ref[pl.ds(c * SORT_CHUNK, SORT_CHUNK), :]), 0.0)
        acc = acc + jnp.dot(pg.astype(BF16), ys.astype(BF16), preferred_element_type=F32)
    y = _rmsnorm(acc, g_ref[...])
    if slabs == 1:
        o_ref[...] = y
    else:
        seqs = TILE // slabs
        for t in range(slabs):
            o_ref[:, t * D_MODEL:(t + 1) * D_MODEL] = y[t * seqs:(t + 1) * seqs]


def _final(h, ys_all, q, gates, used, g_final, *, tile0, slabs):
    n_tok = h.shape[0]
    if slabs == 1:
        out_spec = pl.BlockSpec((TILE, D_MODEL), lambda i, u: (i, 0))
        out_shape = jax.ShapeDtypeStruct((n_tok, D_MODEL), F32)
    else:
        seqs = TILE // slabs
        out_spec = pl.BlockSpec((seqs, slabs * D_MODEL), lambda i, u: (i, 0))
        out_shape = jax.ShapeDtypeStruct((n_tok // slabs, slabs * D_MODEL), F32)
    grid_spec = pltpu.PrefetchScalarGridSpec(
        num_scalar_prefetch=1,
        grid=(n_tok // TILE,),
        in_specs=[pl.BlockSpec((TILE, D_MODEL), lambda i, u: (i, 0)),
                  pl.BlockSpec((SORTED_ROWS, PACKED), lambda i, u: (tile0 + i, 0)),
                  pl.BlockSpec((SUBLANES, TILE), lambda i, u: (0, i)),
                  pl.BlockSpec((SUBLANES, TILE), lambda i, u: (0, i)),
                  pl.BlockSpec((1, D_MODEL), lambda i, u: (0, 0))],
        out_specs=out_spec,
    )
    return pl.pallas_call(
        functools.partial(_final_kernel, tile0=tile0, slabs=slabs),
        grid_spec=grid_spec,
        out_shape=out_shape,
        compiler_params=pltpu.CompilerParams(
            dimension_semantics=("arbitrary",), vmem_limit_bytes=VMEM_LIMIT_BYTES),
        name="final_norm",
    )(used, h, ys_all, q, gates, g_final)


def _block_tables(tbl, n_blocks):
    cnt = tbl[:, 0, :, 0]
    off = tbl[:, 1, :, 0]
    run = jnp.cumsum(cnt, axis=0) - cnt
    total = jnp.sum(cnt, axis=0)
    nblk = (total + MOE_ROWS - 1) // MOE_ROWS
    blk_end = jnp.cumsum(nblk)
    blk_start = blk_end - nblk
    bidx = jnp.arange(n_blocks, dtype=I32)
    eidx = jnp.arange(N_EXPERTS, dtype=I32)
    block_e = jnp.minimum(jnp.sum(blk_end[None, :] <= bidx[:, None], axis=1), N_EXPERTS - 1).astype(I32)
    hot = block_e[:, None] == eidx[None, :]

    def per_block(v):
        return jnp.sum(jnp.where(hot, v[..., None, :], 0), axis=-1)

    row0 = (bidx - per_block(blk_start)) * MOE_ROWS
    n_tiles = cnt.shape[0]
    groups = jnp.arange(MOE_GROUPS, dtype=I32)
    erow = row0[:, None] + groups[None, :] * SUBLANES
    run_b, cnt_b, off_b = per_block(run), per_block(cnt), per_block(off)
    inside = jnp.logical_and(run_b[:, :, None] <= erow[None], erow[None] < (run_b + cnt_b)[:, :, None])
    tile_base = jnp.arange(n_tiles, dtype=I32)[:, None] * SORTED_ROWS + off_b - run_b
    src = jnp.sum(jnp.where(inside, tile_base[:, :, None] + erow[None], 0), axis=0)
    dump = n_tiles * SORTED_ROWS + groups * SUBLANES
    dst = jnp.where(erow < per_block(total)[:, None], src, dump[None, :])
    src = jnp.concatenate([src.reshape(-1), jnp.zeros((HALF_GROUPS,), I32)])
    dst = jnp.concatenate([dst.reshape(-1), dump[:HALF_GROUPS]])
    used = jnp.sum(cnt, axis=1)
    has_rows = nblk > 0
    first = (bidx == per_block(blk_start)).astype(I32)
    wslot = per_block((jnp.cumsum(has_rows) - 1) % 2)
    later = jnp.where(jnp.logical_and(has_rows[None, :], eidx[None, :] > eidx[:, None]), eidx[None, :], N_EXPERTS)
    nxt = per_block(jnp.min(later, axis=1))
    return ((block_e, blk_end[-1:].astype(I32), src.astype(I32), dst.astype(I32), first,
             wslot.astype(I32), nxt.astype(I32)), used.astype(I32))


def kernel(x_prompt, x_sample, state_pool, state_conv, g_mix, w_in, pool_w, pool_scale, conv_w, conv_b,
           conv_ln_g, conv_ln_b, w_out, g_moe, w_router, b_router, w_gate_up, b_gate_up, w_down, b_down,
           g_final):
    nb_p, nt_p, _ = x_prompt.shape
    nb_s, nt_s, _ = x_sample.shape
    n_p, n_s = nb_p * nt_p, nb_s * nt_s
    past_len = 16384
    assert nt_p % TILE == 0 and TILE % nt_s == 0 and n_s % TILE == 0
    tiles_p, tiles_s = n_p // TILE, n_s // TILE
    n_tiles = tiles_p + tiles_s

    tri = (lax.broadcasted_iota(I32, (TILE, TILE), 0) < lax.broadcasted_iota(I32, (TILE, TILE), 1)).astype(BF16)
    weights = (g_mix[0][None], w_in[0].astype(BF16), pool_w[0].astype(BF16), pool_scale[0][None],
               conv_w[0], conv_b[0][None], conv_ln_g[0][None], conv_ln_b[0][None],
               w_out[0].astype(BF16), g_moe[0][None], w_router[0].T.astype(BF16),
               b_router[0][:, None], tri)

    zero_pool = jnp.zeros((1, nb_p, POOL_BUF, POOL_CH), F32)
    zero_conv = jnp.zeros((1, nb_p, CONV_BUF, CONV_CH), F32)
    h_p, q_p, gate_p, tbl_p, xs_all, pool_p, conv_p = _front_prompt(
        x_prompt, zero_pool, zero_conv, weights, pos0=0, n_tiles_all=n_tiles)
    h_s, q_s, gate_s, tbl_s, xs_all, pool_s, conv_s = _front_sample(
        x_sample, state_pool[0], state_conv[0], weights, xs_all, pos0=past_len, tile0=tiles_p)

    sorted_total = n_tiles * (TILE * TOP_K + N_EXPERTS * (SUBLANES - 1))
    n_blocks = -(-sorted_total // MOE_ROWS) + N_EXPERTS
    tables, used = _block_tables(jnp.concatenate([tbl_p, tbl_s], axis=0), n_blocks)
    ys_all = _moe(xs_all, tables, w_gate_up[0], b_gate_up[0], w_down[0], b_down[0])

    gf = g_final[None]
    y_prompt = _final(h_p, ys_all, q_p, gate_p, used, gf, tile0=0, slabs=1)
    y_sample = _final(h_s, ys_all, q_s, gate_s, used, gf, tile0=tiles_p, slabs=nt_s)
    return (y_prompt.reshape(nb_p, nt_p, D_MODEL), y_sample.reshape(nb_s, nt_s, D_MODEL),
            pool_p, conv_p,
            pool_s.reshape(1, nb_s, POOL_BUF, POOL_CH), conv_s.reshape(1, nb_s, CONV_BUF, CONV_CH))
```

```python
import functools

import jax
import jax.numpy as jnp
from jax import lax
from jax.experimental import pallas as pl
from jax.experimental.pallas import tpu as pltpu

D_MODEL = 1024
POOL_CH = 512
CONV_CH = 512
POOL_WINDOWS = (2, 4, 8, 16)
POOL_GROUP = 128
POOL_BUF = 15
CONV_TAPS = 31
CONV_BUF = 30
IN_COLS = POOL_CH + 2 * CONV_CH
N_EXPERTS = 32
TOP_K = 4
D_FF = 1024
SWIGLU_LIMIT = 7.0
SWIGLU_ALPHA = 1.702
RMS_EPS = 1e-5
LN_EPS = 1e-5

LANES = 128
SUBLANES = 8
VMEM_LIMIT_BYTES = 60 * 1024 * 1024

TILE = 512
SAMPLE_SEQS = 64
CONV_CHUNK = 32
SORT_CHUNK = 256
FRONT_SORT_CHUNK = 768
SORTED_ROWS = -(-(TILE * TOP_K + N_EXPERTS * (SUBLANES - 1)) // SORT_CHUNK) * SORT_CHUNK
MOE_ROWS = 512
HALF_ROWS = MOE_ROWS // 2
MOE_GROUPS = MOE_ROWS // SUBLANES
HALF_GROUPS = MOE_GROUPS // 2
WEIGHT_DMA_PRIORITY = 1

POOL_HIST = 24
CONV_HIST = 32

BF16 = jnp.bfloat16
F32 = jnp.float32
I32 = jnp.int32
U32 = jnp.uint32
PACKED = D_MODEL // 2


def _rmsnorm(x, g):
    ms = jnp.mean(x * x, axis=-1, keepdims=True)
    return x * lax.rsqrt(ms + RMS_EPS) * g


def _sigmoid(x):
    return 1.0 / (1.0 + jnp.exp(-x))


def _layernorm_silu(d, g, b):
    mu = jnp.mean(d, axis=-1, keepdims=True)
    c = d - mu
    var = jnp.mean(c * c, axis=-1, keepdims=True)
    n = c * lax.rsqrt(var + LN_EPS) * g + b
    return n * _sigmoid(n)


def _pack_halves(a):
    half = a.shape[1] // 2
    hi = lax.bitcast_convert_type(a[:, :half], U32)
    lo = lax.bitcast_convert_type(a[:, half:], U32)
    return hi | (lo >> 16)


def _unpack_halves(w):
    hi = lax.bitcast_convert_type(w & jnp.uint32(0xFFFF0000), F32)
    lo = lax.bitcast_convert_type(w << 16, F32)
    return jnp.concatenate([hi, lo], axis=1)


def _route_and_sort(hn_bf, wr_t_ref, br_ref, tri_ref, q_ref, gate_ref, tbl_ref, xs_ref):
    rows = hn_bf.shape[0]
    lt = lax.dot_general(wr_t_ref[...], hn_bf, (((1,), (1,)), ((), ())),
                         preferred_element_type=F32)
    lt = lt + br_ref[...]
    eidx = lax.broadcasted_iota(I32, (N_EXPERTS, rows), 0)
    vals, hots = [], []
    for _ in range(TOP_K):
        m = jnp.max(lt, axis=0, keepdims=True)
        sel = jnp.min(jnp.where(lt == m, eidx, N_EXPERTS), axis=0, keepdims=True)
        hit = eidx == sel
        vals.append(m)
        hots.append(hit)
        lt = jnp.where(hit, -jnp.inf, lt)
    v = jnp.concatenate(vals, axis=0)
    e = jnp.exp(v - vals[0])
    gate = e / jnp.sum(e, axis=0, keepdims=True)
    gate_ref[...] = jnp.concatenate([gate, jnp.zeros((SUBLANES - TOP_K, rows), F32)], axis=0)

    sel_cnt = sum(h.astype(F32) for h in hots)
    prefix = jnp.dot(sel_cnt.astype(BF16), tri_ref[...], preferred_element_type=F32)
    cnt = jnp.sum(sel_cnt, axis=1, keepdims=True)
    cnt_u = jnp.floor((cnt + (SUBLANES - 1)) * (1.0 / SUBLANES))
    lower = (lax.broadcasted_iota(I32, (N_EXPERTS, N_EXPERTS), 1)
             < lax.broadcasted_iota(I32, (N_EXPERTS, N_EXPERTS), 0)).astype(BF16)
    cnt_u_l = jnp.broadcast_to(cnt_u, (N_EXPERTS, LANES))
    off_u_l = jnp.dot(lower, cnt_u_l.astype(BF16), preferred_element_type=F32)
    off_l = off_u_l * float(SUBLANES)
    tbl_ref[0, 0] = (cnt_u_l * float(SUBLANES)).astype(I32)
    tbl_ref[0, 1] = off_l.astype(I32)

    pos = prefix + off_l[:, :1]
    qs = [jnp.sum(jnp.where(h, pos, 0.0), axis=0, keepdims=True).astype(I32) for h in hots]
    q_ref[...] = jnp.concatenate(qs + [jnp.zeros((SUBLANES - TOP_K, rows), I32)], axis=0)

    def sort_chunk(c, carry):
        r0 = pl.multiple_of(c * FRONT_SORT_CHUNK, FRONT_SORT_CHUNK)
        jidx = r0 + lax.broadcasted_iota(I32, (FRONT_SORT_CHUNK, rows), 0)
        p = sum(jnp.where(jidx == qk, 1.0, 0.0) for qk in qs)
        xs = jnp.dot(p.astype(BF16), hn_bf, preferred_element_type=F32)
        xs_ref[pl.ds(r0, FRONT_SORT_CHUNK), :] = _pack_halves(xs)
        return carry

    lax.fori_loop(0, SORTED_ROWS // FRONT_SORT_CHUNK, sort_chunk, 0)


def _mix_out_and_route(x, mixbuf, w_out_ref, g_moe_ref, wr_t_ref, br_ref, tri_ref,
                       h_ref, q_ref, gate_ref, tbl_ref, xs_ref):
    h = x + jnp.dot(mixbuf[...], w_out_ref[...], preferred_element_type=F32)
    h_ref[...] = h
    hn = _rmsnorm(h, g_moe_ref[...])
    _route_and_sort(hn.astype(BF16), wr_t_ref, br_ref, tri_ref, q_ref, gate_ref, tbl_ref, xs_ref)


def _front_prompt_kernel(x_ref, sp_ref, sc_ref, g_mix_ref, w_in_ref, pool_w_ref, pool_scale_ref,
                         conv_w_ref, conv_b_ref, ln_g_ref, ln_b_ref, w_out_ref, g_moe_ref,
                         wr_t_ref, br_ref, tri_ref,
                         h_ref, q_ref, gate_ref, tbl_ref, xs_ref, np_ref, nc_ref,
                         ubuf, gsh, dbuf, mixbuf, *, pos0, rows):
    t = pl.program_id(1)

    @pl.when(t == 0)
    def _():
        ubuf[:, pl.ds(0, POOL_HIST), :] = jnp.zeros((len(POOL_WINDOWS), POOL_HIST, POOL_CH), F32)
        ubuf[0, pl.ds(POOL_HIST - POOL_BUF, POOL_BUF), :] = sp_ref[0, 0]
        gsh[0, pl.ds(CONV_HIST - CONV_BUF, CONV_BUF), :] = sc_ref[0, 0]

    x = x_ref[0]
    xn = _rmsnorm(x, g_mix_ref[...]).astype(BF16)
    proj = jnp.dot(xn, w_in_ref[...], preferred_element_type=F32)

    u = proj[:, :POOL_CH]
    ubuf[0, pl.ds(POOL_HIST, rows), :] = u
    pos = pos0 + t * rows + lax.broadcasted_iota(I32, (rows, 1), 0)
    span = rows + POOL_HIST - SUBLANES
    for g, w in enumerate(POOL_WINDOWS):
        assert w == 2 ** (g + 1) and w // 2 <= SUBLANES
        c0 = g * POOL_GROUP
        cs = slice(c0, c0 + POOL_GROUP)
        a = ubuf[g, pl.ds(SUBLANES, span), c0:] + ubuf[g, pl.ds(SUBLANES - w // 2, span), c0:]
        if g + 1 < len(POOL_WINDOWS):
            ubuf[g + 1, pl.ds(SUBLANES, span), c0 + POOL_GROUP:] = a[:, POOL_GROUP:]
        s = a[span - rows:, :POOL_GROUP]
        cnt = jnp.minimum(w, pos + 1).astype(F32)
        z = (s / cnt - u[:, cs]).astype(BF16)
        yg = jnp.dot(z, pool_w_ref[g], preferred_element_type=F32)
        mixbuf[:, cs] = (yg * pool_scale_ref[:, cs]).astype(BF16)

    glu = proj[:, POOL_CH:POOL_CH + CONV_CH] * _sigmoid(proj[:, POOL_CH + CONV_CH:])
    gsh[0, pl.ds(CONV_HIST, rows), :] = glu
    for s in range(1, SUBLANES):
        gsh[s, pl.ds(0, CONV_HIST - s), :] = gsh[0, pl.ds(s, CONV_HIST - s), :]
        gsh[s, pl.ds(CONV_HIST - s, rows), :] = glu

    def conv_chunk(i, carry):
        r0 = pl.multiple_of(i * CONV_CHUNK, CONV_CHUNK)
        for ct in range(CONV_CH // LANES):
            cs = slice(ct * LANES, (ct + 1) * LANES)
            acc = jnp.broadcast_to(conv_b_ref[:, cs], (CONV_CHUNK, LANES))
            for k in range(CONV_TAPS):
                first = CONV_HIST - CONV_BUF + k
                s = first % SUBLANES
                acc = acc + conv_w_ref[k:k + 1, cs] * gsh[s, pl.ds(r0 + (first - s), CONV_CHUNK), cs]
            dbuf[pl.ds(r0, CONV_CHUNK), cs] = acc
        return carry

    lax.fori_loop(0, rows // CONV_CHUNK, conv_chunk, 0)
    y_conv = _layernorm_silu(dbuf[...], ln_g_ref[...], ln_b_ref[...])
    mixbuf[:, POOL_CH:] = y_conv.astype(BF16)

    new_pool = ubuf[0, pl.ds(rows + POOL_HIST - POOL_BUF, POOL_BUF), :]
    new_conv = gsh[0, pl.ds(rows + CONV_HIST - CONV_BUF, CONV_BUF), :]
    np_ref[0, 0] = new_pool
    nc_ref[0, 0] = new_conv
    ubuf[0, pl.ds(POOL_HIST - POOL_BUF, POOL_BUF), :] = new_pool
    gsh[0, pl.ds(CONV_HIST - CONV_BUF, CONV_BUF), :] = new_conv

    _mix_out_and_route(x, mixbuf, w_out_ref, g_moe_ref, wr_t_ref, br_ref, tri_ref,
                       h_ref, q_ref, gate_ref, tbl_ref, xs_ref)


def _front_sample_kernel(x_ref, sp_ref, sc_ref, g_mix_ref, w_in_ref, pool_w_ref, pool_scale_ref,
                         conv_w_ref, conv_b_ref, ln_g_ref, ln_b_ref, w_out_ref, g_moe_ref,
                         wr_t_ref, br_ref, tri_ref, xs_in,
                         h_ref, q_ref, gate_ref, tbl_ref, xs_ref, np_ref, nc_ref,
                         xbuf, dbuf, mixbuf, *, pos0, seqs, steps):
    del xs_in
    for t in range(steps):
        xbuf[pl.ds(t * seqs, seqs), :] = x_ref[:, t * D_MODEL:(t + 1) * D_MODEL]
    x = xbuf[...]
    xn = _rmsnorm(x, g_mix_ref[...]).astype(BF16)
    proj = jnp.dot(xn, w_in_ref[...], preferred_element_type=F32)

    def slab(a, t):
        return a[t * seqs:(t + 1) * seqs]

    u = proj[:, :POOL_CH]
    cat = [sp_ref[:, j * POOL_CH:(j + 1) * POOL_CH] for j in range(POOL_BUF)]
    cat += [slab(u, t) for t in range(steps)]
    for t in range(steps):
        for g, w in enumerate(POOL_WINDOWS):
            cs = slice(g * POOL_GROUP, (g + 1) * POOL_GROUP)
            s = cat[POOL_BUF + t][:, cs]
            for j in range(1, w):
                s = s + cat[POOL_BUF + t - j][:, cs]
            cnt = float(min(w, pos0 + t + 1))
            z = (s / cnt - cat[POOL_BUF + t][:, cs]).astype(BF16)
            yg = jnp.dot(z, pool_w_ref[g], preferred_element_type=F32)
            mixbuf[pl.ds(t * seqs, seqs), cs] = (yg * pool_scale_ref[:, cs]).astype(BF16)
    for j in range(POOL_BUF):
        np_ref[:, j * POOL_CH:(j + 1) * POOL_CH] = cat[len(cat) - POOL_BUF + j]

    glu = proj[:, POOL_CH:POOL_CH + CONV_CH] * _sigmoid(proj[:, POOL_CH + CONV_CH:])
    ccat = [sc_ref[:, j * CONV_CH:(j + 1) * CONV_CH] for j in range(CONV_BUF)]
    ccat += [slab(glu, t) for t in range(steps)]
    for t in range(steps):
        acc = jnp.broadcast_to(conv_b_ref[...], (seqs, CONV_CH))
        for k in range(CONV_TAPS):
            acc = acc + conv_w_ref[k:k + 1, :] * ccat[t + k]
        dbuf[pl.ds(t * seqs, seqs), :] = acc
    for j in range(CONV_BUF):
        nc_ref[:, j * CONV_CH:(j + 1) * CONV_CH] = ccat[len(ccat) - CONV_BUF + j]
    y_conv = _layernorm_silu(dbuf[...], ln_g_ref[...], ln_b_ref[...])
    mixbuf[:, POOL_CH:] = y_conv.astype(BF16)

    _mix_out_and_route(x, mixbuf, w_out_ref, g_moe_ref, wr_t_ref, br_ref, tri_ref,
                       h_ref, q_ref, gate_ref, tbl_ref, xs_ref)


def _const_spec(shape, single_buffer=False):
    mode = pl.Buffered(1) if single_buffer else None
    return pl.BlockSpec(shape, lambda *_: (0,) * len(shape), pipeline_mode=mode)


def _front_weight_specs():
    return [
        _const_spec((1, D_MODEL)),
        _const_spec((D_MODEL, IN_COLS), True),
        _const_spec((len(POOL_WINDOWS), POOL_GROUP, POOL_GROUP)),
        _const_spec((1, POOL_CH)),
        _const_spec((CONV_TAPS, CONV_CH)),
        _const_spec((1, CONV_CH)),
        _const_spec((1, CONV_CH)),
        _const_spec((1, CONV_CH)),
        _const_spec((D_MODEL, D_MODEL), True),
        _const_spec((1, D_MODEL)),
        _const_spec((N_EXPERTS, D_MODEL)),
        _const_spec((N_EXPERTS, 1)),
        _const_spec((TILE, TILE), True),
    ]


def _front_out(n_tok, n_tiles_all, tile_map, tok_map, extra_specs, extra_shapes):
    specs = [pl.BlockSpec((TILE, D_MODEL), tok_map),
             pl.BlockSpec((SUBLANES, TILE), lambda *i: (0, tok_map(*i)[0])),
             pl.BlockSpec((SUBLANES, TILE), lambda *i: (0, tok_map(*i)[0])),
             pl.BlockSpec((1, 2, N_EXPERTS, LANES), lambda *i: (tok_map(*i)[0], 0, 0, 0)),
             pl.BlockSpec((SORTED_ROWS, PACKED), tile_map)]
    shapes = [jax.ShapeDtypeStruct((n_tok, D_MODEL), F32),
              jax.ShapeDtypeStruct((SUBLANES, n_tok), I32),
              jax.ShapeDtypeStruct((SUBLANES, n_tok), F32),
              jax.ShapeDtypeStruct((n_tok // TILE, 2, N_EXPERTS, LANES), I32),
              jax.ShapeDtypeStruct((n_tiles_all * SORTED_ROWS, PACKED), U32)]
    return specs + extra_specs, shapes + extra_shapes


def _front_prompt(x, state_pool, state_conv, weights, *, pos0, n_tiles_all):
    nb, nt, _ = x.shape
    rows = TILE
    tiles = nt // rows
    tok_map = lambda b, t: (b * tiles + t, 0)
    out_specs, out_shape = _front_out(
        nb * nt, n_tiles_all, tok_map, tok_map,
        [pl.BlockSpec((1, 1, POOL_BUF, POOL_CH), lambda b, t: (0, b, 0, 0)),
         pl.BlockSpec((1, 1, CONV_BUF, CONV_CH), lambda b, t: (0, b, 0, 0))],
        [jax.ShapeDtypeStruct((1, nb, POOL_BUF, POOL_CH), F32),
         jax.ShapeDtypeStruct((1, nb, CONV_BUF, CONV_CH), F32)])
    kern = functools.partial(_front_prompt_kernel, pos0=pos0, rows=rows)
    return pl.pallas_call(
        kern,
        grid=(nb, tiles),
        in_specs=[pl.BlockSpec((1, rows, D_MODEL), lambda b, t: (b, t, 0)),
                  pl.BlockSpec((1, 1, POOL_BUF, POOL_CH), lambda b, t: (0, b, 0, 0)),
                  pl.BlockSpec((1, 1, CONV_BUF, CONV_CH), lambda b, t: (0, b, 0, 0))]
                 + _front_weight_specs(),
        out_specs=out_specs,
        out_shape=out_shape,
        scratch_shapes=[pltpu.VMEM((len(POOL_WINDOWS), POOL_HIST + rows, POOL_CH), F32),
                        pltpu.VMEM((SUBLANES, CONV_HIST + rows, CONV_CH), F32),
                        pltpu.VMEM((rows, CONV_CH), F32),
                        pltpu.VMEM((rows, D_MODEL), BF16)],
        compiler_params=pltpu.CompilerParams(
            dimension_semantics=("arbitrary", "arbitrary"), vmem_limit_bytes=VMEM_LIMIT_BYTES),
        name="front_prompt",
    )(x, state_pool, state_conv, *weights)


def _front_sample(x, state_pool, state_conv, weights, xs_all, *, pos0, tile0):
    nb, nt, _ = x.shape
    seqs = TILE // nt
    n_tiles_all = xs_all.shape[0] // SORTED_ROWS
    tok_map = lambda i: (i, 0)
    out_specs, out_shape = _front_out(
        nb * nt, n_tiles_all, lambda i: (tile0 + i, 0), tok_map,
        [pl.BlockSpec((seqs, POOL_BUF * POOL_CH), lambda i: (i, 0)),
         pl.BlockSpec((seqs, CONV_BUF * CONV_CH), lambda i: (i, 0))],
        [jax.ShapeDtypeStruct((nb, POOL_BUF * POOL_CH), F32),
         jax.ShapeDtypeStruct((nb, CONV_BUF * CONV_CH), F32)])
    kern = functools.partial(_front_sample_kernel, pos0=pos0, seqs=seqs, steps=nt)
    n_in = 3 + len(weights)
    return pl.pallas_call(
        kern,
        grid=(nb // seqs,),
        in_specs=[pl.BlockSpec((seqs, nt * D_MODEL), lambda i: (i, 0)),
                  pl.BlockSpec((seqs, POOL_BUF * POOL_CH), lambda i: (i, 0)),
                  pl.BlockSpec((seqs, CONV_BUF * CONV_CH), lambda i: (i, 0))]
                 + _front_weight_specs() + [pl.BlockSpec(memory_space=pl.ANY)],
        out_specs=out_specs,
        out_shape=out_shape,
        scratch_shapes=[pltpu.VMEM((TILE, D_MODEL), F32),
                        pltpu.VMEM((TILE, CONV_CH), F32),
                        pltpu.VMEM((TILE, D_MODEL), BF16)],
        input_output_aliases={n_in: 4},
        compiler_params=pltpu.CompilerParams(
            dimension_semantics=("arbitrary",), vmem_limit_bytes=VMEM_LIMIT_BYTES),
        name="front_sample",
    )(x.reshape(nb, nt * D_MODEL), state_pool.reshape(nb, POOL_BUF * POOL_CH),
      state_conv.reshape(nb, CONV_BUF * CONV_CH), *weights, xs_all)


def _moe_kernel(be_ref, nused_ref, src_ref, dst_ref, first_ref, wslot_ref, nxt_ref,
                xs_hbm, wgu_hbm, bgu_ref, wd_hbm, bd_ref, ys_hbm,
                xa, xb, oa, ob, wgu_f32, wd_f32, wgu_bf, wd_bf,
                gsem_a, gsem_b, ssem_a, ssem_b, wsem):
    i = pl.program_id(0)
    n_used = nused_ref[0]
    par = i % 2
    half_a = 2 * i
    half_b = 2 * i + 1

    def weight_copies(e, s):
        return (pltpu.make_async_copy(wgu_hbm.at[e], wgu_f32.at[s], wsem.at[s]),
                pltpu.make_async_copy(wd_hbm.at[e], wd_f32.at[s], wsem.at[s]))

    def gather(half, buf, sem):
        for g in range(HALF_GROUPS):
            hrow = pl.multiple_of(src_ref[half * HALF_GROUPS + g], SUBLANES)
            pltpu.make_async_copy(xs_hbm.at[pl.ds(hrow, SUBLANES)],
                                  buf.at[pl.ds(g * SUBLANES, SUBLANES)], sem).start()

    def gather_wait(buf, sem):
        pltpu.make_async_copy(xs_hbm.at[pl.ds(0, HALF_ROWS)], buf, sem).wait()

    def scatter(half, buf, sem):
        for g in range(HALF_GROUPS):
            hrow = pl.multiple_of(dst_ref[half * HALF_GROUPS + g], SUBLANES)
            pltpu.make_async_copy(buf.at[pl.ds(g * SUBLANES, SUBLANES)],
                                  ys_hbm.at[pl.ds(hrow, SUBLANES)], sem).start()

    def scatter_wait(buf, sem):
        pltpu.make_async_copy(buf, ys_hbm.at[pl.ds(0, HALF_ROWS)], sem).wait()

    def experts(xbuf, obuf):
        x = _unpack_halves(xbuf[par]).astype(BF16)
        gu = jnp.dot(x, wgu_bf[...], preferred_element_type=F32) + bgu_ref[0]
        g = jnp.minimum(gu[:, :D_FF], SWIGLU_LIMIT)
        up = jnp.clip(gu[:, D_FF:], -SWIGLU_LIMIT, SWIGLU_LIMIT)
        hdn = (up + 1.0) * (g * _sigmoid(SWIGLU_ALPHA * g))
        y = jnp.dot(hdn.astype(BF16), wd_bf[...], preferred_element_type=F32) + bd_ref[0]
        obuf[par] = _pack_halves(y.astype(BF16).astype(F32))

    @pl.when(i == 0)
    def _():
        gather(0, xa.at[0], gsem_a.at[0])
        gather(1, xb.at[0], gsem_b.at[0])
        for c in weight_copies(be_ref[0], 0):
            c.start(priority=WEIGHT_DMA_PRIORITY)

    @pl.when(i < n_used)
    def _():
        @pl.when(first_ref[i] == 1)
        def _():
            s = wslot_ref[i]
            for c in weight_copies(be_ref[i], s):
                c.wait()
            wgu_bf[...] = wgu_f32[s].astype(BF16)
            wd_bf[...] = wd_f32[s].astype(BF16)

            @pl.when(nxt_ref[i] < N_EXPERTS)
            def _():
                for c in weight_copies(nxt_ref[i], 1 - s):
                    c.start(priority=WEIGHT_DMA_PRIORITY)

        @pl.when(i >= 2)
        def _():
            scatter_wait(oa.at[par], ssem_a.at[par])
            scatter_wait(ob.at[par], ssem_b.at[par])

        nxt_blk = jnp.minimum(i + 1, n_used - 1)
        gather(2 * nxt_blk, xa.at[1 - par], gsem_a.at[1 - par])
        gather(2 * nxt_blk + 1, xb.at[1 - par], gsem_b.at[1 - par])
        gather_wait(xa.at[par], gsem_a.at[par])
        gather_wait(xb.at[par], gsem_b.at[par])
        experts(xa, oa)
        scatter(half_a, oa.at[par], ssem_a.at[par])
        experts(xb, ob)
        scatter(half_b, ob.at[par], ssem_b.at[par])

        @pl.when(i == n_used - 1)
        def _():
            scatter_wait(oa.at[par], ssem_a.at[par])
            scatter_wait(ob.at[par], ssem_b.at[par])
            gather_wait(xa.at[1 - par], gsem_a.at[1 - par])
            gather_wait(xb.at[1 - par], gsem_b.at[1 - par])

            @pl.when(i >= 1)
            def _():
                scatter_wait(oa.at[1 - par], ssem_a.at[1 - par])
                scatter_wait(ob.at[1 - par], ssem_b.at[1 - par])


def _moe(xs_all, tables, w_gate_up, b_gate_up, w_down, b_down):
    block_e, n_used, src, dst, first, wslot, nxt = tables
    n_blocks = block_e.shape[0]

    def e_map(i, be, nu, *_):
        return (be[jnp.minimum(i, nu[0] - 1)], 0, 0)

    half_buf = pltpu.VMEM((2, HALF_ROWS, PACKED), U32)
    any_spec = pl.BlockSpec(memory_space=pl.ANY)
    grid_spec = pltpu.PrefetchScalarGridSpec(
        num_scalar_prefetch=7,
        grid=(n_blocks,),
        in_specs=[any_spec,
                  any_spec,
                  pl.BlockSpec((1, 1, 2 * D_FF), e_map),
                  any_spec,
                  pl.BlockSpec((1, 1, D_MODEL), e_map)],
        out_specs=any_spec,
        scratch_shapes=[half_buf, half_buf, half_buf, half_buf,
                        pltpu.VMEM((2, D_MODEL, 2 * D_FF), F32),
                        pltpu.VMEM((2, D_FF, D_MODEL), F32),
                        pltpu.VMEM((D_MODEL, 2 * D_FF), BF16),
                        pltpu.VMEM((D_FF, D_MODEL), BF16)]
                       + [pltpu.SemaphoreType.DMA((2,))] * 5,
    )
    return pl.pallas_call(
        _moe_kernel,
        grid_spec=grid_spec,
        out_shape=jax.ShapeDtypeStruct((xs_all.shape[0] + 2 * MOE_ROWS, PACKED), U32),
        compiler_params=pltpu.CompilerParams(
            dimension_semantics=("arbitrary",), vmem_limit_bytes=VMEM_LIMIT_BYTES),
        name="moe_experts",
    )(block_e, n_used, src, dst, first, wslot, nxt,
      xs_all, w_gate_up, b_gate_up.reshape(N_EXPERTS, 1, 2 * D_FF), w_down,
      b_down.reshape(N_EXPERTS, 1, D_MODEL))


def _final_kernel(used_ref, h_ref, ys_ref, q_ref, gate_ref, g_ref, o_ref, *, tile0, slabs):
    i = pl.program_id(0)
    used = used_ref[tile0 + i]

    def to_cols(a):
        pad = jnp.zeros((LANES - SUBLANES, TILE), F32)
        return jnp.concatenate([a, pad], axis=0).T

    qc = to_cols(q_ref[...].astype(F32))
    gc = to_cols(gate_ref[...])
    acc = h_ref[...]
    for c in range(SORTED_ROWS // SORT_CHUNK):
        jl = (c * SORT_CHUNK + lax.broadcasted_iota(I32, (TILE, SORT_CHUNK), 1)).astype(F32)
        pg = sum(jnp.where(qc[:, k:k + 1] == jl, gc[:, k:k + 1], 0.0) for k in range(TOP_K))
        rid = c * SORT_CHUNK + lax.broadcasted_iota(I32, (SORT_CHUNK, 1), 0)
        ys = jnp.where(rid < used, _unpack_halves(ys_ref[pl.ds(c * SORT_CHUNK, SORT_CHUNK), :]), 0.0)
        acc = acc + jnp.dot(pg.astype(BF16), ys.astype(BF16), preferred_element_type=F32)
    y = _rmsnorm(acc, g_ref[...])
    if slabs == 1:
        o_ref[...] = y
    else:
        seqs = TILE // slabs
        for t in range(slabs):
            o_ref[:, t * D_MODEL:(t + 1) * D_MODEL] = y[t * seqs:(t + 1) * seqs]


def _final(h, ys_all, q, gates, used, g_final, *, tile0, slabs):
    n_tok = h.shape[0]
    if slabs == 1:
        out_spec = pl.BlockSpec((TILE, D_MODEL), lambda i, u: (i, 0))
        out_shape = jax.ShapeDtypeStruct((n_tok, D_MODEL), F32)
    else:
        seqs = TILE // slabs
        out_spec = pl.BlockSpec((seqs, slabs * D_MODEL), lambda i, u: (i, 0))
        out_shape = jax.ShapeDtypeStruct((n_tok // slabs, slabs * D_MODEL), F32)
    grid_spec = pltpu.PrefetchScalarGridSpec(
        num_scalar_prefetch=1,
        grid=(n_tok // TILE,),
        in_specs=[pl.BlockSpec((TILE, D_MODEL), lambda i, u: (i, 0)),
                  pl.BlockSpec((SORTED_ROWS, PACKED), lambda i, u: (tile0 + i, 0)),
                  pl.BlockSpec((SUBLANES, TILE), lambda i, u: (0, i)),
                  pl.BlockSpec((SUBLANES, TILE), lambda i, u: (0, i)),
                  pl.BlockSpec((1, D_MODEL), lambda i, u: (0, 0))],
        out_specs=out_spec,
    )
    return pl.pallas_call(
        functools.partial(_final_kernel, tile0=tile0, slabs=slabs),
        grid_spec=grid_spec,
        out_shape=out_shape,
        compiler_params=pltpu.CompilerParams(
            dimension_semantics=("arbitrary",), vmem_limit_bytes=VMEM_LIMIT_BYTES),
        name="final_norm",
    )(used, h, ys_all, q, gates, g_final)


def _block_tables(tbl, n_blocks):
    cnt = tbl[:, 0, :, 0]
    off = tbl[:, 1, :, 0]
    run = jnp.cumsum(cnt, axis=0) - cnt
    total = jnp.sum(cnt, axis=0)
    nblk = (total + MOE_ROWS - 1) // MOE_ROWS
    blk_end = jnp.cumsum(nblk)
    blk_start = blk_end - nblk
    bidx = jnp.arange(n_blocks, dtype=I32)
    eidx = jnp.arange(N_EXPERTS, dtype=I32)
    block_e = jnp.minimum(jnp.sum(blk_end[None, :] <= bidx[:, None], axis=1), N_EXPERTS - 1).astype(I32)
    hot = block_e[:, None] == eidx[None, :]

    def per_block(v):
        return jnp.sum(jnp.where(hot, v[..., None, :], 0), axis=-1)

    row0 = (bidx - per_block(blk_start)) * MOE_ROWS
    n_tiles = cnt.shape[0]
    groups = jnp.arange(MOE_GROUPS, dtype=I32)
    erow = row0[:, None] + groups[None, :] * SUBLANES
    run_b, cnt_b, off_b = per_block(run), per_block(cnt), per_block(off)
    inside = jnp.logical_and(run_b[:, :, None] <= erow[None], erow[None] < (run_b + cnt_b)[:, :, None])
    tile_base = jnp.arange(n_tiles, dtype=I32)[:, None] * SORTED_ROWS + off_b - run_b
    src = jnp.sum(jnp.where(inside, tile_base[:, :, None] + erow[None], 0), axis=0)
    dump = n_tiles * SORTED_ROWS + (bidx % 2)[:, None] * MOE_ROWS + groups[None, :] * SUBLANES
    dst = jnp.where(erow < per_block(total)[:, None], src, dump).reshape(-1)
    src = src.reshape(-1)
    used = jnp.sum(cnt, axis=1)
    has_rows = nblk > 0
    first = (bidx == per_block(blk_start)).astype(I32)
    wslot = per_block((jnp.cumsum(has_rows) - 1) % 2)
    later = jnp.where(jnp.logical_and(has_rows[None, :], eidx[None, :] > eidx[:, None]), eidx[None, :], N_EXPERTS)
    nxt = per_block(jnp.min(later, axis=1))
    return ((block_e, blk_end[-1:].astype(I32), src.astype(I32), dst.astype(I32), first,
             wslot.astype(I32), nxt.astype(I32)), used.astype(I32))


def kernel(x_prompt, x_sample, state_pool, state_conv, g_mix, w_in, pool_w, pool_scale, conv_w, conv_b,
           conv_ln_g, conv_ln_b, w_out, g_moe, w_router, b_router, w_gate_up, b_gate_up, w_down, b_down,
           g_final):
    nb_p, nt_p, _ = x_prompt.shape
    nb_s, nt_s, _ = x_sample.shape
    n_p, n_s = nb_p * nt_p, nb_s * nt_s
    past_len = 16384
    assert nt_p % TILE == 0 and TILE % nt_s == 0 and n_s % TILE == 0
    tiles_p, tiles_s = n_p // TILE, n_s // TILE
    n_tiles = tiles_p + tiles_s

    tri = (lax.broadcasted_iota(I32, (TILE, TILE), 0) < lax.broadcasted_iota(I32, (TILE, TILE), 1)).astype(BF16)
    weights = (g_mix[0][None], w_in[0].astype(BF16), pool_w[0].astype(BF16), pool_scale[0][None],
               conv_w[0], conv_b[0][None], conv_ln_g[0][None], conv_ln_b[0][None],
               w_out[0].astype(BF16), g_moe[0][None], w_router[0].T.astype(BF16),
               b_router[0][:, None], tri)

    zero_pool = jnp.zeros((1, nb_p, POOL_BUF, POOL_CH), F32)
    zero_conv = jnp.zeros((1, nb_p, CONV_BUF, CONV_CH), F32)
    h_p, q_p, gate_p, tbl_p, xs_all, pool_p, conv_p = _front_prompt(
        x_prompt, zero_pool, zero_conv, weights, pos0=0, n_tiles_all=n_tiles)
    h_s, q_s, gate_s, tbl_s, xs_all, pool_s, conv_s = _front_sample(
        x_sample, state_pool[0], state_conv[0], weights, xs_all, pos0=past_len, tile0=tiles_p)

    sorted_total = n_tiles * (TILE * TOP_K + N_EXPERTS * (SUBLANES - 1))
    n_blocks = -(-sorted_total // MOE_ROWS) + N_EXPERTS
    tables, used = _block_tables(jnp.concatenate([tbl_p, tbl_s], axis=0), n_blocks)
    ys_all = _moe(xs_all, tables, w_gate_up[0], b_gate_up[0], w_down[0], b_down[0])

    gf = g_final[None]
    y_prompt = _final(h_p, ys_all, q_p, gate_p, used, gf, tile0=0, slabs=1)
    y_sample = _final(h_s, ys_all, q_s, gate_s, used, gf, tile0=tiles_p, slabs=nt_s)
    return (y_prompt.reshape(nb_p, nt_p, D_MODEL), y_sample.reshape(nb_s, nt_s, D_MODEL),
            pool_p, conv_p,
            pool_s.reshape(1, nb_s, POOL_BUF, POOL_CH), conv_s.reshape(1, nb_s, CONV_BUF, CONV_CH))
```

```python
import functools

import jax
import jax.numpy as jnp
from jax import lax
from jax.experimental import pallas as pl
from jax.experimental.pallas import tpu as pltpu

D_MODEL = 1024
POOL_CH = 512
CONV_CH = 512
POOL_WINDOWS = (2, 4, 8, 16)
POOL_GROUP = 128
POOL_BUF = 15
CONV_TAPS = 31
CONV_BUF = 30
IN_COLS = POOL_CH + 2 * CONV_CH
N_EXPERTS = 32
TOP_K = 4
D_FF = 1024
SWIGLU_LIMIT = 7.0
SWIGLU_ALPHA = 1.702
RMS_EPS = 1e-5
LN_EPS = 1e-5

LANES = 128
SUBLANES = 8
VMEM_LIMIT_BYTES = 60 * 1024 * 1024

TILE = 512
SUB_TOKENS = 256
SUBTILES = TILE // SUB_TOKENS
SAMPLE_SEQS = 64
CONV_CHUNK = 32
SORT_CHUNK = 256
FRONT_SORT_CHUNK = 640
SUB_SORTED = -(-(SUB_TOKENS * TOP_K + N_EXPERTS * (SUBLANES - 1)) // SORT_CHUNK) * SORT_CHUNK
SORTED_ROWS = SUBTILES * SUB_SORTED
MOE_ROWS = 512
HALF_ROWS = MOE_ROWS // 2
MOE_GROUPS = MOE_ROWS // SUBLANES
HALF_GROUPS = MOE_GROUPS // 2
WEIGHT_DMA_PRIORITY = 1

POOL_HIST = 24
CONV_HIST = 32

BF16 = jnp.bfloat16
F32 = jnp.float32
I32 = jnp.int32
U32 = jnp.uint32
PACKED = D_MODEL // 2


def _rmsnorm(x, g):
    ms = jnp.mean(x * x, axis=-1, keepdims=True)
    return x * lax.rsqrt(ms + RMS_EPS) * g


def _sigmoid(x):
    return 1.0 / (1.0 + jnp.exp(-x))


def _layernorm_silu(d, g, b):
    mu = jnp.mean(d, axis=-1, keepdims=True)
    c = d - mu
    var = jnp.mean(c * c, axis=-1, keepdims=True)
    n = c * lax.rsqrt(var + LN_EPS) * g + b
    return n * _sigmoid(n)


def _pack_halves(a):
    half = a.shape[1] // 2
    hi = lax.bitcast_convert_type(a[:, :half], U32)
    lo = lax.bitcast_convert_type(a[:, half:], U32)
    return hi | (lo >> 16)


def _unpack_halves(w):
    hi = lax.bitcast_convert_type(w & jnp.uint32(0xFFFF0000), F32)
    lo = lax.bitcast_convert_type(w << 16, F32)
    return jnp.concatenate([hi, lo], axis=1)


def _route_and_sort(hn_bf, wr_t_ref, br_ref, tri_ref, q_ref, gate_ref, tbl_ref, xs_ref):
    rows = hn_bf.shape[0]
    lt = lax.dot_general(wr_t_ref[...], hn_bf, (((1,), (1,)), ((), ())),
                         preferred_element_type=F32)
    lt = lt + br_ref[...]
    eidx = lax.broadcasted_iota(I32, (N_EXPERTS, rows), 0)
    vals, hots = [], []
    for _ in range(TOP_K):
        m = jnp.max(lt, axis=0, keepdims=True)
        sel = jnp.min(jnp.where(lt == m, eidx, N_EXPERTS), axis=0, keepdims=True)
        hit = eidx == sel
        vals.append(m)
        hots.append(hit)
        lt = jnp.where(hit, -jnp.inf, lt)
    v = jnp.concatenate(vals, axis=0)
    e = jnp.exp(v - vals[0])
    gate = e / jnp.sum(e, axis=0, keepdims=True)
    gate_ref[...] = jnp.concatenate([gate, jnp.zeros((SUBLANES - TOP_K, rows), F32)], axis=0)

    sel_cnt = sum(h.astype(F32) for h in hots)
    lower = (lax.broadcasted_iota(I32, (N_EXPERTS, N_EXPERTS), 1)
             < lax.broadcasted_iota(I32, (N_EXPERTS, N_EXPERTS), 0)).astype(BF16)
    q_rows = []
    for s in range(SUBTILES):
        ts = slice(s * SUB_TOKENS, (s + 1) * SUB_TOKENS)
        sel_s = sel_cnt[:, ts]
        prefix = jnp.dot(sel_s.astype(BF16), tri_ref[...], preferred_element_type=F32)
        cnt = jnp.sum(sel_s, axis=1, keepdims=True)
        cnt_u = jnp.floor((cnt + (SUBLANES - 1)) * (1.0 / SUBLANES))
        cnt_u_l = jnp.broadcast_to(cnt_u, (N_EXPERTS, LANES))
        off_u_l = jnp.dot(lower, cnt_u_l.astype(BF16), preferred_element_type=F32)
        off_l = off_u_l * float(SUBLANES)
        tbl_ref[0, s, 0] = (cnt_u_l * float(SUBLANES)).astype(I32)
        tbl_ref[0, s, 1] = off_l.astype(I32)

        pos = prefix + off_l[:, :1]
        qs = [jnp.sum(jnp.where(h[:, ts], pos, 0.0), axis=0, keepdims=True).astype(I32) for h in hots]
        q_rows.append(jnp.concatenate(qs + [jnp.zeros((SUBLANES - TOP_K, SUB_TOKENS), I32)], axis=0))

        hn_s = hn_bf[ts]
        for c in range(SUB_SORTED // FRONT_SORT_CHUNK):
            r0 = c * FRONT_SORT_CHUNK
            jidx = r0 + lax.broadcasted_iota(I32, (FRONT_SORT_CHUNK, SUB_TOKENS), 0)
            p = sum(jnp.where(jidx == qk, 1.0, 0.0) for qk in qs)
            xs = jnp.dot(p.astype(BF16), hn_s, preferred_element_type=F32)
            xs_ref[pl.ds(s * SUB_SORTED + r0, FRONT_SORT_CHUNK), :] = _pack_halves(xs)
    q_ref[...] = jnp.concatenate(q_rows, axis=1)


def _mix_out_and_route(x, mixbuf, w_out_ref, g_moe_ref, wr_t_ref, br_ref, tri_ref,
                       h_ref, q_ref, gate_ref, tbl_ref, xs_ref):
    h = x + jnp.dot(mixbuf[...], w_out_ref[...], preferred_element_type=F32)
    h_ref[...] = h
    hn = _rmsnorm(h, g_moe_ref[...])
    _route_and_sort(hn.astype(BF16), wr_t_ref, br_ref, tri_ref, q_ref, gate_ref, tbl_ref, xs_ref)


def _front_prompt_kernel(x_ref, sp_ref, sc_ref, g_mix_ref, w_in_ref, pool_w_ref, pool_scale_ref,
                         conv_w_ref, conv_b_ref, ln_g_ref, ln_b_ref, w_out_ref, g_moe_ref,
                         wr_t_ref, br_ref, tri_ref,
                         h_ref, q_ref, gate_ref, tbl_ref, xs_ref, np_ref, nc_ref,
                         ubuf, gsh, dbuf, mixbuf, *, pos0, rows):
    t = pl.program_id(1)

    @pl.when(t == 0)
    def _():
        ubuf[:, pl.ds(0, POOL_HIST), :] = jnp.zeros((len(POOL_WINDOWS), POOL_HIST, POOL_CH), F32)
        ubuf[0, pl.ds(POOL_HIST - POOL_BUF, POOL_BUF), :] = sp_ref[0, 0]
        gsh[0, pl.ds(CONV_HIST - CONV_BUF, CONV_BUF), :] = sc_ref[0, 0]

    x = x_ref[0]
    xn = _rmsnorm(x, g_mix_ref[...]).astype(BF16)
    proj = jnp.dot(xn, w_in_ref[...], preferred_element_type=F32)

    u = proj[:, :POOL_CH]
    ubuf[0, pl.ds(POOL_HIST, rows), :] = u
    pos = pos0 + t * rows + lax.broadcasted_iota(I32, (rows, 1), 0)
    span = rows + POOL_HIST - SUBLANES
    for g, w in enumerate(POOL_WINDOWS):
        assert w == 2 ** (g + 1) and w // 2 <= SUBLANES
        c0 = g * POOL_GROUP
        cs = slice(c0, c0 + POOL_GROUP)
        a = ubuf[g, pl.ds(SUBLANES, span), c0:] + ubuf[g, pl.ds(SUBLANES - w // 2, span), c0:]
        if g + 1 < len(POOL_WINDOWS):
            ubuf[g + 1, pl.ds(SUBLANES, span), c0 + POOL_GROUP:] = a[:, POOL_GROUP:]
        s = a[span - rows:, :POOL_GROUP]
        cnt = jnp.minimum(w, pos + 1).astype(F32)
        z = (s / cnt - u[:, cs]).astype(BF16)
        yg = jnp.dot(z, pool_w_ref[g], preferred_element_type=F32)
        mixbuf[:, cs] = (yg * pool_scale_ref[:, cs]).astype(BF16)

    glu = proj[:, POOL_CH:POOL_CH + CONV_CH] * _sigmoid(proj[:, POOL_CH + CONV_CH:])
    gsh[0, pl.ds(CONV_HIST, rows), :] = glu
    for s in range(1, SUBLANES):
        gsh[s, pl.ds(0, CONV_HIST - s), :] = gsh[0, pl.ds(s, CONV_HIST - s), :]
        gsh[s, pl.ds(CONV_HIST - s, rows), :] = glu

    def conv_chunk(i, carry):
        r0 = pl.multiple_of(i * CONV_CHUNK, CONV_CHUNK)
        for ct in range(CONV_CH // LANES):
            cs = slice(ct * LANES, (ct + 1) * LANES)
            acc = jnp.broadcast_to(conv_b_ref[:, cs], (CONV_CHUNK, LANES))
            for k in range(CONV_TAPS):
                first = CONV_HIST - CONV_BUF + k
                s = first % SUBLANES
                acc = acc + conv_w_ref[k:k + 1, cs] * gsh[s, pl.ds(r0 + (first - s), CONV_CHUNK), cs]
            dbuf[pl.ds(r0, CONV_CHUNK), cs] = acc
        return carry

    lax.fori_loop(0, rows // CONV_CHUNK, conv_chunk, 0)
    y_conv = _layernorm_silu(dbuf[...], ln_g_ref[...], ln_b_ref[...])
    mixbuf[:, POOL_CH:] = y_conv.astype(BF16)

    new_pool = ubuf[0, pl.ds(rows + POOL_HIST - POOL_BUF, POOL_BUF), :]
    new_conv = gsh[0, pl.ds(rows + CONV_HIST - CONV_BUF, CONV_BUF), :]
    np_ref[0, 0] = new_pool
    nc_ref[0, 0] = new_conv
    ubuf[0, pl.ds(POOL_HIST - POOL_BUF, POOL_BUF), :] = new_pool
    gsh[0, pl.ds(CONV_HIST - CONV_BUF, CONV_BUF), :] = new_conv

    _mix_out_and_route(x, mixbuf, w_out_ref, g_moe_ref, wr_t_ref, br_ref, tri_ref,
                       h_ref, q_ref, gate_ref, tbl_ref, xs_ref)


def _front_sample_kernel(x_ref, sp_ref, sc_ref, g_mix_ref, w_in_ref, pool_w_ref, pool_scale_ref,
                         conv_w_ref, conv_b_ref, ln_g_ref, ln_b_ref, w_out_ref, g_moe_ref,
                         wr_t_ref, br_ref, tri_ref, xs_in,
                         h_ref, q_ref, gate_ref, tbl_ref, xs_ref, np_ref, nc_ref,
                         xbuf, dbuf, mixbuf, *, pos0, seqs, steps):
    del xs_in
    for t in range(steps):
        xbuf[pl.ds(t * seqs, seqs), :] = x_ref[:, t * D_MODEL:(t + 1) * D_MODEL]
    x = xbuf[...]
    xn = _rmsnorm(x, g_mix_ref[...]).astype(BF16)
    proj = jnp.dot(xn, w_in_ref[...], preferred_element_type=F32)

    def slab(a, t):
        return a[t * seqs:(t + 1) * seqs]

    u = proj[:, :POOL_CH]
    cat = [sp_ref[:, j * POOL_CH:(j + 1) * POOL_CH] for j in range(POOL_BUF)]
    cat += [slab(u, t) for t in range(steps)]
    for t in range(steps):
        for g, w in enumerate(POOL_WINDOWS):
            cs = slice(g * POOL_GROUP, (g + 1) * POOL_GROUP)
            s = cat[POOL_BUF + t][:, cs]
            for j in range(1, w):
                s = s + cat[POOL_BUF + t - j][:, cs]
            cnt = float(min(w, pos0 + t + 1))
            z = (s / cnt - cat[POOL_BUF + t][:, cs]).astype(BF16)
            yg = jnp.dot(z, pool_w_ref[g], preferred_element_type=F32)
            mixbuf[pl.ds(t * seqs, seqs), cs] = (yg * pool_scale_ref[:, cs]).astype(BF16)
    for j in range(POOL_BUF):
        np_ref[:, j * POOL_CH:(j + 1) * POOL_CH] = cat[len(cat) - POOL_BUF + j]

    glu = proj[:, POOL_CH:POOL_CH + CONV_CH] * _sigmoid(proj[:, POOL_CH + CONV_CH:])
    ccat = [sc_ref[:, j * CONV_CH:(j + 1) * CONV_CH] for j in range(CONV_BUF)]
    ccat += [slab(glu, t) for t in range(steps)]
    for t in range(steps):
        acc = jnp.broadcast_to(conv_b_ref[...], (seqs, CONV_CH))
        for k in range(CONV_TAPS):
            acc = acc + conv_w_ref[k:k + 1, :] * ccat[t + k]
        dbuf[pl.ds(t * seqs, seqs), :] = acc
    for j in range(CONV_BUF):
        nc_ref[:, j * CONV_CH:(j + 1) * CONV_CH] = ccat[len(ccat) - CONV_BUF + j]
    y_conv = _layernorm_silu(dbuf[...], ln_g_ref[...], ln_b_ref[...])
    mixbuf[:, POOL_CH:] = y_conv.astype(BF16)

    _mix_out_and_route(x, mixbuf, w_out_ref, g_moe_ref, wr_t_ref, br_ref, tri_ref,
                       h_ref, q_ref, gate_ref, tbl_ref, xs_ref)


def _const_spec(shape, single_buffer=False):
    mode = pl.Buffered(1) if single_buffer else None
    return pl.BlockSpec(shape, lambda *_: (0,) * len(shape), pipeline_mode=mode)


def _front_weight_specs():
    return [
        _const_spec((1, D_MODEL)),
        _const_spec((D_MODEL, IN_COLS), True),
        _const_spec((len(POOL_WINDOWS), POOL_GROUP, POOL_GROUP)),
        _const_spec((1, POOL_CH)),
        _const_spec((CONV_TAPS, CONV_CH)),
        _const_spec((1, CONV_CH)),
        _const_spec((1, CONV_CH)),
        _const_spec((1, CONV_CH)),
        _const_spec((D_MODEL, D_MODEL), True),
        _const_spec((1, D_MODEL)),
        _const_spec((N_EXPERTS, D_MODEL)),
        _const_spec((N_EXPERTS, 1)),
        _const_spec((SUB_TOKENS, SUB_TOKENS)),
    ]


def _front_out(n_tok, n_tiles_all, tile_map, tok_map, extra_specs, extra_shapes):
    specs = [pl.BlockSpec((TILE, D_MODEL), tok_map),
             pl.BlockSpec((SUBLANES, TILE), lambda *i: (0, tok_map(*i)[0])),
             pl.BlockSpec((SUBLANES, TILE), lambda *i: (0, tok_map(*i)[0])),
             pl.BlockSpec((1, SUBTILES, 2, N_EXPERTS, LANES),
                          lambda *i: (tok_map(*i)[0], 0, 0, 0, 0)),
             pl.BlockSpec((SORTED_ROWS, PACKED), tile_map)]
    shapes = [jax.ShapeDtypeStruct((n_tok, D_MODEL), F32),
              jax.ShapeDtypeStruct((SUBLANES, n_tok), I32),
              jax.ShapeDtypeStruct((SUBLANES, n_tok), F32),
              jax.ShapeDtypeStruct((n_tok // TILE, SUBTILES, 2, N_EXPERTS, LANES), I32),
              jax.ShapeDtypeStruct((n_tiles_all * SORTED_ROWS, PACKED), U32)]
    return specs + extra_specs, shapes + extra_shapes


def _front_prompt(x, state_pool, state_conv, weights, *, pos0, n_tiles_all):
    nb, nt, _ = x.shape
    rows = TILE
    tiles = nt // rows
    tok_map = lambda b, t: (b * tiles + t, 0)
    out_specs, out_shape = _front_out(
        nb * nt, n_tiles_all, tok_map, tok_map,
        [pl.BlockSpec((1, 1, POOL_BUF, POOL_CH), lambda b, t: (0, b, 0, 0)),
         pl.BlockSpec((1, 1, CONV_BUF, CONV_CH), lambda b, t: (0, b, 0, 0))],
        [jax.ShapeDtypeStruct((1, nb, POOL_BUF, POOL_CH), F32),
         jax.ShapeDtypeStruct((1, nb, CONV_BUF, CONV_CH), F32)])
    kern = functools.partial(_front_prompt_kernel, pos0=pos0, rows=rows)
    return pl.pallas_call(
        kern,
        grid=(nb, tiles),
        in_specs=[pl.BlockSpec((1, rows, D_MODEL), lambda b, t: (b, t, 0)),
                  pl.BlockSpec((1, 1, POOL_BUF, POOL_CH), lambda b, t: (0, b, 0, 0)),
                  pl.BlockSpec((1, 1, CONV_BUF, CONV_CH), lambda b, t: (0, b, 0, 0))]
                 + _front_weight_specs(),
        out_specs=out_specs,
        out_shape=out_shape,
        scratch_shapes=[pltpu.VMEM((len(POOL_WINDOWS), POOL_HIST + rows, POOL_CH), F32),
                        pltpu.VMEM((SUBLANES, CONV_HIST + rows, CONV_CH), F32),
                        pltpu.VMEM((rows, CONV_CH), F32),
                        pltpu.VMEM((rows, D_MODEL), BF16)],
        compiler_params=pltpu.CompilerParams(
            dimension_semantics=("arbitrary", "arbitrary"), vmem_limit_bytes=VMEM_LIMIT_BYTES),
        name="front_prompt",
    )(x, state_pool, state_conv, *weights)


def _front_sample(x, state_pool, state_conv, weights, xs_all, *, pos0, tile0):
    nb, nt, _ = x.shape
    seqs = TILE // nt
    n_tiles_all = xs_all.shape[0] // SORTED_ROWS
    tok_map = lambda i: (i, 0)
    out_specs, out_shape = _front_out(
        nb * nt, n_tiles_all, lambda i: (tile0 + i, 0), tok_map,
        [pl.BlockSpec((seqs, POOL_BUF * POOL_CH), lambda i: (i, 0)),
         pl.BlockSpec((seqs, CONV_BUF * CONV_CH), lambda i: (i, 0))],
        [jax.ShapeDtypeStruct((nb, POOL_BUF * POOL_CH), F32),
         jax.ShapeDtypeStruct((nb, CONV_BUF * CONV_CH), F32)])
    kern = functools.partial(_front_sample_kernel, pos0=pos0, seqs=seqs, steps=nt)
    n_in = 3 + len(weights)
    return pl.pallas_call(
        kern,
        grid=(nb // seqs,),
        in_specs=[pl.BlockSpec((seqs, nt * D_MODEL), lambda i: (i, 0)),
                  pl.BlockSpec((seqs, POOL_BUF * POOL_CH), lambda i: (i, 0)),
                  pl.BlockSpec((seqs, CONV_BUF * CONV_CH), lambda i: (i, 0))]
                 + _front_weight_specs() + [pl.BlockSpec(memory_space=pl.ANY)],
        out_specs=out_specs,
        out_shape=out_shape,
        scratch_shapes=[pltpu.VMEM((TILE, D_MODEL), F32),
                        pltpu.VMEM((TILE, CONV_CH), F32),
                        pltpu.VMEM((TILE, D_MODEL), BF16)],
        input_output_aliases={n_in: 4},
        compiler_params=pltpu.CompilerParams(
            dimension_semantics=("arbitrary",), vmem_limit_bytes=VMEM_LIMIT_BYTES),
        name="front_sample",
    )(x.reshape(nb, nt * D_MODEL), state_pool.reshape(nb, POOL_BUF * POOL_CH),
      state_conv.reshape(nb, CONV_BUF * CONV_CH), *weights, xs_all)


def _moe_kernel(be_ref, nused_ref, src_ref, dst_ref, first_ref, wslot_ref, nxt_ref,
                xs_hbm, wgu_hbm, bgu_ref, wd_hbm, bd_ref, ys_hbm,
                xa, xb, oa, ob, wgu_f32, wd_f32, wgu_bf, wd_bf,
                gsem_a, gsem_b, ssem_a, ssem_b, wsem):
    i = pl.program_id(0)
    n_used = nused_ref[0]
    par = i % 2
    half_a = 2 * i
    half_b = 2 * i + 1

    def weight_copies(e, s):
        return (pltpu.make_async_copy(wgu_hbm.at[e], wgu_f32.at[s], wsem.at[s]),
                pltpu.make_async_copy(wd_hbm.at[e], wd_f32.at[s], wsem.at[s]))

    def gather(half, buf, sem):
        for g in range(HALF_GROUPS):
            hrow = pl.multiple_of(src_ref[half * HALF_GROUPS + g], SUBLANES)
            pltpu.make_async_copy(xs_hbm.at[pl.ds(hrow, SUBLANES)],
                                  buf.at[pl.ds(g * SUBLANES, SUBLANES)], sem).start()

    def gather_wait(buf, sem):
        pltpu.make_async_copy(xs_hbm.at[pl.ds(0, HALF_ROWS)], buf, sem).wait()

    def scatter(half, buf, sem):
        for g in range(HALF_GROUPS):
            hrow = pl.multiple_of(dst_ref[half * HALF_GROUPS + g], SUBLANES)
            pltpu.make_async_copy(buf.at[pl.ds(g * SUBLANES, SUBLANES)],
                                  ys_hbm.at[pl.ds(hrow, SUBLANES)], sem).start()

    def scatter_wait(buf, sem):
        pltpu.make_async_copy(buf, ys_hbm.at[pl.ds(0, HALF_ROWS)], sem).wait()

    def experts(xbuf, obuf):
        x = _unpack_halves(xbuf[par]).astype(BF16)
        gu = jnp.dot(x, wgu_bf[...], preferred_element_type=F32) + bgu_ref[0]
        g = jnp.minimum(gu[:, :D_FF], SWIGLU_LIMIT)
        up = jnp.clip(gu[:, D_FF:], -SWIGLU_LIMIT, SWIGLU_LIMIT)
        hdn = (up + 1.0) * (g * _sigmoid(SWIGLU_ALPHA * g))
        y = jnp.dot(hdn.astype(BF16), wd_bf[...], preferred_element_type=F32) + bd_ref[0]
        obuf[par] = _pack_halves(y.astype(BF16).astype(F32))

    @pl.when(i == 0)
    def _():
        gather(0, xa.at[0], gsem_a.at[0])
        gather(1, xb.at[0], gsem_b.at[0])
        for c in weight_copies(be_ref[0], 0):
            c.start(priority=WEIGHT_DMA_PRIORITY)

    @pl.when(i < n_used)
    def _():
        @pl.when(first_ref[i] == 1)
        def _():
            s = wslot_ref[i]
            for c in weight_copies(be_ref[i], s):
                c.wait()
            wgu_bf[...] = wgu_f32[s].astype(BF16)
            wd_bf[...] = wd_f32[s].astype(BF16)

            @pl.when(nxt_ref[i] < N_EXPERTS)
            def _():
                for c in weight_copies(nxt_ref[i], 1 - s):
                    c.start(priority=WEIGHT_DMA_PRIORITY)

        @pl.when(i >= 2)
        def _():
            scatter_wait(oa.at[par], ssem_a.at[par])
            scatter_wait(ob.at[par], ssem_b.at[par])

        nxt_blk = jnp.minimum(i + 1, n_used - 1)
        gather(2 * nxt_blk, xa.at[1 - par], gsem_a.at[1 - par])
        gather(2 * nxt_blk + 1, xb.at[1 - par], gsem_b.at[1 - par])
        gather_wait(xa.at[par], gsem_a.at[par])
        gather_wait(xb.at[par], gsem_b.at[par])
        experts(xa, oa)
        scatter(half_a, oa.at[par], ssem_a.at[par])
        experts(xb, ob)
        scatter(half_b, ob.at[par], ssem_b.at[par])

        @pl.when(i == n_used - 1)
        def _():
            scatter_wait(oa.at[par], ssem_a.at[par])
            scatter_wait(ob.at[par], ssem_b.at[par])
            gather_wait(xa.at[1 - par], gsem_a.at[1 - par])
            gather_wait(xb.at[1 - par], gsem_b.at[1 - par])

            @pl.when(i >= 1)
            def _():
                scatter_wait(oa.at[1 - par], ssem_a.at[1 - par])
                scatter_wait(ob.at[1 - par], ssem_b.at[1 - par])


def _moe(xs_all, tables, w_gate_up, b_gate_up, w_down, b_down):
    block_e, n_used, src, dst, first, wslot, nxt = tables
    n_blocks = block_e.shape[0]

    def e_map(i, be, nu, *_):
        return (be[jnp.minimum(i, nu[0] - 1)], 0, 0)

    half_buf = pltpu.VMEM((2, HALF_ROWS, PACKED), U32)
    any_spec = pl.BlockSpec(memory_space=pl.ANY)
    grid_spec = pltpu.PrefetchScalarGridSpec(
        num_scalar_prefetch=7,
        grid=(n_blocks,),
        in_specs=[any_spec,
                  any_spec,
                  pl.BlockSpec((1, 1, 2 * D_FF), e_map),
                  any_spec,
                  pl.BlockSpec((1, 1, D_MODEL), e_map)],
        out_specs=any_spec,
        scratch_shapes=[half_buf, half_buf, half_buf, half_buf,
                        pltpu.VMEM((2, D_MODEL, 2 * D_FF), F32),
                        pltpu.VMEM((2, D_FF, D_MODEL), F32),
                        pltpu.VMEM((D_MODEL, 2 * D_FF), BF16),
                        pltpu.VMEM((D_FF, D_MODEL), BF16)]
                       + [pltpu.SemaphoreType.DMA((2,))] * 5,
    )
    return pl.pallas_call(
        _moe_kernel,
        grid_spec=grid_spec,
        out_shape=jax.ShapeDtypeStruct((xs_all.shape[0] + 2 * MOE_ROWS, PACKED), U32),
        compiler_params=pltpu.CompilerParams(
            dimension_semantics=("arbitrary",), vmem_limit_bytes=VMEM_LIMIT_BYTES),
        name="moe_experts",
    )(block_e, n_used, src, dst, first, wslot, nxt,
      xs_all, w_gate_up, b_gate_up.reshape(N_EXPERTS, 1, 2 * D_FF), w_down,
      b_down.reshape(N_EXPERTS, 1, D_MODEL))


def _final_kernel(used_ref, h_ref, ys_ref, q_ref, gate_ref, g_ref, o_ref, *, tile0, slabs):
    i = pl.program_id(0)

    def to_cols(a):
        pad = jnp.zeros((LANES - SUBLANES, TILE), F32)
        return jnp.concatenate([a, pad], axis=0).T

    qc_all = to_cols(q_ref[...].astype(F32))
    gc_all = to_cols(gate_ref[...])
    parts = []
    for s in range(SUBTILES):
        ts = slice(s * SUB_TOKENS, (s + 1) * SUB_TOKENS)
        used = used_ref[(tile0 + i) * SUBTILES + s]
        qc, gc = qc_all[ts], gc_all[ts]
        acc = h_ref[ts, :]
        for c in range(SUB_SORTED // SORT_CHUNK):
            jl = (c * SORT_CHUNK + lax.broadcasted_iota(I32, (SUB_TOKENS, SORT_CHUNK), 1)).astype(F32)
            pg = sum(jnp.where(qc[:, k:k + 1] == jl, gc[:, k:k + 1], 0.0) for k in range(TOP_K))
            rid = c * SORT_CHUNK + lax.broadcasted_iota(I32, (SORT_CHUNK, 1), 0)
            rows = ys_ref[pl.ds(s * SUB_SORTED + c * SORT_CHUNK, SORT_CHUNK), :]
            ys = jnp.where(rid < used, _unpack_halves(rows), 0.0)
            acc = acc + jnp.dot(pg.astype(BF16), ys.astype(BF16), preferred_element_type=F32)
        parts.append(acc)
    y = _rmsnorm(jnp.concatenate(parts, axis=0), g_ref[...])
    if slabs == 1:
        o_ref[...] = y
    else:
        seqs = TILE // slabs
        for t in range(slabs):
            o_ref[:, t * D_MODEL:(t + 1) * D_MODEL] = y[t * seqs:(t + 1) * seqs]


def _final(h, ys_all, q, gates, used, g_final, *, tile0, slabs):
    n_tok = h.shape[0]
    if slabs == 1:
        out_spec = pl.BlockSpec((TILE, D_MODEL), lambda i, u: (i, 0))
        out_shape = jax.ShapeDtypeStruct((n_tok, D_MODEL), F32)
    else:
        seqs = TILE // slabs
        out_spec = pl.BlockSpec((seqs, slabs * D_MODEL), lambda i, u: (i, 0))
        out_shape = jax.ShapeDtypeStruct((n_tok // slabs, slabs * D_MODEL), F32)
    grid_spec = pltpu.PrefetchScalarGridSpec(
        num_scalar_prefetch=1,
        grid=(n_tok // TILE,),
        in_specs=[pl.BlockSpec((TILE, D_MODEL), lambda i, u: (i, 0)),
                  pl.BlockSpec((SORTED_ROWS, PACKED), lambda i, u: (tile0 + i, 0)),
                  pl.BlockSpec((SUBLANES, TILE), lambda i, u: (0, i)),
                  pl.BlockSpec((SUBLANES, TILE), lambda i, u: (0, i)),
                  pl.BlockSpec((1, D_MODEL), lambda i, u: (0, 0))],
        out_specs=out_spec,
    )
    return pl.pallas_call(
        functools.partial(_final_kernel, tile0=tile0, slabs=slabs),
        grid_spec=grid_spec,
        out_shape=out_shape,
        compiler_params=pltpu.CompilerParams(
            dimension_semantics=("arbitrary",), vmem_limit_bytes=VMEM_LIMIT_BYTES),
        name="final_norm",
    )(used, h, ys_all, q, gates, g_final)


def _block_tables(tbl, n_blocks):
    tbl = tbl.reshape(-1, 2, N_EXPERTS, LANES)
    cnt = tbl[:, 0, :, 0]
    off = tbl[:, 1, :, 0]
    run = jnp.cumsum(cnt, axis=0) - cnt
    total = jnp.sum(cnt, axis=0)
    nblk = (total + MOE_ROWS - 1) // MOE_ROWS
    blk_end = jnp.cumsum(nblk)
    blk_start = blk_end - nblk
    bidx = jnp.arange(n_blocks, dtype=I32)
    eidx = jnp.arange(N_EXPERTS, dtype=I32)
    block_e = jnp.minimum(jnp.sum(blk_end[None, :] <= bidx[:, None], axis=1), N_EXPERTS - 1).astype(I32)
    hot = block_e[:, None] == eidx[None, :]

    def per_block(v):
        return jnp.sum(jnp.where(hot, v[..., None, :], 0), axis=-1)

    row0 = (bidx - per_block(blk_start)) * MOE_ROWS
    n_tiles = cnt.shape[0]
    groups = jnp.arange(MOE_GROUPS, dtype=I32)
    erow = row0[:, None] + groups[None, :] * SUBLANES
    run_b, cnt_b, off_b = per_block(run), per_block(cnt), per_block(off)
    inside = jnp.logical_and(run_b[:, :, None] <= erow[None], erow[None] < (run_b + cnt_b)[:, :, None])
    tile_base = jnp.arange(n_tiles, dtype=I32)[:, None] * SUB_SORTED + off_b - run_b
    src = jnp.sum(jnp.where(inside, tile_base[:, :, None] + erow[None], 0), axis=0)
    dump = n_tiles * SUB_SORTED + (bidx % 2)[:, None] * MOE_ROWS + groups[None, :] * SUBLANES
    dst = jnp.where(erow < per_block(total)[:, None], src, dump).reshape(-1)
    src = src.reshape(-1)
    used = jnp.sum(cnt, axis=1)
    has_rows = nblk > 0
    first = (bidx == per_block(blk_start)).astype(I32)
    wslot = per_block((jnp.cumsum(has_rows) - 1) % 2)
    later = jnp.where(jnp.logical_and(has_rows[None, :], eidx[None, :] > eidx[:, None]), eidx[None, :], N_EXPERTS)
    nxt = per_block(jnp.min(later, axis=1))
    return ((block_e, blk_end[-1:].astype(I32), src.astype(I32), dst.astype(I32), first,
             wslot.astype(I32), nxt.astype(I32)), used.astype(I32))


def kernel(x_prompt, x_sample, state_pool, state_conv, g_mix, w_in, pool_w, pool_scale, conv_w, conv_b,
           conv_ln_g, conv_ln_b, w_out, g_moe, w_router, b_router, w_gate_up, b_gate_up, w_down, b_down,
           g_final):
    nb_p, nt_p, _ = x_prompt.shape
    nb_s, nt_s, _ = x_sample.shape
    n_p, n_s = nb_p * nt_p, nb_s * nt_s
    past_len = 16384
    assert nt_p % TILE == 0 and TILE % nt_s == 0 and n_s % TILE == 0
    tiles_p, tiles_s = n_p // TILE, n_s // TILE
    n_tiles = tiles_p + tiles_s

    tri_shape = (SUB_TOKENS, SUB_TOKENS)
    tri = (lax.broadcasted_iota(I32, tri_shape, 0) < lax.broadcasted_iota(I32, tri_shape, 1)).astype(BF16)
    weights = (g_mix[0][None], w_in[0].astype(BF16), pool_w[0].astype(BF16), pool_scale[0][None],
               conv_w[0], conv_b[0][None], conv_ln_g[0][None], conv_ln_b[0][None],
               w_out[0].astype(BF16), g_moe[0][None], w_router[0].T.astype(BF16),
               b_router[0][:, None], tri)

    zero_pool = jnp.zeros((1, nb_p, POOL_BUF, POOL_CH), F32)
    zero_conv = jnp.zeros((1, nb_p, CONV_BUF, CONV_CH), F32)
    h_p, q_p, gate_p, tbl_p, xs_all, pool_p, conv_p = _front_prompt(
        x_prompt, zero_pool, zero_conv, weights, pos0=0, n_tiles_all=n_tiles)
    h_s, q_s, gate_s, tbl_s, xs_all, pool_s, conv_s = _front_sample(
        x_sample, state_pool[0], state_conv[0], weights, xs_all, pos0=past_len, tile0=tiles_p)

    sorted_total = n_tiles * SUBTILES * (SUB_TOKENS * TOP_K + N_EXPERTS * (SUBLANES - 1))
    n_blocks = -(-sorted_total // MOE_ROWS) + N_EXPERTS
    tables, used = _block_tables(jnp.concatenate([tbl_p, tbl_s], axis=0), n_blocks)
    ys_all = _moe(xs_all, tables, w_gate_up[0], b_gate_up[0], w_down[0], b_down[0])

    gf = g_final[None]
    y_prompt = _final(h_p, ys_all, q_p, gate_p, used, gf, tile0=0, slabs=1)
    y_sample = _final(h_s, ys_all, q_s, gate_s, used, gf, tile0=tiles_p, slabs=nt_s)
    return (y_prompt.reshape(nb_p, nt_p, D_MODEL), y_sample.reshape(nb_s, nt_s, D_MODEL),
            pool_p, conv_p,
            pool_s.reshape(1, nb_s, POOL_BUF, POOL_CH), conv_s.reshape(1, nb_s, CONV_BUF, CONV_CH))
```

```python
import functools

import jax
import jax.numpy as jnp
from jax import lax
from jax.experimental import pallas as pl
from jax.experimental.pallas import tpu as pltpu

D_MODEL = 1024
POOL_CH = 512
CONV_CH = 512
POOL_WINDOWS = (2, 4, 8, 16)
POOL_GROUP = 128
POOL_BUF = 15
CONV_TAPS = 31
CONV_BUF = 30
IN_COLS = POOL_CH + 2 * CONV_CH
N_EXPERTS = 32
TOP_K = 4
D_FF = 1024
SWIGLU_LIMIT = 7.0
SWIGLU_ALPHA = 1.702
RMS_EPS = 1e-5
LN_EPS = 1e-5

LANES = 128
SUBLANES = 8
VMEM_LIMIT_BYTES = 60 * 1024 * 1024

TILE = 512
SUB_TOKENS = 256
SUBTILES = TILE // SUB_TOKENS
SAMPLE_SEQS = 64
CONV_CHUNK = 32
SORT_CHUNK = 256
FRONT_SORT_CHUNK = 640
SUB_SORTED = -(-(SUB_TOKENS * TOP_K + N_EXPERTS * (SUBLANES - 1)) // SORT_CHUNK) * SORT_CHUNK
SORTED_ROWS = SUBTILES * SUB_SORTED
MOE_ROWS = 512
HALF_ROWS = MOE_ROWS // 2
MOE_GROUPS = MOE_ROWS // SUBLANES
HALF_GROUPS = MOE_GROUPS // 2
WEIGHT_DMA_PRIORITY = 1

POOL_HIST = 24
CONV_HIST = 32

BF16 = jnp.bfloat16
F32 = jnp.float32
I32 = jnp.int32
U32 = jnp.uint32
PACKED = D_MODEL // 2


def _rmsnorm(x, g):
    ms = jnp.mean(x * x, axis=-1, keepdims=True)
    return x * lax.rsqrt(ms + RMS_EPS) * g


def _sigmoid(x):
    return 1.0 / (1.0 + jnp.exp(-x))


def _layernorm_silu(d, g, b):
    mu = jnp.mean(d, axis=-1, keepdims=True)
    c = d - mu
    var = jnp.mean(c * c, axis=-1, keepdims=True)
    n = c * lax.rsqrt(var + LN_EPS) * g + b
    return n * _sigmoid(n)


def _pack_halves(a):
    half = a.shape[1] // 2
    hi = lax.bitcast_convert_type(a[:, :half], U32)
    lo = lax.bitcast_convert_type(a[:, half:], U32)
    return hi | (lo >> 16)


def _unpack_halves(w):
    hi = lax.bitcast_convert_type(w & jnp.uint32(0xFFFF0000), F32)
    lo = lax.bitcast_convert_type(w << 16, F32)
    return jnp.concatenate([hi, lo], axis=1)


def _route_and_sort(hn_bf, wr_t_ref, br_ref, tri_ref, q_ref, gate_ref, tbl_ref, xs_ref):
    rows = hn_bf.shape[0]
    lt = lax.dot_general(wr_t_ref[...], hn_bf, (((1,), (1,)), ((), ())),
                         preferred_element_type=F32)
    lt = lt + br_ref[...]
    eidx = lax.broadcasted_iota(I32, (N_EXPERTS, rows), 0)
    vals, hots = [], []
    for _ in range(TOP_K):
        m = jnp.max(lt, axis=0, keepdims=True)
        sel = jnp.min(jnp.where(lt == m, eidx, N_EXPERTS), axis=0, keepdims=True)
        hit = eidx == sel
        vals.append(m)
        hots.append(hit)
        lt = jnp.where(hit, -jnp.inf, lt)
    v = jnp.concatenate(vals, axis=0)
    e = jnp.exp(v - vals[0])
    gate = e / jnp.sum(e, axis=0, keepdims=True)
    gate_ref[...] = jnp.concatenate([gate, jnp.zeros((SUBLANES - TOP_K, rows), F32)], axis=0)

    sel_cnt = sum(h.astype(F32) for h in hots)
    lower = (lax.broadcasted_iota(I32, (N_EXPERTS, N_EXPERTS), 1)
             < lax.broadcasted_iota(I32, (N_EXPERTS, N_EXPERTS), 0)).astype(BF16)
    q_rows = []
    for s in range(SUBTILES):
        ts = slice(s * SUB_TOKENS, (s + 1) * SUB_TOKENS)
        sel_s = sel_cnt[:, ts]
        prefix = jnp.dot(sel_s.astype(BF16), tri_ref[...], preferred_element_type=F32)
        cnt = jnp.sum(sel_s, axis=1, keepdims=True)
        cnt_u = jnp.floor((cnt + (SUBLANES - 1)) * (1.0 / SUBLANES))
        cnt_u_l = jnp.broadcast_to(cnt_u, (N_EXPERTS, LANES))
        off_u_l = jnp.dot(lower, cnt_u_l.astype(BF16), preferred_element_type=F32)
        off_l = off_u_l * float(SUBLANES)
        tbl_ref[0, s, 0] = (cnt_u_l * float(SUBLANES)).astype(I32)
        tbl_ref[0, s, 1] = off_l.astype(I32)

        pos = prefix + off_l[:, :1]
        qs = [jnp.sum(jnp.where(h[:, ts], pos, 0.0), axis=0, keepdims=True).astype(I32) for h in hots]
        q_rows.append(jnp.concatenate(qs + [jnp.zeros((SUBLANES - TOP_K, SUB_TOKENS), I32)], axis=0))

        hn_s = hn_bf[ts]
        for c in range(SUB_SORTED // FRONT_SORT_CHUNK):
            r0 = c * FRONT_SORT_CHUNK
            jidx = r0 + lax.broadcasted_iota(I32, (FRONT_SORT_CHUNK, SUB_TOKENS), 0)
            p = sum(jnp.where(jidx == qk, 1.0, 0.0) for qk in qs)
            xs = jnp.dot(p.astype(BF16), hn_s, preferred_element_type=F32)
            xs_ref[pl.ds(s * SUB_SORTED + r0, FRONT_SORT_CHUNK), :] = _pack_halves(xs)
    q_ref[...] = jnp.concatenate(q_rows, axis=1)


def _mix_out_and_route(x, mixbuf, w_out_ref, g_moe_ref, wr_t_ref, br_ref, tri_ref,
                       h_ref, q_ref, gate_ref, tbl_ref, xs_ref):
    h = x + jnp.dot(mixbuf[...], w_out_ref[...], preferred_element_type=F32)
    h_ref[...] = h
    hn = _rmsnorm(h, g_moe_ref[...])
    _route_and_sort(hn.astype(BF16), wr_t_ref, br_ref, tri_ref, q_ref, gate_ref, tbl_ref, xs_ref)


def _front_prompt_kernel(x_ref, sp_ref, sc_ref, g_mix_ref, w_in_ref, pool_w_ref, pool_scale_ref,
                         conv_w_ref, conv_b_ref, ln_g_ref, ln_b_ref, w_out_ref, g_moe_ref,
                         wr_t_ref, br_ref, tri_ref, conv_w8_ref,
                         h_ref, q_ref, gate_ref, tbl_ref, xs_ref, np_ref, nc_ref,
                         ubuf, gsh, dbuf, mixbuf, *, pos0, rows):
    del conv_w_ref
    t = pl.program_id(1)

    @pl.when(t == 0)
    def _():
        ubuf[:, pl.ds(0, POOL_HIST), :] = jnp.zeros((len(POOL_WINDOWS), POOL_HIST, POOL_CH), F32)
        ubuf[0, pl.ds(POOL_HIST - POOL_BUF, POOL_BUF), :] = sp_ref[0, 0]
        gsh[0, pl.ds(CONV_HIST - CONV_BUF, CONV_BUF), :] = sc_ref[0, 0]

    x = x_ref[0]
    xn = _rmsnorm(x, g_mix_ref[...]).astype(BF16)
    proj = jnp.dot(xn, w_in_ref[...], preferred_element_type=F32)

    u = proj[:, :POOL_CH]
    ubuf[0, pl.ds(POOL_HIST, rows), :] = u
    pos = pos0 + t * rows + lax.broadcasted_iota(I32, (rows, 1), 0)
    span = rows + POOL_HIST - SUBLANES
    for g, w in enumerate(POOL_WINDOWS):
        assert w == 2 ** (g + 1) and w // 2 <= SUBLANES
        c0 = g * POOL_GROUP
        cs = slice(c0, c0 + POOL_GROUP)
        a = ubuf[g, pl.ds(SUBLANES, span), c0:] + ubuf[g, pl.ds(SUBLANES - w // 2, span), c0:]
        if g + 1 < len(POOL_WINDOWS):
            ubuf[g + 1, pl.ds(SUBLANES, span), c0 + POOL_GROUP:] = a[:, POOL_GROUP:]
        s = a[span - rows:, :POOL_GROUP]
        cnt = jnp.minimum(w, pos + 1).astype(F32)
        z = (s / cnt - u[:, cs]).astype(BF16)
        yg = jnp.dot(z, pool_w_ref[g], preferred_element_type=F32)
        mixbuf[:, cs] = (yg * pool_scale_ref[:, cs]).astype(BF16)

    glu = proj[:, POOL_CH:POOL_CH + CONV_CH] * _sigmoid(proj[:, POOL_CH + CONV_CH:])
    gsh[0, pl.ds(CONV_HIST, rows), :] = glu
    for s in range(1, SUBLANES):
        gsh[s, pl.ds(0, CONV_HIST - s), :] = gsh[0, pl.ds(s, CONV_HIST - s), :]
        gsh[s, pl.ds(CONV_HIST - s, rows), :] = glu

    def conv_chunk(i, carry):
        r0 = pl.multiple_of(i * CONV_CHUNK, CONV_CHUNK)
        acc = jnp.broadcast_to(conv_b_ref[...], (CONV_CHUNK // SUBLANES, SUBLANES, CONV_CH))
        for k in range(CONV_TAPS):
            first = CONV_HIST - CONV_BUF + k
            s = first % SUBLANES
            rows_k = gsh[s, pl.ds(r0 + (first - s), CONV_CHUNK), :]
            acc = acc + conv_w8_ref[k][None] * rows_k.reshape(CONV_CHUNK // SUBLANES, SUBLANES, CONV_CH)
        dbuf[pl.ds(r0, CONV_CHUNK), :] = acc.reshape(CONV_CHUNK, CONV_CH)
        return carry

    lax.fori_loop(0, rows // CONV_CHUNK, conv_chunk, 0)
    y_conv = _layernorm_silu(dbuf[...], ln_g_ref[...], ln_b_ref[...])
    mixbuf[:, POOL_CH:] = y_conv.astype(BF16)

    new_pool = ubuf[0, pl.ds(rows + POOL_HIST - POOL_BUF, POOL_BUF), :]
    new_conv = gsh[0, pl.ds(rows + CONV_HIST - CONV_BUF, CONV_BUF), :]
    np_ref[0, 0] = new_pool
    nc_ref[0, 0] = new_conv
    ubuf[0, pl.ds(POOL_HIST - POOL_BUF, POOL_BUF), :] = new_pool
    gsh[0, pl.ds(CONV_HIST - CONV_BUF, CONV_BUF), :] = new_conv

    _mix_out_and_route(x, mixbuf, w_out_ref, g_moe_ref, wr_t_ref, br_ref, tri_ref,
                       h_ref, q_ref, gate_ref, tbl_ref, xs_ref)


def _front_sample_kernel(x_ref, sp_ref, sc_ref, g_mix_ref, w_in_ref, pool_w_ref, pool_scale_ref,
                         conv_w_ref, conv_b_ref, ln_g_ref, ln_b_ref, w_out_ref, g_moe_ref,
                         wr_t_ref, br_ref, tri_ref, xs_in,
                         h_ref, q_ref, gate_ref, tbl_ref, xs_ref, np_ref, nc_ref,
                         xbuf, dbuf, mixbuf, *, pos0, seqs, steps):
    del xs_in
    for t in range(steps):
        xbuf[pl.ds(t * seqs, seqs), :] = x_ref[:, t * D_MODEL:(t + 1) * D_MODEL]
    x = xbuf[...]
    xn = _rmsnorm(x, g_mix_ref[...]).astype(BF16)
    proj = jnp.dot(xn, w_in_ref[...], preferred_element_type=F32)

    def slab(a, t):
        return a[t * seqs:(t + 1) * seqs]

    u = proj[:, :POOL_CH]
    cat = [sp_ref[:, j * POOL_CH:(j + 1) * POOL_CH] for j in range(POOL_BUF)]
    cat += [slab(u, t) for t in range(steps)]
    for t in range(steps):
        for g, w in enumerate(POOL_WINDOWS):
            cs = slice(g * POOL_GROUP, (g + 1) * POOL_GROUP)
            s = cat[POOL_BUF + t][:, cs]
            for j in range(1, w):
                s = s + cat[POOL_BUF + t - j][:, cs]
            cnt = float(min(w, pos0 + t + 1))
            z = (s / cnt - cat[POOL_BUF + t][:, cs]).astype(BF16)
            yg = jnp.dot(z, pool_w_ref[g], preferred_element_type=F32)
            mixbuf[pl.ds(t * seqs, seqs), cs] = (yg * pool_scale_ref[:, cs]).astype(BF16)
    for j in range(POOL_BUF):
        np_ref[:, j * POOL_CH:(j + 1) * POOL_CH] = cat[len(cat) - POOL_BUF + j]

    glu = proj[:, POOL_CH:POOL_CH + CONV_CH] * _sigmoid(proj[:, POOL_CH + CONV_CH:])
    ccat = [sc_ref[:, j * CONV_CH:(j + 1) * CONV_CH] for j in range(CONV_BUF)]
    ccat += [slab(glu, t) for t in range(steps)]
    for t in range(steps):
        acc = jnp.broadcast_to(conv_b_ref[...], (seqs, CONV_CH))
        for k in range(CONV_TAPS):
            acc = acc + conv_w_ref[k:k + 1, :] * ccat[t + k]
        dbuf[pl.ds(t * seqs, seqs), :] = acc
    for j in range(CONV_BUF):
        nc_ref[:, j * CONV_CH:(j + 1) * CONV_CH] = ccat[len(ccat) - CONV_BUF + j]
    y_conv = _layernorm_silu(dbuf[...], ln_g_ref[...], ln_b_ref[...])
    mixbuf[:, POOL_CH:] = y_conv.astype(BF16)

    _mix_out_and_route(x, mixbuf, w_out_ref, g_moe_ref, wr_t_ref, br_ref, tri_ref,
                       h_ref, q_ref, gate_ref, tbl_ref, xs_ref)


def _const_spec(shape, single_buffer=False):
    mode = pl.Buffered(1) if single_buffer else None
    return pl.BlockSpec(shape, lambda *_: (0,) * len(shape), pipeline_mode=mode)


def _front_weight_specs():
    return [
        _const_spec((1, D_MODEL)),
        _const_spec((D_MODEL, IN_COLS), True),
        _const_spec((len(POOL_WINDOWS), POOL_GROUP, POOL_GROUP)),
        _const_spec((1, POOL_CH)),
        _const_spec((CONV_TAPS, CONV_CH)),
        _const_spec((1, CONV_CH)),
        _const_spec((1, CONV_CH)),
        _const_spec((1, CONV_CH)),
        _const_spec((D_MODEL, D_MODEL), True),
        _const_spec((1, D_MODEL)),
        _const_spec((N_EXPERTS, D_MODEL)),
        _const_spec((N_EXPERTS, 1)),
        _const_spec((SUB_TOKENS, SUB_TOKENS)),
    ]


def _front_out(n_tok, n_tiles_all, tile_map, tok_map, extra_specs, extra_shapes):
    specs = [pl.BlockSpec((TILE, D_MODEL), tok_map),
             pl.BlockSpec((SUBLANES, TILE), lambda *i: (0, tok_map(*i)[0])),
             pl.BlockSpec((SUBLANES, TILE), lambda *i: (0, tok_map(*i)[0])),
             pl.BlockSpec((1, SUBTILES, 2, N_EXPERTS, LANES),
                          lambda *i: (tok_map(*i)[0], 0, 0, 0, 0)),
             pl.BlockSpec((SORTED_ROWS, PACKED), tile_map)]
    shapes = [jax.ShapeDtypeStruct((n_tok, D_MODEL), F32),
              jax.ShapeDtypeStruct((SUBLANES, n_tok), I32),
              jax.ShapeDtypeStruct((SUBLANES, n_tok), F32),
              jax.ShapeDtypeStruct((n_tok // TILE, SUBTILES, 2, N_EXPERTS, LANES), I32),
              jax.ShapeDtypeStruct((n_tiles_all * SORTED_ROWS, PACKED), U32)]
    return specs + extra_specs, shapes + extra_shapes


def _front_prompt(x, state_pool, state_conv, weights, conv_w8, *, pos0, n_tiles_all):
    nb, nt, _ = x.shape
    rows = TILE
    tiles = nt // rows
    tok_map = lambda b, t: (b * tiles + t, 0)
    out_specs, out_shape = _front_out(
        nb * nt, n_tiles_all, tok_map, tok_map,
        [pl.BlockSpec((1, 1, POOL_BUF, POOL_CH), lambda b, t: (0, b, 0, 0)),
         pl.BlockSpec((1, 1, CONV_BUF, CONV_CH), lambda b, t: (0, b, 0, 0))],
        [jax.ShapeDtypeStruct((1, nb, POOL_BUF, POOL_CH), F32),
         jax.ShapeDtypeStruct((1, nb, CONV_BUF, CONV_CH), F32)])
    kern = functools.partial(_front_prompt_kernel, pos0=pos0, rows=rows)
    return pl.pallas_call(
        kern,
        grid=(nb, tiles),
        in_specs=[pl.BlockSpec((1, rows, D_MODEL), lambda b, t: (b, t, 0)),
                  pl.BlockSpec((1, 1, POOL_BUF, POOL_CH), lambda b, t: (0, b, 0, 0)),
                  pl.BlockSpec((1, 1, CONV_BUF, CONV_CH), lambda b, t: (0, b, 0, 0))]
                 + _front_weight_specs() + [_const_spec((CONV_TAPS, SUBLANES, CONV_CH))],
        out_specs=out_specs,
        out_shape=out_shape,
        scratch_shapes=[pltpu.VMEM((len(POOL_WINDOWS), POOL_HIST + rows, POOL_CH), F32),
                        pltpu.VMEM((SUBLANES, CONV_HIST + rows, CONV_CH), F32),
                        pltpu.VMEM((rows, CONV_CH), F32),
                        pltpu.VMEM((rows, D_MODEL), BF16)],
        compiler_params=pltpu.CompilerParams(
            dimension_semantics=("arbitrary", "arbitrary"), vmem_limit_bytes=VMEM_LIMIT_BYTES),
        name="front_prompt",
    )(x, state_pool, state_conv, *weights, conv_w8)


def _front_sample(x, state_pool, state_conv, weights, xs_all, *, pos0, tile0):
    nb, nt, _ = x.shape
    seqs = TILE // nt
    n_tiles_all = xs_all.shape[0] // SORTED_ROWS
    tok_map = lambda i: (i, 0)
    out_specs, out_shape = _front_out(
        nb * nt, n_tiles_all, lambda i: (tile0 + i, 0), tok_map,
        [pl.BlockSpec((seqs, POOL_BUF * POOL_CH), lambda i: (i, 0)),
         pl.BlockSpec((seqs, CONV_BUF * CONV_CH), lambda i: (i, 0))],
        [jax.ShapeDtypeStruct((nb, POOL_BUF * POOL_CH), F32),
         jax.ShapeDtypeStruct((nb, CONV_BUF * CONV_CH), F32)])
    kern = functools.partial(_front_sample_kernel, pos0=pos0, seqs=seqs, steps=nt)
    n_in = 3 + len(weights)
    return pl.pallas_call(
        kern,
        grid=(nb // seqs,),
        in_specs=[pl.BlockSpec((seqs, nt * D_MODEL), lambda i: (i, 0)),
                  pl.BlockSpec((seqs, POOL_BUF * POOL_CH), lambda i: (i, 0)),
                  pl.BlockSpec((seqs, CONV_BUF * CONV_CH), lambda i: (i, 0))]
                 + _front_weight_specs() + [pl.BlockSpec(memory_space=pl.ANY)],
        out_specs=out_specs,
        out_shape=out_shape,
        scratch_shapes=[pltpu.VMEM((TILE, D_MODEL), F32),
                        pltpu.VMEM((TILE, CONV_CH), F32),
                        pltpu.VMEM((TILE, D_MODEL), BF16)],
        input_output_aliases={n_in: 4},
        compiler_params=pltpu.CompilerParams(
            dimension_semantics=("arbitrary",), vmem_limit_bytes=VMEM_LIMIT_BYTES),
        name="front_sample",
    )(x.reshape(nb, nt * D_MODEL), state_pool.reshape(nb, POOL_BUF * POOL_CH),
      state_conv.reshape(nb, CONV_BUF * CONV_CH), *weights, xs_all)


def _moe_kernel(be_ref, nused_ref, src_ref, dst_ref, first_ref, wslot_ref, nxt_ref,
                xs_hbm, wgu_hbm, bgu_ref, wd_hbm, bd_ref, ys_hbm,
                xa, xb, oa, ob, wgu_f32, wd_f32, wgu_bf, wd_bf,
                gsem_a, gsem_b, ssem_a, ssem_b, wsem):
    i = pl.program_id(0)
    n_used = nused_ref[0]
    par = i % 2
    half_a = 2 * i
    half_b = 2 * i + 1

    def weight_copies(e, s):
        return (pltpu.make_async_copy(wgu_hbm.at[e], wgu_f32.at[s], wsem.at[s]),
                pltpu.make_async_copy(wd_hbm.at[e], wd_f32.at[s], wsem.at[s]))

    def gather(half, buf, sem):
        for g in range(HALF_GROUPS):
            hrow = pl.multiple_of(src_ref[half * HALF_GROUPS + g], SUBLANES)
            pltpu.make_async_copy(xs_hbm.at[pl.ds(hrow, SUBLANES)],
                                  buf.at[pl.ds(g * SUBLANES, SUBLANES)], sem).start()

    def gather_wait(buf, sem):
        pltpu.make_async_copy(xs_hbm.at[pl.ds(0, HALF_ROWS)], buf, sem).wait()

    def scatter(half, buf, sem):
        for g in range(HALF_GROUPS):
            hrow = pl.multiple_of(dst_ref[half * HALF_GROUPS + g], SUBLANES)
            pltpu.make_async_copy(buf.at[pl.ds(g * SUBLANES, SUBLANES)],
                                  ys_hbm.at[pl.ds(hrow, SUBLANES)], sem).start()

    def scatter_wait(buf, sem):
        pltpu.make_async_copy(buf, ys_hbm.at[pl.ds(0, HALF_ROWS)], sem).wait()

    def experts(xbuf, obuf):
        x = _unpack_halves(xbuf[par]).astype(BF16)
        gu = jnp.dot(x, wgu_bf[...], preferred_element_type=F32) + bgu_ref[0]
        g = jnp.minimum(gu[:, :D_FF], SWIGLU_LIMIT)
        up = jnp.clip(gu[:, D_FF:], -SWIGLU_LIMIT, SWIGLU_LIMIT)
        hdn = (up + 1.0) * (g * _sigmoid(SWIGLU_ALPHA * g))
        y = jnp.dot(hdn.astype(BF16), wd_bf[...], preferred_element_type=F32) + bd_ref[0]
        obuf[par] = _pack_halves(y.astype(BF16).astype(F32))

    @pl.when(i == 0)
    def _():
        gather(0, xa.at[0], gsem_a.at[0])
        gather(1, xb.at[0], gsem_b.at[0])
        for c in weight_copies(be_ref[0], 0):
            c.start(priority=WEIGHT_DMA_PRIORITY)

    @pl.when(i < n_used)
    def _():
        @pl.when(first_ref[i] == 1)
        def _():
            s = wslot_ref[i]
            for c in weight_copies(be_ref[i], s):
                c.wait()
            wgu_bf[...] = wgu_f32[s].astype(BF16)
            wd_bf[...] = wd_f32[s].astype(BF16)

            @pl.when(nxt_ref[i] < N_EXPERTS)
            def _():
                for c in weight_copies(nxt_ref[i], 1 - s):
                    c.start(priority=WEIGHT_DMA_PRIORITY)

        @pl.when(i >= 2)
        def _():
            scatter_wait(oa.at[par], ssem_a.at[par])
            scatter_wait(ob.at[par], ssem_b.at[par])

        nxt_blk = jnp.minimum(i + 1, n_used - 1)
        gather(2 * nxt_blk, xa.at[1 - par], gsem_a.at[1 - par])
        gather(2 * nxt_blk + 1, xb.at[1 - par], gsem_b.at[1 - par])
        gather_wait(xa.at[par], gsem_a.at[par])
        gather_wait(xb.at[par], gsem_b.at[par])
        experts(xa, oa)
        scatter(half_a, oa.at[par], ssem_a.at[par])
        experts(xb, ob)
        scatter(half_b, ob.at[par], ssem_b.at[par])

        @pl.when(i == n_used - 1)
        def _():
            scatter_wait(oa.at[par], ssem_a.at[par])
            scatter_wait(ob.at[par], ssem_b.at[par])
            gather_wait(xa.at[1 - par], gsem_a.at[1 - par])
            gather_wait(xb.at[1 - par], gsem_b.at[1 - par])

            @pl.when(i >= 1)
            def _():
                scatter_wait(oa.at[1 - par], ssem_a.at[1 - par])
                scatter_wait(ob.at[1 - par], ssem_b.at[1 - par])


def _moe(xs_all, tables, w_gate_up, b_gate_up, w_down, b_down):
    block_e, n_used, src, dst, first, wslot, nxt = tables
    n_blocks = block_e.shape[0]

    def e_map(i, be, nu, *_):
        return (be[jnp.minimum(i, nu[0] - 1)], 0, 0)

    half_buf = pltpu.VMEM((2, HALF_ROWS, PACKED), U32)
    any_spec = pl.BlockSpec(memory_space=pl.ANY)
    grid_spec = pltpu.PrefetchScalarGridSpec(
        num_scalar_prefetch=7,
        grid=(n_blocks,),
        in_specs=[any_spec,
                  any_spec,
                  pl.BlockSpec((1, 1, 2 * D_FF), e_map),
                  any_spec,
                  pl.BlockSpec((1, 1, D_MODEL), e_map)],
        out_specs=any_spec,
        scratch_shapes=[half_buf, half_buf, half_buf, half_buf,
                        pltpu.VMEM((2, D_MODEL, 2 * D_FF), F32),
                        pltpu.VMEM((2, D_FF, D_MODEL), F32),
                        pltpu.VMEM((D_MODEL, 2 * D_FF), BF16),
                        pltpu.VMEM((D_FF, D_MODEL), BF16)]
                       + [pltpu.SemaphoreType.DMA((2,))] * 5,
    )
    return pl.pallas_call(
        _moe_kernel,
        grid_spec=grid_spec,
        out_shape=jax.ShapeDtypeStruct((xs_all.shape[0] + 2 * MOE_ROWS, PACKED), U32),
        compiler_params=pltpu.CompilerParams(
            dimension_semantics=("arbitrary",), vmem_limit_bytes=VMEM_LIMIT_BYTES),
        name="moe_experts",
    )(block_e, n_used, src, dst, first, wslot, nxt,
      xs_all, w_gate_up, b_gate_up.reshape(N_EXPERTS, 1, 2 * D_FF), w_down,
      b_down.reshape(N_EXPERTS, 1, D_MODEL))


def _final_kernel(used_ref, h_ref, ys_ref, q_ref, gate_ref, g_ref, o_ref, *, tile0, slabs):
    i = pl.program_id(0)

    def to_cols(a):
        pad = jnp.zeros((LANES - SUBLANES, TILE), F32)
        return jnp.concatenate([a, pad], axis=0).T

    qc_all = to_cols(q_ref[...].astype(F32))
    gc_all = to_cols(gate_ref[...])
    parts = []
    for s in range(SUBTILES):
        ts = slice(s * SUB_TOKENS, (s + 1) * SUB_TOKENS)
        used = used_ref[(tile0 + i) * SUBTILES + s]
        qc, gc = qc_all[ts], gc_all[ts]
        acc = h_ref[ts, :]
        for c in range(SUB_SORTED // SORT_CHUNK):
            jl = (c * SORT_CHUNK + lax.broadcasted_iota(I32, (SUB_TOKENS, SORT_CHUNK), 1)).astype(F32)
            pg = sum(jnp.where(qc[:, k:k + 1] == jl, gc[:, k:k + 1], 0.0) for k in range(TOP_K))
            rid = c * SORT_CHUNK + lax.broadcasted_iota(I32, (SORT_CHUNK, 1), 0)
            rows = ys_ref[pl.ds(s * SUB_SORTED + c * SORT_CHUNK, SORT_CHUNK), :]
            ys = jnp.where(rid < used, _unpack_halves(rows), 0.0)
            acc = acc + jnp.dot(pg.astype(BF16), ys.astype(BF16), preferred_element_type=F32)
        parts.append(acc)
    y = _rmsnorm(jnp.concatenate(parts, axis=0), g_ref[...])
    if slabs == 1:
        o_ref[...] = y
    else:
        seqs = TILE // slabs
        for t in range(slabs):
            o_ref[:, t * D_MODEL:(t + 1) * D_MODEL] = y[t * seqs:(t + 1) * seqs]


def _final(h, ys_all, q, gates, used, g_final, *, tile0, slabs):
    n_tok = h.shape[0]
    if slabs == 1:
        out_spec = pl.BlockSpec((TILE, D_MODEL), lambda i, u: (i, 0))
        out_shape = jax.ShapeDtypeStruct((n_tok, D_MODEL), F32)
    else:
        seqs = TILE // slabs
        out_spec = pl.BlockSpec((seqs, slabs * D_MODEL), lambda i, u: (i, 0))
        out_shape = jax.ShapeDtypeStruct((n_tok // slabs, slabs * D_MODEL), F32)
    grid_spec = pltpu.PrefetchScalarGridSpec(
        num_scalar_prefetch=1,
        grid=(n_tok // TILE,),
        in_specs=[pl.BlockSpec((TILE, D_MODEL), lambda i, u: (i, 0)),
                  pl.BlockSpec((SORTED_ROWS, PACKED), lambda i, u: (tile0 + i, 0)),
                  pl.BlockSpec((SUBLANES, TILE), lambda i, u: (0, i)),
                  pl.BlockSpec((SUBLANES, TILE), lambda i, u: (0, i)),
                  pl.BlockSpec((1, D_MODEL), lambda i, u: (0, 0))],
        out_specs=out_spec,
    )
    return pl.pallas_call(
        functools.partial(_final_kernel, tile0=tile0, slabs=slabs),
        grid_spec=grid_spec,
        out_shape=out_shape,
        compiler_params=pltpu.CompilerParams(
            dimension_semantics=("arbitrary",), vmem_limit_bytes=VMEM_LIMIT_BYTES),
        name="final_norm",
    )(used, h, ys_all, q, gates, g_final)


def _block_tables(tbl, n_blocks):
    tbl = tbl.reshape(-1, 2, N_EXPERTS, LANES)
    cnt = tbl[:, 0, :, 0]
    off = tbl[:, 1, :, 0]
    run = jnp.cumsum(cnt, axis=0) - cnt
    total = jnp.sum(cnt, axis=0)
    nblk = (total + MOE_ROWS - 1) // MOE_ROWS
    blk_end = jnp.cumsum(nblk)
    blk_start = blk_end - nblk
    bidx = jnp.arange(n_blocks, dtype=I32)
    eidx = jnp.arange(N_EXPERTS, dtype=I32)
    block_e = jnp.minimum(jnp.sum(blk_end[None, :] <= bidx[:, None], axis=1), N_EXPERTS - 1).astype(I32)
    hot = block_e[:, None] == eidx[None, :]

    def per_block(v):
        return jnp.sum(jnp.where(hot, v[..., None, :], 0), axis=-1)

    row0 = (bidx - per_block(blk_start)) * MOE_ROWS
    n_tiles = cnt.shape[0]
    groups = jnp.arange(MOE_GROUPS, dtype=I32)
    erow = row0[:, None] + groups[None, :] * SUBLANES
    run_b, cnt_b, off_b = per_block(run), per_block(cnt), per_block(off)
    inside = jnp.logical_and(run_b[:, :, None] <= erow[None], erow[None] < (run_b + cnt_b)[:, :, None])
    tile_base = jnp.arange(n_tiles, dtype=I32)[:, None] * SUB_SORTED + off_b - run_b
    src = jnp.sum(jnp.where(inside, tile_base[:, :, None] + erow[None], 0), axis=0)
    dump = n_tiles * SUB_SORTED + (bidx % 2)[:, None] * MOE_ROWS + groups[None, :] * SUBLANES
    dst = jnp.where(erow < per_block(total)[:, None], src, dump).reshape(-1)
    src = src.reshape(-1)
    used = jnp.sum(cnt, axis=1)
    has_rows = nblk > 0
    first = (bidx == per_block(blk_start)).astype(I32)
    wslot = per_block((jnp.cumsum(has_rows) - 1) % 2)
    later = jnp.where(jnp.logical_and(has_rows[None, :], eidx[None, :] > eidx[:, None]), eidx[None, :], N_EXPERTS)
    nxt = per_block(jnp.min(later, axis=1))
    return ((block_e, blk_end[-1:].astype(I32), src.astype(I32), dst.astype(I32), first,
             wslot.astype(I32), nxt.astype(I32)), used.astype(I32))


def kernel(x_prompt, x_sample, state_pool, state_conv, g_mix, w_in, pool_w, pool_scale, conv_w, conv_b,
           conv_ln_g, conv_ln_b, w_out, g_moe, w_router, b_router, w_gate_up, b_gate_up, w_down, b_down,
           g_final):
    nb_p, nt_p, _ = x_prompt.shape
    nb_s, nt_s, _ = x_sample.shape
    n_p, n_s = nb_p * nt_p, nb_s * nt_s
    past_len = 16384
    assert nt_p % TILE == 0 and TILE % nt_s == 0 and n_s % TILE == 0
    tiles_p, tiles_s = n_p // TILE, n_s // TILE
    n_tiles = tiles_p + tiles_s

    tri_shape = (SUB_TOKENS, SUB_TOKENS)
    tri = (lax.broadcasted_iota(I32, tri_shape, 0) < lax.broadcasted_iota(I32, tri_shape, 1)).astype(BF16)
    weights = (g_mix[0][None], w_in[0].astype(BF16), pool_w[0].astype(BF16), pool_scale[0][None],
               conv_w[0], conv_b[0][None], conv_ln_g[0][None], conv_ln_b[0][None],
               w_out[0].astype(BF16), g_moe[0][None], w_router[0].T.astype(BF16),
               b_router[0][:, None], tri)

    zero_pool = jnp.zeros((1, nb_p, POOL_BUF, POOL_CH), F32)
    zero_conv = jnp.zeros((1, nb_p, CONV_BUF, CONV_CH), F32)
    h_p, q_p, gate_p, tbl_p, xs_all, pool_p, conv_p = _front_prompt(
        x_prompt, zero_pool, zero_conv, weights,
        jnp.broadcast_to(conv_w[0][:, None, :], (CONV_TAPS, SUBLANES, CONV_CH)), pos0=0, n_tiles_all=n_tiles)
    h_s, q_s, gate_s, tbl_s, xs_all, pool_s, conv_s = _front_sample(
        x_sample, state_pool[0], state_conv[0], weights, xs_all, pos0=past_len, tile0=tiles_p)

    sorted_total = n_tiles * SUBTILES * (SUB_TOKENS * TOP_K + N_EXPERTS * (SUBLANES - 1))
    n_blocks = -(-sorted_total // MOE_ROWS) + N_EXPERTS
    tables, used = _block_tables(jnp.concatenate([tbl_p, tbl_s], axis=0), n_blocks)
    ys_all = _moe(xs_all, tables, w_gate_up[0], b_gate_up[0], w_down[0], b_down[0])

    gf = g_final[None]
    y_prompt = _final(h_p, ys_all, q_p, gate_p, used, gf, tile0=0, slabs=1)
    y_sample = _final(h_s, ys_all, q_s, gate_s, used, gf, tile0=tiles_p, slabs=nt_s)
    return (y_prompt.reshape(nb_p, nt_p, D_MODEL), y_sample.reshape(nb_s, nt_s, D_MODEL),
            pool_p, conv_p,
            pool_s.reshape(1, nb_s, POOL_BUF, POOL_CH), conv_s.reshape(1, nb_s, CONV_BUF, CONV_CH))
```

```python
import functools

import jax
import jax.numpy as jnp
from jax import lax
from jax.experimental import pallas as pl
from jax.experimental.pallas import tpu as pltpu

D_MODEL = 1024
POOL_CH = 512
CONV_CH = 512
POOL_WINDOWS = (2, 4, 8, 16)
POOL_GROUP = 128
POOL_BUF = 15
CONV_TAPS = 31
CONV_BUF = 30
IN_COLS = POOL_CH + 2 * CONV_CH
N_EXPERTS = 32
TOP_K = 4
D_FF = 1024
SWIGLU_LIMIT = 7.0
SWIGLU_ALPHA = 1.702
RMS_EPS = 1e-5
LN_EPS = 1e-5

LANES = 128
SUBLANES = 8
VMEM_LIMIT_BYTES = 60 * 1024 * 1024

TILE = 512
SUB_TOKENS = 256
SUBTILES = TILE // SUB_TOKENS
SAMPLE_SEQS = 64
CONV_CHUNK = 32
SORT_CHUNK = 256
FRONT_SORT_CHUNK = 640
SUB_SORTED = -(-(SUB_TOKENS * TOP_K + N_EXPERTS * (SUBLANES - 1)) // SORT_CHUNK) * SORT_CHUNK
SORTED_ROWS = SUBTILES * SUB_SORTED
MOE_ROWS = 512
HALF_ROWS = MOE_ROWS // 2
MOE_GROUPS = MOE_ROWS // SUBLANES
HALF_GROUPS = MOE_GROUPS // 2
X_SLOTS = 3
WEIGHT_DMA_PRIORITY = 1

POOL_HIST = 24
CONV_HIST = 32

BF16 = jnp.bfloat16
F32 = jnp.float32
I32 = jnp.int32
U32 = jnp.uint32
PACKED = D_MODEL // 2


def _rmsnorm(x, g):
    ms = jnp.mean(x * x, axis=-1, keepdims=True)
    return x * lax.rsqrt(ms + RMS_EPS) * g


def _sigmoid(x):
    return 1.0 / (1.0 + jnp.exp(-x))


def _layernorm_silu(d, g, b):
    mu = jnp.mean(d, axis=-1, keepdims=True)
    c = d - mu
    var = jnp.mean(c * c, axis=-1, keepdims=True)
    n = c * lax.rsqrt(var + LN_EPS) * g + b
    return n * _sigmoid(n)


def _pack_halves(a):
    half = a.shape[1] // 2
    hi = lax.bitcast_convert_type(a[:, :half], U32)
    lo = lax.bitcast_convert_type(a[:, half:], U32)
    return hi | (lo >> 16)


def _unpack_halves(w):
    hi = lax.bitcast_convert_type(w & jnp.uint32(0xFFFF0000), F32)
    lo = lax.bitcast_convert_type(w << 16, F32)
    return jnp.concatenate([hi, lo], axis=1)


def _route_and_sort(hn_bf, wr_t_ref, br_ref, tri_ref, q_ref, gate_ref, tbl_ref, xs_ref):
    rows = hn_bf.shape[0]
    lt = lax.dot_general(wr_t_ref[...], hn_bf, (((1,), (1,)), ((), ())),
                         preferred_element_type=F32)
    lt = lt + br_ref[...]
    eidx = lax.broadcasted_iota(I32, (N_EXPERTS, rows), 0)
    vals, hots = [], []
    for _ in range(TOP_K):
        m = jnp.max(lt, axis=0, keepdims=True)
        sel = jnp.min(jnp.where(lt == m, eidx, N_EXPERTS), axis=0, keepdims=True)
        hit = eidx == sel
        vals.append(m)
        hots.append(hit)
        lt = jnp.where(hit, -jnp.inf, lt)
    v = jnp.concatenate(vals, axis=0)
    e = jnp.exp(v - vals[0])
    gate = e / jnp.sum(e, axis=0, keepdims=True)
    gate_ref[...] = jnp.concatenate([gate, jnp.zeros((SUBLANES - TOP_K, rows), F32)], axis=0)

    sel_cnt = sum(h.astype(F32) for h in hots)
    lower = (lax.broadcasted_iota(I32, (N_EXPERTS, N_EXPERTS), 1)
             < lax.broadcasted_iota(I32, (N_EXPERTS, N_EXPERTS), 0)).astype(BF16)
    q_rows = []
    for s in range(SUBTILES):
        ts = slice(s * SUB_TOKENS, (s + 1) * SUB_TOKENS)
        sel_s = sel_cnt[:, ts]
        prefix = jnp.dot(sel_s.astype(BF16), tri_ref[...], preferred_element_type=F32)
        cnt = jnp.sum(sel_s, axis=1, keepdims=True)
        cnt_u = jnp.floor((cnt + (SUBLANES - 1)) * (1.0 / SUBLANES))
        cnt_u_l = jnp.broadcast_to(cnt_u, (N_EXPERTS, LANES))
        off_u_l = jnp.dot(lower, cnt_u_l.astype(BF16), preferred_element_type=F32)
        off_l = off_u_l * float(SUBLANES)
        tbl_ref[0, s, 0] = (cnt_u_l * float(SUBLANES)).astype(I32)
        tbl_ref[0, s, 1] = off_l.astype(I32)

        pos = prefix + off_l[:, :1]
        qs = [jnp.sum(jnp.where(h[:, ts], pos, 0.0), axis=0, keepdims=True).astype(I32) for h in hots]
        q_rows.append(jnp.concatenate(qs + [jnp.zeros((SUBLANES - TOP_K, SUB_TOKENS), I32)], axis=0))

        hn_s = hn_bf[ts]
        for c in range(SUB_SORTED // FRONT_SORT_CHUNK):
            r0 = c * FRONT_SORT_CHUNK
            jidx = r0 + lax.broadcasted_iota(I32, (FRONT_SORT_CHUNK, SUB_TOKENS), 0)
            p = sum(jnp.where(jidx == qk, 1.0, 0.0) for qk in qs)
            xs = jnp.dot(p.astype(BF16), hn_s, preferred_element_type=F32)
            xs_ref[pl.ds(s * SUB_SORTED + r0, FRONT_SORT_CHUNK), :] = _pack_halves(xs)
    q_ref[...] = jnp.concatenate(q_rows, axis=1)


def _mix_out_and_route(x, mixbuf, w_out_ref, g_moe_ref, wr_t_ref, br_ref, tri_ref,
                       h_ref, q_ref, gate_ref, tbl_ref, xs_ref):
    h = x + jnp.dot(mixbuf[...], w_out_ref[...], preferred_element_type=F32)
    h_ref[...] = h
    hn = _rmsnorm(h, g_moe_ref[...])
    _route_and_sort(hn.astype(BF16), wr_t_ref, br_ref, tri_ref, q_ref, gate_ref, tbl_ref, xs_ref)


def _front_prompt_kernel(x_ref, sp_ref, sc_ref, g_mix_ref, w_in_ref, pool_w_ref, pool_scale_ref,
                         conv_w_ref, conv_b_ref, ln_g_ref, ln_b_ref, w_out_ref, g_moe_ref,
                         wr_t_ref, br_ref, tri_ref, conv_w8_ref,
                         h_ref, q_ref, gate_ref, tbl_ref, xs_ref, np_ref, nc_ref,
                         ubuf, gsh, dbuf, mixbuf, *, pos0, rows):
    del conv_w_ref
    t = pl.program_id(1)

    @pl.when(t == 0)
    def _():
        ubuf[:, pl.ds(0, POOL_HIST), :] = jnp.zeros((len(POOL_WINDOWS), POOL_HIST, POOL_CH), F32)
        ubuf[0, pl.ds(POOL_HIST - POOL_BUF, POOL_BUF), :] = sp_ref[0, 0]
        gsh[0, pl.ds(CONV_HIST - CONV_BUF, CONV_BUF), :] = sc_ref[0, 0]

    x = x_ref[0]
    xn = _rmsnorm(x, g_mix_ref[...]).astype(BF16)
    proj = jnp.dot(xn, w_in_ref[...], preferred_element_type=F32)

    u = proj[:, :POOL_CH]
    ubuf[0, pl.ds(POOL_HIST, rows), :] = u
    pos = pos0 + t * rows + lax.broadcasted_iota(I32, (rows, 1), 0)
    span = rows + POOL_HIST - SUBLANES
    for g, w in enumerate(POOL_WINDOWS):
        assert w == 2 ** (g + 1) and w // 2 <= SUBLANES
        c0 = g * POOL_GROUP
        cs = slice(c0, c0 + POOL_GROUP)
        a = ubuf[g, pl.ds(SUBLANES, span), c0:] + ubuf[g, pl.ds(SUBLANES - w // 2, span), c0:]
        if g + 1 < len(POOL_WINDOWS):
            ubuf[g + 1, pl.ds(SUBLANES, span), c0 + POOL_GROUP:] = a[:, POOL_GROUP:]
        s = a[span - rows:, :POOL_GROUP]
        cnt = jnp.minimum(w, pos + 1).astype(F32)
        z = (s / cnt - u[:, cs]).astype(BF16)
        yg = jnp.dot(z, pool_w_ref[g], preferred_element_type=F32)
        mixbuf[:, cs] = (yg * pool_scale_ref[:, cs]).astype(BF16)

    glu = proj[:, POOL_CH:POOL_CH + CONV_CH] * _sigmoid(proj[:, POOL_CH + CONV_CH:])
    gsh[0, pl.ds(CONV_HIST, rows), :] = glu
    for s in range(1, SUBLANES):
        gsh[s, pl.ds(0, CONV_HIST - s), :] = gsh[0, pl.ds(s, CONV_HIST - s), :]
        gsh[s, pl.ds(CONV_HIST - s, rows), :] = glu

    def conv_chunk(i, carry):
        r0 = pl.multiple_of(i * CONV_CHUNK, CONV_CHUNK)
        acc = jnp.broadcast_to(conv_b_ref[...], (CONV_CHUNK // SUBLANES, SUBLANES, CONV_CH))
        for k in range(CONV_TAPS):
            first = CONV_HIST - CONV_BUF + k
            s = first % SUBLANES
            rows_k = gsh[s, pl.ds(r0 + (first - s), CONV_CHUNK), :]
            acc = acc + conv_w8_ref[k][None] * rows_k.reshape(CONV_CHUNK // SUBLANES, SUBLANES, CONV_CH)
        dbuf[pl.ds(r0, CONV_CHUNK), :] = acc.reshape(CONV_CHUNK, CONV_CH)
        return carry

    lax.fori_loop(0, rows // CONV_CHUNK, conv_chunk, 0)
    y_conv = _layernorm_silu(dbuf[...], ln_g_ref[...], ln_b_ref[...])
    mixbuf[:, POOL_CH:] = y_conv.astype(BF16)

    new_pool = ubuf[0, pl.ds(rows + POOL_HIST - POOL_BUF, POOL_BUF), :]
    new_conv = gsh[0, pl.ds(rows + CONV_HIST - CONV_BUF, CONV_BUF), :]
    np_ref[0, 0] = new_pool
    nc_ref[0, 0] = new_conv
    ubuf[0, pl.ds(POOL_HIST - POOL_BUF, POOL_BUF), :] = new_pool
    gsh[0, pl.ds(CONV_HIST - CONV_BUF, CONV_BUF), :] = new_conv

    _mix_out_and_route(x, mixbuf, w_out_ref, g_moe_ref, wr_t_ref, br_ref, tri_ref,
                       h_ref, q_ref, gate_ref, tbl_ref, xs_ref)


def _front_sample_kernel(x_ref, sp_ref, sc_ref, g_mix_ref, w_in_ref, pool_w_ref, pool_scale_ref,
                         conv_w_ref, conv_b_ref, ln_g_ref, ln_b_ref, w_out_ref, g_moe_ref,
                         wr_t_ref, br_ref, tri_ref, xs_in,
                         h_ref, q_ref, gate_ref, tbl_ref, xs_ref, np_ref, nc_ref,
                         xbuf, dbuf, mixbuf, *, pos0, seqs, steps):
    del xs_in
    for t in range(steps):
        xbuf[pl.ds(t * seqs, seqs), :] = x_ref[:, t * D_MODEL:(t + 1) * D_MODEL]
    x = xbuf[...]
    xn = _rmsnorm(x, g_mix_ref[...]).astype(BF16)
    proj = jnp.dot(xn, w_in_ref[...], preferred_element_type=F32)

    def slab(a, t):
        return a[t * seqs:(t + 1) * seqs]

    u = proj[:, :POOL_CH]
    cat = [sp_ref[:, j * POOL_CH:(j + 1) * POOL_CH] for j in range(POOL_BUF)]
    cat += [slab(u, t) for t in range(steps)]
    for t in range(steps):
        for g, w in enumerate(POOL_WINDOWS):
            cs = slice(g * POOL_GROUP, (g + 1) * POOL_GROUP)
            s = cat[POOL_BUF + t][:, cs]
            for j in range(1, w):
                s = s + cat[POOL_BUF + t - j][:, cs]
            cnt = float(min(w, pos0 + t + 1))
            z = (s / cnt - cat[POOL_BUF + t][:, cs]).astype(BF16)
            yg = jnp.dot(z, pool_w_ref[g], preferred_element_type=F32)
            mixbuf[pl.ds(t * seqs, seqs), cs] = (yg * pool_scale_ref[:, cs]).astype(BF16)
    for j in range(POOL_BUF):
        np_ref[:, j * POOL_CH:(j + 1) * POOL_CH] = cat[len(cat) - POOL_BUF + j]

    glu = proj[:, POOL_CH:POOL_CH + CONV_CH] * _sigmoid(proj[:, POOL_CH + CONV_CH:])
    ccat = [sc_ref[:, j * CONV_CH:(j + 1) * CONV_CH] for j in range(CONV_BUF)]
    ccat += [slab(glu, t) for t in range(steps)]
    for t in range(steps):
        acc = jnp.broadcast_to(conv_b_ref[...], (seqs, CONV_CH))
        for k in range(CONV_TAPS):
            acc = acc + conv_w_ref[k:k + 1, :] * ccat[t + k]
        dbuf[pl.ds(t * seqs, seqs), :] = acc
    for j in range(CONV_BUF):
        nc_ref[:, j * CONV_CH:(j + 1) * CONV_CH] = ccat[len(ccat) - CONV_BUF + j]
    y_conv = _layernorm_silu(dbuf[...], ln_g_ref[...], ln_b_ref[...])
    mixbuf[:, POOL_CH:] = y_conv.astype(BF16)

    _mix_out_and_route(x, mixbuf, w_out_ref, g_moe_ref, wr_t_ref, br_ref, tri_ref,
                       h_ref, q_ref, gate_ref, tbl_ref, xs_ref)


def _const_spec(shape, single_buffer=False):
    mode = pl.Buffered(1) if single_buffer else None
    return pl.BlockSpec(shape, lambda *_: (0,) * len(shape), pipeline_mode=mode)


def _front_weight_specs():
    return [
        _const_spec((1, D_MODEL)),
        _const_spec((D_MODEL, IN_COLS), True),
        _const_spec((len(POOL_WINDOWS), POOL_GROUP, POOL_GROUP)),
        _const_spec((1, POOL_CH)),
        _const_spec((CONV_TAPS, CONV_CH)),
        _const_spec((1, CONV_CH)),
        _const_spec((1, CONV_CH)),
        _const_spec((1, CONV_CH)),
        _const_spec((D_MODEL, D_MODEL), True),
        _const_spec((1, D_MODEL)),
        _const_spec((N_EXPERTS, D_MODEL)),
        _const_spec((N_EXPERTS, 1)),
        _const_spec((SUB_TOKENS, SUB_TOKENS)),
    ]


def _front_out(n_tok, n_tiles_all, tile_map, tok_map, extra_specs, extra_shapes):
    specs = [pl.BlockSpec((TILE, D_MODEL), tok_map),
             pl.BlockSpec((SUBLANES, TILE), lambda *i: (0, tok_map(*i)[0])),
             pl.BlockSpec((SUBLANES, TILE), lambda *i: (0, tok_map(*i)[0])),
             pl.BlockSpec((1, SUBTILES, 2, N_EXPERTS, LANES),
                          lambda *i: (tok_map(*i)[0], 0, 0, 0, 0)),
             pl.BlockSpec((SORTED_ROWS, PACKED), tile_map)]
    shapes = [jax.ShapeDtypeStruct((n_tok, D_MODEL), F32),
              jax.ShapeDtypeStruct((SUBLANES, n_tok), I32),
              jax.ShapeDtypeStruct((SUBLANES, n_tok), F32),
              jax.ShapeDtypeStruct((n_tok // TILE, SUBTILES, 2, N_EXPERTS, LANES), I32),
              jax.ShapeDtypeStruct((n_tiles_all * SORTED_ROWS, PACKED), U32)]
    return specs + extra_specs, shapes + extra_shapes


def _front_prompt(x, state_pool, state_conv, weights, conv_w8, *, pos0, n_tiles_all):
    nb, nt, _ = x.shape
    rows = TILE
    tiles = nt // rows
    tok_map = lambda b, t: (b * tiles + t, 0)
    out_specs, out_shape = _front_out(
        nb * nt, n_tiles_all, tok_map, tok_map,
        [pl.BlockSpec((1, 1, POOL_BUF, POOL_CH), lambda b, t: (0, b, 0, 0)),
         pl.BlockSpec((1, 1, CONV_BUF, CONV_CH), lambda b, t: (0, b, 0, 0))],
        [jax.ShapeDtypeStruct((1, nb, POOL_BUF, POOL_CH), F32),
         jax.ShapeDtypeStruct((1, nb, CONV_BUF, CONV_CH), F32)])
    kern = functools.partial(_front_prompt_kernel, pos0=pos0, rows=rows)
    return pl.pallas_call(
        kern,
        grid=(nb, tiles),
        in_specs=[pl.BlockSpec((1, rows, D_MODEL), lambda b, t: (b, t, 0)),
                  pl.BlockSpec((1, 1, POOL_BUF, POOL_CH), lambda b, t: (0, b, 0, 0)),
                  pl.BlockSpec((1, 1, CONV_BUF, CONV_CH), lambda b, t: (0, b, 0, 0))]
                 + _front_weight_specs() + [_const_spec((CONV_TAPS, SUBLANES, CONV_CH))],
        out_specs=out_specs,
        out_shape=out_shape,
        scratch_shapes=[pltpu.VMEM((len(POOL_WINDOWS), POOL_HIST + rows, POOL_CH), F32),
                        pltpu.VMEM((SUBLANES, CONV_HIST + rows, CONV_CH), F32),
                        pltpu.VMEM((rows, CONV_CH), F32),
                        pltpu.VMEM((rows, D_MODEL), BF16)],
        compiler_params=pltpu.CompilerParams(
            dimension_semantics=("arbitrary", "arbitrary"), vmem_limit_bytes=VMEM_LIMIT_BYTES),
        name="front_prompt",
    )(x, state_pool, state_conv, *weights, conv_w8)


def _front_sample(x, state_pool, state_conv, weights, xs_all, *, pos0, tile0):
    nb, nt, _ = x.shape
    seqs = TILE // nt
    n_tiles_all = xs_all.shape[0] // SORTED_ROWS
    tok_map = lambda i: (i, 0)
    out_specs, out_shape = _front_out(
        nb * nt, n_tiles_all, lambda i: (tile0 + i, 0), tok_map,
        [pl.BlockSpec((seqs, POOL_BUF * POOL_CH), lambda i: (i, 0)),
         pl.BlockSpec((seqs, CONV_BUF * CONV_CH), lambda i: (i, 0))],
        [jax.ShapeDtypeStruct((nb, POOL_BUF * POOL_CH), F32),
         jax.ShapeDtypeStruct((nb, CONV_BUF * CONV_CH), F32)])
    kern = functools.partial(_front_sample_kernel, pos0=pos0, seqs=seqs, steps=nt)
    n_in = 3 + len(weights)
    return pl.pallas_call(
        kern,
        grid=(nb // seqs,),
        in_specs=[pl.BlockSpec((seqs, nt * D_MODEL), lambda i: (i, 0)),
                  pl.BlockSpec((seqs, POOL_BUF * POOL_CH), lambda i: (i, 0)),
                  pl.BlockSpec((seqs, CONV_BUF * CONV_CH), lambda i: (i, 0))]
                 + _front_weight_specs() + [pl.BlockSpec(memory_space=pl.ANY)],
        out_specs=out_specs,
        out_shape=out_shape,
        scratch_shapes=[pltpu.VMEM((TILE, D_MODEL), F32),
                        pltpu.VMEM((TILE, CONV_CH), F32),
                        pltpu.VMEM((TILE, D_MODEL), BF16)],
        input_output_aliases={n_in: 4},
        compiler_params=pltpu.CompilerParams(
            dimension_semantics=("arbitrary",), vmem_limit_bytes=VMEM_LIMIT_BYTES),
        name="front_sample",
    )(x.reshape(nb, nt * D_MODEL), state_pool.reshape(nb, POOL_BUF * POOL_CH),
      state_conv.reshape(nb, CONV_BUF * CONV_CH), *weights, xs_all)


def _moe_kernel(be_ref, nused_ref, src_ref, dst_ref, first_ref, wslot_ref, nxt_ref,
                xs_hbm, wgu_hbm, bgu_ref, wd_hbm, bd_ref, ys_hbm,
                xa, xb, oa, ob, wgu_f32, wd_f32, wgu_bf, wd_bf,
                gsem_a, gsem_b, ssem_a, ssem_b, wsem):
    i = pl.program_id(0)
    n_used = nused_ref[0]
    par = i % 2
    slot = lax.rem(i, X_SLOTS)
    half_a = 2 * i
    half_b = 2 * i + 1

    def weight_copies(e, s):
        return (pltpu.make_async_copy(wgu_hbm.at[e], wgu_f32.at[s], wsem.at[s]),
                pltpu.make_async_copy(wd_hbm.at[e], wd_f32.at[s], wsem.at[s]))

    def gather(half, buf, sem):
        for g in range(HALF_GROUPS):
            hrow = pl.multiple_of(src_ref[half * HALF_GROUPS + g], SUBLANES)
            pltpu.make_async_copy(xs_hbm.at[pl.ds(hrow, SUBLANES)],
                                  buf.at[pl.ds(g * SUBLANES, SUBLANES)], sem).start()

    def gather_wait(buf, sem):
        pltpu.make_async_copy(xs_hbm.at[pl.ds(0, HALF_ROWS)], buf, sem).wait()

    def scatter(half, buf, sem):
        for g in range(HALF_GROUPS):
            hrow = pl.multiple_of(dst_ref[half * HALF_GROUPS + g], SUBLANES)
            pltpu.make_async_copy(buf.at[pl.ds(g * SUBLANES, SUBLANES)],
                                  ys_hbm.at[pl.ds(hrow, SUBLANES)], sem).start()

    def scatter_wait(buf, sem):
        pltpu.make_async_copy(buf, ys_hbm.at[pl.ds(0, HALF_ROWS)], sem).wait()

    def experts(xbuf, obuf):
        x = _unpack_halves(xbuf[slot]).astype(BF16)
        gu = jnp.dot(x, wgu_bf[...], preferred_element_type=F32) + bgu_ref[0]
        g = jnp.minimum(gu[:, :D_FF], SWIGLU_LIMIT)
        up = jnp.clip(gu[:, D_FF:], -SWIGLU_LIMIT, SWIGLU_LIMIT)
        hdn = (up + 1.0) * (g * _sigmoid(SWIGLU_ALPHA * g))
        y = jnp.dot(hdn.astype(BF16), wd_bf[...], preferred_element_type=F32) + bd_ref[0]
        obuf[par] = _pack_halves(y.astype(BF16).astype(F32))

    def gather_block(blk, slot):
        blk = jnp.minimum(blk, n_used - 1)
        gather(2 * blk, xa.at[slot], gsem_a.at[slot])
        gather(2 * blk + 1, xb.at[slot], gsem_b.at[slot])

    def gather_block_wait(slot):
        gather_wait(xa.at[slot], gsem_a.at[slot])
        gather_wait(xb.at[slot], gsem_b.at[slot])

    @pl.when(i == 0)
    def _():
        gather_block(0, 0)
        gather_block(1, 1)
        for c in weight_copies(be_ref[0], 0):
            c.start(priority=WEIGHT_DMA_PRIORITY)

    @pl.when(i < n_used)
    def _():
        @pl.when(first_ref[i] == 1)
        def _():
            s = wslot_ref[i]
            for c in weight_copies(be_ref[i], s):
                c.wait()
            wgu_bf[...] = wgu_f32[s].astype(BF16)
            wd_bf[...] = wd_f32[s].astype(BF16)

            @pl.when(nxt_ref[i] < N_EXPERTS)
            def _():
                for c in weight_copies(nxt_ref[i], 1 - s):
                    c.start(priority=WEIGHT_DMA_PRIORITY)

        @pl.when(i >= 2)
        def _():
            scatter_wait(oa.at[par], ssem_a.at[par])
            scatter_wait(ob.at[par], ssem_b.at[par])

        gather_block_wait(slot)
        gather_block(i + 2, lax.rem(i + 2, X_SLOTS))
        experts(xa, oa)
        scatter(half_a, oa.at[par], ssem_a.at[par])
        experts(xb, ob)
        scatter(half_b, ob.at[par], ssem_b.at[par])

        @pl.when(i == n_used - 1)
        def _():
            scatter_wait(oa.at[par], ssem_a.at[par])
            scatter_wait(ob.at[par], ssem_b.at[par])
            gather_block_wait(lax.rem(i + 1, X_SLOTS))
            gather_block_wait(lax.rem(i + 2, X_SLOTS))

            @pl.when(i >= 1)
            def _():
                scatter_wait(oa.at[1 - par], ssem_a.at[1 - par])
                scatter_wait(ob.at[1 - par], ssem_b.at[1 - par])


def _moe(xs_all, tables, w_gate_up, b_gate_up, w_down, b_down):
    block_e, n_used, src, dst, first, wslot, nxt = tables
    n_blocks = block_e.shape[0]

    def e_map(i, be, nu, *_):
        return (be[jnp.minimum(i, nu[0] - 1)], 0, 0)

    x_buf = pltpu.VMEM((X_SLOTS, HALF_ROWS, PACKED), U32)
    o_buf = pltpu.VMEM((2, HALF_ROWS, PACKED), U32)
    any_spec = pl.BlockSpec(memory_space=pl.ANY)
    grid_spec = pltpu.PrefetchScalarGridSpec(
        num_scalar_prefetch=7,
        grid=(n_blocks,),
        in_specs=[any_spec,
                  any_spec,
                  pl.BlockSpec((1, 1, 2 * D_FF), e_map),
                  any_spec,
                  pl.BlockSpec((1, 1, D_MODEL), e_map)],
        out_specs=any_spec,
        scratch_shapes=[x_buf, x_buf, o_buf, o_buf,
                        pltpu.VMEM((2, D_MODEL, 2 * D_FF), F32),
                        pltpu.VMEM((2, D_FF, D_MODEL), F32),
                        pltpu.VMEM((D_MODEL, 2 * D_FF), BF16),
                        pltpu.VMEM((D_FF, D_MODEL), BF16)]
                       + [pltpu.SemaphoreType.DMA((X_SLOTS,))] * 2 + [pltpu.SemaphoreType.DMA((2,))] * 3,
    )
    return pl.pallas_call(
        _moe_kernel,
        grid_spec=grid_spec,
        out_shape=jax.ShapeDtypeStruct((xs_all.shape[0] + 2 * MOE_ROWS, PACKED), U32),
        compiler_params=pltpu.CompilerParams(
            dimension_semantics=("arbitrary",), vmem_limit_bytes=VMEM_LIMIT_BYTES),
        name="moe_experts",
    )(block_e, n_used, src, dst, first, wslot, nxt,
      xs_all, w_gate_up, b_gate_up.reshape(N_EXPERTS, 1, 2 * D_FF), w_down,
      b_down.reshape(N_EXPERTS, 1, D_MODEL))


def _final_kernel(used_ref, h_ref, ys_ref, q_ref, gate_ref, g_ref, o_ref, *, tile0, slabs):
    i = pl.program_id(0)

    def to_cols(a):
        pad = jnp.zeros((LANES - SUBLANES, TILE), F32)
        return jnp.concatenate([a, pad], axis=0).T

    qc_all = to_cols(q_ref[...].astype(F32))
    gc_all = to_cols(gate_ref[...])
    parts = []
    for s in range(SUBTILES):
        ts = slice(s * SUB_TOKENS, (s + 1) * SUB_TOKENS)
        used = used_ref[(tile0 + i) * SUBTILES + s]
        qc, gc = qc_all[ts], gc_all[ts]
        acc = h_ref[ts, :]
        for c in range(SUB_SORTED // SORT_CHUNK):
            jl = (c * SORT_CHUNK + lax.broadcasted_iota(I32, (SUB_TOKENS, SORT_CHUNK), 1)).astype(F32)
            pg = sum(jnp.where(qc[:, k:k + 1] == jl, gc[:, k:k + 1], 0.0) for k in range(TOP_K))
            rid = c * SORT_CHUNK + lax.broadcasted_iota(I32, (SORT_CHUNK, 1), 0)
            rows = ys_ref[pl.ds(s * SUB_SORTED + c * SORT_CHUNK, SORT_CHUNK), :]
            ys = jnp.where(rid < used, _unpack_halves(rows), 0.0)
            acc = acc + jnp.dot(pg.astype(BF16), ys.astype(BF16), preferred_element_type=F32)
        parts.append(acc)
    y = _rmsnorm(jnp.concatenate(parts, axis=0), g_ref[...])
    if slabs == 1:
        o_ref[...] = y
    else:
        seqs = TILE // slabs
        for t in range(slabs):
            o_ref[:, t * D_MODEL:(t + 1) * D_MODEL] = y[t * seqs:(t + 1) * seqs]


def _final(h, ys_all, q, gates, used, g_final, *, tile0, slabs):
    n_tok = h.shape[0]
    if slabs == 1:
        out_spec = pl.BlockSpec((TILE, D_MODEL), lambda i, u: (i, 0))
        out_shape = jax.ShapeDtypeStruct((n_tok, D_MODEL), F32)
    else:
        seqs = TILE // slabs
        out_spec = pl.BlockSpec((seqs, slabs * D_MODEL), lambda i, u: (i, 0))
        out_shape = jax.ShapeDtypeStruct((n_tok // slabs, slabs * D_MODEL), F32)
    grid_spec = pltpu.PrefetchScalarGridSpec(
        num_scalar_prefetch=1,
        grid=(n_tok // TILE,),
        in_specs=[pl.BlockSpec((TILE, D_MODEL), lambda i, u: (i, 0)),
                  pl.BlockSpec((SORTED_ROWS, PACKED), lambda i, u: (tile0 + i, 0)),
                  pl.BlockSpec((SUBLANES, TILE), lambda i, u: (0, i)),
                  pl.BlockSpec((SUBLANES, TILE), lambda i, u: (0, i)),
                  pl.BlockSpec((1, D_MODEL), lambda i, u: (0, 0))],
        out_specs=out_spec,
    )
    return pl.pallas_call(
        functools.partial(_final_kernel, tile0=tile0, slabs=slabs),
        grid_spec=grid_spec,
        out_shape=out_shape,
        compiler_params=pltpu.CompilerParams(
            dimension_semantics=("arbitrary",), vmem_limit_bytes=VMEM_LIMIT_BYTES),
        name="final_norm",
    )(used, h, ys_all, q, gates, g_final)


def _block_tables(tbl, n_blocks):
    tbl = tbl.reshape(-1, 2, N_EXPERTS, LANES)
    cnt = tbl[:, 0, :, 0]
    off = tbl[:, 1, :, 0]
    run = jnp.cumsum(cnt, axis=0) - cnt
    total = jnp.sum(cnt, axis=0)
    nblk = (total + MOE_ROWS - 1) // MOE_ROWS
    blk_end = jnp.cumsum(nblk)
    blk_start = blk_end - nblk
    bidx = jnp.arange(n_blocks, dtype=I32)
    eidx = jnp.arange(N_EXPERTS, dtype=I32)
    block_e = jnp.minimum(jnp.sum(blk_end[None, :] <= bidx[:, None], axis=1), N_EXPERTS - 1).astype(I32)
    hot = block_e[:, None] == eidx[None, :]

    def per_block(v):
        return jnp.sum(jnp.where(hot, v[..., None, :], 0), axis=-1)

    row0 = (bidx - per_block(blk_start)) * MOE_ROWS
    n_tiles = cnt.shape[0]
    groups = jnp.arange(MOE_GROUPS, dtype=I32)
    erow = row0[:, None] + groups[None, :] * SUBLANES
    run_b, cnt_b, off_b = per_block(run), per_block(cnt), per_block(off)
    inside = jnp.logical_and(run_b[:, :, None] <= erow[None], erow[None] < (run_b + cnt_b)[:, :, None])
    tile_base = jnp.arange(n_tiles, dtype=I32)[:, None] * SUB_SORTED + off_b - run_b
    src = jnp.sum(jnp.where(inside, tile_base[:, :, None] + erow[None], 0), axis=0)
    dump = n_tiles * SUB_SORTED + (bidx % 2)[:, None] * MOE_ROWS + groups[None, :] * SUBLANES
    dst = jnp.where(erow < per_block(total)[:, None], src, dump).reshape(-1)
    src = src.reshape(-1)
    used = jnp.sum(cnt, axis=1)
    has_rows = nblk > 0
    first = (bidx == per_block(blk_start)).astype(I32)
    wslot = per_block((jnp.cumsum(has_rows) - 1) % 2)
    later = jnp.where(jnp.logical_and(has_rows[None, :], eidx[None, :] > eidx[:, None]), eidx[None, :], N_EXPERTS)
    nxt = per_block(jnp.min(later, axis=1))
    return ((block_e, blk_end[-1:].astype(I32), src.astype(I32), dst.astype(I32), first,
             wslot.astype(I32), nxt.astype(I32)), used.astype(I32))


def kernel(x_prompt, x_sample, state_pool, state_conv, g_mix, w_in, pool_w, pool_scale, conv_w, conv_b,
           conv_ln_g, conv_ln_b, w_out, g_moe, w_router, b_router, w_gate_up, b_gate_up, w_down, b_down,
           g_final):
    nb_p, nt_p, _ = x_prompt.shape
    nb_s, nt_s, _ = x_sample.shape
    n_p, n_s = nb_p * nt_p, nb_s * nt_s
    past_len = 16384
    assert nt_p % TILE == 0 and TILE % nt_s == 0 and n_s % TILE == 0
    tiles_p, tiles_s = n_p // TILE, n_s // TILE
    n_tiles = tiles_p + tiles_s

    tri_shape = (SUB_TOKENS, SUB_TOKENS)
    tri = (lax.broadcasted_iota(I32, tri_shape, 0) < lax.broadcasted_iota(I32, tri_shape, 1)).astype(BF16)
    weights = (g_mix[0][None], w_in[0].astype(BF16), pool_w[0].astype(BF16), pool_scale[0][None],
               conv_w[0], conv_b[0][None], conv_ln_g[0][None], conv_ln_b[0][None],
               w_out[0].astype(BF16), g_moe[0][None], w_router[0].T.astype(BF16),
               b_router[0][:, None], tri)

    zero_pool = jnp.zeros((1, nb_p, POOL_BUF, POOL_CH), F32)
    zero_conv = jnp.zeros((1, nb_p, CONV_BUF, CONV_CH), F32)
    h_p, q_p, gate_p, tbl_p, xs_all, pool_p, conv_p = _front_prompt(
        x_prompt, zero_pool, zero_conv, weights,
        jnp.broadcast_to(conv_w[0][:, None, :], (CONV_TAPS, SUBLANES, CONV_CH)), pos0=0, n_tiles_all=n_tiles)
    h_s, q_s, gate_s, tbl_s, xs_all, pool_s, conv_s = _front_sample(
        x_sample, state_pool[0], state_conv[0], weights, xs_all, pos0=past_len, tile0=tiles_p)

    sorted_total = n_tiles * SUBTILES * (SUB_TOKENS * TOP_K + N_EXPERTS * (SUBLANES - 1))
    n_blocks = -(-sorted_total // MOE_ROWS) + N_EXPERTS
    tables, used = _block_tables(jnp.concatenate([tbl_p, tbl_s], axis=0), n_blocks)
    ys_all = _moe(xs_all, tables, w_gate_up[0], b_gate_up[0], w_down[0], b_down[0])

    gf = g_final[None]
    y_prompt = _final(h_p, ys_all, q_p, gate_p, used, gf, tile0=0, slabs=1)
    y_sample = _final(h_s, ys_all, q_s, gate_s, used, gf, tile0=tiles_p, slabs=nt_s)
    return (y_prompt.reshape(nb_p, nt_p, D_MODEL), y_sample.reshape(nb_s, nt_s, D_MODEL),
            pool_p, conv_p,
            pool_s.reshape(1, nb_s, POOL_BUF, POOL_CH), conv_s.reshape(1, nb_s, CONV_BUF, CONV_CH))
```

```python
import functools

import jax
import jax.numpy as jnp
from jax import lax
from jax.experimental import pallas as pl
from jax.experimental.pallas import tpu as pltpu

D_MODEL = 1024
POOL_CH = 512
CONV_CH = 512
POOL_WINDOWS = (2, 4, 8, 16)
POOL_GROUP = 128
POOL_BUF = 15
CONV_TAPS = 31
CONV_BUF = 30
IN_COLS = POOL_CH + 2 * CONV_CH
N_EXPERTS = 32
TOP_K = 4
D_FF = 1024
SWIGLU_LIMIT = 7.0
SWIGLU_ALPHA = 1.702
RMS_EPS = 1e-5
LN_EPS = 1e-5

LANES = 128
SUBLANES = 8
VMEM_LIMIT_BYTES = 60 * 1024 * 1024

TILE = 512
SUB_TOKENS = 256
SUBTILES = TILE // SUB_TOKENS
SAMPLE_SEQS = 64
CONV_CHUNK = 32
SORT_CHUNK = 256
FRONT_SORT_CHUNK = 640
SUB_SORTED = -(-(SUB_TOKENS * TOP_K + N_EXPERTS * (SUBLANES - 1)) // SORT_CHUNK) * SORT_CHUNK
SORTED_ROWS = SUBTILES * SUB_SORTED
MOE_ROWS = 512
HALF_ROWS = MOE_ROWS // 2
MOE_GROUPS = MOE_ROWS // SUBLANES
HALF_GROUPS = MOE_GROUPS // 2
X_SLOTS = 3
WEIGHT_DMA_PRIORITY = 1

POOL_HIST = 24
CONV_HIST = 32

BF16 = jnp.bfloat16
F32 = jnp.float32
I32 = jnp.int32
U32 = jnp.uint32
PACKED = D_MODEL // 2


def _rmsnorm(x, g):
    ms = jnp.mean(x * x, axis=-1, keepdims=True)
    return x * lax.rsqrt(ms + RMS_EPS) * g


def _sigmoid(x):
    return 1.0 / (1.0 + jnp.exp(-x))


def _layernorm_silu(d, g, b):
    mu = jnp.mean(d, axis=-1, keepdims=True)
    c = d - mu
    var = jnp.mean(c * c, axis=-1, keepdims=True)
    n = c * lax.rsqrt(var + LN_EPS) * g + b
    return n * _sigmoid(n)


def _pack_halves(a):
    half = a.shape[1] // 2
    hi = lax.bitcast_convert_type(a[:, :half], U32)
    lo = lax.bitcast_convert_type(a[:, half:], U32)
    return hi | (lo >> 16)


def _unpack_halves(w):
    hi = lax.bitcast_convert_type(w & jnp.uint32(0xFFFF0000), F32)
    lo = lax.bitcast_convert_type(w << 16, F32)
    return jnp.concatenate([hi, lo], axis=1)


def _route_and_sort(hn_bf, wr_t_ref, br_ref, tri_ref, q_ref, gate_ref, tbl_ref, xs_ref):
    rows = hn_bf.shape[0]
    lt = lax.dot_general(wr_t_ref[...], hn_bf, (((1,), (1,)), ((), ())),
                         preferred_element_type=F32)
    lt = lt + br_ref[...]
    eidx = lax.broadcasted_iota(I32, (N_EXPERTS, rows), 0)
    vals, hots = [], []
    for _ in range(TOP_K):
        m = jnp.max(lt, axis=0, keepdims=True)
        sel = jnp.min(jnp.where(lt == m, eidx, N_EXPERTS), axis=0, keepdims=True)
        hit = eidx == sel
        vals.append(m)
        hots.append(hit)
        lt = jnp.where(hit, -jnp.inf, lt)
    v = jnp.concatenate(vals, axis=0)
    e = jnp.exp(v - vals[0])
    gate = e / jnp.sum(e, axis=0, keepdims=True)
    gate_ref[...] = jnp.concatenate([gate, jnp.zeros((SUBLANES - TOP_K, rows), F32)], axis=0)

    sel_cnt = sum(h.astype(F32) for h in hots)
    lower = (lax.broadcasted_iota(I32, (N_EXPERTS, N_EXPERTS), 1)
             < lax.broadcasted_iota(I32, (N_EXPERTS, N_EXPERTS), 0)).astype(BF16)
    q_rows = []
    for s in range(SUBTILES):
        ts = slice(s * SUB_TOKENS, (s + 1) * SUB_TOKENS)
        sel_s = sel_cnt[:, ts]
        prefix = jnp.dot(sel_s.astype(BF16), tri_ref[...], preferred_element_type=F32)
        cnt = jnp.sum(sel_s, axis=1, keepdims=True)
        cnt_u = jnp.floor((cnt + (SUBLANES - 1)) * (1.0 / SUBLANES))
        cnt_u_l = jnp.broadcast_to(cnt_u, (N_EXPERTS, LANES))
        off_u_l = jnp.dot(lower, cnt_u_l.astype(BF16), preferred_element_type=F32)
        off_l = off_u_l * float(SUBLANES)
        tbl_ref[0, s, 0] = (cnt_u_l * float(SUBLANES)).astype(I32)
        tbl_ref[0, s, 1] = off_l.astype(I32)

        pos = prefix + off_l[:, :1]
        qs = [jnp.sum(jnp.where(h[:, ts], pos, 0.0), axis=0, keepdims=True).astype(I32) for h in hots]
        q_rows.append(jnp.concatenate(qs + [jnp.zeros((SUBLANES - TOP_K, SUB_TOKENS), I32)], axis=0))

        hn_s = hn_bf[ts]
        for c in range(SUB_SORTED // FRONT_SORT_CHUNK):
            r0 = c * FRONT_SORT_CHUNK
            jidx = r0 + lax.broadcasted_iota(I32, (FRONT_SORT_CHUNK, SUB_TOKENS), 0)
            p = sum(jnp.where(jidx == qk, 1.0, 0.0) for qk in qs)
            xs = jnp.dot(p.astype(BF16), hn_s, preferred_element_type=F32)
            xs_ref[pl.ds(s * SUB_SORTED + r0, FRONT_SORT_CHUNK), :] = _pack_halves(xs)
    q_ref[...] = jnp.concatenate(q_rows, axis=1)


def _mix_out_and_route(x, mixbuf, w_out_ref, g_moe_ref, wr_t_ref, br_ref, tri_ref,
                       h_ref, q_ref, gate_ref, tbl_ref, xs_ref):
    h = x + jnp.dot(mixbuf[...], w_out_ref[...], preferred_element_type=F32)
    h_ref[...] = h
    hn = _rmsnorm(h, g_moe_ref[...])
    _route_and_sort(hn.astype(BF16), wr_t_ref, br_ref, tri_ref, q_ref, gate_ref, tbl_ref, xs_ref)


def _front_prompt_kernel(x_ref, sp_ref, sc_ref, g_mix_ref, w_in_ref, pool_w_ref, pool_scale_ref,
                         conv_w_ref, conv_b_ref, ln_g_ref, ln_b_ref, w_out_ref, g_moe_ref,
                         wr_t_ref, br_ref, tri_ref, conv_w8_ref,
                         h_ref, q_ref, gate_ref, tbl_ref, xs_ref, np_ref, nc_ref,
                         ubuf, gsh, dbuf, mixbuf, *, pos0, rows):
    del conv_w_ref
    t = pl.program_id(1)

    @pl.when(t == 0)
    def _():
        ubuf[:, pl.ds(0, POOL_HIST), :] = jnp.zeros((len(POOL_WINDOWS), POOL_HIST, POOL_CH), F32)
        ubuf[0, pl.ds(POOL_HIST - POOL_BUF, POOL_BUF), :] = sp_ref[0, 0]
        gsh[0, pl.ds(CONV_HIST - CONV_BUF, CONV_BUF), :] = sc_ref[0, 0]

    x = x_ref[0]
    xn = _rmsnorm(x, g_mix_ref[...]).astype(BF16)
    proj = jnp.dot(xn, w_in_ref[...], preferred_element_type=F32)

    u = proj[:, :POOL_CH]
    ubuf[0, pl.ds(POOL_HIST, rows), :] = u
    pos = pos0 + t * rows + lax.broadcasted_iota(I32, (rows, 1), 0)
    span = rows + POOL_HIST - SUBLANES
    for g, w in enumerate(POOL_WINDOWS):
        assert w == 2 ** (g + 1) and w // 2 <= SUBLANES
        c0 = g * POOL_GROUP
        cs = slice(c0, c0 + POOL_GROUP)
        a = ubuf[g, pl.ds(SUBLANES, span), c0:] + ubuf[g, pl.ds(SUBLANES - w // 2, span), c0:]
        if g + 1 < len(POOL_WINDOWS):
            ubuf[g + 1, pl.ds(SUBLANES, span), c0 + POOL_GROUP:] = a[:, POOL_GROUP:]
        s = a[span - rows:, :POOL_GROUP]
        cnt = jnp.minimum(w, pos + 1).astype(F32)
        z = (s / cnt - u[:, cs]).astype(BF16)
        yg = jnp.dot(z, pool_w_ref[g], preferred_element_type=F32)
        mixbuf[:, cs] = (yg * pool_scale_ref[:, cs]).astype(BF16)

    glu = proj[:, POOL_CH:POOL_CH + CONV_CH] * _sigmoid(proj[:, POOL_CH + CONV_CH:])
    gsh[0, pl.ds(CONV_HIST, rows), :] = glu
    for s in range(1, SUBLANES):
        gsh[s, pl.ds(0, CONV_HIST - s), :] = gsh[0, pl.ds(s, CONV_HIST - s), :]
        gsh[s, pl.ds(CONV_HIST - s, rows), :] = glu

    def conv_chunk(i, carry):
        r0 = pl.multiple_of(i * CONV_CHUNK, CONV_CHUNK)
        acc = jnp.broadcast_to(conv_b_ref[...], (CONV_CHUNK // SUBLANES, SUBLANES, CONV_CH))
        for k in range(CONV_TAPS):
            first = CONV_HIST - CONV_BUF + k
            s = first % SUBLANES
            rows_k = gsh[s, pl.ds(r0 + (first - s), CONV_CHUNK), :]
            acc = acc + conv_w8_ref[k][None] * rows_k.reshape(CONV_CHUNK // SUBLANES, SUBLANES, CONV_CH)
        dbuf[pl.ds(r0, CONV_CHUNK), :] = acc.reshape(CONV_CHUNK, CONV_CH)
        return carry

    lax.fori_loop(0, rows // CONV_CHUNK, conv_chunk, 0)
    y_conv = _layernorm_silu(dbuf[...], ln_g_ref[...], ln_b_ref[...])
    mixbuf[:, POOL_CH:] = y_conv.astype(BF16)

    new_pool = ubuf[0, pl.ds(rows + POOL_HIST - POOL_BUF, POOL_BUF), :]
    new_conv = gsh[0, pl.ds(rows + CONV_HIST - CONV_BUF, CONV_BUF), :]
    np_ref[0, 0] = new_pool
    nc_ref[0, 0] = new_conv
    ubuf[0, pl.ds(POOL_HIST - POOL_BUF, POOL_BUF), :] = new_pool
    gsh[0, pl.ds(CONV_HIST - CONV_BUF, CONV_BUF), :] = new_conv

    _mix_out_and_route(x, mixbuf, w_out_ref, g_moe_ref, wr_t_ref, br_ref, tri_ref,
                       h_ref, q_ref, gate_ref, tbl_ref, xs_ref)


def _front_sample_kernel(x_ref, sp_ref, sc_ref, g_mix_ref, w_in_ref, pool_w_ref, pool_scale_ref,
                         conv_w_ref, conv_b_ref, ln_g_ref, ln_b_ref, w_out_ref, g_moe_ref,
                         wr_t_ref, br_ref, tri_ref, xs_in,
                         h_ref, q_ref, gate_ref, tbl_ref, xs_ref, np_ref, nc_ref,
                         xbuf, dbuf, mixbuf, *, pos0, seqs, steps):
    del xs_in
    for t in range(steps):
        xbuf[pl.ds(t * seqs, seqs), :] = x_ref[:, t * D_MODEL:(t + 1) * D_MODEL]
    x = xbuf[...]
    xn = _rmsnorm(x, g_mix_ref[...]).astype(BF16)
    proj = jnp.dot(xn, w_in_ref[...], preferred_element_type=F32)

    def slab(a, t):
        return a[t * seqs:(t + 1) * seqs]

    u = proj[:, :POOL_CH]
    cat = [sp_ref[:, j * POOL_CH:(j + 1) * POOL_CH] for j in range(POOL_BUF)]
    cat += [slab(u, t) for t in range(steps)]
    for t in range(steps):
        for g, w in enumerate(POOL_WINDOWS):
            cs = slice(g * POOL_GROUP, (g + 1) * POOL_GROUP)
            s = cat[POOL_BUF + t][:, cs]
            for j in range(1, w):
                s = s + cat[POOL_BUF + t - j][:, cs]
            cnt = float(min(w, pos0 + t + 1))
            z = (s / cnt - cat[POOL_BUF + t][:, cs]).astype(BF16)
            yg = jnp.dot(z, pool_w_ref[g], preferred_element_type=F32)
            mixbuf[pl.ds(t * seqs, seqs), cs] = (yg * pool_scale_ref[:, cs]).astype(BF16)
    for j in range(POOL_BUF):
        np_ref[:, j * POOL_CH:(j + 1) * POOL_CH] = cat[len(cat) - POOL_BUF + j]

    glu = proj[:, POOL_CH:POOL_CH + CONV_CH] * _sigmoid(proj[:, POOL_CH + CONV_CH:])
    ccat = [sc_ref[:, j * CONV_CH:(j + 1) * CONV_CH] for j in range(CONV_BUF)]
    ccat += [slab(glu, t) for t in range(steps)]
    for t in range(steps):
        acc = jnp.broadcast_to(conv_b_ref[...], (seqs, CONV_CH))
        for k in range(CONV_TAPS):
            acc = acc + conv_w_ref[k:k + 1, :] * ccat[t + k]
        dbuf[pl.ds(t * seqs, seqs), :] = acc
    for j in range(CONV_BUF):
        nc_ref[:, j * CONV_CH:(j + 1) * CONV_CH] = ccat[len(ccat) - CONV_BUF + j]
    y_conv = _layernorm_silu(dbuf[...], ln_g_ref[...], ln_b_ref[...])
    mixbuf[:, POOL_CH:] = y_conv.astype(BF16)

    _mix_out_and_route(x, mixbuf, w_out_ref, g_moe_ref, wr_t_ref, br_ref, tri_ref,
                       h_ref, q_ref, gate_ref, tbl_ref, xs_ref)


def _const_spec(shape, single_buffer=False):
    mode = pl.Buffered(1) if single_buffer else None
    return pl.BlockSpec(shape, lambda *_: (0,) * len(shape), pipeline_mode=mode)


def _front_weight_specs():
    return [
        _const_spec((1, D_MODEL)),
        _const_spec((D_MODEL, IN_COLS), True),
        _const_spec((len(POOL_WINDOWS), POOL_GROUP, POOL_GROUP)),
        _const_spec((1, POOL_CH)),
        _const_spec((CONV_TAPS, CONV_CH)),
        _const_spec((1, CONV_CH)),
        _const_spec((1, CONV_CH)),
        _const_spec((1, CONV_CH)),
        _const_spec((D_MODEL, D_MODEL), True),
        _const_spec((1, D_MODEL)),
        _const_spec((N_EXPERTS, D_MODEL)),
        _const_spec((N_EXPERTS, 1)),
        _const_spec((SUB_TOKENS, SUB_TOKENS)),
    ]


def _front_out(n_tok, n_tiles_all, tile_map, tok_map, extra_specs, extra_shapes):
    specs = [pl.BlockSpec((TILE, D_MODEL), tok_map),
             pl.BlockSpec((SUBLANES, TILE), lambda *i: (0, tok_map(*i)[0])),
             pl.BlockSpec((SUBLANES, TILE), lambda *i: (0, tok_map(*i)[0])),
             pl.BlockSpec((1, SUBTILES, 2, N_EXPERTS, LANES),
                          lambda *i: (tok_map(*i)[0], 0, 0, 0, 0)),
             pl.BlockSpec((SORTED_ROWS, PACKED), tile_map)]
    shapes = [jax.ShapeDtypeStruct((n_tok, D_MODEL), F32),
              jax.ShapeDtypeStruct((SUBLANES, n_tok), I32),
              jax.ShapeDtypeStruct((SUBLANES, n_tok), F32),
              jax.ShapeDtypeStruct((n_tok // TILE, SUBTILES, 2, N_EXPERTS, LANES), I32),
              jax.ShapeDtypeStruct((n_tiles_all * SORTED_ROWS, PACKED), U32)]
    return specs + extra_specs, shapes + extra_shapes


def _front_prompt(x, state_pool, state_conv, weights, conv_w8, *, pos0, n_tiles_all):
    nb, nt, _ = x.shape
    rows = TILE
    tiles = nt // rows
    tok_map = lambda b, t: (b * tiles + t, 0)
    out_specs, out_shape = _front_out(
        nb * nt, n_tiles_all, tok_map, tok_map,
        [pl.BlockSpec((1, 1, POOL_BUF, POOL_CH), lambda b, t: (0, b, 0, 0)),
         pl.BlockSpec((1, 1, CONV_BUF, CONV_CH), lambda b, t: (0, b, 0, 0))],
        [jax.ShapeDtypeStruct((1, nb, POOL_BUF, POOL_CH), F32),
         jax.ShapeDtypeStruct((1, nb, CONV_BUF, CONV_CH), F32)])
    kern = functools.partial(_front_prompt_kernel, pos0=pos0, rows=rows)
    return pl.pallas_call(
        kern,
        grid=(nb, tiles),
        in_specs=[pl.BlockSpec((1, rows, D_MODEL), lambda b, t: (b, t, 0)),
                  pl.BlockSpec((1, 1, POOL_BUF, POOL_CH), lambda b, t: (0, b, 0, 0)),
                  pl.BlockSpec((1, 1, CONV_BUF, CONV_CH), lambda b, t: (0, b, 0, 0))]
                 + _front_weight_specs() + [_const_spec((CONV_TAPS, SUBLANES, CONV_CH))],
        out_specs=out_specs,
        out_shape=out_shape,
        scratch_shapes=[pltpu.VMEM((len(POOL_WINDOWS), POOL_HIST + rows, POOL_CH), F32),
                        pltpu.VMEM((SUBLANES, CONV_HIST + rows, CONV_CH), F32),
                        pltpu.VMEM((rows, CONV_CH), F32),
                        pltpu.VMEM((rows, D_MODEL), BF16)],
        compiler_params=pltpu.CompilerParams(
            dimension_semantics=("arbitrary", "arbitrary"), vmem_limit_bytes=VMEM_LIMIT_BYTES),
        name="front_prompt",
    )(x, state_pool, state_conv, *weights, conv_w8)


def _front_sample(x, state_pool, state_conv, weights, xs_all, *, pos0, tile0):
    nb, nt, _ = x.shape
    seqs = TILE // nt
    n_tiles_all = xs_all.shape[0] // SORTED_ROWS
    tok_map = lambda i: (i, 0)
    out_specs, out_shape = _front_out(
        nb * nt, n_tiles_all, lambda i: (tile0 + i, 0), tok_map,
        [pl.BlockSpec((seqs, POOL_BUF * POOL_CH), lambda i: (i, 0)),
         pl.BlockSpec((seqs, CONV_BUF * CONV_CH), lambda i: (i, 0))],
        [jax.ShapeDtypeStruct((nb, POOL_BUF * POOL_CH), F32),
         jax.ShapeDtypeStruct((nb, CONV_BUF * CONV_CH), F32)])
    kern = functools.partial(_front_sample_kernel, pos0=pos0, seqs=seqs, steps=nt)
    n_in = 3 + len(weights)
    return pl.pallas_call(
        kern,
        grid=(nb // seqs,),
        in_specs=[pl.BlockSpec((seqs, nt * D_MODEL), lambda i: (i, 0)),
                  pl.BlockSpec((seqs, POOL_BUF * POOL_CH), lambda i: (i, 0)),
                  pl.BlockSpec((seqs, CONV_BUF * CONV_CH), lambda i: (i, 0))]
                 + _front_weight_specs() + [pl.BlockSpec(memory_space=pl.ANY)],
        out_specs=out_specs,
        out_shape=out_shape,
        scratch_shapes=[pltpu.VMEM((TILE, D_MODEL), F32),
                        pltpu.VMEM((TILE, CONV_CH), F32),
                        pltpu.VMEM((TILE, D_MODEL), BF16)],
        input_output_aliases={n_in: 4},
        compiler_params=pltpu.CompilerParams(
            dimension_semantics=("arbitrary",), vmem_limit_bytes=VMEM_LIMIT_BYTES),
        name="front_sample",
    )(x.reshape(nb, nt * D_MODEL), state_pool.reshape(nb, POOL_BUF * POOL_CH),
      state_conv.reshape(nb, CONV_BUF * CONV_CH), *weights, xs_all)


def _moe_kernel(be_ref, nused_ref, src_ref, dst_ref, first_ref, wslot_ref, nxt_ref, hasb_ref,
                xs_hbm, wgu_hbm, bgu_ref, wd_hbm, bd_ref, ys_hbm,
                xa, xb, oa, ob, wgu_f32, wd_f32, wgu_bf, wd_bf,
                gsem_a, gsem_b, ssem_a, ssem_b, wsem):
    i = pl.program_id(0)
    n_used = nused_ref[0]
    par = i % 2
    slot = lax.rem(i, X_SLOTS)
    half_a = 2 * i
    half_b = 2 * i + 1

    def weight_copies(e, s):
        return (pltpu.make_async_copy(wgu_hbm.at[e], wgu_f32.at[s], wsem.at[s]),
                pltpu.make_async_copy(wd_hbm.at[e], wd_f32.at[s], wsem.at[s]))

    def gather(half, buf, sem):
        for g in range(HALF_GROUPS):
            hrow = pl.multiple_of(src_ref[half * HALF_GROUPS + g], SUBLANES)
            pltpu.make_async_copy(xs_hbm.at[pl.ds(hrow, SUBLANES)],
                                  buf.at[pl.ds(g * SUBLANES, SUBLANES)], sem).start()

    def gather_wait(buf, sem):
        pltpu.make_async_copy(xs_hbm.at[pl.ds(0, HALF_ROWS)], buf, sem).wait()

    def scatter(half, buf, sem):
        for g in range(HALF_GROUPS):
            hrow = pl.multiple_of(dst_ref[half * HALF_GROUPS + g], SUBLANES)
            pltpu.make_async_copy(buf.at[pl.ds(g * SUBLANES, SUBLANES)],
                                  ys_hbm.at[pl.ds(hrow, SUBLANES)], sem).start()

    def scatter_wait(buf, sem):
        pltpu.make_async_copy(buf, ys_hbm.at[pl.ds(0, HALF_ROWS)], sem).wait()

    def experts(xbuf, obuf):
        x = _unpack_halves(xbuf[slot]).astype(BF16)
        gu = jnp.dot(x, wgu_bf[...], preferred_element_type=F32) + bgu_ref[0]
        g = jnp.minimum(gu[:, :D_FF], SWIGLU_LIMIT)
        up = jnp.clip(gu[:, D_FF:], -SWIGLU_LIMIT, SWIGLU_LIMIT)
        hdn = (up + 1.0) * (g * _sigmoid(SWIGLU_ALPHA * g))
        y = jnp.dot(hdn.astype(BF16), wd_bf[...], preferred_element_type=F32) + bd_ref[0]
        obuf[par] = _pack_halves(y.astype(BF16).astype(F32))

    def gather_block(blk, slot):
        blk = jnp.minimum(blk, n_used - 1)
        gather(2 * blk, xa.at[slot], gsem_a.at[slot])
        gather(2 * blk + 1, xb.at[slot], gsem_b.at[slot])

    def gather_block_wait(slot):
        gather_wait(xa.at[slot], gsem_a.at[slot])
        gather_wait(xb.at[slot], gsem_b.at[slot])

    @pl.when(i == 0)
    def _():
        gather_block(0, 0)
        gather_block(1, 1)
        for c in weight_copies(be_ref[0], 0):
            c.start(priority=WEIGHT_DMA_PRIORITY)

    @pl.when(i < n_used)
    def _():
        @pl.when(first_ref[i] == 1)
        def _():
            s = wslot_ref[i]
            for c in weight_copies(be_ref[i], s):
                c.wait()
            wgu_bf[...] = wgu_f32[s].astype(BF16)
            wd_bf[...] = wd_f32[s].astype(BF16)

            @pl.when(nxt_ref[i] < N_EXPERTS)
            def _():
                for c in weight_copies(nxt_ref[i], 1 - s):
                    c.start(priority=WEIGHT_DMA_PRIORITY)

        def wait_writeback(step, p):
            scatter_wait(oa.at[p], ssem_a.at[p])

            @pl.when(hasb_ref[step] == 1)
            def _():
                scatter_wait(ob.at[p], ssem_b.at[p])

        @pl.when(i >= 2)
        def _():
            wait_writeback(i - 2, par)

        gather_block_wait(slot)

        @pl.when(hasb_ref[i] == 1)
        def _():
            gather_block(i + 2, lax.rem(i + 2, X_SLOTS))
            experts(xa, oa)
            scatter(half_a, oa.at[par], ssem_a.at[par])
            experts(xb, ob)
            scatter(half_b, ob.at[par], ssem_b.at[par])

        @pl.when(hasb_ref[i] == 0)
        def _():
            gather_block(i + 2, lax.rem(i + 2, X_SLOTS))
            experts(xa, oa)
            scatter(half_a, oa.at[par], ssem_a.at[par])

        @pl.when(i == n_used - 1)
        def _():
            wait_writeback(i, par)
            gather_block_wait(lax.rem(i + 1, X_SLOTS))
            gather_block_wait(lax.rem(i + 2, X_SLOTS))

            @pl.when(i >= 1)
            def _():
                wait_writeback(i - 1, 1 - par)


def _moe(xs_all, tables, w_gate_up, b_gate_up, w_down, b_down):
    block_e, n_used, src, dst, first, wslot, nxt, hasb = tables
    n_blocks = block_e.shape[0]

    def e_map(i, be, nu, *_):
        return (be[jnp.minimum(i, nu[0] - 1)], 0, 0)

    x_buf = pltpu.VMEM((X_SLOTS, HALF_ROWS, PACKED), U32)
    o_buf = pltpu.VMEM((2, HALF_ROWS, PACKED), U32)
    any_spec = pl.BlockSpec(memory_space=pl.ANY)
    grid_spec = pltpu.PrefetchScalarGridSpec(
        num_scalar_prefetch=8,
        grid=(n_blocks,),
        in_specs=[any_spec,
                  any_spec,
                  pl.BlockSpec((1, 1, 2 * D_FF), e_map),
                  any_spec,
                  pl.BlockSpec((1, 1, D_MODEL), e_map)],
        out_specs=any_spec,
        scratch_shapes=[x_buf, x_buf, o_buf, o_buf,
                        pltpu.VMEM((2, D_MODEL, 2 * D_FF), F32),
                        pltpu.VMEM((2, D_FF, D_MODEL), F32),
                        pltpu.VMEM((D_MODEL, 2 * D_FF), BF16),
                        pltpu.VMEM((D_FF, D_MODEL), BF16)]
                       + [pltpu.SemaphoreType.DMA((X_SLOTS,))] * 2 + [pltpu.SemaphoreType.DMA((2,))] * 3,
    )
    return pl.pallas_call(
        _moe_kernel,
        grid_spec=grid_spec,
        out_shape=jax.ShapeDtypeStruct((xs_all.shape[0] + 2 * MOE_ROWS, PACKED), U32),
        compiler_params=pltpu.CompilerParams(
            dimension_semantics=("arbitrary",), vmem_limit_bytes=VMEM_LIMIT_BYTES),
        name="moe_experts",
    )(block_e, n_used, src, dst, first, wslot, nxt, hasb,
      xs_all, w_gate_up, b_gate_up.reshape(N_EXPERTS, 1, 2 * D_FF), w_down,
      b_down.reshape(N_EXPERTS, 1, D_MODEL))


def _final_kernel(used_ref, h_ref, ys_ref, q_ref, gate_ref, g_ref, o_ref, *, tile0, slabs):
    i = pl.program_id(0)

    def to_cols(a):
        pad = jnp.zeros((LANES - SUBLANES, TILE), F32)
        return jnp.concatenate([a, pad], axis=0).T

    qc_all = to_cols(q_ref[...].astype(F32))
    gc_all = to_cols(gate_ref[...])
    parts = []
    for s in range(SUBTILES):
        ts = slice(s * SUB_TOKENS, (s + 1) * SUB_TOKENS)
        used = used_ref[(tile0 + i) * SUBTILES + s]
        qc, gc = qc_all[ts], gc_all[ts]
        acc = h_ref[ts, :]
        for c in range(SUB_SORTED // SORT_CHUNK):
            jl = (c * SORT_CHUNK + lax.broadcasted_iota(I32, (SUB_TOKENS, SORT_CHUNK), 1)).astype(F32)
            pg = sum(jnp.where(qc[:, k:k + 1] == jl, gc[:, k:k + 1], 0.0) for k in range(TOP_K))
            rid = c * SORT_CHUNK + lax.broadcasted_iota(I32, (SORT_CHUNK, 1), 0)
            rows = ys_ref[pl.ds(s * SUB_SORTED + c * SORT_CHUNK, SORT_CHUNK), :]
            ys = jnp.where(rid < used, _unpack_halves(rows), 0.0)
            acc = acc + jnp.dot(pg.astype(BF16), ys.astype(BF16), preferred_element_type=F32)
        parts.append(acc)
    y = _rmsnorm(jnp.concatenate(parts, axis=0), g_ref[...])
    if slabs == 1:
        o_ref[...] = y
    else:
        seqs = TILE // slabs
        for t in range(slabs):
            o_ref[:, t * D_MODEL:(t + 1) * D_MODEL] = y[t * seqs:(t + 1) * seqs]


def _final(h, ys_all, q, gates, used, g_final, *, tile0, slabs):
    n_tok = h.shape[0]
    if slabs == 1:
        out_spec = pl.BlockSpec((TILE, D_MODEL), lambda i, u: (i, 0))
        out_shape = jax.ShapeDtypeStruct((n_tok, D_MODEL), F32)
    else:
        seqs = TILE // slabs
        out_spec = pl.BlockSpec((seqs, slabs * D_MODEL), lambda i, u: (i, 0))
        out_shape = jax.ShapeDtypeStruct((n_tok // slabs, slabs * D_MODEL), F32)
    grid_spec = pltpu.PrefetchScalarGridSpec(
        num_scalar_prefetch=1,
        grid=(n_tok // TILE,),
        in_specs=[pl.BlockSpec((TILE, D_MODEL), lambda i, u: (i, 0)),
                  pl.BlockSpec((SORTED_ROWS, PACKED), lambda i, u: (tile0 + i, 0)),
                  pl.BlockSpec((SUBLANES, TILE), lambda i, u: (0, i)),
                  pl.BlockSpec((SUBLANES, TILE), lambda i, u: (0, i)),
                  pl.BlockSpec((1, D_MODEL), lambda i, u: (0, 0))],
        out_specs=out_spec,
    )
    return pl.pallas_call(
        functools.partial(_final_kernel, tile0=tile0, slabs=slabs),
        grid_spec=grid_spec,
        out_shape=out_shape,
        compiler_params=pltpu.CompilerParams(
            dimension_semantics=("arbitrary",), vmem_limit_bytes=VMEM_LIMIT_BYTES),
        name="final_norm",
    )(used, h, ys_all, q, gates, g_final)


def _block_tables(tbl, n_blocks):
    tbl = tbl.reshape(-1, 2, N_EXPERTS, LANES)
    cnt = tbl[:, 0, :, 0]
    off = tbl[:, 1, :, 0]
    run = jnp.cumsum(cnt, axis=0) - cnt
    total = jnp.sum(cnt, axis=0)
    nblk = (total + MOE_ROWS - 1) // MOE_ROWS
    blk_end = jnp.cumsum(nblk)
    blk_start = blk_end - nblk
    bidx = jnp.arange(n_blocks, dtype=I32)
    eidx = jnp.arange(N_EXPERTS, dtype=I32)
    block_e = jnp.minimum(jnp.sum(blk_end[None, :] <= bidx[:, None], axis=1), N_EXPERTS - 1).astype(I32)
    hot = block_e[:, None] == eidx[None, :]

    def per_block(v):
        return jnp.sum(jnp.where(hot, v[..., None, :], 0), axis=-1)

    row0 = (bidx - per_block(blk_start)) * MOE_ROWS
    n_tiles = cnt.shape[0]
    groups = jnp.arange(MOE_GROUPS, dtype=I32)
    erow = row0[:, None] + groups[None, :] * SUBLANES
    run_b, cnt_b, off_b = per_block(run), per_block(cnt), per_block(off)
    inside = jnp.logical_and(run_b[:, :, None] <= erow[None], erow[None] < (run_b + cnt_b)[:, :, None])
    tile_base = jnp.arange(n_tiles, dtype=I32)[:, None] * SUB_SORTED + off_b - run_b
    src = jnp.sum(jnp.where(inside, tile_base[:, :, None] + erow[None], 0), axis=0)
    dump = n_tiles * SUB_SORTED + (bidx % 2)[:, None] * MOE_ROWS + groups[None, :] * SUBLANES
    dst = jnp.where(erow < per_block(total)[:, None], src, dump).reshape(-1)
    src = src.reshape(-1)
    used = jnp.sum(cnt, axis=1)
    has_rows = nblk > 0
    first = (bidx == per_block(blk_start)).astype(I32)
    wslot = per_block((jnp.cumsum(has_rows) - 1) % 2)
    later = jnp.where(jnp.logical_and(has_rows[None, :], eidx[None, :] > eidx[:, None]), eidx[None, :], N_EXPERTS)
    nxt = per_block(jnp.min(later, axis=1))
    hasb = (per_block(total) - row0 > HALF_ROWS).astype(I32)
    return ((block_e, blk_end[-1:].astype(I32), src.astype(I32), dst.astype(I32), first,
             wslot.astype(I32), nxt.astype(I32), hasb), used.astype(I32))


def kernel(x_prompt, x_sample, state_pool, state_conv, g_mix, w_in, pool_w, pool_scale, conv_w, conv_b,
           conv_ln_g, conv_ln_b, w_out, g_moe, w_router, b_router, w_gate_up, b_gate_up, w_down, b_down,
           g_final):
    nb_p, nt_p, _ = x_prompt.shape
    nb_s, nt_s, _ = x_sample.shape
    n_p, n_s = nb_p * nt_p, nb_s * nt_s
    past_len = 16384
    assert nt_p % TILE == 0 and TILE % nt_s == 0 and n_s % TILE == 0
    tiles_p, tiles_s = n_p // TILE, n_s // TILE
    n_tiles = tiles_p + tiles_s

    tri_shape = (SUB_TOKENS, SUB_TOKENS)
    tri = (lax.broadcasted_iota(I32, tri_shape, 0) < lax.broadcasted_iota(I32, tri_shape, 1)).astype(BF16)
    weights = (g_mix[0][None], w_in[0].astype(BF16), pool_w[0].astype(BF16), pool_scale[0][None],
               conv_w[0], conv_b[0][None], conv_ln_g[0][None], conv_ln_b[0][None],
               w_out[0].astype(BF16), g_moe[0][None], w_router[0].T.astype(BF16),
               b_router[0][:, None], tri)

    zero_pool = jnp.zeros((1, nb_p, POOL_BUF, POOL_CH), F32)
    zero_conv = jnp.zeros((1, nb_p, CONV_BUF, CONV_CH), F32)
    h_p, q_p, gate_p, tbl_p, xs_all, pool_p, conv_p = _front_prompt(
        x_prompt, zero_pool, zero_conv, weights,
        jnp.broadcast_to(conv_w[0][:, None, :], (CONV_TAPS, SUBLANES, CONV_CH)), pos0=0, n_tiles_all=n_tiles)
    h_s, q_s, gate_s, tbl_s, xs_all, pool_s, conv_s = _front_sample(
        x_sample, state_pool[0], state_conv[0], weights, xs_all, pos0=past_len, tile0=tiles_p)

    sorted_total = n_tiles * SUBTILES * (SUB_TOKENS * TOP_K + N_EXPERTS * (SUBLANES - 1))
    n_blocks = -(-sorted_total // MOE_ROWS) + N_EXPERTS
    tables, used = _block_tables(jnp.concatenate([tbl_p, tbl_s], axis=0), n_blocks)
    ys_all = _moe(xs_all, tables, w_gate_up[0], b_gate_up[0], w_down[0], b_down[0])

    gf = g_final[None]
    y_prompt = _final(h_p, ys_all, q_p, gate_p, used, gf, tile0=0, slabs=1)
    y_sample = _final(h_s, ys_all, q_s, gate_s, used, gf, tile0=tiles_p, slabs=nt_s)
    return (y_prompt.reshape(nb_p, nt_p, D_MODEL), y_sample.reshape(nb_s, nt_s, D_MODEL),
            pool_p, conv_p,
            pool_s.reshape(1, nb_s, POOL_BUF, POOL_CH), conv_s.reshape(1, nb_s, CONV_BUF, CONV_CH))
```

```python
import functools

import jax
import jax.numpy as jnp
from jax import lax
from jax.experimental import pallas as pl
from jax.experimental.pallas import tpu as pltpu

D_MODEL = 1024
POOL_CH = 512
CONV_CH = 512
POOL_WINDOWS = (2, 4, 8, 16)
POOL_GROUP = 128
POOL_BUF = 15
CONV_TAPS = 31
CONV_BUF = 30
IN_COLS = POOL_CH + 2 * CONV_CH
N_EXPERTS = 32
TOP_K = 4
D_FF = 1024
SWIGLU_LIMIT = 7.0
SWIGLU_ALPHA = 1.702
RMS_EPS = 1e-5
LN_EPS = 1e-5
PAST_LEN = 16384

LANES = 128
SUBLANES = 8
VMEM_LIMIT_BYTES = 60 * 1024 * 1024

TILE = 512
SUB_TOKENS = 256
SUBTILES = TILE // SUB_TOKENS
SAMPLE_SEQS = 64
CONV_CHUNK = 32
SORT_CHUNK = 256
FRONT_SORT_CHUNK = 640
SUB_SORTED = -(-(SUB_TOKENS * TOP_K + N_EXPERTS * (SUBLANES - 1)) // SORT_CHUNK) * SORT_CHUNK
SORTED_ROWS = SUBTILES * SUB_SORTED
MOE_ROWS = 512
HALF_ROWS = MOE_ROWS // 2
MOE_GROUPS = MOE_ROWS // SUBLANES
HALF_GROUPS = MOE_GROUPS // 2
X_SLOTS = 3
WEIGHT_DMA_PRIORITY = 1

POOL_HIST = 24
CONV_HIST = 32

BF16 = jnp.bfloat16
F32 = jnp.float32
I32 = jnp.int32
U32 = jnp.uint32
PACKED = D_MODEL // 2


def _rmsnorm(x, g):
    ms = jnp.mean(x * x, axis=-1, keepdims=True)
    return x * lax.rsqrt(ms + RMS_EPS) * g


def _sigmoid(x):
    return 1.0 / (1.0 + jnp.exp(-x))


def _layernorm_silu(d, g, b):
    mu = jnp.mean(d, axis=-1, keepdims=True)
    c = d - mu
    var = jnp.mean(c * c, axis=-1, keepdims=True)
    n = c * lax.rsqrt(var + LN_EPS) * g + b
    return n * _sigmoid(n)


def _pack_halves(a):
    half = a.shape[1] // 2
    hi = lax.bitcast_convert_type(a[:, :half], U32)
    lo = lax.bitcast_convert_type(a[:, half:], U32)
    return hi | (lo >> 16)


def _unpack_halves(w):
    hi = lax.bitcast_convert_type(w & jnp.uint32(0xFFFF0000), F32)
    lo = lax.bitcast_convert_type(w << 16, F32)
    return jnp.concatenate([hi, lo], axis=1)


def _route_and_sort(hn_bf, wr_t_ref, br_ref, tri_ref, q_ref, gate_ref, tbl_ref, xs_ref):
    rows = hn_bf.shape[0]
    lt = lax.dot_general(wr_t_ref[...], hn_bf, (((1,), (1,)), ((), ())),
                         preferred_element_type=F32)
    lt = lt + br_ref[...]
    eidx = lax.broadcasted_iota(I32, (N_EXPERTS, rows), 0)
    vals, hots = [], []
    for _ in range(TOP_K):
        m = jnp.max(lt, axis=0, keepdims=True)
        sel = jnp.min(jnp.where(lt == m, eidx, N_EXPERTS), axis=0, keepdims=True)
        hit = eidx == sel
        vals.append(m)
        hots.append(hit)
        lt = jnp.where(hit, -jnp.inf, lt)
    v = jnp.concatenate(vals, axis=0)
    e = jnp.exp(v - vals[0])
    gate = e / jnp.sum(e, axis=0, keepdims=True)
    gate_ref[...] = jnp.concatenate([gate, jnp.zeros((SUBLANES - TOP_K, rows), F32)], axis=0)

    sel_cnt = sum(h.astype(F32) for h in hots)
    lower = (lax.broadcasted_iota(I32, (N_EXPERTS, N_EXPERTS), 1)
             < lax.broadcasted_iota(I32, (N_EXPERTS, N_EXPERTS), 0)).astype(BF16)
    q_rows = []
    for s in range(SUBTILES):
        ts = slice(s * SUB_TOKENS, (s + 1) * SUB_TOKENS)
        sel_s = sel_cnt[:, ts]
        prefix = jnp.dot(sel_s.astype(BF16), tri_ref[...], preferred_element_type=F32)
        cnt = jnp.sum(sel_s, axis=1, keepdims=True)
        cnt_u = jnp.floor((cnt + (SUBLANES - 1)) * (1.0 / SUBLANES))
        cnt_u_l = jnp.broadcast_to(cnt_u, (N_EXPERTS, LANES))
        off_u_l = jnp.dot(lower, cnt_u_l.astype(BF16), preferred_element_type=F32)
        off_l = off_u_l * float(SUBLANES)
        tbl_ref[0, s, 0] = (cnt_u_l * float(SUBLANES)).astype(I32)
        tbl_ref[0, s, 1] = off_l.astype(I32)

        pos = prefix + off_l[:, :1]
        qs = [jnp.sum(jnp.where(h[:, ts], pos, 0.0), axis=0, keepdims=True).astype(I32) for h in hots]
        q_rows.append(jnp.concatenate(qs + [jnp.zeros((SUBLANES - TOP_K, SUB_TOKENS), I32)], axis=0))

        hn_s = hn_bf[ts]
        for c in range(SUB_SORTED // FRONT_SORT_CHUNK):
            r0 = c * FRONT_SORT_CHUNK
            jidx = r0 + lax.broadcasted_iota(I32, (FRONT_SORT_CHUNK, SUB_TOKENS), 0)
            p = sum(jnp.where(jidx == qk, 1.0, 0.0) for qk in qs)
            xs = jnp.dot(p.astype(BF16), hn_s, preferred_element_type=F32)
            xs_ref[pl.ds(s * SUB_SORTED + r0, FRONT_SORT_CHUNK), :] = _pack_halves(xs)
    q_ref[...] = jnp.concatenate(q_rows, axis=1)


def _mix_out_and_route(x, mixbuf, w_out_ref, g_moe_ref, wr_t_ref, br_ref, tri_ref,
                       h_ref, q_ref, gate_ref, tbl_ref, xs_ref):
    h = x + jnp.dot(mixbuf[...], w_out_ref[...], preferred_element_type=F32)
    h_ref[...] = h
    hn = _rmsnorm(h, g_moe_ref[...])
    _route_and_sort(hn.astype(BF16), wr_t_ref, br_ref, tri_ref, q_ref, gate_ref, tbl_ref, xs_ref)


def _front_prompt_kernel(x_ref, sp_ref, sc_ref, g_mix_ref, w_in_ref, pool_w_ref, pool_scale_ref,
                         conv_w_ref, conv_b_ref, ln_g_ref, ln_b_ref, w_out_ref, g_moe_ref,
                         wr_t_ref, br_ref, tri_ref, conv_w8_ref,
                         h_ref, q_ref, gate_ref, tbl_ref, xs_ref, np_ref, nc_ref,
                         ubuf, gsh, dbuf, mixbuf, *, pos0, rows):
    del conv_w_ref
    t = pl.program_id(1)

    @pl.when(t == 0)
    def _():
        ubuf[:, pl.ds(0, POOL_HIST), :] = jnp.zeros((len(POOL_WINDOWS), POOL_HIST, POOL_CH), F32)
        ubuf[0, pl.ds(POOL_HIST - POOL_BUF, POOL_BUF), :] = sp_ref[0, 0]
        gsh[0, pl.ds(CONV_HIST - CONV_BUF, CONV_BUF), :] = sc_ref[0, 0]

    x = x_ref[0]
    xn = _rmsnorm(x, g_mix_ref[...]).astype(BF16)
    proj = jnp.dot(xn, w_in_ref[...], preferred_element_type=F32)

    u = proj[:, :POOL_CH]
    ubuf[0, pl.ds(POOL_HIST, rows), :] = u
    pos = pos0 + t * rows + lax.broadcasted_iota(I32, (rows, 1), 0)
    span = rows + POOL_HIST - SUBLANES
    for g, w in enumerate(POOL_WINDOWS):
        assert w == 2 ** (g + 1) and w // 2 <= SUBLANES
        c0 = g * POOL_GROUP
        cs = slice(c0, c0 + POOL_GROUP)
        a = ubuf[g, pl.ds(SUBLANES, span), c0:] + ubuf[g, pl.ds(SUBLANES - w // 2, span), c0:]
        if g + 1 < len(POOL_WINDOWS):
            ubuf[g + 1, pl.ds(SUBLANES, span), c0 + POOL_GROUP:] = a[:, POOL_GROUP:]
        s = a[span - rows:, :POOL_GROUP]
        cnt = jnp.minimum(w, pos + 1).astype(F32)
        z = (s / cnt - u[:, cs]).astype(BF16)
        yg = jnp.dot(z, pool_w_ref[g], preferred_element_type=F32)
        mixbuf[:, cs] = (yg * pool_scale_ref[:, cs]).astype(BF16)

    glu = proj[:, POOL_CH:POOL_CH + CONV_CH] * _sigmoid(proj[:, POOL_CH + CONV_CH:])
    gsh[0, pl.ds(CONV_HIST, rows), :] = glu
    for s in range(1, SUBLANES):
        gsh[s, pl.ds(0, CONV_HIST - s), :] = gsh[0, pl.ds(s, CONV_HIST - s), :]
        gsh[s, pl.ds(CONV_HIST - s, rows), :] = glu

    def conv_chunk(i, carry):
        r0 = pl.multiple_of(i * CONV_CHUNK, CONV_CHUNK)
        acc = jnp.broadcast_to(conv_b_ref[...], (CONV_CHUNK // SUBLANES, SUBLANES, CONV_CH))
        for k in range(CONV_TAPS):
            first = CONV_HIST - CONV_BUF + k
            s = first % SUBLANES
            rows_k = gsh[s, pl.ds(r0 + (first - s), CONV_CHUNK), :]
            acc = acc + conv_w8_ref[k][None] * rows_k.reshape(CONV_CHUNK // SUBLANES, SUBLANES, CONV_CH)
        dbuf[pl.ds(r0, CONV_CHUNK), :] = acc.reshape(CONV_CHUNK, CONV_CH)
        return carry

    lax.fori_loop(0, rows // CONV_CHUNK, conv_chunk, 0)
    y_conv = _layernorm_silu(dbuf[...], ln_g_ref[...], ln_b_ref[...])
    mixbuf[:, POOL_CH:] = y_conv.astype(BF16)

    new_pool = ubuf[0, pl.ds(rows + POOL_HIST - POOL_BUF, POOL_BUF), :]
    new_conv = gsh[0, pl.ds(rows + CONV_HIST - CONV_BUF, CONV_BUF), :]
    np_ref[0, 0] = new_pool
    nc_ref[0, 0] = new_conv
    ubuf[0, pl.ds(POOL_HIST - POOL_BUF, POOL_BUF), :] = new_pool
    gsh[0, pl.ds(CONV_HIST - CONV_BUF, CONV_BUF), :] = new_conv

    _mix_out_and_route(x, mixbuf, w_out_ref, g_moe_ref, wr_t_ref, br_ref, tri_ref,
                       h_ref, q_ref, gate_ref, tbl_ref, xs_ref)


def _front_sample_kernel(x_ref, sp_ref, sc_ref, g_mix_ref, w_in_ref, pool_w_ref, pool_scale_ref,
                         conv_w_ref, conv_b_ref, ln_g_ref, ln_b_ref, w_out_ref, g_moe_ref,
                         wr_t_ref, br_ref, tri_ref, xs_in,
                         h_ref, q_ref, gate_ref, tbl_ref, xs_ref, np_ref, nc_ref,
                         xbuf, dbuf, mixbuf, *, pos0, seqs, steps):
    del xs_in
    for t in range(steps):
        xbuf[pl.ds(t * seqs, seqs), :] = x_ref[:, t, :]
    x = xbuf[...]
    xn = _rmsnorm(x, g_mix_ref[...]).astype(BF16)
    proj = jnp.dot(xn, w_in_ref[...], preferred_element_type=F32)

    def slab(a, t):
        return a[t * seqs:(t + 1) * seqs]

    u = proj[:, :POOL_CH]
    cat = [sp_ref[:, j, :] for j in range(POOL_BUF)]
    cat += [slab(u, t) for t in range(steps)]
    for t in range(steps):
        for g, w in enumerate(POOL_WINDOWS):
            cs = slice(g * POOL_GROUP, (g + 1) * POOL_GROUP)
            s = cat[POOL_BUF + t][:, cs]
            for j in range(1, w):
                s = s + cat[POOL_BUF + t - j][:, cs]
            cnt = float(min(w, pos0 + t + 1))
            z = (s / cnt - cat[POOL_BUF + t][:, cs]).astype(BF16)
            yg = jnp.dot(z, pool_w_ref[g], preferred_element_type=F32)
            mixbuf[pl.ds(t * seqs, seqs), cs] = (yg * pool_scale_ref[:, cs]).astype(BF16)
    for j in range(POOL_BUF):
        np_ref[:, j, :] = cat[len(cat) - POOL_BUF + j]

    glu = proj[:, POOL_CH:POOL_CH + CONV_CH] * _sigmoid(proj[:, POOL_CH + CONV_CH:])
    ccat = [sc_ref[:, j, :] for j in range(CONV_BUF)]
    ccat += [slab(glu, t) for t in range(steps)]
    for t in range(steps):
        acc = jnp.broadcast_to(conv_b_ref[...], (seqs, CONV_CH))
        for k in range(CONV_TAPS):
            acc = acc + conv_w_ref[k:k + 1, :] * ccat[t + k]
        dbuf[pl.ds(t * seqs, seqs), :] = acc
    for j in range(CONV_BUF):
        nc_ref[:, j, :] = ccat[len(ccat) - CONV_BUF + j]
    y_conv = _layernorm_silu(dbuf[...], ln_g_ref[...], ln_b_ref[...])
    mixbuf[:, POOL_CH:] = y_conv.astype(BF16)

    _mix_out_and_route(x, mixbuf, w_out_ref, g_moe_ref, wr_t_ref, br_ref, tri_ref,
                       h_ref, q_ref, gate_ref, tbl_ref, xs_ref)


def _const_spec(shape, single_buffer=False):
    mode = pl.Buffered(1) if single_buffer else None
    return pl.BlockSpec(shape, lambda *_: (0,) * len(shape), pipeline_mode=mode)


def _front_weight_specs():
    return [
        _const_spec((1, D_MODEL)),
        _const_spec((D_MODEL, IN_COLS), True),
        _const_spec((len(POOL_WINDOWS), POOL_GROUP, POOL_GROUP)),
        _const_spec((1, POOL_CH)),
        _const_spec((CONV_TAPS, CONV_CH)),
        _const_spec((1, CONV_CH)),
        _const_spec((1, CONV_CH)),
        _const_spec((1, CONV_CH)),
        _const_spec((D_MODEL, D_MODEL), True),
        _const_spec((1, D_MODEL)),
        _const_spec((N_EXPERTS, D_MODEL)),
        _const_spec((N_EXPERTS, 1)),
        _const_spec((SUB_TOKENS, SUB_TOKENS)),
    ]


def _front_out(n_tok, n_tiles_all, tile_map, tok_map, extra_specs, extra_shapes):
    specs = [pl.BlockSpec((TILE, D_MODEL), tok_map),
             pl.BlockSpec((SUBLANES, TILE), lambda *i: (0, tok_map(*i)[0])),
             pl.BlockSpec((SUBLANES, TILE), lambda *i: (0, tok_map(*i)[0])),
             pl.BlockSpec((1, SUBTILES, 2, N_EXPERTS, LANES),
                          lambda *i: (tok_map(*i)[0], 0, 0, 0, 0)),
             pl.BlockSpec((SORTED_ROWS, PACKED), tile_map)]
    shapes = [jax.ShapeDtypeStruct((n_tok, D_MODEL), F32),
              jax.ShapeDtypeStruct((SUBLANES, n_tok), I32),
              jax.ShapeDtypeStruct((SUBLANES, n_tok), F32),
              jax.ShapeDtypeStruct((n_tok // TILE, SUBTILES, 2, N_EXPERTS, LANES), I32),
              jax.ShapeDtypeStruct((n_tiles_all * SORTED_ROWS, PACKED), U32)]
    return specs + extra_specs, shapes + extra_shapes


def _front_prompt(x, state_pool, state_conv, weights, conv_w8, *, pos0, n_tiles_all):
    nb, nt, _ = x.shape
    rows = TILE
    tiles = nt // rows
    tok_map = lambda b, t: (b * tiles + t, 0)
    out_specs, out_shape = _front_out(
        nb * nt, n_tiles_all, tok_map, tok_map,
        [pl.BlockSpec((1, 1, POOL_BUF, POOL_CH), lambda b, t: (0, b, 0, 0)),
         pl.BlockSpec((1, 1, CONV_BUF, CONV_CH), lambda b, t: (0, b, 0, 0))],
        [jax.ShapeDtypeStruct((1, nb, POOL_BUF, POOL_CH), F32),
         jax.ShapeDtypeStruct((1, nb, CONV_BUF, CONV_CH), F32)])
    kern = functools.partial(_front_prompt_kernel, pos0=pos0, rows=rows)
    return pl.pallas_call(
        kern,
        grid=(nb, tiles),
        in_specs=[pl.BlockSpec((1, rows, D_MODEL), lambda b, t: (b, t, 0)),
                  pl.BlockSpec((1, 1, POOL_BUF, POOL_CH), lambda b, t: (0, b, 0, 0)),
                  pl.BlockSpec((1, 1, CONV_BUF, CONV_CH), lambda b, t: (0, b, 0, 0))]
                 + _front_weight_specs() + [_const_spec((CONV_TAPS, SUBLANES, CONV_CH))],
        out_specs=out_specs,
        out_shape=out_shape,
        scratch_shapes=[pltpu.VMEM((len(POOL_WINDOWS), POOL_HIST + rows, POOL_CH), F32),
                        pltpu.VMEM((SUBLANES, CONV_HIST + rows, CONV_CH), F32),
                        pltpu.VMEM((rows, CONV_CH), F32),
                        pltpu.VMEM((rows, D_MODEL), BF16)],
        compiler_params=pltpu.CompilerParams(
            dimension_semantics=("arbitrary", "arbitrary"), vmem_limit_bytes=VMEM_LIMIT_BYTES),
        name="front_prompt",
    )(x, state_pool, state_conv, *weights, conv_w8)


def _front_sample(x, state_pool, state_conv, weights, xs_all, *, pos0, tile0):
    nb, nt, _ = x.shape
    seqs = TILE // nt
    n_tiles_all = xs_all.shape[0] // SORTED_ROWS
    tok_map = lambda i: (i, 0)
    out_specs, out_shape = _front_out(
        nb * nt, n_tiles_all, lambda i: (tile0 + i, 0), tok_map,
        [pl.BlockSpec((seqs, POOL_BUF, POOL_CH), lambda i: (i, 0, 0)),
         pl.BlockSpec((seqs, CONV_BUF, CONV_CH), lambda i: (i, 0, 0))],
        [jax.ShapeDtypeStruct((nb, POOL_BUF, POOL_CH), F32),
         jax.ShapeDtypeStruct((nb, CONV_BUF, CONV_CH), F32)])
    kern = functools.partial(_front_sample_kernel, pos0=pos0, seqs=seqs, steps=nt)
    n_in = 3 + len(weights)
    return pl.pallas_call(
        kern,
        grid=(nb // seqs,),
        in_specs=[pl.BlockSpec((seqs, nt, D_MODEL), lambda i: (i, 0, 0)),
                  pl.BlockSpec((seqs, POOL_BUF, POOL_CH), lambda i: (i, 0, 0)),
                  pl.BlockSpec((seqs, CONV_BUF, CONV_CH), lambda i: (i, 0, 0))]
                 + _front_weight_specs() + [pl.BlockSpec(memory_space=pl.ANY)],
        out_specs=out_specs,
        out_shape=out_shape,
        scratch_shapes=[pltpu.VMEM((TILE, D_MODEL), F32),
                        pltpu.VMEM((TILE, CONV_CH), F32),
                        pltpu.VMEM((TILE, D_MODEL), BF16)],
        input_output_aliases={n_in: 4},
        compiler_params=pltpu.CompilerParams(
            dimension_semantics=("arbitrary",), vmem_limit_bytes=VMEM_LIMIT_BYTES),
        name="front_sample",
    )(x, state_pool, state_conv, *weights, xs_all)


def _moe_kernel(be_ref, nused_ref, src_ref, dst_ref, first_ref, wslot_ref, nxt_ref, hasb_ref,
                xs_hbm, wgu_hbm, bgu_ref, wd_hbm, bd_ref, ys_hbm,
                xa, xb, oa, ob, wgu_f32, wd_f32, wgu_bf, wd_bf,
                gsem_a, gsem_b, ssem_a, ssem_b, wsem):
    i = pl.program_id(0)
    n_used = nused_ref[0]
    par = i % 2
    slot = lax.rem(i, X_SLOTS)
    half_a = 2 * i
    half_b = 2 * i + 1

    def weight_copies(e, s):
        return (pltpu.make_async_copy(wgu_hbm.at[e], wgu_f32.at[s], wsem.at[s]),
                pltpu.make_async_copy(wd_hbm.at[e], wd_f32.at[s], wsem.at[s]))

    def gather(half, buf, sem):
        for g in range(HALF_GROUPS):
            hrow = pl.multiple_of(src_ref[half * HALF_GROUPS + g], SUBLANES)
            pltpu.make_async_copy(xs_hbm.at[pl.ds(hrow, SUBLANES)],
                                  buf.at[pl.ds(g * SUBLANES, SUBLANES)], sem).start()

    def gather_wait(buf, sem):
        pltpu.make_async_copy(xs_hbm.at[pl.ds(0, HALF_ROWS)], buf, sem).wait()

    def scatter(half, buf, sem):
        for g in range(HALF_GROUPS):
            hrow = pl.multiple_of(dst_ref[half * HALF_GROUPS + g], SUBLANES)
            pltpu.make_async_copy(buf.at[pl.ds(g * SUBLANES, SUBLANES)],
                                  ys_hbm.at[pl.ds(hrow, SUBLANES)], sem).start()

    def scatter_wait(buf, sem):
        pltpu.make_async_copy(buf, ys_hbm.at[pl.ds(0, HALF_ROWS)], sem).wait()

    def experts(xbuf, obuf):
        x = _unpack_halves(xbuf[slot]).astype(BF16)
        gu = jnp.dot(x, wgu_bf[...], preferred_element_type=F32) + bgu_ref[0]
        g = jnp.minimum(gu[:, :D_FF], SWIGLU_LIMIT)
        up = jnp.clip(gu[:, D_FF:], -SWIGLU_LIMIT, SWIGLU_LIMIT)
        hdn = (up + 1.0) * (g * _sigmoid(SWIGLU_ALPHA * g))
        y = jnp.dot(hdn.astype(BF16), wd_bf[...], preferred_element_type=F32) + bd_ref[0]
        obuf[par] = _pack_halves(y.astype(BF16).astype(F32))

    def gather_block(blk, slot):
        blk = jnp.minimum(blk, n_used - 1)
        gather(2 * blk, xa.at[slot], gsem_a.at[slot])
        gather(2 * blk + 1, xb.at[slot], gsem_b.at[slot])

    def gather_block_wait(slot):
        gather_wait(xa.at[slot], gsem_a.at[slot])
        gather_wait(xb.at[slot], gsem_b.at[slot])

    @pl.when(i == 0)
    def _():
        gather_block(0, 0)
        gather_block(1, 1)
        for c in weight_copies(be_ref[0], 0):
            c.start(priority=WEIGHT_DMA_PRIORITY)

    @pl.when(i < n_used)
    def _():
        @pl.when(first_ref[i] == 1)
        def _():
            s = wslot_ref[i]
            for c in weight_copies(be_ref[i], s):
                c.wait()
            wgu_bf[...] = wgu_f32[s].astype(BF16)
            wd_bf[...] = wd_f32[s].astype(BF16)

            @pl.when(nxt_ref[i] < N_EXPERTS)
            def _():
                for c in weight_copies(nxt_ref[i], 1 - s):
                    c.start(priority=WEIGHT_DMA_PRIORITY)

        def wait_writeback(step, p):
            scatter_wait(oa.at[p], ssem_a.at[p])

            @pl.when(hasb_ref[step] == 1)
            def _():
                scatter_wait(ob.at[p], ssem_b.at[p])

        @pl.when(i >= 2)
        def _():
            wait_writeback(i - 2, par)

        gather_block_wait(slot)

        @pl.when(hasb_ref[i] == 1)
        def _():
            gather_block(i + 2, lax.rem(i + 2, X_SLOTS))
            experts(xa, oa)
            scatter(half_a, oa.at[par], ssem_a.at[par])
            experts(xb, ob)
            scatter(half_b, ob.at[par], ssem_b.at[par])

        @pl.when(hasb_ref[i] == 0)
        def _():
            gather_block(i + 2, lax.rem(i + 2, X_SLOTS))
            experts(xa, oa)
            scatter(half_a, oa.at[par], ssem_a.at[par])

        @pl.when(i == n_used - 1)
        def _():
            wait_writeback(i, par)
            gather_block_wait(lax.rem(i + 1, X_SLOTS))
            gather_block_wait(lax.rem(i + 2, X_SLOTS))

            @pl.when(i >= 1)
            def _():
                wait_writeback(i - 1, 1 - par)


def _moe(xs_all, tables, w_gate_up, b_gate_up, w_down, b_down):
    block_e, n_used, src, dst, first, wslot, nxt, hasb = tables
    n_blocks = block_e.shape[0]

    def e_map(i, be, nu, *_):
        return (be[jnp.minimum(i, nu[0] - 1)], 0, 0)

    x_buf = pltpu.VMEM((X_SLOTS, HALF_ROWS, PACKED), U32)
    o_buf = pltpu.VMEM((2, HALF_ROWS, PACKED), U32)
    any_spec = pl.BlockSpec(memory_space=pl.ANY)
    grid_spec = pltpu.PrefetchScalarGridSpec(
        num_scalar_prefetch=8,
        grid=(n_blocks,),
        in_specs=[any_spec,
                  any_spec,
                  pl.BlockSpec((1, 1, 2 * D_FF), e_map),
                  any_spec,
                  pl.BlockSpec((1, 1, D_MODEL), e_map)],
        out_specs=any_spec,
        scratch_shapes=[x_buf, x_buf, o_buf, o_buf,
                        pltpu.VMEM((2, D_MODEL, 2 * D_FF), F32),
                        pltpu.VMEM((2, D_FF, D_MODEL), F32),
                        pltpu.VMEM((D_MODEL, 2 * D_FF), BF16),
                        pltpu.VMEM((D_FF, D_MODEL), BF16)]
                       + [pltpu.SemaphoreType.DMA((X_SLOTS,))] * 2 + [pltpu.SemaphoreType.DMA((2,))] * 3,
    )
    return pl.pallas_call(
        _moe_kernel,
        grid_spec=grid_spec,
        out_shape=jax.ShapeDtypeStruct((xs_all.shape[0] + 2 * MOE_ROWS, PACKED), U32),
        compiler_params=pltpu.CompilerParams(
            dimension_semantics=("arbitrary",), vmem_limit_bytes=VMEM_LIMIT_BYTES),
        name="moe_experts",
    )(block_e, n_used, src, dst, first, wslot, nxt, hasb,
      xs_all, w_gate_up, b_gate_up.reshape(N_EXPERTS, 1, 2 * D_FF), w_down,
      b_down.reshape(N_EXPERTS, 1, D_MODEL))


def _final_kernel(used_ref, h_ref, ys_ref, q_ref, gate_ref, g_ref, o_ref, *, tile0, slabs):
    i = pl.program_id(0)

    def to_cols(a):
        pad = jnp.zeros((LANES - SUBLANES, TILE), F32)
        return jnp.concatenate([a, pad], axis=0).T

    qc_all = to_cols(q_ref[...].astype(F32))
    gc_all = to_cols(gate_ref[...])
    parts = []
    for s in range(SUBTILES):
        ts = slice(s * SUB_TOKENS, (s + 1) * SUB_TOKENS)
        used = used_ref[(tile0 + i) * SUBTILES + s]
        qc, gc = qc_all[ts], gc_all[ts]
        acc = h_ref[ts, :]
        for c in range(SUB_SORTED // SORT_CHUNK):
            jl = (c * SORT_CHUNK + lax.broadcasted_iota(I32, (SUB_TOKENS, SORT_CHUNK), 1)).astype(F32)
            pg = sum(jnp.where(qc[:, k:k + 1] == jl, gc[:, k:k + 1], 0.0) for k in range(TOP_K))
            rid = c * SORT_CHUNK + lax.broadcasted_iota(I32, (SORT_CHUNK, 1), 0)
            rows = ys_ref[pl.ds(s * SUB_SORTED + c * SORT_CHUNK, SORT_CHUNK), :]
            ys = jnp.where(rid < used, _unpack_halves(rows), 0.0)
            acc = acc + jnp.dot(pg.astype(BF16), ys.astype(BF16), preferred_element_type=F32)
        parts.append(acc)
    y = _rmsnorm(jnp.concatenate(parts, axis=0), g_ref[...])
    if slabs == 1:
        o_ref[...] = y
    else:
        seqs = TILE // slabs
        for t in range(slabs):
            o_ref[:, t, :] = y[t * seqs:(t + 1) * seqs]


def _final(h, ys_all, q, gates, used, g_final, *, tile0, slabs):
    n_tok = h.shape[0]
    if slabs == 1:
        out_spec = pl.BlockSpec((TILE, D_MODEL), lambda i, u: (i, 0))
        out_shape = jax.ShapeDtypeStruct((n_tok, D_MODEL), F32)
    else:
        seqs = TILE // slabs
        out_spec = pl.BlockSpec((seqs, slabs, D_MODEL), lambda i, u: (i, 0, 0))
        out_shape = jax.ShapeDtypeStruct((n_tok // slabs, slabs, D_MODEL), F32)
    grid_spec = pltpu.PrefetchScalarGridSpec(
        num_scalar_prefetch=1,
        grid=(n_tok // TILE,),
        in_specs=[pl.BlockSpec((TILE, D_MODEL), lambda i, u: (i, 0)),
                  pl.BlockSpec((SORTED_ROWS, PACKED), lambda i, u: (tile0 + i, 0)),
                  pl.BlockSpec((SUBLANES, TILE), lambda i, u: (0, i)),
                  pl.BlockSpec((SUBLANES, TILE), lambda i, u: (0, i)),
                  pl.BlockSpec((1, D_MODEL), lambda i, u: (0, 0))],
        out_specs=out_spec,
    )
    return pl.pallas_call(
        functools.partial(_final_kernel, tile0=tile0, slabs=slabs),
        grid_spec=grid_spec,
        out_shape=out_shape,
        compiler_params=pltpu.CompilerParams(
            dimension_semantics=("arbitrary",), vmem_limit_bytes=VMEM_LIMIT_BYTES),
        name="final_norm",
    )(used, h, ys_all, q, gates, g_final)


def _block_tables(tbl, n_blocks):
    tbl = tbl.reshape(-1, 2, N_EXPERTS, LANES)
    cnt = tbl[:, 0, :, 0]
    off = tbl[:, 1, :, 0]
    run = jnp.cumsum(cnt, axis=0) - cnt
    total = jnp.sum(cnt, axis=0)
    nblk = (total + MOE_ROWS - 1) // MOE_ROWS
    blk_end = jnp.cumsum(nblk)
    blk_start = blk_end - nblk
    bidx = jnp.arange(n_blocks, dtype=I32)
    eidx = jnp.arange(N_EXPERTS, dtype=I32)
    block_e = jnp.minimum(jnp.sum(blk_end[None, :] <= bidx[:, None], axis=1), N_EXPERTS - 1).astype(I32)
    hot = block_e[:, None] == eidx[None, :]

    def per_block(v):
        return jnp.sum(jnp.where(hot, v[..., None, :], 0), axis=-1)

    row0 = (bidx - per_block(blk_start)) * MOE_ROWS
    n_tiles = cnt.shape[0]
    groups = jnp.arange(MOE_GROUPS, dtype=I32)
    erow = row0[:, None] + groups[None, :] * SUBLANES
    run_b, cnt_b, off_b = per_block(run), per_block(cnt), per_block(off)
    inside = jnp.logical_and(run_b[:, :, None] <= erow[None], erow[None] < (run_b + cnt_b)[:, :, None])
    tile_base = jnp.arange(n_tiles, dtype=I32)[:, None] * SUB_SORTED + off_b - run_b
    src = jnp.sum(jnp.where(inside, tile_base[:, :, None] + erow[None], 0), axis=0)
    dump = n_tiles * SUB_SORTED + (bidx % 2)[:, None] * MOE_ROWS + groups[None, :] * SUBLANES
    dst = jnp.where(erow < per_block(total)[:, None], src, dump).reshape(-1)
    src = src.reshape(-1)
    used = jnp.sum(cnt, axis=1)
    has_rows = nblk > 0
    first = (bidx == per_block(blk_start)).astype(I32)
    wslot = per_block((jnp.cumsum(has_rows) - 1) % 2)
    later = jnp.where(jnp.logical_and(has_rows[None, :], eidx[None, :] > eidx[:, None]), eidx[None, :], N_EXPERTS)
    nxt = per_block(jnp.min(later, axis=1))
    hasb = (per_block(total) - row0 > HALF_ROWS).astype(I32)
    return ((block_e, blk_end[-1:].astype(I32), src.astype(I32), dst.astype(I32), first,
             wslot.astype(I32), nxt.astype(I32), hasb), used.astype(I32))


def kernel(x_prompt, x_sample, state_pool, state_conv, g_mix, w_in, pool_w, pool_scale, conv_w, conv_b,
           conv_ln_g, conv_ln_b, w_out, g_moe, w_router, b_router, w_gate_up, b_gate_up, w_down, b_down,
           g_final):
    nb_p, nt_p, _ = x_prompt.shape
    nb_s, nt_s, _ = x_sample.shape
    n_p, n_s = nb_p * nt_p, nb_s * nt_s
    assert nt_p % TILE == 0 and TILE % nt_s == 0 and n_s % TILE == 0
    tiles_p, tiles_s = n_p // TILE, n_s // TILE
    n_tiles = tiles_p + tiles_s

    tri_shape = (SUB_TOKENS, SUB_TOKENS)
    tri = (lax.broadcasted_iota(I32, tri_shape, 0) < lax.broadcasted_iota(I32, tri_shape, 1)).astype(BF16)
    weights = (g_mix[0][None], w_in[0].astype(BF16), pool_w[0].astype(BF16), pool_scale[0][None],
               conv_w[0], conv_b[0][None], conv_ln_g[0][None], conv_ln_b[0][None],
               w_out[0].astype(BF16), g_moe[0][None], w_router[0].T.astype(BF16),
               b_router[0][:, None], tri)

    zero_pool = jnp.zeros((1, nb_p, POOL_BUF, POOL_CH), F32)
    zero_conv = jnp.zeros((1, nb_p, CONV_BUF, CONV_CH), F32)
    h_p, q_p, gate_p, tbl_p, xs_all, pool_p, conv_p = _front_prompt(
        x_prompt, zero_pool, zero_conv, weights,
        jnp.broadcast_to(conv_w[0][:, None, :], (CONV_TAPS, SUBLANES, CONV_CH)), pos0=0, n_tiles_all=n_tiles)
    h_s, q_s, gate_s, tbl_s, xs_all, pool_s, conv_s = _front_sample(
        x_sample, state_pool[0], state_conv[0], weights, xs_all, pos0=PAST_LEN, tile0=tiles_p)

    sorted_total = n_tiles * SUBTILES * (SUB_TOKENS * TOP_K + N_EXPERTS * (SUBLANES - 1))
    n_blocks = -(-sorted_total // MOE_ROWS) + N_EXPERTS
    tables, used = _block_tables(jnp.concatenate([tbl_p, tbl_s], axis=0), n_blocks)
    ys_all = _moe(xs_all, tables, w_gate_up[0], b_gate_up[0], w_down[0], b_down[0])

    gf = g_final[None]
    y_prompt = _final(h_p, ys_all, q_p, gate_p, used, gf, tile0=0, slabs=1)
    y_sample = _final(h_s, ys_all, q_s, gate_s, used, gf, tile0=tiles_p, slabs=nt_s)
    return (y_prompt.reshape(nb_p, nt_p, D_MODEL), y_sample,
            pool_p, conv_p,
            pool_s[None], conv_s[None])
```

```python
import functools

import jax
import jax.numpy as jnp
from jax import lax
from jax.experimental import pallas as pl
from jax.experimental.pallas import tpu as pltpu

D_MODEL = 1024
POOL_CH = 512
CONV_CH = 512
POOL_WINDOWS = (2, 4, 8, 16)
POOL_GROUP = 128
POOL_BUF = 15
CONV_TAPS = 31
CONV_BUF = 30
IN_COLS = POOL_CH + 2 * CONV_CH
N_EXPERTS = 32
TOP_K = 4
D_FF = 1024
SWIGLU_LIMIT = 7.0
SWIGLU_ALPHA = 1.702
RMS_EPS = 1e-5
LN_EPS = 1e-5
PAST_LEN = 16384

LANES = 128
SUBLANES = 8
VMEM_LIMIT_BYTES = 60 * 1024 * 1024

TILE = 512
SUB_TOKENS = 256
SUBTILES = TILE // SUB_TOKENS
SAMPLE_SEQS = 64
CONV_CHUNK = 32
SORT_CHUNK = 256
FRONT_SORT_CHUNK = 640
SUB_SORTED = -(-(SUB_TOKENS * TOP_K + N_EXPERTS * (SUBLANES - 1)) // SORT_CHUNK) * SORT_CHUNK
SORTED_ROWS = SUBTILES * SUB_SORTED
MOE_ROWS = 512
HALF_ROWS = MOE_ROWS // 2
MOE_GROUPS = MOE_ROWS // SUBLANES
HALF_GROUPS = MOE_GROUPS // 2
X_SLOTS = 3
WEIGHT_DMA_PRIORITY = 1

POOL_HIST = 24
CONV_HIST = 32

BF16 = jnp.bfloat16
F32 = jnp.float32
I32 = jnp.int32
U32 = jnp.uint32
PACKED = D_MODEL // 2


def _rmsnorm(x, g):
    ms = jnp.mean(x * x, axis=-1, keepdims=True)
    return x * lax.rsqrt(ms + RMS_EPS) * g


def _sigmoid(x):
    return 1.0 / (1.0 + jnp.exp(-x))


def _layernorm_silu(d, g, b):
    mu = jnp.mean(d, axis=-1, keepdims=True)
    c = d - mu
    var = jnp.mean(c * c, axis=-1, keepdims=True)
    n = c * lax.rsqrt(var + LN_EPS) * g + b
    return n * _sigmoid(n)


def _pack_halves(a):
    half = a.shape[1] // 2
    hi = lax.bitcast_convert_type(a[:, :half], U32)
    lo = lax.bitcast_convert_type(a[:, half:], U32)
    return hi | (lo >> 16)


def _unpack_halves(w):
    hi = lax.bitcast_convert_type(w & jnp.uint32(0xFFFF0000), F32)
    lo = lax.bitcast_convert_type(w << 16, F32)
    return jnp.concatenate([hi, lo], axis=1)


def _route_and_sort(hn_bf, wr_t_ref, br_ref, tri_ref, q_ref, gate_ref, tbl_ref, xs_ref):
    rows = hn_bf.shape[0]
    lt = lax.dot_general(wr_t_ref[...], hn_bf, (((1,), (1,)), ((), ())),
                         preferred_element_type=F32)
    lt = lt + br_ref[...]
    eidx = lax.broadcasted_iota(I32, (N_EXPERTS, rows), 0)
    vals, hots = [], []
    for _ in range(TOP_K):
        m = jnp.max(lt, axis=0, keepdims=True)
        sel = jnp.min(jnp.where(lt == m, eidx, N_EXPERTS), axis=0, keepdims=True)
        hit = eidx == sel
        vals.append(m)
        hots.append(hit)
        lt = jnp.where(hit, -jnp.inf, lt)
    v = jnp.concatenate(vals, axis=0)
    e = jnp.exp(v - vals[0])
    gate = e / jnp.sum(e, axis=0, keepdims=True)
    gate_ref[...] = jnp.concatenate([gate, jnp.zeros((SUBLANES - TOP_K, rows), F32)], axis=0)

    sel_cnt = sum(h.astype(F32) for h in hots)
    lower = (lax.broadcasted_iota(I32, (N_EXPERTS, N_EXPERTS), 1)
             < lax.broadcasted_iota(I32, (N_EXPERTS, N_EXPERTS), 0)).astype(BF16)
    q_rows = []
    for s in range(SUBTILES):
        ts = slice(s * SUB_TOKENS, (s + 1) * SUB_TOKENS)
        sel_s = sel_cnt[:, ts]
        prefix = jnp.dot(sel_s.astype(BF16), tri_ref[...], preferred_element_type=F32)
        cnt = jnp.sum(sel_s, axis=1, keepdims=True)
        cnt_u = jnp.floor((cnt + (SUBLANES - 1)) * (1.0 / SUBLANES))
        cnt_u_l = jnp.broadcast_to(cnt_u, (N_EXPERTS, LANES))
        off_u_l = jnp.dot(lower, cnt_u_l.astype(BF16), preferred_element_type=F32)
        off_l = off_u_l * float(SUBLANES)
        tbl_ref[0, s, 0] = (cnt_u_l * float(SUBLANES)).astype(I32)
        tbl_ref[0, s, 1] = off_l.astype(I32)

        pos = prefix + off_l[:, :1]
        qs = [jnp.sum(jnp.where(h[:, ts], pos, 0.0), axis=0, keepdims=True).astype(I32) for h in hots]
        q_rows.append(jnp.concatenate(qs + [jnp.zeros((SUBLANES - TOP_K, SUB_TOKENS), I32)], axis=0))

        hn_s = hn_bf[ts]
        for c in range(SUB_SORTED // FRONT_SORT_CHUNK):
            r0 = c * FRONT_SORT_CHUNK
            jidx = r0 + lax.broadcasted_iota(I32, (FRONT_SORT_CHUNK, SUB_TOKENS), 0)
            p = sum(jnp.where(jidx == qk, 1.0, 0.0) for qk in qs)
            xs = jnp.dot(p.astype(BF16), hn_s, preferred_element_type=F32)
            xs_ref[pl.ds(s * SUB_SORTED + r0, FRONT_SORT_CHUNK), :] = _pack_halves(xs)
    q_ref[...] = jnp.concatenate(q_rows, axis=1)


def _mix_out_and_route(x, mixbuf, w_out_ref, g_moe_ref, wr_t_ref, br_ref, tri_ref,
                       h_ref, q_ref, gate_ref, tbl_ref, xs_ref):
    h = x + jnp.dot(mixbuf[...], w_out_ref[...], preferred_element_type=F32)
    h_ref[...] = h
    hn = _rmsnorm(h, g_moe_ref[...])
    _route_and_sort(hn.astype(BF16), wr_t_ref, br_ref, tri_ref, q_ref, gate_ref, tbl_ref, xs_ref)


def _front_prompt_kernel(x_ref, sp_ref, sc_ref, g_mix_ref, w_in_ref, pool_w_ref, pool_scale_ref,
                         conv_w_ref, conv_b_ref, ln_g_ref, ln_b_ref, w_out_ref, g_moe_ref,
                         wr_t_ref, br_ref, tri_ref, conv_w8_ref,
                         h_ref, q_ref, gate_ref, tbl_ref, xs_ref, np_ref, nc_ref,
                         ubuf, gsh, dbuf, mixbuf, *, pos0, rows):
    del conv_w_ref
    t = pl.program_id(1)

    @pl.when(t == 0)
    def _():
        ubuf[:, pl.ds(0, POOL_HIST), :] = jnp.zeros((len(POOL_WINDOWS), POOL_HIST, POOL_CH), F32)
        ubuf[0, pl.ds(POOL_HIST - POOL_BUF, POOL_BUF), :] = sp_ref[0, 0]
        gsh[0, pl.ds(CONV_HIST - CONV_BUF, CONV_BUF), :] = sc_ref[0, 0]

    x = x_ref[0]
    xn = _rmsnorm(x, g_mix_ref[...]).astype(BF16)
    proj = jnp.dot(xn, w_in_ref[...], preferred_element_type=F32)

    u = proj[:, :POOL_CH]
    ubuf[0, pl.ds(POOL_HIST, rows), :] = u
    pos = pos0 + t * rows + lax.broadcasted_iota(I32, (rows, 1), 0)
    span = rows + POOL_HIST - SUBLANES
    for g, w in enumerate(POOL_WINDOWS):
        assert w == 2 ** (g + 1) and w // 2 <= SUBLANES
        c0 = g * POOL_GROUP
        cs = slice(c0, c0 + POOL_GROUP)
        a = ubuf[g, pl.ds(SUBLANES, span), c0:] + ubuf[g, pl.ds(SUBLANES - w // 2, span), c0:]
        if g + 1 < len(POOL_WINDOWS):
            ubuf[g + 1, pl.ds(SUBLANES, span), c0 + POOL_GROUP:] = a[:, POOL_GROUP:]
        s = a[span - rows:, :POOL_GROUP]
        cnt = jnp.minimum(w, pos + 1).astype(F32)
        z = (s / cnt - u[:, cs]).astype(BF16)
        yg = jnp.dot(z, pool_w_ref[g], preferred_element_type=F32)
        mixbuf[:, cs] = (yg * pool_scale_ref[:, cs]).astype(BF16)

    glu = proj[:, POOL_CH:POOL_CH + CONV_CH] * _sigmoid(proj[:, POOL_CH + CONV_CH:])
    gsh[0, pl.ds(CONV_HIST, rows), :] = glu
    for s in range(1, SUBLANES):
        gsh[s, pl.ds(0, CONV_HIST - s), :] = gsh[0, pl.ds(s, CONV_HIST - s), :]
        gsh[s, pl.ds(CONV_HIST - s, rows), :] = glu

    def conv_chunk(i, carry):
        r0 = pl.multiple_of(i * CONV_CHUNK, CONV_CHUNK)
        acc = jnp.broadcast_to(conv_b_ref[...], (CONV_CHUNK // SUBLANES, SUBLANES, CONV_CH))
        for k in range(CONV_TAPS):
            first = CONV_HIST - CONV_BUF + k
            s = first % SUBLANES
            rows_k = gsh[s, pl.ds(r0 + (first - s), CONV_CHUNK), :]
            acc = acc + conv_w8_ref[k][None] * rows_k.reshape(CONV_CHUNK // SUBLANES, SUBLANES, CONV_CH)
        dbuf[pl.ds(r0, CONV_CHUNK), :] = acc.reshape(CONV_CHUNK, CONV_CH)
        return carry

    lax.fori_loop(0, rows // CONV_CHUNK, conv_chunk, 0)
    y_conv = _layernorm_silu(dbuf[...], ln_g_ref[...], ln_b_ref[...])
    mixbuf[:, POOL_CH:] = y_conv.astype(BF16)

    new_pool = ubuf[0, pl.ds(rows + POOL_HIST - POOL_BUF, POOL_BUF), :]
    new_conv = gsh[0, pl.ds(rows + CONV_HIST - CONV_BUF, CONV_BUF), :]
    np_ref[0, 0] = new_pool
    nc_ref[0, 0] = new_conv
    ubuf[0, pl.ds(POOL_HIST - POOL_BUF, POOL_BUF), :] = new_pool
    gsh[0, pl.ds(CONV_HIST - CONV_BUF, CONV_BUF), :] = new_conv

    _mix_out_and_route(x, mixbuf, w_out_ref, g_moe_ref, wr_t_ref, br_ref, tri_ref,
                       h_ref, q_ref, gate_ref, tbl_ref, xs_ref)


def _front_sample_kernel(x_ref, sp_ref, sc_ref, g_mix_ref, w_in_ref, pool_w_ref, pool_scale_ref,
                         conv_w_ref, conv_b_ref, ln_g_ref, ln_b_ref, w_out_ref, g_moe_ref,
                         wr_t_ref, br_ref, tri_ref, xs_in,
                         h_ref, q_ref, gate_ref, tbl_ref, xs_ref, np_ref, nc_ref,
                         xbuf, dbuf, mixbuf, *, pos0, seqs, steps):
    del xs_in
    for t in range(steps):
        xbuf[pl.ds(t * seqs, seqs), :] = x_ref[:, t, :]
    x = xbuf[...]
    xn = _rmsnorm(x, g_mix_ref[...]).astype(BF16)
    proj = jnp.dot(xn, w_in_ref[...], preferred_element_type=F32)

    def slab(a, t):
        return a[t * seqs:(t + 1) * seqs]

    u = proj[:, :POOL_CH]
    cat = [sp_ref[j] for j in range(POOL_BUF)]
    cat += [slab(u, t) for t in range(steps)]
    for t in range(steps):
        for g, w in enumerate(POOL_WINDOWS):
            cs = slice(g * POOL_GROUP, (g + 1) * POOL_GROUP)
            s = cat[POOL_BUF + t][:, cs]
            for j in range(1, w):
                s = s + cat[POOL_BUF + t - j][:, cs]
            cnt = float(min(w, pos0 + t + 1))
            z = (s / cnt - cat[POOL_BUF + t][:, cs]).astype(BF16)
            yg = jnp.dot(z, pool_w_ref[g], preferred_element_type=F32)
            mixbuf[pl.ds(t * seqs, seqs), cs] = (yg * pool_scale_ref[:, cs]).astype(BF16)
    for j in range(POOL_BUF):
        np_ref[j] = cat[len(cat) - POOL_BUF + j]

    glu = proj[:, POOL_CH:POOL_CH + CONV_CH] * _sigmoid(proj[:, POOL_CH + CONV_CH:])
    ccat = [sc_ref[j] for j in range(CONV_BUF)]
    ccat += [slab(glu, t) for t in range(steps)]
    for t in range(steps):
        acc = jnp.broadcast_to(conv_b_ref[...], (seqs, CONV_CH))
        for k in range(CONV_TAPS):
            acc = acc + conv_w_ref[k:k + 1, :] * ccat[t + k]
        dbuf[pl.ds(t * seqs, seqs), :] = acc
    for j in range(CONV_BUF):
        nc_ref[j] = ccat[len(ccat) - CONV_BUF + j]
    y_conv = _layernorm_silu(dbuf[...], ln_g_ref[...], ln_b_ref[...])
    mixbuf[:, POOL_CH:] = y_conv.astype(BF16)

    _mix_out_and_route(x, mixbuf, w_out_ref, g_moe_ref, wr_t_ref, br_ref, tri_ref,
                       h_ref, q_ref, gate_ref, tbl_ref, xs_ref)


def _const_spec(shape, single_buffer=False):
    mode = pl.Buffered(1) if single_buffer else None
    return pl.BlockSpec(shape, lambda *_: (0,) * len(shape), pipeline_mode=mode)


def _front_weight_specs():
    return [
        _const_spec((1, D_MODEL)),
        _const_spec((D_MODEL, IN_COLS), True),
        _const_spec((len(POOL_WINDOWS), POOL_GROUP, POOL_GROUP)),
        _const_spec((1, POOL_CH)),
        _const_spec((CONV_TAPS, CONV_CH)),
        _const_spec((1, CONV_CH)),
        _const_spec((1, CONV_CH)),
        _const_spec((1, CONV_CH)),
        _const_spec((D_MODEL, D_MODEL), True),
        _const_spec((1, D_MODEL)),
        _const_spec((N_EXPERTS, D_MODEL)),
        _const_spec((N_EXPERTS, 1)),
        _const_spec((SUB_TOKENS, SUB_TOKENS)),
    ]


def _front_out(n_tok, n_tiles_all, tile_map, tok_map, extra_specs, extra_shapes):
    specs = [pl.BlockSpec((TILE, D_MODEL), tok_map),
             pl.BlockSpec((SUBLANES, TILE), lambda *i: (0, tok_map(*i)[0])),
             pl.BlockSpec((SUBLANES, TILE), lambda *i: (0, tok_map(*i)[0])),
             pl.BlockSpec((1, SUBTILES, 2, N_EXPERTS, LANES),
                          lambda *i: (tok_map(*i)[0], 0, 0, 0, 0)),
             pl.BlockSpec((SORTED_ROWS, PACKED), tile_map)]
    shapes = [jax.ShapeDtypeStruct((n_tok, D_MODEL), F32),
              jax.ShapeDtypeStruct((SUBLANES, n_tok), I32),
              jax.ShapeDtypeStruct((SUBLANES, n_tok), F32),
              jax.ShapeDtypeStruct((n_tok // TILE, SUBTILES, 2, N_EXPERTS, LANES), I32),
              jax.ShapeDtypeStruct((n_tiles_all * SORTED_ROWS, PACKED), U32)]
    return specs + extra_specs, shapes + extra_shapes


def _front_prompt(x, state_pool, state_conv, weights, conv_w8, *, pos0, n_tiles_all):
    nb, nt, _ = x.shape
    rows = TILE
    tiles = nt // rows
    tok_map = lambda b, t: (b * tiles + t, 0)
    out_specs, out_shape = _front_out(
        nb * nt, n_tiles_all, tok_map, tok_map,
        [pl.BlockSpec((1, 1, POOL_BUF, POOL_CH), lambda b, t: (0, b, 0, 0)),
         pl.BlockSpec((1, 1, CONV_BUF, CONV_CH), lambda b, t: (0, b, 0, 0))],
        [jax.ShapeDtypeStruct((1, nb, POOL_BUF, POOL_CH), F32),
         jax.ShapeDtypeStruct((1, nb, CONV_BUF, CONV_CH), F32)])
    kern = functools.partial(_front_prompt_kernel, pos0=pos0, rows=rows)
    return pl.pallas_call(
        kern,
        grid=(nb, tiles),
        in_specs=[pl.BlockSpec((1, rows, D_MODEL), lambda b, t: (b, t, 0)),
                  pl.BlockSpec((1, 1, POOL_BUF, POOL_CH), lambda b, t: (0, b, 0, 0)),
                  pl.BlockSpec((1, 1, CONV_BUF, CONV_CH), lambda b, t: (0, b, 0, 0))]
                 + _front_weight_specs() + [_const_spec((CONV_TAPS, SUBLANES, CONV_CH))],
        out_specs=out_specs,
        out_shape=out_shape,
        scratch_shapes=[pltpu.VMEM((len(POOL_WINDOWS), POOL_HIST + rows, POOL_CH), F32),
                        pltpu.VMEM((SUBLANES, CONV_HIST + rows, CONV_CH), F32),
                        pltpu.VMEM((rows, CONV_CH), F32),
                        pltpu.VMEM((rows, D_MODEL), BF16)],
        compiler_params=pltpu.CompilerParams(
            dimension_semantics=("arbitrary", "arbitrary"), vmem_limit_bytes=VMEM_LIMIT_BYTES),
        name="front_prompt",
    )(x, state_pool, state_conv, *weights, conv_w8)


def _front_sample(x, state_pool, state_conv, weights, xs_all, *, pos0, tile0):
    nb, nt, _ = x.shape
    seqs = TILE // nt
    n_tiles_all = xs_all.shape[0] // SORTED_ROWS
    tok_map = lambda i: (i, 0)
    out_specs, out_shape = _front_out(
        nb * nt, n_tiles_all, lambda i: (tile0 + i, 0), tok_map,
        [pl.BlockSpec((POOL_BUF, seqs, POOL_CH), lambda i: (0, i, 0)),
         pl.BlockSpec((CONV_BUF, seqs, CONV_CH), lambda i: (0, i, 0))],
        [jax.ShapeDtypeStruct((POOL_BUF, nb, POOL_CH), F32),
         jax.ShapeDtypeStruct((CONV_BUF, nb, CONV_CH), F32)])
    kern = functools.partial(_front_sample_kernel, pos0=pos0, seqs=seqs, steps=nt)
    n_in = 3 + len(weights)
    return pl.pallas_call(
        kern,
        grid=(nb // seqs,),
        in_specs=[pl.BlockSpec((seqs, nt, D_MODEL), lambda i: (i, 0, 0)),
                  pl.BlockSpec((POOL_BUF, seqs, POOL_CH), lambda i: (0, i, 0)),
                  pl.BlockSpec((CONV_BUF, seqs, CONV_CH), lambda i: (0, i, 0))]
                 + _front_weight_specs() + [pl.BlockSpec(memory_space=pl.ANY)],
        out_specs=out_specs,
        out_shape=out_shape,
        scratch_shapes=[pltpu.VMEM((TILE, D_MODEL), F32),
                        pltpu.VMEM((TILE, CONV_CH), F32),
                        pltpu.VMEM((TILE, D_MODEL), BF16)],
        input_output_aliases={n_in: 4},
        compiler_params=pltpu.CompilerParams(
            dimension_semantics=("arbitrary",), vmem_limit_bytes=VMEM_LIMIT_BYTES),
        name="front_sample",
    )(x, state_pool, state_conv, *weights, xs_all)


def _moe_kernel(be_ref, nused_ref, src_ref, dst_ref, first_ref, wslot_ref, nxt_ref, hasb_ref,
                xs_hbm, wgu_hbm, bgu_ref, wd_hbm, bd_ref, ys_hbm,
                xa, xb, oa, ob, wgu_f32, wd_f32, wgu_bf, wd_bf,
                gsem_a, gsem_b, ssem_a, ssem_b, wsem):
    i = pl.program_id(0)
    n_used = nused_ref[0]
    par = i % 2
    slot = lax.rem(i, X_SLOTS)
    half_a = 2 * i
    half_b = 2 * i + 1

    def weight_copies(e, s):
        return (pltpu.make_async_copy(wgu_hbm.at[e], wgu_f32.at[s], wsem.at[s]),
                pltpu.make_async_copy(wd_hbm.at[e], wd_f32.at[s], wsem.at[s]))

    def gather(half, buf, sem):
        for g in range(HALF_GROUPS):
            hrow = pl.multiple_of(src_ref[half * HALF_GROUPS + g], SUBLANES)
            pltpu.make_async_copy(xs_hbm.at[pl.ds(hrow, SUBLANES)],
                                  buf.at[pl.ds(g * SUBLANES, SUBLANES)], sem).start()

    def gather_wait(buf, sem):
        pltpu.make_async_copy(xs_hbm.at[pl.ds(0, HALF_ROWS)], buf, sem).wait()

    def scatter(half, buf, sem):
        for g in range(HALF_GROUPS):
            hrow = pl.multiple_of(dst_ref[half * HALF_GROUPS + g], SUBLANES)
            pltpu.make_async_copy(buf.at[pl.ds(g * SUBLANES, SUBLANES)],
                                  ys_hbm.at[pl.ds(hrow, SUBLANES)], sem).start()

    def scatter_wait(buf, sem):
        pltpu.make_async_copy(buf, ys_hbm.at[pl.ds(0, HALF_ROWS)], sem).wait()

    def experts(xbuf, obuf):
        x = _unpack_halves(xbuf[slot]).astype(BF16)
        gu = jnp.dot(x, wgu_bf[...], preferred_element_type=F32) + bgu_ref[0]
        g = jnp.minimum(gu[:, :D_FF], SWIGLU_LIMIT)
        up = jnp.clip(gu[:, D_FF:], -SWIGLU_LIMIT, SWIGLU_LIMIT)
        hdn = (up + 1.0) * (g * _sigmoid(SWIGLU_ALPHA * g))
        y = jnp.dot(hdn.astype(BF16), wd_bf[...], preferred_element_type=F32) + bd_ref[0]
        obuf[par] = _pack_halves(y.astype(BF16).astype(F32))

    def gather_block(blk, slot):
        blk = jnp.minimum(blk, n_used - 1)
        gather(2 * blk, xa.at[slot], gsem_a.at[slot])
        gather(2 * blk + 1, xb.at[slot], gsem_b.at[slot])

    def gather_block_wait(slot):
        gather_wait(xa.at[slot], gsem_a.at[slot])
        gather_wait(xb.at[slot], gsem_b.at[slot])

    @pl.when(i == 0)
    def _():
        gather_block(0, 0)
        gather_block(1, 1)
        for c in weight_copies(be_ref[0], 0):
            c.start(priority=WEIGHT_DMA_PRIORITY)

    @pl.when(i < n_used)
    def _():
        @pl.when(first_ref[i] == 1)
        def _():
            s = wslot_ref[i]
            for c in weight_copies(be_ref[i], s):
                c.wait()
            wgu_bf[...] = wgu_f32[s].astype(BF16)
            wd_bf[...] = wd_f32[s].astype(BF16)

            @pl.when(nxt_ref[i] < N_EXPERTS)
            def _():
                for c in weight_copies(nxt_ref[i], 1 - s):
                    c.start(priority=WEIGHT_DMA_PRIORITY)

        def wait_writeback(step, p):
            scatter_wait(oa.at[p], ssem_a.at[p])

            @pl.when(hasb_ref[step] == 1)
            def _():
                scatter_wait(ob.at[p], ssem_b.at[p])

        @pl.when(i >= 2)
        def _():
            wait_writeback(i - 2, par)

        gather_block_wait(slot)

        @pl.when(hasb_ref[i] == 1)
        def _():
            gather_block(i + 2, lax.rem(i + 2, X_SLOTS))
            experts(xa, oa)
            scatter(half_a, oa.at[par], ssem_a.at[par])
            experts(xb, ob)
            scatter(half_b, ob.at[par], ssem_b.at[par])

        @pl.when(hasb_ref[i] == 0)
        def _():
            gather_block(i + 2, lax.rem(i + 2, X_SLOTS))
            experts(xa, oa)
            scatter(half_a, oa.at[par], ssem_a.at[par])

        @pl.when(i == n_used - 1)
        def _():
            wait_writeback(i, par)
            gather_block_wait(lax.rem(i + 1, X_SLOTS))
            gather_block_wait(lax.rem(i + 2, X_SLOTS))

            @pl.when(i >= 1)
            def _():
                wait_writeback(i - 1, 1 - par)


def _moe(xs_all, tables, w_gate_up, b_gate_up, w_down, b_down):
    block_e, n_used, src, dst, first, wslot, nxt, hasb = tables
    n_blocks = block_e.shape[0]

    def e_map(i, be, nu, *_):
        return (be[jnp.minimum(i, nu[0] - 1)], 0, 0)

    x_buf = pltpu.VMEM((X_SLOTS, HALF_ROWS, PACKED), U32)
    o_buf = pltpu.VMEM((2, HALF_ROWS, PACKED), U32)
    any_spec = pl.BlockSpec(memory_space=pl.ANY)
    grid_spec = pltpu.PrefetchScalarGridSpec(
        num_scalar_prefetch=8,
        grid=(n_blocks,),
        in_specs=[any_spec,
                  any_spec,
                  pl.BlockSpec((1, 1, 2 * D_FF), e_map),
                  any_spec,
                  pl.BlockSpec((1, 1, D_MODEL), e_map)],
        out_specs=any_spec,
        scratch_shapes=[x_buf, x_buf, o_buf, o_buf,
                        pltpu.VMEM((2, D_MODEL, 2 * D_FF), F32),
                        pltpu.VMEM((2, D_FF, D_MODEL), F32),
                        pltpu.VMEM((D_MODEL, 2 * D_FF), BF16),
                        pltpu.VMEM((D_FF, D_MODEL), BF16)]
                       + [pltpu.SemaphoreType.DMA((X_SLOTS,))] * 2 + [pltpu.SemaphoreType.DMA((2,))] * 3,
    )
    return pl.pallas_call(
        _moe_kernel,
        grid_spec=grid_spec,
        out_shape=jax.ShapeDtypeStruct((xs_all.shape[0] + 2 * MOE_ROWS, PACKED), U32),
        compiler_params=pltpu.CompilerParams(
            dimension_semantics=("arbitrary",), vmem_limit_bytes=VMEM_LIMIT_BYTES),
        name="moe_experts",
    )(block_e, n_used, src, dst, first, wslot, nxt, hasb,
      xs_all, w_gate_up, b_gate_up.reshape(N_EXPERTS, 1, 2 * D_FF), w_down,
      b_down.reshape(N_EXPERTS, 1, D_MODEL))


def _final_kernel(used_ref, h_ref, ys_ref, q_ref, gate_ref, g_ref, o_ref, *, tile0, slabs):
    i = pl.program_id(0)

    def to_cols(a):
        pad = jnp.zeros((LANES - SUBLANES, TILE), F32)
        return jnp.concatenate([a, pad], axis=0).T

    qc_all = to_cols(q_ref[...].astype(F32))
    gc_all = to_cols(gate_ref[...])
    parts = []
    for s in range(SUBTILES):
        ts = slice(s * SUB_TOKENS, (s + 1) * SUB_TOKENS)
        used = used_ref[(tile0 + i) * SUBTILES + s]
        qc, gc = qc_all[ts], gc_all[ts]
        acc = h_ref[ts, :]
        for c in range(SUB_SORTED // SORT_CHUNK):
            jl = (c * SORT_CHUNK + lax.broadcasted_iota(I32, (SUB_TOKENS, SORT_CHUNK), 1)).astype(F32)
            pg = sum(jnp.where(qc[:, k:k + 1] == jl, gc[:, k:k + 1], 0.0) for k in range(TOP_K))
            rid = c * SORT_CHUNK + lax.broadcasted_iota(I32, (SORT_CHUNK, 1), 0)
            rows = ys_ref[pl.ds(s * SUB_SORTED + c * SORT_CHUNK, SORT_CHUNK), :]
            ys = jnp.where(rid < used, _unpack_halves(rows), 0.0)
            acc = acc + jnp.dot(pg.astype(BF16), ys.astype(BF16), preferred_element_type=F32)
        parts.append(acc)
    y = _rmsnorm(jnp.concatenate(parts, axis=0), g_ref[...])
    if slabs == 1:
        o_ref[...] = y
    else:
        seqs = TILE // slabs
        for t in range(slabs):
            o_ref[:, t, :] = y[t * seqs:(t + 1) * seqs]


def _final(h, ys_all, q, gates, used, g_final, *, tile0, slabs):
    n_tok = h.shape[0]
    if slabs == 1:
        out_spec = pl.BlockSpec((TILE, D_MODEL), lambda i, u: (i, 0))
        out_shape = jax.ShapeDtypeStruct((n_tok, D_MODEL), F32)
    else:
        seqs = TILE // slabs
        out_spec = pl.BlockSpec((seqs, slabs, D_MODEL), lambda i, u: (i, 0, 0))
        out_shape = jax.ShapeDtypeStruct((n_tok // slabs, slabs, D_MODEL), F32)
    grid_spec = pltpu.PrefetchScalarGridSpec(
        num_scalar_prefetch=1,
        grid=(n_tok // TILE,),
        in_specs=[pl.BlockSpec((TILE, D_MODEL), lambda i, u: (i, 0)),
                  pl.BlockSpec((SORTED_ROWS, PACKED), lambda i, u: (tile0 + i, 0)),
                  pl.BlockSpec((SUBLANES, TILE), lambda i, u: (0, i)),
                  pl.BlockSpec((SUBLANES, TILE), lambda i, u: (0, i)),
                  pl.BlockSpec((1, D_MODEL), lambda i, u: (0, 0))],
        out_specs=out_spec,
    )
    return pl.pallas_call(
        functools.partial(_final_kernel, tile0=tile0, slabs=slabs),
        grid_spec=grid_spec,
        out_shape=out_shape,
        compiler_params=pltpu.CompilerParams(
            dimension_semantics=("arbitrary",), vmem_limit_bytes=VMEM_LIMIT_BYTES),
        name="final_norm",
    )(used, h, ys_all, q, gates, g_final)


def _block_tables(tbl, n_blocks):
    tbl = tbl.reshape(-1, 2, N_EXPERTS, LANES)
    cnt = tbl[:, 0, :, 0]
    off = tbl[:, 1, :, 0]
    run = jnp.cumsum(cnt, axis=0) - cnt
    total = jnp.sum(cnt, axis=0)
    nblk = (total + MOE_ROWS - 1) // MOE_ROWS
    blk_end = jnp.cumsum(nblk)
    blk_start = blk_end - nblk
    bidx = jnp.arange(n_blocks, dtype=I32)
    eidx = jnp.arange(N_EXPERTS, dtype=I32)
    block_e = jnp.minimum(jnp.sum(blk_end[None, :] <= bidx[:, None], axis=1), N_EXPERTS - 1).astype(I32)
    hot = block_e[:, None] == eidx[None, :]

    def per_block(v):
        return jnp.sum(jnp.where(hot, v[..., None, :], 0), axis=-1)

    row0 = (bidx - per_block(blk_start)) * MOE_ROWS
    n_tiles = cnt.shape[0]
    groups = jnp.arange(MOE_GROUPS, dtype=I32)
    erow = row0[:, None] + groups[None, :] * SUBLANES
    run_b, cnt_b, off_b = per_block(run), per_block(cnt), per_block(off)
    inside = jnp.logical_and(run_b[:, :, None] <= erow[None], erow[None] < (run_b + cnt_b)[:, :, None])
    tile_base = jnp.arange(n_tiles, dtype=I32)[:, None] * SUB_SORTED + off_b - run_b
    src = jnp.sum(jnp.where(inside, tile_base[:, :, None] + erow[None], 0), axis=0)
    dump = n_tiles * SUB_SORTED + (bidx % 2)[:, None] * MOE_ROWS + groups[None, :] * SUBLANES
    dst = jnp.where(erow < per_block(total)[:, None], src, dump).reshape(-1)
    src = src.reshape(-1)
    used = jnp.sum(cnt, axis=1)
    has_rows = nblk > 0
    first = (bidx == per_block(blk_start)).astype(I32)
    wslot = per_block((jnp.cumsum(has_rows) - 1) % 2)
    later = jnp.where(jnp.logical_and(has_rows[None, :], eidx[None, :] > eidx[:, None]), eidx[None, :], N_EXPERTS)
    nxt = per_block(jnp.min(later, axis=1))
    hasb = (per_block(total) - row0 > HALF_ROWS).astype(I32)
    return ((block_e, blk_end[-1:].astype(I32), src.astype(I32), dst.astype(I32), first,
             wslot.astype(I32), nxt.astype(I32), hasb), used.astype(I32))


def kernel(x_prompt, x_sample, state_pool, state_conv, g_mix, w_in, pool_w, pool_scale, conv_w, conv_b,
           conv_ln_g, conv_ln_b, w_out, g_moe, w_router, b_router, w_gate_up, b_gate_up, w_down, b_down,
           g_final):
    nb_p, nt_p, _ = x_prompt.shape
    nb_s, nt_s, _ = x_sample.shape
    n_p, n_s = nb_p * nt_p, nb_s * nt_s
    assert nt_p % TILE == 0 and TILE % nt_s == 0 and n_s % TILE == 0
    tiles_p, tiles_s = n_p // TILE, n_s // TILE
    n_tiles = tiles_p + tiles_s

    tri_shape = (SUB_TOKENS, SUB_TOKENS)
    tri = (lax.broadcasted_iota(I32, tri_shape, 0) < lax.broadcasted_iota(I32, tri_shape, 1)).astype(BF16)
    weights = (g_mix[0][None], w_in[0].astype(BF16), pool_w[0].astype(BF16), pool_scale[0][None],
               conv_w[0], conv_b[0][None], conv_ln_g[0][None], conv_ln_b[0][None],
               w_out[0].astype(BF16), g_moe[0][None], w_router[0].T.astype(BF16),
               b_router[0][:, None], tri)

    time_major = lambda a: jnp.transpose(a, (1, 0, 2))
    zero_pool = jnp.zeros((1, nb_p, POOL_BUF, POOL_CH), F32)
    zero_conv = jnp.zeros((1, nb_p, CONV_BUF, CONV_CH), F32)
    h_p, q_p, gate_p, tbl_p, xs_all, pool_p, conv_p = _front_prompt(
        x_prompt, zero_pool, zero_conv, weights,
        jnp.broadcast_to(conv_w[0][:, None, :], (CONV_TAPS, SUBLANES, CONV_CH)), pos0=0, n_tiles_all=n_tiles)
    h_s, q_s, gate_s, tbl_s, xs_all, pool_s, conv_s = _front_sample(
        x_sample, time_major(state_pool[0]), time_major(state_conv[0]), weights, xs_all,
        pos0=PAST_LEN, tile0=tiles_p)

    sorted_total = n_tiles * SUBTILES * (SUB_TOKENS * TOP_K + N_EXPERTS * (SUBLANES - 1))
    n_blocks = -(-sorted_total // MOE_ROWS) + N_EXPERTS
    tables, used = _block_tables(jnp.concatenate([tbl_p, tbl_s], axis=0), n_blocks)
    ys_all = _moe(xs_all, tables, w_gate_up[0], b_gate_up[0], w_down[0], b_down[0])

    gf = g_final[None]
    y_prompt = _final(h_p, ys_all, q_p, gate_p, used, gf, tile0=0, slabs=1)
    y_sample = _final(h_s, ys_all, q_s, gate_s, used, gf, tile0=tiles_p, slabs=nt_s)
    return (y_prompt.reshape(nb_p, nt_p, D_MODEL), y_sample,
            pool_p, conv_p,
            time_major(pool_s)[None], time_major(conv_s)[None])
```

```python
import functools

import jax
import jax.numpy as jnp
from jax import lax
from jax.experimental import pallas as pl
from jax.experimental.pallas import tpu as pltpu

D_MODEL = 1024
POOL_CH = 512
CONV_CH = 512
POOL_WINDOWS = (2, 4, 8, 16)
POOL_GROUP = 128
POOL_BUF = 15
CONV_TAPS = 31
CONV_BUF = 30
IN_COLS = POOL_CH + 2 * CONV_CH
N_EXPERTS = 32
TOP_K = 4
D_FF = 1024
SWIGLU_LIMIT = 7.0
SWIGLU_ALPHA = 1.702
RMS_EPS = 1e-5
LN_EPS = 1e-5
PAST_LEN = 16384

LANES = 128
SUBLANES = 8
VMEM_LIMIT_BYTES = 60 * 1024 * 1024

TILE = 512
SUB_TOKENS = 256
SUBTILES = TILE // SUB_TOKENS
CONV_CHUNK = 32
SORT_CHUNK = 256
FRONT_SORT_CHUNK = 640
SUB_SORTED = -(-(SUB_TOKENS * TOP_K + N_EXPERTS * (SUBLANES - 1)) // SORT_CHUNK) * SORT_CHUNK
SORTED_ROWS = SUBTILES * SUB_SORTED
MOE_ROWS = 512
HALF_ROWS = MOE_ROWS // 2
MOE_GROUPS = MOE_ROWS // SUBLANES
HALF_GROUPS = MOE_GROUPS // 2
X_SLOTS = 3
WEIGHT_DMA_PRIORITY = 1

POOL_HIST = 24
CONV_HIST = 32

BF16 = jnp.bfloat16
F32 = jnp.float32
I32 = jnp.int32
U32 = jnp.uint32
PACKED = D_MODEL // 2


def _rmsnorm(x, g):
    ms = jnp.mean(x * x, axis=-1, keepdims=True)
    return x * lax.rsqrt(ms + RMS_EPS) * g


def _sigmoid(x):
    return 1.0 / (1.0 + jnp.exp(-x))


def _layernorm_silu(d, g, b):
    mu = jnp.mean(d, axis=-1, keepdims=True)
    c = d - mu
    var = jnp.mean(c * c, axis=-1, keepdims=True)
    n = c * lax.rsqrt(var + LN_EPS) * g + b
    return n * _sigmoid(n)


def _pack_halves(a):
    half = a.shape[1] // 2
    hi = lax.bitcast_convert_type(a[:, :half], U32)
    lo = lax.bitcast_convert_type(a[:, half:], U32)
    return hi | (lo >> 16)


def _unpack_halves(w):
    hi = lax.bitcast_convert_type(w & jnp.uint32(0xFFFF0000), F32)
    lo = lax.bitcast_convert_type(w << 16, F32)
    return jnp.concatenate([hi, lo], axis=1)


def _route_and_sort(hn_bf, wr_t_ref, br_ref, tri_ref, q_ref, gate_ref, tbl_ref, xs_ref):
    rows = hn_bf.shape[0]
    lt = lax.dot_general(wr_t_ref[...], hn_bf, (((1,), (1,)), ((), ())),
                         preferred_element_type=F32)
    lt = lt + br_ref[...]
    eidx = lax.broadcasted_iota(I32, (N_EXPERTS, rows), 0)
    vals, hots = [], []
    for _ in range(TOP_K):
        m = jnp.max(lt, axis=0, keepdims=True)
        sel = jnp.min(jnp.where(lt == m, eidx, N_EXPERTS), axis=0, keepdims=True)
        hit = eidx == sel
        vals.append(m)
        hots.append(hit)
        lt = jnp.where(hit, -jnp.inf, lt)
    v = jnp.concatenate(vals, axis=0)
    e = jnp.exp(v - vals[0])
    gate = e / jnp.sum(e, axis=0, keepdims=True)
    gate_ref[...] = jnp.concatenate([gate, jnp.zeros((SUBLANES - TOP_K, rows), F32)], axis=0)

    sel_cnt = sum(h.astype(F32) for h in hots)
    lower = (lax.broadcasted_iota(I32, (N_EXPERTS, N_EXPERTS), 1)
             < lax.broadcasted_iota(I32, (N_EXPERTS, N_EXPERTS), 0)).astype(BF16)
    q_rows = []
    for s in range(SUBTILES):
        ts = slice(s * SUB_TOKENS, (s + 1) * SUB_TOKENS)
        sel_s = sel_cnt[:, ts]
        prefix = jnp.dot(sel_s.astype(BF16), tri_ref[...], preferred_element_type=F32)
        cnt = jnp.sum(sel_s, axis=1, keepdims=True)
        cnt_u = jnp.floor((cnt + (SUBLANES - 1)) * (1.0 / SUBLANES))
        cnt_u_l = jnp.broadcast_to(cnt_u, (N_EXPERTS, LANES))
        off_u_l = jnp.dot(lower, cnt_u_l.astype(BF16), preferred_element_type=F32)
        off_l = off_u_l * float(SUBLANES)
        tbl_ref[0, s, 0] = (cnt_u_l * float(SUBLANES)).astype(I32)
        tbl_ref[0, s, 1] = off_l.astype(I32)

        pos = prefix + off_l[:, :1]
        qs = [jnp.sum(jnp.where(h[:, ts], pos, 0.0), axis=0, keepdims=True).astype(I32) for h in hots]
        q_rows.append(jnp.concatenate(qs + [jnp.zeros((SUBLANES - TOP_K, SUB_TOKENS), I32)], axis=0))

        hn_s = hn_bf[ts]
        for c in range(SUB_SORTED // FRONT_SORT_CHUNK):
            r0 = c * FRONT_SORT_CHUNK
            jidx = r0 + lax.broadcasted_iota(I32, (FRONT_SORT_CHUNK, SUB_TOKENS), 0)
            p = sum(jnp.where(jidx == qk, 1.0, 0.0) for qk in qs)
            xs = jnp.dot(p.astype(BF16), hn_s, preferred_element_type=F32)
            xs_ref[pl.ds(s * SUB_SORTED + r0, FRONT_SORT_CHUNK), :] = _pack_halves(xs)
    q_ref[...] = jnp.concatenate(q_rows, axis=1)


def _mix_out_and_route(x, mixbuf, w_out_ref, g_moe_ref, wr_t_ref, br_ref, tri_ref,
                       h_ref, q_ref, gate_ref, tbl_ref, xs_ref):
    h = x + jnp.dot(mixbuf[...], w_out_ref[...], preferred_element_type=F32)
    h_ref[...] = h
    hn = _rmsnorm(h, g_moe_ref[...])
    _route_and_sort(hn.astype(BF16), wr_t_ref, br_ref, tri_ref, q_ref, gate_ref, tbl_ref, xs_ref)


def _front_prompt_kernel(x_ref, sp_ref, sc_ref, g_mix_ref, w_in_ref, pool_w_ref, pool_scale_ref,
                         conv_w_ref, conv_b_ref, ln_g_ref, ln_b_ref, w_out_ref, g_moe_ref,
                         wr_t_ref, br_ref, tri_ref, conv_w8_ref,
                         h_ref, q_ref, gate_ref, tbl_ref, xs_ref, np_ref, nc_ref,
                         ubuf, gsh, dbuf, mixbuf, *, pos0, rows):
    del conv_w_ref
    t = pl.program_id(1)

    @pl.when(t == 0)
    def _():
        ubuf[:, pl.ds(0, POOL_HIST), :] = jnp.zeros((len(POOL_WINDOWS), POOL_HIST, POOL_CH), F32)
        ubuf[0, pl.ds(POOL_HIST - POOL_BUF, POOL_BUF), :] = sp_ref[0, 0]
        gsh[0, pl.ds(CONV_HIST - CONV_BUF, CONV_BUF), :] = sc_ref[0, 0]

    x = x_ref[0]
    xn = _rmsnorm(x, g_mix_ref[...]).astype(BF16)
    proj = jnp.dot(xn, w_in_ref[...], preferred_element_type=F32)

    u = proj[:, :POOL_CH]
    ubuf[0, pl.ds(POOL_HIST, rows), :] = u
    pos = pos0 + t * rows + lax.broadcasted_iota(I32, (rows, 1), 0)
    span = rows + POOL_HIST - SUBLANES
    for g, w in enumerate(POOL_WINDOWS):
        assert w == 2 ** (g + 1) and w // 2 <= SUBLANES
        c0 = g * POOL_GROUP
        cs = slice(c0, c0 + POOL_GROUP)
        a = ubuf[g, pl.ds(SUBLANES, span), c0:] + ubuf[g, pl.ds(SUBLANES - w // 2, span), c0:]
        if g + 1 < len(POOL_WINDOWS):
            ubuf[g + 1, pl.ds(SUBLANES, span), c0 + POOL_GROUP:] = a[:, POOL_GROUP:]
        s = a[span - rows:, :POOL_GROUP]
        cnt = jnp.minimum(w, pos + 1).astype(F32)
        z = (s / cnt - u[:, cs]).astype(BF16)
        yg = jnp.dot(z, pool_w_ref[g], preferred_element_type=F32)
        mixbuf[:, cs] = (yg * pool_scale_ref[:, cs]).astype(BF16)

    glu = proj[:, POOL_CH:POOL_CH + CONV_CH] * _sigmoid(proj[:, POOL_CH + CONV_CH:])
    gsh[0, pl.ds(CONV_HIST, rows), :] = glu
    for s in range(1, SUBLANES):
        gsh[s, pl.ds(0, CONV_HIST - s), :] = gsh[0, pl.ds(s, CONV_HIST - s), :]
        gsh[s, pl.ds(CONV_HIST - s, rows), :] = glu

    def conv_chunk(i, carry):
        r0 = pl.multiple_of(i * CONV_CHUNK, CONV_CHUNK)
        acc = jnp.broadcast_to(conv_b_ref[...], (CONV_CHUNK // SUBLANES, SUBLANES, CONV_CH))
        for k in range(CONV_TAPS):
            first = CONV_HIST - CONV_BUF + k
            s = first % SUBLANES
            rows_k = gsh[s, pl.ds(r0 + (first - s), CONV_CHUNK), :]
            acc = acc + conv_w8_ref[k][None] * rows_k.reshape(CONV_CHUNK // SUBLANES, SUBLANES, CONV_CH)
        dbuf[pl.ds(r0, CONV_CHUNK), :] = acc.reshape(CONV_CHUNK, CONV_CH)
        return carry

    lax.fori_loop(0, rows // CONV_CHUNK, conv_chunk, 0)
    y_conv = _layernorm_silu(dbuf[...], ln_g_ref[...], ln_b_ref[...])
    mixbuf[:, POOL_CH:] = y_conv.astype(BF16)

    new_pool = ubuf[0, pl.ds(rows + POOL_HIST - POOL_BUF, POOL_BUF), :]
    new_conv = gsh[0, pl.ds(rows + CONV_HIST - CONV_BUF, CONV_BUF), :]
    np_ref[0, 0] = new_pool
    nc_ref[0, 0] = new_conv
    ubuf[0, pl.ds(POOL_HIST - POOL_BUF, POOL_BUF), :] = new_pool
    gsh[0, pl.ds(CONV_HIST - CONV_BUF, CONV_BUF), :] = new_conv

    _mix_out_and_route(x, mixbuf, w_out_ref, g_moe_ref, wr_t_ref, br_ref, tri_ref,
                       h_ref, q_ref, gate_ref, tbl_ref, xs_ref)


def _front_sample_kernel(x_ref, sp_ref, sc_ref, g_mix_ref, w_in_ref, pool_w_ref, pool_scale_ref,
                         conv_w_ref, conv_b_ref, ln_g_ref, ln_b_ref, w_out_ref, g_moe_ref,
                         wr_t_ref, br_ref, tri_ref, xs_in,
                         h_ref, q_ref, gate_ref, tbl_ref, xs_ref, np_ref, nc_ref,
                         xbuf, dbuf, mixbuf, *, pos0, seqs, steps):
    del xs_in
    for t in range(steps):
        xbuf[pl.ds(t * seqs, seqs), :] = x_ref[:, t, :]
    x = xbuf[...]
    xn = _rmsnorm(x, g_mix_ref[...]).astype(BF16)
    proj = jnp.dot(xn, w_in_ref[...], preferred_element_type=F32)

    def slab(a, t):
        return a[t * seqs:(t + 1) * seqs]

    u = proj[:, :POOL_CH]
    cat = [sp_ref[j] for j in range(POOL_BUF)]
    cat += [slab(u, t) for t in range(steps)]
    for t in range(steps):
        for g, w in enumerate(POOL_WINDOWS):
            cs = slice(g * POOL_GROUP, (g + 1) * POOL_GROUP)
            s = cat[POOL_BUF + t][:, cs]
            for j in range(1, w):
                s = s + cat[POOL_BUF + t - j][:, cs]
            cnt = float(min(w, pos0 + t + 1))
            z = (s / cnt - cat[POOL_BUF + t][:, cs]).astype(BF16)
            yg = jnp.dot(z, pool_w_ref[g], preferred_element_type=F32)
            mixbuf[pl.ds(t * seqs, seqs), cs] = (yg * pool_scale_ref[:, cs]).astype(BF16)
    for j in range(POOL_BUF):
        np_ref[j] = cat[len(cat) - POOL_BUF + j]

    glu = proj[:, POOL_CH:POOL_CH + CONV_CH] * _sigmoid(proj[:, POOL_CH + CONV_CH:])
    ccat = [sc_ref[j] for j in range(CONV_BUF)]
    ccat += [slab(glu, t) for t in range(steps)]
    for t in range(steps):
        acc = jnp.broadcast_to(conv_b_ref[...], (seqs, CONV_CH))
        for k in range(CONV_TAPS):
            acc = acc + conv_w_ref[k:k + 1, :] * ccat[t + k]
        dbuf[pl.ds(t * seqs, seqs), :] = acc
    for j in range(CONV_BUF):
        nc_ref[j] = ccat[len(ccat) - CONV_BUF + j]
    y_conv = _layernorm_silu(dbuf[...], ln_g_ref[...], ln_b_ref[...])
    mixbuf[:, POOL_CH:] = y_conv.astype(BF16)

    _mix_out_and_route(x, mixbuf, w_out_ref, g_moe_ref, wr_t_ref, br_ref, tri_ref,
                       h_ref, q_ref, gate_ref, tbl_ref, xs_ref)


def _const_spec(shape, single_buffer=False):
    mode = pl.Buffered(1) if single_buffer else None
    return pl.BlockSpec(shape, lambda *_: (0,) * len(shape), pipeline_mode=mode)


def _front_weight_specs():
    return [
        _const_spec((1, D_MODEL)),
        _const_spec((D_MODEL, IN_COLS), True),
        _const_spec((len(POOL_WINDOWS), POOL_GROUP, POOL_GROUP)),
        _const_spec((1, POOL_CH)),
        _const_spec((CONV_TAPS, CONV_CH)),
        _const_spec((1, CONV_CH)),
        _const_spec((1, CONV_CH)),
        _const_spec((1, CONV_CH)),
        _const_spec((D_MODEL, D_MODEL), True),
        _const_spec((1, D_MODEL)),
        _const_spec((N_EXPERTS, D_MODEL)),
        _const_spec((N_EXPERTS, 1)),
        _const_spec((SUB_TOKENS, SUB_TOKENS)),
    ]


def _front_out(n_tok, n_tiles_all, tile_map, tok_map, extra_specs, extra_shapes):
    specs = [pl.BlockSpec((TILE, D_MODEL), tok_map),
             pl.BlockSpec((SUBLANES, TILE), lambda *i: (0, tok_map(*i)[0])),
             pl.BlockSpec((SUBLANES, TILE), lambda *i: (0, tok_map(*i)[0])),
             pl.BlockSpec((1, SUBTILES, 2, N_EXPERTS, LANES),
                          lambda *i: (tok_map(*i)[0], 0, 0, 0, 0)),
             pl.BlockSpec((SORTED_ROWS, PACKED), tile_map)]
    shapes = [jax.ShapeDtypeStruct((n_tok, D_MODEL), F32),
              jax.ShapeDtypeStruct((SUBLANES, n_tok), I32),
              jax.ShapeDtypeStruct((SUBLANES, n_tok), F32),
              jax.ShapeDtypeStruct((n_tok // TILE, SUBTILES, 2, N_EXPERTS, LANES), I32),
              jax.ShapeDtypeStruct((n_tiles_all * SORTED_ROWS, PACKED), U32)]
    return specs + extra_specs, shapes + extra_shapes


def _front_prompt(x, state_pool, state_conv, weights, conv_w8, *, pos0, n_tiles_all):
    nb, nt, _ = x.shape
    rows = TILE
    tiles = nt // rows
    tok_map = lambda b, t: (b * tiles + t, 0)
    out_specs, out_shape = _front_out(
        nb * nt, n_tiles_all, tok_map, tok_map,
        [pl.BlockSpec((1, 1, POOL_BUF, POOL_CH), lambda b, t: (0, b, 0, 0)),
         pl.BlockSpec((1, 1, CONV_BUF, CONV_CH), lambda b, t: (0, b, 0, 0))],
        [jax.ShapeDtypeStruct((1, nb, POOL_BUF, POOL_CH), F32),
         jax.ShapeDtypeStruct((1, nb, CONV_BUF, CONV_CH), F32)])
    kern = functools.partial(_front_prompt_kernel, pos0=pos0, rows=rows)
    return pl.pallas_call(
        kern,
        grid=(nb, tiles),
        in_specs=[pl.BlockSpec((1, rows, D_MODEL), lambda b, t: (b, t, 0)),
                  pl.BlockSpec((1, 1, POOL_BUF, POOL_CH), lambda b, t: (0, b, 0, 0)),
                  pl.BlockSpec((1, 1, CONV_BUF, CONV_CH), lambda b, t: (0, b, 0, 0))]
                 + _front_weight_specs() + [_const_spec((CONV_TAPS, SUBLANES, CONV_CH))],
        out_specs=out_specs,
        out_shape=out_shape,
        scratch_shapes=[pltpu.VMEM((len(POOL_WINDOWS), POOL_HIST + rows, POOL_CH), F32),
                        pltpu.VMEM((SUBLANES, CONV_HIST + rows, CONV_CH), F32),
                        pltpu.VMEM((rows, CONV_CH), F32),
                        pltpu.VMEM((rows, D_MODEL), BF16)],
        compiler_params=pltpu.CompilerParams(
            dimension_semantics=("arbitrary", "arbitrary"), vmem_limit_bytes=VMEM_LIMIT_BYTES),
        name="front_prompt",
    )(x, state_pool, state_conv, *weights, conv_w8)


def _front_sample(x, state_pool, state_conv, weights, xs_all, *, pos0, tile0):
    nb, nt, _ = x.shape
    seqs = TILE // nt
    n_tiles_all = xs_all.shape[0] // SORTED_ROWS
    tok_map = lambda i: (i, 0)
    out_specs, out_shape = _front_out(
        nb * nt, n_tiles_all, lambda i: (tile0 + i, 0), tok_map,
        [pl.BlockSpec((POOL_BUF, seqs, POOL_CH), lambda i: (0, i, 0)),
         pl.BlockSpec((CONV_BUF, seqs, CONV_CH), lambda i: (0, i, 0))],
        [jax.ShapeDtypeStruct((POOL_BUF, nb, POOL_CH), F32),
         jax.ShapeDtypeStruct((CONV_BUF, nb, CONV_CH), F32)])
    kern = functools.partial(_front_sample_kernel, pos0=pos0, seqs=seqs, steps=nt)
    n_in = 3 + len(weights)
    return pl.pallas_call(
        kern,
        grid=(nb // seqs,),
        in_specs=[pl.BlockSpec((seqs, nt, D_MODEL), lambda i: (i, 0, 0)),
                  pl.BlockSpec((POOL_BUF, seqs, POOL_CH), lambda i: (0, i, 0)),
                  pl.BlockSpec((CONV_BUF, seqs, CONV_CH), lambda i: (0, i, 0))]
                 + _front_weight_specs() + [pl.BlockSpec(memory_space=pl.ANY)],
        out_specs=out_specs,
        out_shape=out_shape,
        scratch_shapes=[pltpu.VMEM((TILE, D_MODEL), F32),
                        pltpu.VMEM((TILE, CONV_CH), F32),
                        pltpu.VMEM((TILE, D_MODEL), BF16)],
        input_output_aliases={n_in: 4},
        compiler_params=pltpu.CompilerParams(
            dimension_semantics=("arbitrary",), vmem_limit_bytes=VMEM_LIMIT_BYTES),
        name="front_sample",
    )(x, state_pool, state_conv, *weights, xs_all)


def _moe_kernel(be_ref, nused_ref, src_ref, dst_ref, first_ref, wslot_ref, nxt_ref, hasb_ref,
                xs_hbm, wgu_hbm, bgu_ref, wd_hbm, bd_ref, ys_hbm,
                xa, xb, oa, ob, wgu_f32, wd_f32, wgu_bf, wd_bf,
                gsem_a, gsem_b, ssem_a, ssem_b, wsem):
    i = pl.program_id(0)
    n_used = nused_ref[0]
    par = i % 2
    slot = lax.rem(i, X_SLOTS)
    half_a = 2 * i
    half_b = 2 * i + 1

    def weight_copies(e, s):
        return (pltpu.make_async_copy(wgu_hbm.at[e], wgu_f32.at[s], wsem.at[s]),
                pltpu.make_async_copy(wd_hbm.at[e], wd_f32.at[s], wsem.at[s]))

    def gather(half, buf, sem):
        for g in range(HALF_GROUPS):
            hrow = pl.multiple_of(src_ref[half * HALF_GROUPS + g], SUBLANES)
            pltpu.make_async_copy(xs_hbm.at[pl.ds(hrow, SUBLANES)],
                                  buf.at[pl.ds(g * SUBLANES, SUBLANES)], sem).start()

    def gather_wait(buf, sem):
        pltpu.make_async_copy(xs_hbm.at[pl.ds(0, HALF_ROWS)], buf, sem).wait()

    def scatter(half, buf, sem):
        for g in range(HALF_GROUPS):
            hrow = pl.multiple_of(dst_ref[half * HALF_GROUPS + g], SUBLANES)
            pltpu.make_async_copy(buf.at[pl.ds(g * SUBLANES, SUBLANES)],
                                  ys_hbm.at[pl.ds(hrow, SUBLANES)], sem).start()

    def scatter_wait(buf, sem):
        pltpu.make_async_copy(buf, ys_hbm.at[pl.ds(0, HALF_ROWS)], sem).wait()

    def experts(xbuf, obuf):
        x = _unpack_halves(xbuf[slot]).astype(BF16)
        gu = jnp.dot(x, wgu_bf[...], preferred_element_type=F32) + bgu_ref[0]
        g = jnp.minimum(gu[:, :D_FF], SWIGLU_LIMIT)
        up = jnp.clip(gu[:, D_FF:], -SWIGLU_LIMIT, SWIGLU_LIMIT)
        hdn = (up + 1.0) * (g * _sigmoid(SWIGLU_ALPHA * g))
        y = jnp.dot(hdn.astype(BF16), wd_bf[...], preferred_element_type=F32) + bd_ref[0]
        obuf[par] = _pack_halves(y.astype(BF16).astype(F32))

    def gather_block(blk, slot):
        blk = jnp.minimum(blk, n_used - 1)
        gather(2 * blk, xa.at[slot], gsem_a.at[slot])
        gather(2 * blk + 1, xb.at[slot], gsem_b.at[slot])

    def gather_block_wait(slot):
        gather_wait(xa.at[slot], gsem_a.at[slot])
        gather_wait(xb.at[slot], gsem_b.at[slot])

    @pl.when(i == 0)
    def _():
        gather_block(0, 0)
        gather_block(1, 1)
        for c in weight_copies(be_ref[0], 0):
            c.start(priority=WEIGHT_DMA_PRIORITY)

    @pl.when(i < n_used)
    def _():
        @pl.when(first_ref[i] == 1)
        def _():
            s = wslot_ref[i]
            for c in weight_copies(be_ref[i], s):
                c.wait()
            wgu_bf[...] = wgu_f32[s].astype(BF16)
            wd_bf[...] = wd_f32[s].astype(BF16)

            @pl.when(nxt_ref[i] < N_EXPERTS)
            def _():
                for c in weight_copies(nxt_ref[i], 1 - s):
                    c.start(priority=WEIGHT_DMA_PRIORITY)

        def wait_writeback(step, p):
            scatter_wait(oa.at[p], ssem_a.at[p])

            @pl.when(hasb_ref[step] == 1)
            def _():
                scatter_wait(ob.at[p], ssem_b.at[p])

        @pl.when(i >= 2)
        def _():
            wait_writeback(i - 2, par)

        gather_block_wait(slot)

        @pl.when(hasb_ref[i] == 1)
        def _():
            gather_block(i + 2, lax.rem(i + 2, X_SLOTS))
            experts(xa, oa)
            scatter(half_a, oa.at[par], ssem_a.at[par])
            experts(xb, ob)
            scatter(half_b, ob.at[par], ssem_b.at[par])

        @pl.when(hasb_ref[i] == 0)
        def _():
            gather_block(i + 2, lax.rem(i + 2, X_SLOTS))
            experts(xa, oa)
            scatter(half_a, oa.at[par], ssem_a.at[par])

        @pl.when(i == n_used - 1)
        def _():
            wait_writeback(i, par)
            gather_block_wait(lax.rem(i + 1, X_SLOTS))
            gather_block_wait(lax.rem(i + 2, X_SLOTS))

            @pl.when(i >= 1)
            def _():
                wait_writeback(i - 1, 1 - par)


def _moe(xs_all, tables, w_gate_up, b_gate_up, w_down, b_down):
    block_e, n_used, src, dst, first, wslot, nxt, hasb = tables
    n_blocks = block_e.shape[0]

    def e_map(i, be, nu, *_):
        return (be[jnp.minimum(i, nu[0] - 1)], 0, 0)

    x_buf = pltpu.VMEM((X_SLOTS, HALF_ROWS, PACKED), U32)
    o_buf = pltpu.VMEM((2, HALF_ROWS, PACKED), U32)
    any_spec = pl.BlockSpec(memory_space=pl.ANY)
    grid_spec = pltpu.PrefetchScalarGridSpec(
        num_scalar_prefetch=8,
        grid=(n_blocks,),
        in_specs=[any_spec,
                  any_spec,
                  pl.BlockSpec((1, 1, 2 * D_FF), e_map),
                  any_spec,
                  pl.BlockSpec((1, 1, D_MODEL), e_map)],
        out_specs=any_spec,
        scratch_shapes=[x_buf, x_buf, o_buf, o_buf,
                        pltpu.VMEM((2, D_MODEL, 2 * D_FF), F32),
                        pltpu.VMEM((2, D_FF, D_MODEL), F32),
                        pltpu.VMEM((D_MODEL, 2 * D_FF), BF16),
                        pltpu.VMEM((D_FF, D_MODEL), BF16)]
                       + [pltpu.SemaphoreType.DMA((X_SLOTS,))] * 2 + [pltpu.SemaphoreType.DMA((2,))] * 3,
    )
    return pl.pallas_call(
        _moe_kernel,
        grid_spec=grid_spec,
        out_shape=jax.ShapeDtypeStruct((xs_all.shape[0] + 2 * MOE_ROWS, PACKED), U32),
        compiler_params=pltpu.CompilerParams(
            dimension_semantics=("arbitrary",), vmem_limit_bytes=VMEM_LIMIT_BYTES),
        name="moe_experts",
    )(block_e, n_used, src, dst, first, wslot, nxt, hasb,
      xs_all, w_gate_up, b_gate_up.reshape(N_EXPERTS, 1, 2 * D_FF), w_down,
      b_down.reshape(N_EXPERTS, 1, D_MODEL))


def _final_kernel(used_ref, h_ref, ys_ref, q_ref, gate_ref, g_ref, o_ref, *, tile0, slabs):
    i = pl.program_id(0)

    def to_cols(a):
        pad = jnp.zeros((LANES - SUBLANES, TILE), F32)
        return jnp.concatenate([a, pad], axis=0).T

    qc_all = to_cols(q_ref[...].astype(F32))
    gc_all = to_cols(gate_ref[...])
    parts = []
    for s in range(SUBTILES):
        ts = slice(s * SUB_TOKENS, (s + 1) * SUB_TOKENS)
        used = used_ref[(tile0 + i) * SUBTILES + s]
        qc, gc = qc_all[ts], gc_all[ts]
        acc = h_ref[ts, :]
        for c in range(SUB_SORTED // SORT_CHUNK):
            jl = (c * SORT_CHUNK + lax.broadcasted_iota(I32, (SUB_TOKENS, SORT_CHUNK), 1)).astype(F32)
            pg = sum(jnp.where(qc[:, k:k + 1] == jl, gc[:, k:k + 1], 0.0) for k in range(TOP_K))
            rid = c * SORT_CHUNK + lax.broadcasted_iota(I32, (SORT_CHUNK, 1), 0)
            rows = ys_ref[pl.ds(s * SUB_SORTED + c * SORT_CHUNK, SORT_CHUNK), :]
            ys = jnp.where(rid < used, _unpack_halves(rows), 0.0)
            acc = acc + jnp.dot(pg.astype(BF16), ys.astype(BF16), preferred_element_type=F32)
        parts.append(acc)
    y = _rmsnorm(jnp.concatenate(parts, axis=0), g_ref[...])
    if slabs == 1:
        o_ref[...] = y
    else:
        seqs = TILE // slabs
        for t in range(slabs):
            o_ref[:, t, :] = y[t * seqs:(t + 1) * seqs]


def _final(h, ys_all, q, gates, used, g_final, *, tile0, slabs):
    n_tok = h.shape[0]
    if slabs == 1:
        out_spec = pl.BlockSpec((TILE, D_MODEL), lambda i, u: (i, 0))
        out_shape = jax.ShapeDtypeStruct((n_tok, D_MODEL), F32)
    else:
        seqs = TILE // slabs
        out_spec = pl.BlockSpec((seqs, slabs, D_MODEL), lambda i, u: (i, 0, 0))
        out_shape = jax.ShapeDtypeStruct((n_tok // slabs, slabs, D_MODEL), F32)
    grid_spec = pltpu.PrefetchScalarGridSpec(
        num_scalar_prefetch=1,
        grid=(n_tok // TILE,),
        in_specs=[pl.BlockSpec((TILE, D_MODEL), lambda i, u: (i, 0)),
                  pl.BlockSpec((SORTED_ROWS, PACKED), lambda i, u: (tile0 + i, 0)),
                  pl.BlockSpec((SUBLANES, TILE), lambda i, u: (0, i)),
                  pl.BlockSpec((SUBLANES, TILE), lambda i, u: (0, i)),
                  pl.BlockSpec((1, D_MODEL), lambda i, u: (0, 0))],
        out_specs=out_spec,
    )
    return pl.pallas_call(
        functools.partial(_final_kernel, tile0=tile0, slabs=slabs),
        grid_spec=grid_spec,
        out_shape=out_shape,
        compiler_params=pltpu.CompilerParams(
            dimension_semantics=("arbitrary",), vmem_limit_bytes=VMEM_LIMIT_BYTES),
        name="final_norm",
    )(used, h, ys_all, q, gates, g_final)


def _block_tables(tbl, n_blocks):
    tbl = tbl.reshape(-1, 2, N_EXPERTS, LANES)
    cnt = tbl[:, 0, :, 0]
    off = tbl[:, 1, :, 0]
    run = jnp.cumsum(cnt, axis=0) - cnt
    total = jnp.sum(cnt, axis=0)
    nblk = (total + MOE_ROWS - 1) // MOE_ROWS
    blk_end = jnp.cumsum(nblk)
    blk_start = blk_end - nblk
    bidx = jnp.arange(n_blocks, dtype=I32)
    eidx = jnp.arange(N_EXPERTS, dtype=I32)
    block_e = jnp.minimum(jnp.sum(blk_end[None, :] <= bidx[:, None], axis=1), N_EXPERTS - 1).astype(I32)
    hot = block_e[:, None] == eidx[None, :]

    def per_block(v):
        return jnp.sum(jnp.where(hot, v[..., None, :], 0), axis=-1)

    row0 = (bidx - per_block(blk_start)) * MOE_ROWS
    n_tiles = cnt.shape[0]
    groups = jnp.arange(MOE_GROUPS, dtype=I32)
    erow = row0[:, None] + groups[None, :] * SUBLANES
    run_b, cnt_b, off_b = per_block(run), per_block(cnt), per_block(off)
    inside = jnp.logical_and(run_b[:, :, None] <= erow[None], erow[None] < (run_b + cnt_b)[:, :, None])
    tile_base = jnp.arange(n_tiles, dtype=I32)[:, None] * SUB_SORTED + off_b - run_b
    src = jnp.sum(jnp.where(inside, tile_base[:, :, None] + erow[None], 0), axis=0)
    dump = n_tiles * SUB_SORTED + (bidx % 2)[:, None] * MOE_ROWS + groups[None, :] * SUBLANES
    dst = jnp.where(erow < per_block(total)[:, None], src, dump).reshape(-1)
    src = src.reshape(-1)
    used = jnp.sum(cnt, axis=1)
    has_rows = nblk > 0
    first = (bidx == per_block(blk_start)).astype(I32)
    wslot = per_block((jnp.cumsum(has_rows) - 1) % 2)
    later = jnp.where(jnp.logical_and(has_rows[None, :], eidx[None, :] > eidx[:, None]), eidx[None, :], N_EXPERTS)
    nxt = per_block(jnp.min(later, axis=1))
    hasb = (per_block(total) - row0 > HALF_ROWS).astype(I32)
    return ((block_e, blk_end[-1:].astype(I32), src.astype(I32), dst.astype(I32), first,
             wslot.astype(I32), nxt.astype(I32), hasb), used.astype(I32))


def kernel(x_prompt, x_sample, state_pool, state_conv, g_mix, w_in, pool_w, pool_scale, conv_w, conv_b,
           conv_ln_g, conv_ln_b, w_out, g_moe, w_router, b_router, w_gate_up, b_gate_up, w_down, b_down,
           g_final):
    nb_p, nt_p, _ = x_prompt.shape
    nb_s, nt_s, _ = x_sample.shape
    n_p, n_s = nb_p * nt_p, nb_s * nt_s
    assert nt_p % TILE == 0 and TILE % nt_s == 0 and n_s % TILE == 0
    tiles_p, tiles_s = n_p // TILE, n_s // TILE
    n_tiles = tiles_p + tiles_s

    tri_shape = (SUB_TOKENS, SUB_TOKENS)
    tri = (lax.broadcasted_iota(I32, tri_shape, 0) < lax.broadcasted_iota(I32, tri_shape, 1)).astype(BF16)
    weights = (g_mix[0][None], w_in[0].astype(BF16), pool_w[0].astype(BF16), pool_scale[0][None],
               conv_w[0], conv_b[0][None], conv_ln_g[0][None], conv_ln_b[0][None],
               w_out[0].astype(BF16), g_moe[0][None], w_router[0].T.astype(BF16),
               b_router[0][:, None], tri)

    time_major = lambda a: jnp.transpose(a, (1, 0, 2))
    zero_pool = jnp.zeros((1, nb_p, POOL_BUF, POOL_CH), F32)
    zero_conv = jnp.zeros((1, nb_p, CONV_BUF, CONV_CH), F32)
    h_p, q_p, gate_p, tbl_p, xs_all, pool_p, conv_p = _front_prompt(
        x_prompt, zero_pool, zero_conv, weights,
        jnp.broadcast_to(conv_w[0][:, None, :], (CONV_TAPS, SUBLANES, CONV_CH)), pos0=0, n_tiles_all=n_tiles)
    h_s, q_s, gate_s, tbl_s, xs_all, pool_s, conv_s = _front_sample(
        x_sample, time_major(state_pool[0]), time_major(state_conv[0]), weights, xs_all,
        pos0=PAST_LEN, tile0=tiles_p)

    sorted_total = n_tiles * SUBTILES * (SUB_TOKENS * TOP_K + N_EXPERTS * (SUBLANES - 1))
    n_blocks = -(-sorted_total // MOE_ROWS) + N_EXPERTS
    tables, used = _block_tables(jnp.concatenate([tbl_p, tbl_s], axis=0), n_blocks)
    ys_all = _moe(xs_all, tables, w_gate_up[0], b_gate_up[0], w_down[0], b_down[0])

    gf = g_final[None]
    y_prompt = _final(h_p, ys_all, q_p, gate_p, used, gf, tile0=0, slabs=1)
    y_sample = _final(h_s, ys_all, q_s, gate_s, used, gf, tile0=tiles_p, slabs=nt_s)
    return (y_prompt.reshape(nb_p, nt_p, D_MODEL), y_sample,
            pool_p, conv_p,
            time_major(pool_s)[None], time_major(conv_s)[None])
```

```python
import functools

import jax
import jax.numpy as jnp
from jax import lax
from jax.experimental import pallas as pl
from jax.experimental.pallas import tpu as pltpu

D_MODEL = 1024
POOL_CH = 512
CONV_CH = 512
POOL_WINDOWS = (2, 4, 8, 16)
POOL_GROUP = 128
POOL_BUF = 15
CONV_TAPS = 31
CONV_BUF = 30
IN_COLS = POOL_CH + 2 * CONV_CH
N_EXPERTS = 32
TOP_K = 4
D_FF = 1024
SWIGLU_LIMIT = 7.0
SWIGLU_ALPHA = 1.702
RMS_EPS = 1e-5
LN_EPS = 1e-5
PAST_LEN = 16384

LANES = 128
SUBLANES = 8
VMEM_LIMIT_BYTES = 60 * 1024 * 1024

TILE = 512
SUB_TOKENS = 256
SUBTILES = TILE // SUB_TOKENS
CONV_CHUNK = 32
SORT_CHUNK = 256
FRONT_SORT_CHUNK = 640
SUB_SORTED = -(-(SUB_TOKENS * TOP_K + N_EXPERTS * (SUBLANES - 1)) // SORT_CHUNK) * SORT_CHUNK
SORTED_ROWS = SUBTILES * SUB_SORTED
MOE_ROWS = 512
HALF_ROWS = MOE_ROWS // 2
MOE_GROUPS = MOE_ROWS // SUBLANES
HALF_GROUPS = MOE_GROUPS // 2
X_SLOTS = 3
WEIGHT_DMA_PRIORITY = 1

POOL_HIST = 24
CONV_HIST = 32

BF16 = jnp.bfloat16
F32 = jnp.float32
I32 = jnp.int32
U32 = jnp.uint32
PACKED = D_MODEL // 2


def _rmsnorm(x, g):
    ms = jnp.mean(x * x, axis=-1, keepdims=True)
    return x * lax.rsqrt(ms + RMS_EPS) * g


def _sigmoid(x):
    return 1.0 / (1.0 + jnp.exp(-x))


def _layernorm_silu(d, g, b):
    mu = jnp.mean(d, axis=-1, keepdims=True)
    c = d - mu
    var = jnp.mean(c * c, axis=-1, keepdims=True)
    n = c * lax.rsqrt(var + LN_EPS) * g + b
    return n * _sigmoid(n)


def _pack_halves(a):
    half = a.shape[1] // 2
    hi = lax.bitcast_convert_type(a[:, :half], U32)
    lo = lax.bitcast_convert_type(a[:, half:], U32)
    return hi | (lo >> 16)


def _unpack_halves(w):
    hi = lax.bitcast_convert_type(w & jnp.uint32(0xFFFF0000), F32)
    lo = lax.bitcast_convert_type(w << 16, F32)
    return jnp.concatenate([hi, lo], axis=1)


def _route_and_sort(hn_bf, wr_t_ref, br_ref, tri_ref, q_ref, gate_ref, tbl_ref, xs_ref):
    rows = hn_bf.shape[0]
    lt = lax.dot_general(wr_t_ref[...], hn_bf, (((1,), (1,)), ((), ())),
                         preferred_element_type=F32)
    lt = lt + br_ref[...]
    eidx = lax.broadcasted_iota(I32, (N_EXPERTS, rows), 0)
    vals, hots = [], []
    for _ in range(TOP_K):
        m = jnp.max(lt, axis=0, keepdims=True)
        sel = jnp.min(jnp.where(lt == m, eidx, N_EXPERTS), axis=0, keepdims=True)
        hit = eidx == sel
        vals.append(m)
        hots.append(hit)
        lt = jnp.where(hit, -jnp.inf, lt)
    v = jnp.concatenate(vals, axis=0)
    e = jnp.exp(v - vals[0])
    gate = e / jnp.sum(e, axis=0, keepdims=True)
    gate_ref[...] = jnp.concatenate([gate, jnp.zeros((SUBLANES - TOP_K, rows), F32)], axis=0)

    sel_cnt = sum(h.astype(F32) for h in hots)
    lower = (lax.broadcasted_iota(I32, (N_EXPERTS, N_EXPERTS), 1)
             < lax.broadcasted_iota(I32, (N_EXPERTS, N_EXPERTS), 0)).astype(BF16)
    q_rows = []
    for s in range(SUBTILES):
        ts = slice(s * SUB_TOKENS, (s + 1) * SUB_TOKENS)
        sel_s = sel_cnt[:, ts]
        prefix = jnp.dot(sel_s.astype(BF16), tri_ref[...], preferred_element_type=F32)
        cnt = jnp.sum(sel_s, axis=1, keepdims=True)
        cnt_u = jnp.floor((cnt + (SUBLANES - 1)) * (1.0 / SUBLANES))
        cnt_u_l = jnp.broadcast_to(cnt_u, (N_EXPERTS, LANES))
        off_u_l = jnp.dot(lower, cnt_u_l.astype(BF16), preferred_element_type=F32)
        off_l = off_u_l * float(SUBLANES)
        tbl_ref[0, s, 0] = (cnt_u_l * float(SUBLANES)).astype(I32)
        tbl_ref[0, s, 1] = off_l.astype(I32)

        pos = prefix + off_l[:, :1]
        qs = [jnp.sum(jnp.where(h[:, ts], pos, 0.0), axis=0, keepdims=True).astype(I32) for h in hots]
        q_rows.append(jnp.concatenate(qs + [jnp.zeros((SUBLANES - TOP_K, SUB_TOKENS), I32)], axis=0))

        hn_s = hn_bf[ts]
        for c in range(SUB_SORTED // FRONT_SORT_CHUNK):
            r0 = c * FRONT_SORT_CHUNK
            jidx = r0 + lax.broadcasted_iota(I32, (FRONT_SORT_CHUNK, SUB_TOKENS), 0)
            p = sum(jnp.where(jidx == qk, 1.0, 0.0) for qk in qs)
            xs = jnp.dot(p.astype(BF16), hn_s, preferred_element_type=F32)
            xs_ref[pl.ds(s * SUB_SORTED + r0, FRONT_SORT_CHUNK), :] = _pack_halves(xs)
    q_ref[...] = jnp.concatenate(q_rows, axis=1)


def _mix_out_and_route(x, mixbuf, w_out_ref, g_moe_ref, wr_t_ref, br_ref, tri_ref,
                       h_ref, q_ref, gate_ref, tbl_ref, xs_ref):
    h = x + jnp.dot(mixbuf[...], w_out_ref[...], preferred_element_type=F32)
    h_ref[...] = h
    hn = _rmsnorm(h, g_moe_ref[...])
    _route_and_sort(hn.astype(BF16), wr_t_ref, br_ref, tri_ref, q_ref, gate_ref, tbl_ref, xs_ref)


def _front_prompt_kernel(x_ref, sp_ref, sc_ref, g_mix_ref, w_in_ref, pool_w_ref, pool_scale_ref,
                         conv_w_ref, conv_b_ref, ln_g_ref, ln_b_ref, w_out_ref, g_moe_ref,
                         wr_t_ref, br_ref, tri_ref, conv_w8_ref,
                         h_ref, q_ref, gate_ref, tbl_ref, xs_ref, np_ref, nc_ref,
                         ubuf, gsh, dbuf, mixbuf, *, pos0, rows):
    del conv_w_ref
    t = pl.program_id(1)

    @pl.when(t == 0)
    def _():
        ubuf[:, pl.ds(0, POOL_HIST), :] = jnp.zeros((len(POOL_WINDOWS), POOL_HIST, POOL_CH), F32)
        ubuf[0, pl.ds(POOL_HIST - POOL_BUF, POOL_BUF), :] = sp_ref[0, 0]
        gsh[0, pl.ds(CONV_HIST - CONV_BUF, CONV_BUF), :] = sc_ref[0, 0]

    x = x_ref[0]
    xn = _rmsnorm(x, g_mix_ref[...]).astype(BF16)
    proj = jnp.dot(xn, w_in_ref[...], preferred_element_type=F32)

    u = proj[:, :POOL_CH]
    ubuf[0, pl.ds(POOL_HIST, rows), :] = u
    pos = pos0 + t * rows + lax.broadcasted_iota(I32, (rows, 1), 0)
    span = rows + POOL_HIST - SUBLANES
    for g, w in enumerate(POOL_WINDOWS):
        assert w == 2 ** (g + 1) and w // 2 <= SUBLANES
        c0 = g * POOL_GROUP
        cs = slice(c0, c0 + POOL_GROUP)
        a = ubuf[g, pl.ds(SUBLANES, span), c0:] + ubuf[g, pl.ds(SUBLANES - w // 2, span), c0:]
        if g + 1 < len(POOL_WINDOWS):
            ubuf[g + 1, pl.ds(SUBLANES, span), c0 + POOL_GROUP:] = a[:, POOL_GROUP:]
        s = a[span - rows:, :POOL_GROUP]
        cnt = jnp.minimum(w, pos + 1).astype(F32)
        z = (s / cnt - u[:, cs]).astype(BF16)
        yg = jnp.dot(z, pool_w_ref[g], preferred_element_type=F32)
        mixbuf[:, cs] = (yg * pool_scale_ref[:, cs]).astype(BF16)

    glu = proj[:, POOL_CH:POOL_CH + CONV_CH] * _sigmoid(proj[:, POOL_CH + CONV_CH:])
    gsh[0, pl.ds(CONV_HIST, rows), :] = glu
    for s in range(1, SUBLANES):
        gsh[s, pl.ds(0, CONV_HIST - s), :] = gsh[0, pl.ds(s, CONV_HIST - s), :]
        gsh[s, pl.ds(CONV_HIST - s, rows), :] = glu

    def conv_chunk(i, carry):
        r0 = pl.multiple_of(i * CONV_CHUNK, CONV_CHUNK)
        acc = jnp.broadcast_to(conv_b_ref[...], (CONV_CHUNK // SUBLANES, SUBLANES, CONV_CH))
        for k in range(CONV_TAPS):
            first = CONV_HIST - CONV_BUF + k
            s = first % SUBLANES
            rows_k = gsh[s, pl.ds(r0 + (first - s), CONV_CHUNK), :]
            acc = acc + conv_w8_ref[k][None] * rows_k.reshape(CONV_CHUNK // SUBLANES, SUBLANES, CONV_CH)
        dbuf[pl.ds(r0, CONV_CHUNK), :] = acc.reshape(CONV_CHUNK, CONV_CH)
        return carry

    lax.fori_loop(0, rows // CONV_CHUNK, conv_chunk, 0, unroll=4)
    y_conv = _layernorm_silu(dbuf[...], ln_g_ref[...], ln_b_ref[...])
    mixbuf[:, POOL_CH:] = y_conv.astype(BF16)

    new_pool = ubuf[0, pl.ds(rows + POOL_HIST - POOL_BUF, POOL_BUF), :]
    new_conv = gsh[0, pl.ds(rows + CONV_HIST - CONV_BUF, CONV_BUF), :]
    np_ref[0, 0] = new_pool
    nc_ref[0, 0] = new_conv
    ubuf[0, pl.ds(POOL_HIST - POOL_BUF, POOL_BUF), :] = new_pool
    gsh[0, pl.ds(CONV_HIST - CONV_BUF, CONV_BUF), :] = new_conv

    _mix_out_and_route(x, mixbuf, w_out_ref, g_moe_ref, wr_t_ref, br_ref, tri_ref,
                       h_ref, q_ref, gate_ref, tbl_ref, xs_ref)


def _front_sample_kernel(x_ref, sp_ref, sc_ref, g_mix_ref, w_in_ref, pool_w_ref, pool_scale_ref,
                         conv_w_ref, conv_b_ref, ln_g_ref, ln_b_ref, w_out_ref, g_moe_ref,
                         wr_t_ref, br_ref, tri_ref, xs_in,
                         h_ref, q_ref, gate_ref, tbl_ref, xs_ref, np_ref, nc_ref,
                         xbuf, dbuf, mixbuf, *, pos0, seqs, steps):
    del xs_in
    for t in range(steps):
        xbuf[pl.ds(t * seqs, seqs), :] = x_ref[:, t, :]
    x = xbuf[...]
    xn = _rmsnorm(x, g_mix_ref[...]).astype(BF16)
    proj = jnp.dot(xn, w_in_ref[...], preferred_element_type=F32)

    def slab(a, t):
        return a[t * seqs:(t + 1) * seqs]

    u = proj[:, :POOL_CH]
    cat = [sp_ref[j] for j in range(POOL_BUF)]
    cat += [slab(u, t) for t in range(steps)]
    for t in range(steps):
        for g, w in enumerate(POOL_WINDOWS):
            cs = slice(g * POOL_GROUP, (g + 1) * POOL_GROUP)
            s = cat[POOL_BUF + t][:, cs]
            for j in range(1, w):
                s = s + cat[POOL_BUF + t - j][:, cs]
            cnt = float(min(w, pos0 + t + 1))
            z = (s / cnt - cat[POOL_BUF + t][:, cs]).astype(BF16)
            yg = jnp.dot(z, pool_w_ref[g], preferred_element_type=F32)
            mixbuf[pl.ds(t * seqs, seqs), cs] = (yg * pool_scale_ref[:, cs]).astype(BF16)
    for j in range(POOL_BUF):
        np_ref[j] = cat[len(cat) - POOL_BUF + j]

    glu = proj[:, POOL_CH:POOL_CH + CONV_CH] * _sigmoid(proj[:, POOL_CH + CONV_CH:])
    ccat = [sc_ref[j] for j in range(CONV_BUF)]
    ccat += [slab(glu, t) for t in range(steps)]
    for t in range(steps):
        acc = jnp.broadcast_to(conv_b_ref[...], (seqs, CONV_CH))
        for k in range(CONV_TAPS):
            acc = acc + conv_w_ref[k:k + 1, :] * ccat[t + k]
        dbuf[pl.ds(t * seqs, seqs), :] = acc
    for j in range(CONV_BUF):
        nc_ref[j] = ccat[len(ccat) - CONV_BUF + j]
    y_conv = _layernorm_silu(dbuf[...], ln_g_ref[...], ln_b_ref[...])
    mixbuf[:, POOL_CH:] = y_conv.astype(BF16)

    _mix_out_and_route(x, mixbuf, w_out_ref, g_moe_ref, wr_t_ref, br_ref, tri_ref,
                       h_ref, q_ref, gate_ref, tbl_ref, xs_ref)


def _const_spec(shape, single_buffer=False):
    mode = pl.Buffered(1) if single_buffer else None
    return pl.BlockSpec(shape, lambda *_: (0,) * len(shape), pipeline_mode=mode)


def _front_weight_specs():
    return [
        _const_spec((1, D_MODEL)),
        _const_spec((D_MODEL, IN_COLS), True),
        _const_spec((len(POOL_WINDOWS), POOL_GROUP, POOL_GROUP)),
        _const_spec((1, POOL_CH)),
        _const_spec((CONV_TAPS, CONV_CH)),
        _const_spec((1, CONV_CH)),
        _const_spec((1, CONV_CH)),
        _const_spec((1, CONV_CH)),
        _const_spec((D_MODEL, D_MODEL), True),
        _const_spec((1, D_MODEL)),
        _const_spec((N_EXPERTS, D_MODEL)),
        _const_spec((N_EXPERTS, 1)),
        _const_spec((SUB_TOKENS, SUB_TOKENS)),
    ]


def _front_out(n_tok, n_tiles_all, tile_map, tok_map, extra_specs, extra_shapes):
    specs = [pl.BlockSpec((TILE, D_MODEL), tok_map),
             pl.BlockSpec((SUBLANES, TILE), lambda *i: (0, tok_map(*i)[0])),
             pl.BlockSpec((SUBLANES, TILE), lambda *i: (0, tok_map(*i)[0])),
             pl.BlockSpec((1, SUBTILES, 2, N_EXPERTS, LANES),
                          lambda *i: (tok_map(*i)[0], 0, 0, 0, 0)),
             pl.BlockSpec((SORTED_ROWS, PACKED), tile_map)]
    shapes = [jax.ShapeDtypeStruct((n_tok, D_MODEL), F32),
              jax.ShapeDtypeStruct((SUBLANES, n_tok), I32),
              jax.ShapeDtypeStruct((SUBLANES, n_tok), F32),
              jax.ShapeDtypeStruct((n_tok // TILE, SUBTILES, 2, N_EXPERTS, LANES), I32),
              jax.ShapeDtypeStruct((n_tiles_all * SORTED_ROWS, PACKED), U32)]
    return specs + extra_specs, shapes + extra_shapes


def _front_prompt(x, state_pool, state_conv, weights, conv_w8, *, pos0, n_tiles_all):
    nb, nt, _ = x.shape
    rows = TILE
    tiles = nt // rows
    tok_map = lambda b, t: (b * tiles + t, 0)
    out_specs, out_shape = _front_out(
        nb * nt, n_tiles_all, tok_map, tok_map,
        [pl.BlockSpec((1, 1, POOL_BUF, POOL_CH), lambda b, t: (0, b, 0, 0)),
         pl.BlockSpec((1, 1, CONV_BUF, CONV_CH), lambda b, t: (0, b, 0, 0))],
        [jax.ShapeDtypeStruct((1, nb, POOL_BUF, POOL_CH), F32),
         jax.ShapeDtypeStruct((1, nb, CONV_BUF, CONV_CH), F32)])
    kern = functools.partial(_front_prompt_kernel, pos0=pos0, rows=rows)
    return pl.pallas_call(
        kern,
        grid=(nb, tiles),
        in_specs=[pl.BlockSpec((1, rows, D_MODEL), lambda b, t: (b, t, 0)),
                  pl.BlockSpec((1, 1, POOL_BUF, POOL_CH), lambda b, t: (0, b, 0, 0)),
                  pl.BlockSpec((1, 1, CONV_BUF, CONV_CH), lambda b, t: (0, b, 0, 0))]
                 + _front_weight_specs() + [_const_spec((CONV_TAPS, SUBLANES, CONV_CH))],
        out_specs=out_specs,
        out_shape=out_shape,
        scratch_shapes=[pltpu.VMEM((len(POOL_WINDOWS), POOL_HIST + rows, POOL_CH), F32),
                        pltpu.VMEM((SUBLANES, CONV_HIST + rows, CONV_CH), F32),
                        pltpu.VMEM((rows, CONV_CH), F32),
                        pltpu.VMEM((rows, D_MODEL), BF16)],
        compiler_params=pltpu.CompilerParams(
            dimension_semantics=("arbitrary", "arbitrary"), vmem_limit_bytes=VMEM_LIMIT_BYTES),
        name="front_prompt",
    )(x, state_pool, state_conv, *weights, conv_w8)


def _front_sample(x, state_pool, state_conv, weights, xs_all, *, pos0, tile0):
    nb, nt, _ = x.shape
    seqs = TILE // nt
    n_tiles_all = xs_all.shape[0] // SORTED_ROWS
    tok_map = lambda i: (i, 0)
    out_specs, out_shape = _front_out(
        nb * nt, n_tiles_all, lambda i: (tile0 + i, 0), tok_map,
        [pl.BlockSpec((POOL_BUF, seqs, POOL_CH), lambda i: (0, i, 0)),
         pl.BlockSpec((CONV_BUF, seqs, CONV_CH), lambda i: (0, i, 0))],
        [jax.ShapeDtypeStruct((POOL_BUF, nb, POOL_CH), F32),
         jax.ShapeDtypeStruct((CONV_BUF, nb, CONV_CH), F32)])
    kern = functools.partial(_front_sample_kernel, pos0=pos0, seqs=seqs, steps=nt)
    n_in = 3 + len(weights)
    return pl.pallas_call(
        kern,
        grid=(nb // seqs,),
        in_specs=[pl.BlockSpec((seqs, nt, D_MODEL), lambda i: (i, 0, 0)),
                  pl.BlockSpec((POOL_BUF, seqs, POOL_CH), lambda i: (0, i, 0)),
                  pl.BlockSpec((CONV_BUF, seqs, CONV_CH), lambda i: (0, i, 0))]
                 + _front_weight_specs() + [pl.BlockSpec(memory_space=pl.ANY)],
        out_specs=out_specs,
        out_shape=out_shape,
        scratch_shapes=[pltpu.VMEM((TILE, D_MODEL), F32),
                        pltpu.VMEM((TILE, CONV_CH), F32),
                        pltpu.VMEM((TILE, D_MODEL), BF16)],
        input_output_aliases={n_in: 4},
        compiler_params=pltpu.CompilerParams(
            dimension_semantics=("arbitrary",), vmem_limit_bytes=VMEM_LIMIT_BYTES),
        name="front_sample",
    )(x, state_pool, state_conv, *weights, xs_all)


def _moe_kernel(be_ref, nused_ref, src_ref, dst_ref, first_ref, wslot_ref, nxt_ref, hasb_ref,
                xs_hbm, wgu_hbm, bgu_ref, wd_hbm, bd_ref, ys_hbm,
                xa, xb, oa, ob, wgu_f32, wd_f32, wgu_bf, wd_bf,
                gsem_a, gsem_b, ssem_a, ssem_b, wsem):
    i = pl.program_id(0)
    n_used = nused_ref[0]
    par = i % 2
    slot = lax.rem(i, X_SLOTS)
    half_a = 2 * i
    half_b = 2 * i + 1

    def weight_copies(e, s):
        return (pltpu.make_async_copy(wgu_hbm.at[e], wgu_f32.at[s], wsem.at[s]),
                pltpu.make_async_copy(wd_hbm.at[e], wd_f32.at[s], wsem.at[s]))

    def gather(half, buf, sem):
        for g in range(HALF_GROUPS):
            hrow = pl.multiple_of(src_ref[half * HALF_GROUPS + g], SUBLANES)
            pltpu.make_async_copy(xs_hbm.at[pl.ds(hrow, SUBLANES)],
                                  buf.at[pl.ds(g * SUBLANES, SUBLANES)], sem).start()

    def gather_wait(buf, sem):
        pltpu.make_async_copy(xs_hbm.at[pl.ds(0, HALF_ROWS)], buf, sem).wait()

    def scatter(half, buf, sem):
        for g in range(HALF_GROUPS):
            hrow = pl.multiple_of(dst_ref[half * HALF_GROUPS + g], SUBLANES)
            pltpu.make_async_copy(buf.at[pl.ds(g * SUBLANES, SUBLANES)],
                                  ys_hbm.at[pl.ds(hrow, SUBLANES)], sem).start()

    def scatter_wait(buf, sem):
        pltpu.make_async_copy(buf, ys_hbm.at[pl.ds(0, HALF_ROWS)], sem).wait()

    def experts(xbuf, obuf):
        x = _unpack_halves(xbuf[slot]).astype(BF16)
        gu = jnp.dot(x, wgu_bf[...], preferred_element_type=F32) + bgu_ref[0]
        g = jnp.minimum(gu[:, :D_FF], SWIGLU_LIMIT)
        up = jnp.clip(gu[:, D_FF:], -SWIGLU_LIMIT, SWIGLU_LIMIT)
        hdn = (up + 1.0) * (g * _sigmoid(SWIGLU_ALPHA * g))
        y = jnp.dot(hdn.astype(BF16), wd_bf[...], preferred_element_type=F32) + bd_ref[0]
        obuf[par] = _pack_halves(y.astype(BF16).astype(F32))

    def gather_block(blk, slot):
        blk = jnp.minimum(blk, n_used - 1)
        gather(2 * blk, xa.at[slot], gsem_a.at[slot])
        gather(2 * blk + 1, xb.at[slot], gsem_b.at[slot])

    def gather_block_wait(slot):
        gather_wait(xa.at[slot], gsem_a.at[slot])
        gather_wait(xb.at[slot], gsem_b.at[slot])

    @pl.when(i == 0)
    def _():
        gather_block(0, 0)
        gather_block(1, 1)
        for c in weight_copies(be_ref[0], 0):
            c.start(priority=WEIGHT_DMA_PRIORITY)

    @pl.when(i < n_used)
    def _():
        @pl.when(first_ref[i] == 1)
        def _():
            s = wslot_ref[i]
            for c in weight_copies(be_ref[i], s):
                c.wait()
            wgu_bf[...] = wgu_f32[s].astype(BF16)
            wd_bf[...] = wd_f32[s].astype(BF16)

            @pl.when(nxt_ref[i] < N_EXPERTS)
            def _():
                for c in weight_copies(nxt_ref[i], 1 - s):
                    c.start(priority=WEIGHT_DMA_PRIORITY)

        def wait_writeback(step, p):
            scatter_wait(oa.at[p], ssem_a.at[p])

            @pl.when(hasb_ref[step] == 1)
            def _():
                scatter_wait(ob.at[p], ssem_b.at[p])

        @pl.when(i >= 2)
        def _():
            wait_writeback(i - 2, par)

        gather_block_wait(slot)

        @pl.when(hasb_ref[i] == 1)
        def _():
            gather_block(i + 2, lax.rem(i + 2, X_SLOTS))
            experts(xa, oa)
            scatter(half_a, oa.at[par], ssem_a.at[par])
            experts(xb, ob)
            scatter(half_b, ob.at[par], ssem_b.at[par])

        @pl.when(hasb_ref[i] == 0)
        def _():
            gather_block(i + 2, lax.rem(i + 2, X_SLOTS))
            experts(xa, oa)
            scatter(half_a, oa.at[par], ssem_a.at[par])

        @pl.when(i == n_used - 1)
        def _():
            wait_writeback(i, par)
            gather_block_wait(lax.rem(i + 1, X_SLOTS))
            gather_block_wait(lax.rem(i + 2, X_SLOTS))

            @pl.when(i >= 1)
            def _():
                wait_writeback(i - 1, 1 - par)


def _moe(xs_all, tables, w_gate_up, b_gate_up, w_down, b_down):
    block_e, n_used, src, dst, first, wslot, nxt, hasb = tables
    n_blocks = block_e.shape[0]

    def e_map(i, be, nu, *_):
        return (be[jnp.minimum(i, nu[0] - 1)], 0, 0)

    x_buf = pltpu.VMEM((X_SLOTS, HALF_ROWS, PACKED), U32)
    o_buf = pltpu.VMEM((2, HALF_ROWS, PACKED), U32)
    any_spec = pl.BlockSpec(memory_space=pl.ANY)
    grid_spec = pltpu.PrefetchScalarGridSpec(
        num_scalar_prefetch=8,
        grid=(n_blocks,),
        in_specs=[any_spec,
                  any_spec,
                  pl.BlockSpec((1, 1, 2 * D_FF), e_map),
                  any_spec,
                  pl.BlockSpec((1, 1, D_MODEL), e_map)],
        out_specs=any_spec,
        scratch_shapes=[x_buf, x_buf, o_buf, o_buf,
                        pltpu.VMEM((2, D_MODEL, 2 * D_FF), F32),
                        pltpu.VMEM((2, D_FF, D_MODEL), F32),
                        pltpu.VMEM((D_MODEL, 2 * D_FF), BF16),
                        pltpu.VMEM((D_FF, D_MODEL), BF16)]
                       + [pltpu.SemaphoreType.DMA((X_SLOTS,))] * 2 + [pltpu.SemaphoreType.DMA((2,))] * 3,
    )
    return pl.pallas_call(
        _moe_kernel,
        grid_spec=grid_spec,
        out_shape=jax.ShapeDtypeStruct((xs_all.shape[0] + 2 * MOE_ROWS, PACKED), U32),
        compiler_params=pltpu.CompilerParams(
            dimension_semantics=("arbitrary",), vmem_limit_bytes=VMEM_LIMIT_BYTES),
        name="moe_experts",
    )(block_e, n_used, src, dst, first, wslot, nxt, hasb,
      xs_all, w_gate_up, b_gate_up.reshape(N_EXPERTS, 1, 2 * D_FF), w_down,
      b_down.reshape(N_EXPERTS, 1, D_MODEL))


def _final_kernel(used_ref, h_ref, ys_ref, q_ref, gate_ref, g_ref, o_ref, *, tile0, slabs):
    i = pl.program_id(0)

    def to_cols(a):
        pad = jnp.zeros((LANES - SUBLANES, TILE), F32)
        return jnp.concatenate([a, pad], axis=0).T

    qc_all = to_cols(q_ref[...].astype(F32))
    gc_all = to_cols(gate_ref[...])
    parts = []
    for s in range(SUBTILES):
        ts = slice(s * SUB_TOKENS, (s + 1) * SUB_TOKENS)
        used = used_ref[(tile0 + i) * SUBTILES + s]
        qc, gc = qc_all[ts], gc_all[ts]
        acc = h_ref[ts, :]
        for c in range(SUB_SORTED // SORT_CHUNK):
            jl = (c * SORT_CHUNK + lax.broadcasted_iota(I32, (SUB_TOKENS, SORT_CHUNK), 1)).astype(F32)
            pg = sum(jnp.where(qc[:, k:k + 1] == jl, gc[:, k:k + 1], 0.0) for k in range(TOP_K))
            rid = c * SORT_CHUNK + lax.broadcasted_iota(I32, (SORT_CHUNK, 1), 0)
            rows = ys_ref[pl.ds(s * SUB_SORTED + c * SORT_CHUNK, SORT_CHUNK), :]
            ys = jnp.where(rid < used, _unpack_halves(rows), 0.0)
            acc = acc + jnp.dot(pg.astype(BF16), ys.astype(BF16), preferred_element_type=F32)
        parts.append(acc)
    y = _rmsnorm(jnp.concatenate(parts, axis=0), g_ref[...])
    if slabs == 1:
        o_ref[...] = y
    else:
        seqs = TILE // slabs
        for t in range(slabs):
            o_ref[:, t, :] = y[t * seqs:(t + 1) * seqs]


def _final(h, ys_all, q, gates, used, g_final, *, tile0, slabs):
    n_tok = h.shape[0]
    if slabs == 1:
        out_spec = pl.BlockSpec((TILE, D_MODEL), lambda i, u: (i, 0))
        out_shape = jax.ShapeDtypeStruct((n_tok, D_MODEL), F32)
    else:
        seqs = TILE // slabs
        out_spec = pl.BlockSpec((seqs, slabs, D_MODEL), lambda i, u: (i, 0, 0))
        out_shape = jax.ShapeDtypeStruct((n_tok // slabs, slabs, D_MODEL), F32)
    grid_spec = pltpu.PrefetchScalarGridSpec(
        num_scalar_prefetch=1,
        grid=(n_tok // TILE,),
        in_specs=[pl.BlockSpec((TILE, D_MODEL), lambda i, u: (i, 0)),
                  pl.BlockSpec((SORTED_ROWS, PACKED), lambda i, u: (tile0 + i, 0)),
                  pl.BlockSpec((SUBLANES, TILE), lambda i, u: (0, i)),
                  pl.BlockSpec((SUBLANES, TILE), lambda i, u: (0, i)),
                  pl.BlockSpec((1, D_MODEL), lambda i, u: (0, 0))],
        out_specs=out_spec,
    )
    return pl.pallas_call(
        functools.partial(_final_kernel, tile0=tile0, slabs=slabs),
        grid_spec=grid_spec,
        out_shape=out_shape,
        compiler_params=pltpu.CompilerParams(
            dimension_semantics=("arbitrary",), vmem_limit_bytes=VMEM_LIMIT_BYTES),
        name="final_norm",
    )(used, h, ys_all, q, gates, g_final)


def _block_tables(tbl, n_blocks):
    tbl = tbl.reshape(-1, 2, N_EXPERTS, LANES)
    cnt = tbl[:, 0, :, 0]
    off = tbl[:, 1, :, 0]
    run = jnp.cumsum(cnt, axis=0) - cnt
    total = jnp.sum(cnt, axis=0)
    nblk = (total + MOE_ROWS - 1) // MOE_ROWS
    blk_end = jnp.cumsum(nblk)
    blk_start = blk_end - nblk
    bidx = jnp.arange(n_blocks, dtype=I32)
    eidx = jnp.arange(N_EXPERTS, dtype=I32)
    block_e = jnp.minimum(jnp.sum(blk_end[None, :] <= bidx[:, None], axis=1), N_EXPERTS - 1).astype(I32)
    hot = block_e[:, None] == eidx[None, :]

    def per_block(v):
        return jnp.sum(jnp.where(hot, v[..., None, :], 0), axis=-1)

    row0 = (bidx - per_block(blk_start)) * MOE_ROWS
    n_tiles = cnt.shape[0]
    groups = jnp.arange(MOE_GROUPS, dtype=I32)
    erow = row0[:, None] + groups[None, :] * SUBLANES
    run_b, cnt_b, off_b = per_block(run), per_block(cnt), per_block(off)
    inside = jnp.logical_and(run_b[:, :, None] <= erow[None], erow[None] < (run_b + cnt_b)[:, :, None])
    tile_base = jnp.arange(n_tiles, dtype=I32)[:, None] * SUB_SORTED + off_b - run_b
    src = jnp.sum(jnp.where(inside, tile_base[:, :, None] + erow[None], 0), axis=0)
    dump = n_tiles * SUB_SORTED + (bidx % 2)[:, None] * MOE_ROWS + groups[None, :] * SUBLANES
    dst = jnp.where(erow < per_block(total)[:, None], src, dump).reshape(-1)
    src = src.reshape(-1)
    used = jnp.sum(cnt, axis=1)
    has_rows = nblk > 0
    first = (bidx == per_block(blk_start)).astype(I32)
    wslot = per_block((jnp.cumsum(has_rows) - 1) % 2)
    later = jnp.where(jnp.logical_and(has_rows[None, :], eidx[None, :] > eidx[:, None]), eidx[None, :], N_EXPERTS)
    nxt = per_block(jnp.min(later, axis=1))
    hasb = (per_block(total) - row0 > HALF_ROWS).astype(I32)
    return ((block_e, blk_end[-1:].astype(I32), src.astype(I32), dst.astype(I32), first,
             wslot.astype(I32), nxt.astype(I32), hasb), used.astype(I32))


def kernel(x_prompt, x_sample, state_pool, state_conv, g_mix, w_in, pool_w, pool_scale, conv_w, conv_b,
           conv_ln_g, conv_ln_b, w_out, g_moe, w_router, b_router, w_gate_up, b_gate_up, w_down, b_down,
           g_final):
    nb_p, nt_p, _ = x_prompt.shape
    nb_s, nt_s, _ = x_sample.shape
    n_p, n_s = nb_p * nt_p, nb_s * nt_s
    assert nt_p % TILE == 0 and TILE % nt_s == 0 and n_s % TILE == 0
    tiles_p, tiles_s = n_p // TILE, n_s // TILE
    n_tiles = tiles_p + tiles_s

    tri_shape = (SUB_TOKENS, SUB_TOKENS)
    tri = (lax.broadcasted_iota(I32, tri_shape, 0) < lax.broadcasted_iota(I32, tri_shape, 1)).astype(BF16)
    weights = (g_mix[0][None], w_in[0].astype(BF16), pool_w[0].astype(BF16), pool_scale[0][None],
               conv_w[0], conv_b[0][None], conv_ln_g[0][None], conv_ln_b[0][None],
               w_out[0].astype(BF16), g_moe[0][None], w_router[0].T.astype(BF16),
               b_router[0][:, None], tri)

    time_major = lambda a: jnp.transpose(a, (1, 0, 2))
    zero_pool = jnp.zeros((1, nb_p, POOL_BUF, POOL_CH), F32)
    zero_conv = jnp.zeros((1, nb_p, CONV_BUF, CONV_CH), F32)
    h_p, q_p, gate_p, tbl_p, xs_all, pool_p, conv_p = _front_prompt(
        x_prompt, zero_pool, zero_conv, weights,
        jnp.broadcast_to(conv_w[0][:, None, :], (CONV_TAPS, SUBLANES, CONV_CH)), pos0=0, n_tiles_all=n_tiles)
    h_s, q_s, gate_s, tbl_s, xs_all, pool_s, conv_s = _front_sample(
        x_sample, time_major(state_pool[0]), time_major(state_conv[0]), weights, xs_all,
        pos0=PAST_LEN, tile0=tiles_p)

    sorted_total = n_tiles * SUBTILES * (SUB_TOKENS * TOP_K + N_EXPERTS * (SUBLANES - 1))
    n_blocks = -(-sorted_total // MOE_ROWS) + N_EXPERTS
    tables, used = _block_tables(jnp.concatenate([tbl_p, tbl_s], axis=0), n_blocks)
    ys_all = _moe(xs_all, tables, w_gate_up[0], b_gate_up[0], w_down[0], b_down[0])

    gf = g_final[None]
    y_prompt = _final(h_p, ys_all, q_p, gate_p, used, gf, tile0=0, slabs=1)
    y_sample = _final(h_s, ys_all, q_s, gate_s, used, gf, tile0=tiles_p, slabs=nt_s)
    return (y_prompt.reshape(nb_p, nt_p, D_MODEL), y_sample,
            pool_p, conv_p,
            time_major(pool_s)[None], time_major(conv_s)[None])
```

```python
import functools

import jax
import jax.numpy as jnp
from jax import lax
from jax.experimental import pallas as pl
from jax.experimental.pallas import tpu as pltpu

D_MODEL = 1024
POOL_CH = 512
CONV_CH = 512
POOL_WINDOWS = (2, 4, 8, 16)
POOL_GROUP = 128
POOL_BUF = 15
CONV_TAPS = 31
CONV_BUF = 30
IN_COLS = POOL_CH + 2 * CONV_CH
N_EXPERTS = 32
TOP_K = 4
D_FF = 1024
SWIGLU_LIMIT = 7.0
SWIGLU_ALPHA = 1.702
RMS_EPS = 1e-5
LN_EPS = 1e-5
PAST_LEN = 16384

LANES = 128
SUBLANES = 8
VMEM_LIMIT_BYTES = 60 * 1024 * 1024

TILE = 512
SUB_TOKENS = 256
SUBTILES = TILE // SUB_TOKENS
CONV_CHUNK = 32
SORT_CHUNK = 256
FRONT_SORT_CHUNK = 640
SUB_SORTED = -(-(SUB_TOKENS * TOP_K + N_EXPERTS * (SUBLANES - 1)) // SORT_CHUNK) * SORT_CHUNK
SORTED_ROWS = SUBTILES * SUB_SORTED
MOE_ROWS = 512
HALF_ROWS = MOE_ROWS // 2
MOE_GROUPS = MOE_ROWS // SUBLANES
HALF_GROUPS = MOE_GROUPS // 2
X_SLOTS = 3
WEIGHT_DMA_PRIORITY = 1

POOL_HIST = 24
CONV_HIST = 32

BF16 = jnp.bfloat16
F32 = jnp.float32
I32 = jnp.int32
U32 = jnp.uint32
PACKED = D_MODEL // 2


def _rmsnorm(x, g):
    ms = jnp.mean(x * x, axis=-1, keepdims=True)
    return x * lax.rsqrt(ms + RMS_EPS) * g


def _sigmoid(x):
    return 1.0 / (1.0 + jnp.exp(-x))


def _layernorm_silu(d, g, b):
    mu = jnp.mean(d, axis=-1, keepdims=True)
    c = d - mu
    var = jnp.mean(c * c, axis=-1, keepdims=True)
    n = c * lax.rsqrt(var + LN_EPS) * g + b
    return n * _sigmoid(n)


def _pack_halves(a):
    half = a.shape[1] // 2
    hi = lax.bitcast_convert_type(a[:, :half], U32)
    lo = lax.bitcast_convert_type(a[:, half:], U32)
    return hi | (lo >> 16)


def _unpack_halves(w):
    hi = lax.bitcast_convert_type(w & jnp.uint32(0xFFFF0000), F32)
    lo = lax.bitcast_convert_type(w << 16, F32)
    return jnp.concatenate([hi, lo], axis=1)


def _route_and_sort(hn_bf, wr_t_ref, br_ref, tri_ref, q_ref, gate_ref, tbl_ref, xs_ref):
    rows = hn_bf.shape[0]
    lt = lax.dot_general(wr_t_ref[...], hn_bf, (((1,), (1,)), ((), ())),
                         preferred_element_type=F32)
    lt = lt + br_ref[...]
    eidx = lax.broadcasted_iota(I32, (N_EXPERTS, rows), 0)
    vals, hots = [], []
    for _ in range(TOP_K):
        m = jnp.max(lt, axis=0, keepdims=True)
        sel = jnp.min(jnp.where(lt == m, eidx, N_EXPERTS), axis=0, keepdims=True)
        hit = eidx == sel
        vals.append(m)
        hots.append(hit)
        lt = jnp.where(hit, -jnp.inf, lt)
    v = jnp.concatenate(vals, axis=0)
    e = jnp.exp(v - vals[0])
    gate = e / jnp.sum(e, axis=0, keepdims=True)
    gate_ref[...] = jnp.concatenate([gate, jnp.zeros((SUBLANES - TOP_K, rows), F32)], axis=0)

    sel_cnt = sum(h.astype(F32) for h in hots)
    lower = (lax.broadcasted_iota(I32, (N_EXPERTS, N_EXPERTS), 1)
             < lax.broadcasted_iota(I32, (N_EXPERTS, N_EXPERTS), 0)).astype(BF16)
    q_rows = []
    for s in range(SUBTILES):
        ts = slice(s * SUB_TOKENS, (s + 1) * SUB_TOKENS)
        sel_s = sel_cnt[:, ts]
        prefix = jnp.dot(sel_s.astype(BF16), tri_ref[...], preferred_element_type=F32)
        cnt = jnp.sum(sel_s, axis=1, keepdims=True)
        cnt_u = jnp.floor((cnt + (SUBLANES - 1)) * (1.0 / SUBLANES))
        cnt_u_l = jnp.broadcast_to(cnt_u, (N_EXPERTS, LANES))
        off_u_l = jnp.dot(lower, cnt_u_l.astype(BF16), preferred_element_type=F32)
        off_l = off_u_l * float(SUBLANES)
        tbl_ref[0, s, 0] = (cnt_u_l * float(SUBLANES)).astype(I32)
        tbl_ref[0, s, 1] = off_l.astype(I32)

        pos = prefix + off_l[:, :1]
        qs = [jnp.sum(jnp.where(h[:, ts], pos, 0.0), axis=0, keepdims=True).astype(I32) for h in hots]
        q_rows.append(jnp.concatenate(qs + [jnp.zeros((SUBLANES - TOP_K, SUB_TOKENS), I32)], axis=0))

        hn_s = hn_bf[ts]
        for c in range(SUB_SORTED // FRONT_SORT_CHUNK):
            r0 = c * FRONT_SORT_CHUNK
            jidx = r0 + lax.broadcasted_iota(I32, (FRONT_SORT_CHUNK, SUB_TOKENS), 0)
            p = sum(jnp.where(jidx == qk, 1.0, 0.0) for qk in qs)
            xs = jnp.dot(p.astype(BF16), hn_s, preferred_element_type=F32)
            xs_ref[pl.ds(s * SUB_SORTED + r0, FRONT_SORT_CHUNK), :] = _pack_halves(xs)
    q_ref[...] = jnp.concatenate(q_rows, axis=1)


def _mix_out_and_route(x, mixbuf, w_out_ref, g_moe_ref, wr_t_ref, br_ref, tri_ref,
                       h_ref, q_ref, gate_ref, tbl_ref, xs_ref):
    h = x + jnp.dot(mixbuf[...], w_out_ref[...], preferred_element_type=F32)
    h_ref[...] = h
    hn = _rmsnorm(h, g_moe_ref[...])
    _route_and_sort(hn.astype(BF16), wr_t_ref, br_ref, tri_ref, q_ref, gate_ref, tbl_ref, xs_ref)


def _front_prompt_kernel(x_ref, sp_ref, sc_ref, g_mix_ref, w_in_ref, pool_w_ref, pool_scale_ref,
                         conv_w_ref, conv_b_ref, ln_g_ref, ln_b_ref, w_out_ref, g_moe_ref,
                         wr_t_ref, br_ref, tri_ref, conv_w8_ref,
                         h_ref, q_ref, gate_ref, tbl_ref, xs_ref, np_ref, nc_ref,
                         ubuf, gsh, dbuf, mixbuf, *, pos0, rows):
    del conv_w_ref
    t = pl.program_id(1)

    @pl.when(t == 0)
    def _():
        ubuf[:, pl.ds(0, POOL_HIST), :] = jnp.zeros((len(POOL_WINDOWS), POOL_HIST, POOL_CH), F32)
        ubuf[0, pl.ds(POOL_HIST - POOL_BUF, POOL_BUF), :] = sp_ref[0, 0]
        gsh[0, pl.ds(CONV_HIST - CONV_BUF, CONV_BUF), :] = sc_ref[0, 0]

    x = x_ref[0]
    xn = _rmsnorm(x, g_mix_ref[...]).astype(BF16)
    proj = jnp.dot(xn, w_in_ref[...], preferred_element_type=F32)

    u = proj[:, :POOL_CH]
    ubuf[0, pl.ds(POOL_HIST, rows), :] = u
    pos = pos0 + t * rows + lax.broadcasted_iota(I32, (rows, 1), 0)
    span = rows + POOL_HIST - SUBLANES
    for g, w in enumerate(POOL_WINDOWS):
        assert w == 2 ** (g + 1) and w // 2 <= SUBLANES
        c0 = g * POOL_GROUP
        cs = slice(c0, c0 + POOL_GROUP)
        a = ubuf[g, pl.ds(SUBLANES, span), c0:] + ubuf[g, pl.ds(SUBLANES - w // 2, span), c0:]
        if g + 1 < len(POOL_WINDOWS):
            ubuf[g + 1, pl.ds(SUBLANES, span), c0 + POOL_GROUP:] = a[:, POOL_GROUP:]
        s = a[span - rows:, :POOL_GROUP]
        cnt = jnp.minimum(w, pos + 1).astype(F32)
        z = (s / cnt - u[:, cs]).astype(BF16)
        yg = jnp.dot(z, pool_w_ref[g], preferred_element_type=F32)
        mixbuf[:, cs] = (yg * pool_scale_ref[:, cs]).astype(BF16)

    glu = proj[:, POOL_CH:POOL_CH + CONV_CH] * _sigmoid(proj[:, POOL_CH + CONV_CH:])
    gsh[0, pl.ds(CONV_HIST, rows), :] = glu
    for s in range(1, SUBLANES):
        gsh[s, pl.ds(0, CONV_HIST - s), :] = gsh[0, pl.ds(s, CONV_HIST - s), :]
        gsh[s, pl.ds(CONV_HIST - s, rows), :] = glu

    def conv_chunk(i, carry):
        r0 = pl.multiple_of(i * CONV_CHUNK, CONV_CHUNK)
        acc = jnp.broadcast_to(conv_b_ref[...], (CONV_CHUNK // SUBLANES, SUBLANES, CONV_CH))
        for k in range(CONV_TAPS):
            first = CONV_HIST - CONV_BUF + k
            s = first % SUBLANES
            rows_k = gsh[s, pl.ds(r0 + (first - s), CONV_CHUNK), :]
            acc = acc + conv_w8_ref[k][None] * rows_k.reshape(CONV_CHUNK // SUBLANES, SUBLANES, CONV_CH)
        dbuf[pl.ds(r0, CONV_CHUNK), :] = acc.reshape(CONV_CHUNK, CONV_CH)
        return carry

    lax.fori_loop(0, rows // CONV_CHUNK, conv_chunk, 0, unroll=8)
    y_conv = _layernorm_silu(dbuf[...], ln_g_ref[...], ln_b_ref[...])
    mixbuf[:, POOL_CH:] = y_conv.astype(BF16)

    new_pool = ubuf[0, pl.ds(rows + POOL_HIST - POOL_BUF, POOL_BUF), :]
    new_conv = gsh[0, pl.ds(rows + CONV_HIST - CONV_BUF, CONV_BUF), :]
    np_ref[0, 0] = new_pool
    nc_ref[0, 0] = new_conv
    ubuf[0, pl.ds(POOL_HIST - POOL_BUF, POOL_BUF), :] = new_pool
    gsh[0, pl.ds(CONV_HIST - CONV_BUF, CONV_BUF), :] = new_conv

    _mix_out_and_route(x, mixbuf, w_out_ref, g_moe_ref, wr_t_ref, br_ref, tri_ref,
                       h_ref, q_ref, gate_ref, tbl_ref, xs_ref)


def _front_sample_kernel(x_ref, sp_ref, sc_ref, g_mix_ref, w_in_ref, pool_w_ref, pool_scale_ref,
                         conv_w_ref, conv_b_ref, ln_g_ref, ln_b_ref, w_out_ref, g_moe_ref,
                         wr_t_ref, br_ref, tri_ref, xs_in,
                         h_ref, q_ref, gate_ref, tbl_ref, xs_ref, np_ref, nc_ref,
                         xbuf, dbuf, mixbuf, *, pos0, seqs, steps):
    del xs_in
    for t in range(steps):
        xbuf[pl.ds(t * seqs, seqs), :] = x_ref[:, t, :]
    x = xbuf[...]
    xn = _rmsnorm(x, g_mix_ref[...]).astype(BF16)
    proj = jnp.dot(xn, w_in_ref[...], preferred_element_type=F32)

    def slab(a, t):
        return a[t * seqs:(t + 1) * seqs]

    u = proj[:, :POOL_CH]
    cat = [sp_ref[j] for j in range(POOL_BUF)]
    cat += [slab(u, t) for t in range(steps)]
    for t in range(steps):
        for g, w in enumerate(POOL_WINDOWS):
            cs = slice(g * POOL_GROUP, (g + 1) * POOL_GROUP)
            s = cat[POOL_BUF + t][:, cs]
            for j in range(1, w):
                s = s + cat[POOL_BUF + t - j][:, cs]
            cnt = float(min(w, pos0 + t + 1))
            z = (s / cnt - cat[POOL_BUF + t][:, cs]).astype(BF16)
            yg = jnp.dot(z, pool_w_ref[g], preferred_element_type=F32)
            mixbuf[pl.ds(t * seqs, seqs), cs] = (yg * pool_scale_ref[:, cs]).astype(BF16)
    for j in range(POOL_BUF):
        np_ref[j] = cat[len(cat) - POOL_BUF + j]

    glu = proj[:, POOL_CH:POOL_CH + CONV_CH] * _sigmoid(proj[:, POOL_CH + CONV_CH:])
    ccat = [sc_ref[j] for j in range(CONV_BUF)]
    ccat += [slab(glu, t) for t in range(steps)]
    for t in range(steps):
        acc = jnp.broadcast_to(conv_b_ref[...], (seqs, CONV_CH))
        for k in range(CONV_TAPS):
            acc = acc + conv_w_ref[k:k + 1, :] * ccat[t + k]
        dbuf[pl.ds(t * seqs, seqs), :] = acc
    for j in range(CONV_BUF):
        nc_ref[j] = ccat[len(ccat) - CONV_BUF + j]
    y_conv = _layernorm_silu(dbuf[...], ln_g_ref[...], ln_b_ref[...])
    mixbuf[:, POOL_CH:] = y_conv.astype(BF16)

    _mix_out_and_route(x, mixbuf, w_out_ref, g_moe_ref, wr_t_ref, br_ref, tri_ref,
                       h_ref, q_ref, gate_ref, tbl_ref, xs_ref)


def _const_spec(shape, single_buffer=False):
    mode = pl.Buffered(1) if single_buffer else None
    return pl.BlockSpec(shape, lambda *_: (0,) * len(shape), pipeline_mode=mode)


def _front_weight_specs():
    return [
        _const_spec((1, D_MODEL)),
        _const_spec((D_MODEL, IN_COLS), True),
        _const_spec((len(POOL_WINDOWS), POOL_GROUP, POOL_GROUP)),
        _const_spec((1, POOL_CH)),
        _const_spec((CONV_TAPS, CONV_CH)),
        _const_spec((1, CONV_CH)),
        _const_spec((1, CONV_CH)),
        _const_spec((1, CONV_CH)),
        _const_spec((D_MODEL, D_MODEL), True),
        _const_spec((1, D_MODEL)),
        _const_spec((N_EXPERTS, D_MODEL)),
        _const_spec((N_EXPERTS, 1)),
        _const_spec((SUB_TOKENS, SUB_TOKENS)),
    ]


def _front_out(n_tok, n_tiles_all, tile_map, tok_map, extra_specs, extra_shapes):
    specs = [pl.BlockSpec((TILE, D_MODEL), tok_map),
             pl.BlockSpec((SUBLANES, TILE), lambda *i: (0, tok_map(*i)[0])),
             pl.BlockSpec((SUBLANES, TILE), lambda *i: (0, tok_map(*i)[0])),
             pl.BlockSpec((1, SUBTILES, 2, N_EXPERTS, LANES),
                          lambda *i: (tok_map(*i)[0], 0, 0, 0, 0)),
             pl.BlockSpec((SORTED_ROWS, PACKED), tile_map)]
    shapes = [jax.ShapeDtypeStruct((n_tok, D_MODEL), F32),
              jax.ShapeDtypeStruct((SUBLANES, n_tok), I32),
              jax.ShapeDtypeStruct((SUBLANES, n_tok), F32),
              jax.ShapeDtypeStruct((n_tok // TILE, SUBTILES, 2, N_EXPERTS, LANES), I32),
              jax.ShapeDtypeStruct((n_tiles_all * SORTED_ROWS, PACKED), U32)]
    return specs + extra_specs, shapes + extra_shapes


def _front_prompt(x, state_pool, state_conv, weights, conv_w8, *, pos0, n_tiles_all):
    nb, nt, _ = x.shape
    rows = TILE
    tiles = nt // rows
    tok_map = lambda b, t: (b * tiles + t, 0)
    out_specs, out_shape = _front_out(
        nb * nt, n_tiles_all, tok_map, tok_map,
        [pl.BlockSpec((1, 1, POOL_BUF, POOL_CH), lambda b, t: (0, b, 0, 0)),
         pl.BlockSpec((1, 1, CONV_BUF, CONV_CH), lambda b, t: (0, b, 0, 0))],
        [jax.ShapeDtypeStruct((1, nb, POOL_BUF, POOL_CH), F32),
         jax.ShapeDtypeStruct((1, nb, CONV_BUF, CONV_CH), F32)])
    kern = functools.partial(_front_prompt_kernel, pos0=pos0, rows=rows)
    return pl.pallas_call(
        kern,
        grid=(nb, tiles),
        in_specs=[pl.BlockSpec((1, rows, D_MODEL), lambda b, t: (b, t, 0)),
                  pl.BlockSpec((1, 1, POOL_BUF, POOL_CH), lambda b, t: (0, b, 0, 0)),
                  pl.BlockSpec((1, 1, CONV_BUF, CONV_CH), lambda b, t: (0, b, 0, 0))]
                 + _front_weight_specs() + [_const_spec((CONV_TAPS, SUBLANES, CONV_CH))],
        out_specs=out_specs,
        out_shape=out_shape,
        scratch_shapes=[pltpu.VMEM((len(POOL_WINDOWS), POOL_HIST + rows, POOL_CH), F32),
                        pltpu.VMEM((SUBLANES, CONV_HIST + rows, CONV_CH), F32),
                        pltpu.VMEM((rows, CONV_CH), F32),
                        pltpu.VMEM((rows, D_MODEL), BF16)],
        compiler_params=pltpu.CompilerParams(
            dimension_semantics=("arbitrary", "arbitrary"), vmem_limit_bytes=VMEM_LIMIT_BYTES),
        name="front_prompt",
    )(x, state_pool, state_conv, *weights, conv_w8)


def _front_sample(x, state_pool, state_conv, weights, xs_all, *, pos0, tile0):
    nb, nt, _ = x.shape
    seqs = TILE // nt
    n_tiles_all = xs_all.shape[0] // SORTED_ROWS
    tok_map = lambda i: (i, 0)
    out_specs, out_shape = _front_out(
        nb * nt, n_tiles_all, lambda i: (tile0 + i, 0), tok_map,
        [pl.BlockSpec((POOL_BUF, seqs, POOL_CH), lambda i: (0, i, 0)),
         pl.BlockSpec((CONV_BUF, seqs, CONV_CH), lambda i: (0, i, 0))],
        [jax.ShapeDtypeStruct((POOL_BUF, nb, POOL_CH), F32),
         jax.ShapeDtypeStruct((CONV_BUF, nb, CONV_CH), F32)])
    kern = functools.partial(_front_sample_kernel, pos0=pos0, seqs=seqs, steps=nt)
    n_in = 3 + len(weights)
    return pl.pallas_call(
        kern,
        grid=(nb // seqs,),
        in_specs=[pl.BlockSpec((seqs, nt, D_MODEL), lambda i: (i, 0, 0)),
                  pl.BlockSpec((POOL_BUF, seqs, POOL_CH), lambda i: (0, i, 0)),
                  pl.BlockSpec((CONV_BUF, seqs, CONV_CH), lambda i: (0, i, 0))]
                 + _front_weight_specs() + [pl.BlockSpec(memory_space=pl.ANY)],
        out_specs=out_specs,
        out_shape=out_shape,
        scratch_shapes=[pltpu.VMEM((TILE, D_MODEL), F32),
                        pltpu.VMEM((TILE, CONV_CH), F32),
                        pltpu.VMEM((TILE, D_MODEL), BF16)],
        input_output_aliases={n_in: 4},
        compiler_params=pltpu.CompilerParams(
            dimension_semantics=("arbitrary",), vmem_limit_bytes=VMEM_LIMIT_BYTES),
        name="front_sample",
    )(x, state_pool, state_conv, *weights, xs_all)


def _moe_kernel(be_ref, nused_ref, src_ref, dst_ref, first_ref, wslot_ref, nxt_ref, hasb_ref,
                xs_hbm, wgu_hbm, bgu_ref, wd_hbm, bd_ref, ys_hbm,
                xa, xb, oa, ob, wgu_f32, wd_f32, wgu_bf, wd_bf,
                gsem_a, gsem_b, ssem_a, ssem_b, wsem):
    i = pl.program_id(0)
    n_used = nused_ref[0]
    par = i % 2
    slot = lax.rem(i, X_SLOTS)
    half_a = 2 * i
    half_b = 2 * i + 1

    def weight_copies(e, s):
        return (pltpu.make_async_copy(wgu_hbm.at[e], wgu_f32.at[s], wsem.at[s]),
                pltpu.make_async_copy(wd_hbm.at[e], wd_f32.at[s], wsem.at[s]))

    def gather(half, buf, sem):
        for g in range(HALF_GROUPS):
            hrow = pl.multiple_of(src_ref[half * HALF_GROUPS + g], SUBLANES)
            pltpu.make_async_copy(xs_hbm.at[pl.ds(hrow, SUBLANES)],
                                  buf.at[pl.ds(g * SUBLANES, SUBLANES)], sem).start()

    def gather_wait(buf, sem):
        pltpu.make_async_copy(xs_hbm.at[pl.ds(0, HALF_ROWS)], buf, sem).wait()

    def scatter(half, buf, sem):
        for g in range(HALF_GROUPS):
            hrow = pl.multiple_of(dst_ref[half * HALF_GROUPS + g], SUBLANES)
            pltpu.make_async_copy(buf.at[pl.ds(g * SUBLANES, SUBLANES)],
                                  ys_hbm.at[pl.ds(hrow, SUBLANES)], sem).start()

    def scatter_wait(buf, sem):
        pltpu.make_async_copy(buf, ys_hbm.at[pl.ds(0, HALF_ROWS)], sem).wait()

    def experts(xbuf, obuf):
        x = _unpack_halves(xbuf[slot]).astype(BF16)
        gu = jnp.dot(x, wgu_bf[...], preferred_element_type=F32) + bgu_ref[0]
        g = jnp.minimum(gu[:, :D_FF], SWIGLU_LIMIT)
        up = jnp.clip(gu[:, D_FF:], -SWIGLU_LIMIT, SWIGLU_LIMIT)
        hdn = (up + 1.0) * (g * _sigmoid(SWIGLU_ALPHA * g))
        y = jnp.dot(hdn.astype(BF16), wd_bf[...], preferred_element_type=F32) + bd_ref[0]
        obuf[par] = _pack_halves(y.astype(BF16).astype(F32))

    def gather_block(blk, slot):
        blk = jnp.minimum(blk, n_used - 1)
        gather(2 * blk, xa.at[slot], gsem_a.at[slot])
        gather(2 * blk + 1, xb.at[slot], gsem_b.at[slot])

    def gather_block_wait(slot):
        gather_wait(xa.at[slot], gsem_a.at[slot])
        gather_wait(xb.at[slot], gsem_b.at[slot])

    @pl.when(i == 0)
    def _():
        gather_block(0, 0)
        gather_block(1, 1)
        for c in weight_copies(be_ref[0], 0):
            c.start(priority=WEIGHT_DMA_PRIORITY)

    @pl.when(i < n_used)
    def _():
        @pl.when(first_ref[i] == 1)
        def _():
            s = wslot_ref[i]
            for c in weight_copies(be_ref[i], s):
                c.wait()
            wgu_bf[...] = wgu_f32[s].astype(BF16)
            wd_bf[...] = wd_f32[s].astype(BF16)

            @pl.when(nxt_ref[i] < N_EXPERTS)
            def _():
                for c in weight_copies(nxt_ref[i], 1 - s):
                    c.start(priority=WEIGHT_DMA_PRIORITY)

        def wait_writeback(step, p):
            scatter_wait(oa.at[p], ssem_a.at[p])

            @pl.when(hasb_ref[step] == 1)
            def _():
                scatter_wait(ob.at[p], ssem_b.at[p])

        @pl.when(i >= 2)
        def _():
            wait_writeback(i - 2, par)

        gather_block_wait(slot)

        @pl.when(hasb_ref[i] == 1)
        def _():
            gather_block(i + 2, lax.rem(i + 2, X_SLOTS))
            experts(xa, oa)
            scatter(half_a, oa.at[par], ssem_a.at[par])
            experts(xb, ob)
            scatter(half_b, ob.at[par], ssem_b.at[par])

        @pl.when(hasb_ref[i] == 0)
        def _():
            gather_block(i + 2, lax.rem(i + 2, X_SLOTS))
            experts(xa, oa)
            scatter(half_a, oa.at[par], ssem_a.at[par])

        @pl.when(i == n_used - 1)
        def _():
            wait_writeback(i, par)
            gather_block_wait(lax.rem(i + 1, X_SLOTS))
            gather_block_wait(lax.rem(i + 2, X_SLOTS))

            @pl.when(i >= 1)
            def _():
                wait_writeback(i - 1, 1 - par)


def _moe(xs_all, tables, w_gate_up, b_gate_up, w_down, b_down):
    block_e, n_used, src, dst, first, wslot, nxt, hasb = tables
    n_blocks = block_e.shape[0]

    def e_map(i, be, nu, *_):
        return (be[jnp.minimum(i, nu[0] - 1)], 0, 0)

    x_buf = pltpu.VMEM((X_SLOTS, HALF_ROWS, PACKED), U32)
    o_buf = pltpu.VMEM((2, HALF_ROWS, PACKED), U32)
    any_spec = pl.BlockSpec(memory_space=pl.ANY)
    grid_spec = pltpu.PrefetchScalarGridSpec(
        num_scalar_prefetch=8,
        grid=(n_blocks,),
        in_specs=[any_spec,
                  any_spec,
                  pl.BlockSpec((1, 1, 2 * D_FF), e_map),
                  any_spec,
                  pl.BlockSpec((1, 1, D_MODEL), e_map)],
        out_specs=any_spec,
        scratch_shapes=[x_buf, x_buf, o_buf, o_buf,
                        pltpu.VMEM((2, D_MODEL, 2 * D_FF), F32),
                        pltpu.VMEM((2, D_FF, D_MODEL), F32),
                        pltpu.VMEM((D_MODEL, 2 * D_FF), BF16),
                        pltpu.VMEM((D_FF, D_MODEL), BF16)]
                       + [pltpu.SemaphoreType.DMA((X_SLOTS,))] * 2 + [pltpu.SemaphoreType.DMA((2,))] * 3,
    )
    return pl.pallas_call(
        _moe_kernel,
        grid_spec=grid_spec,
        out_shape=jax.ShapeDtypeStruct((xs_all.shape[0] + 2 * MOE_ROWS, PACKED), U32),
        compiler_params=pltpu.CompilerParams(
            dimension_semantics=("arbitrary",), vmem_limit_bytes=VMEM_LIMIT_BYTES),
        name="moe_experts",
    )(block_e, n_used, src, dst, first, wslot, nxt, hasb,
      xs_all, w_gate_up, b_gate_up.reshape(N_EXPERTS, 1, 2 * D_FF), w_down,
      b_down.reshape(N_EXPERTS, 1, D_MODEL))


def _final_kernel(used_ref, h_ref, ys_ref, q_ref, gate_ref, g_ref, o_ref, *, tile0, slabs):
    i = pl.program_id(0)

    def to_cols(a):
        pad = jnp.zeros((LANES - SUBLANES, TILE), F32)
        return jnp.concatenate([a, pad], axis=0).T

    qc_all = to_cols(q_ref[...].astype(F32))
    gc_all = to_cols(gate_ref[...])
    parts = []
    for s in range(SUBTILES):
        ts = slice(s * SUB_TOKENS, (s + 1) * SUB_TOKENS)
        used = used_ref[(tile0 + i) * SUBTILES + s]
        qc, gc = qc_all[ts], gc_all[ts]
        acc = h_ref[ts, :]
        for c in range(SUB_SORTED // SORT_CHUNK):
            jl = (c * SORT_CHUNK + lax.broadcasted_iota(I32, (SUB_TOKENS, SORT_CHUNK), 1)).astype(F32)
            pg = sum(jnp.where(qc[:, k:k + 1] == jl, gc[:, k:k + 1], 0.0) for k in range(TOP_K))
            rid = c * SORT_CHUNK + lax.broadcasted_iota(I32, (SORT_CHUNK, 1), 0)
            rows = ys_ref[pl.ds(s * SUB_SORTED + c * SORT_CHUNK, SORT_CHUNK), :]
            ys = jnp.where(rid < used, _unpack_halves(rows), 0.0)
            acc = acc + jnp.dot(pg.astype(BF16), ys.astype(BF16), preferred_element_type=F32)
        parts.append(acc)
    y = _rmsnorm(jnp.concatenate(parts, axis=0), g_ref[...])
    if slabs == 1:
        o_ref[...] = y
    else:
        seqs = TILE // slabs
        for t in range(slabs):
            o_ref[:, t, :] = y[t * seqs:(t + 1) * seqs]


def _final(h, ys_all, q, gates, used, g_final, *, tile0, slabs):
    n_tok = h.shape[0]
    if slabs == 1:
        out_spec = pl.BlockSpec((TILE, D_MODEL), lambda i, u: (i, 0))
        out_shape = jax.ShapeDtypeStruct((n_tok, D_MODEL), F32)
    else:
        seqs = TILE // slabs
        out_spec = pl.BlockSpec((seqs, slabs, D_MODEL), lambda i, u: (i, 0, 0))
        out_shape = jax.ShapeDtypeStruct((n_tok // slabs, slabs, D_MODEL), F32)
    grid_spec = pltpu.PrefetchScalarGridSpec(
        num_scalar_prefetch=1,
        grid=(n_tok // TILE,),
        in_specs=[pl.BlockSpec((TILE, D_MODEL), lambda i, u: (i, 0)),
                  pl.BlockSpec((SORTED_ROWS, PACKED), lambda i, u: (tile0 + i, 0)),
                  pl.BlockSpec((SUBLANES, TILE), lambda i, u: (0, i)),
                  pl.BlockSpec((SUBLANES, TILE), lambda i, u: (0, i)),
                  pl.BlockSpec((1, D_MODEL), lambda i, u: (0, 0))],
        out_specs=out_spec,
    )
    return pl.pallas_call(
        functools.partial(_final_kernel, tile0=tile0, slabs=slabs),
        grid_spec=grid_spec,
        out_shape=out_shape,
        compiler_params=pltpu.CompilerParams(
            dimension_semantics=("arbitrary",), vmem_limit_bytes=VMEM_LIMIT_BYTES),
        name="final_norm",
    )(used, h, ys_all, q, gates, g_final)


def _block_tables(tbl, n_blocks):
    tbl = tbl.reshape(-1, 2, N_EXPERTS, LANES)
    cnt = tbl[:, 0, :, 0]
    off = tbl[:, 1, :, 0]
    run = jnp.cumsum(cnt, axis=0) - cnt
    total = jnp.sum(cnt, axis=0)
    nblk = (total + MOE_ROWS - 1) // MOE_ROWS
    blk_end = jnp.cumsum(nblk)
    blk_start = blk_end - nblk
    bidx = jnp.arange(n_blocks, dtype=I32)
    eidx = jnp.arange(N_EXPERTS, dtype=I32)
    block_e = jnp.minimum(jnp.sum(blk_end[None, :] <= bidx[:, None], axis=1), N_EXPERTS - 1).astype(I32)
    hot = block_e[:, None] == eidx[None, :]

    def per_block(v):
        return jnp.sum(jnp.where(hot, v[..., None, :], 0), axis=-1)

    row0 = (bidx - per_block(blk_start)) * MOE_ROWS
    n_tiles = cnt.shape[0]
    groups = jnp.arange(MOE_GROUPS, dtype=I32)
    erow = row0[:, None] + groups[None, :] * SUBLANES
    run_b, cnt_b, off_b = per_block(run), per_block(cnt), per_block(off)
    inside = jnp.logical_and(run_b[:, :, None] <= erow[None], erow[None] < (run_b + cnt_b)[:, :, None])
    tile_base = jnp.arange(n_tiles, dtype=I32)[:, None] * SUB_SORTED + off_b - run_b
    src = jnp.sum(jnp.where(inside, tile_base[:, :, None] + erow[None], 0), axis=0)
    dump = n_tiles * SUB_SORTED + (bidx % 2)[:, None] * MOE_ROWS + groups[None, :] * SUBLANES
    dst = jnp.where(erow < per_block(total)[:, None], src, dump).reshape(-1)
    src = src.reshape(-1)
    used = jnp.sum(cnt, axis=1)
    has_rows = nblk > 0
    first = (bidx == per_block(blk_start)).astype(I32)
    wslot = per_block((jnp.cumsum(has_rows) - 1) % 2)
    later = jnp.where(jnp.logical_and(has_rows[None, :], eidx[None, :] > eidx[:, None]), eidx[None, :], N_EXPERTS)
    nxt = per_block(jnp.min(later, axis=1))
    hasb = (per_block(total) - row0 > HALF_ROWS).astype(I32)
    return ((block_e, blk_end[-1:].astype(I32), src.astype(I32), dst.astype(I32), first,
             wslot.astype(I32), nxt.astype(I32), hasb), used.astype(I32))


def kernel(x_prompt, x_sample, state_pool, state_conv, g_mix, w_in, pool_w, pool_scale, conv_w, conv_b,
           conv_ln_g, conv_ln_b, w_out, g_moe, w_router, b_router, w_gate_up, b_gate_up, w_down, b_down,
           g_final):
    nb_p, nt_p, _ = x_prompt.shape
    nb_s, nt_s, _ = x_sample.shape
    n_p, n_s = nb_p * nt_p, nb_s * nt_s
    assert nt_p % TILE == 0 and TILE % nt_s == 0 and n_s % TILE == 0
    tiles_p, tiles_s = n_p // TILE, n_s // TILE
    n_tiles = tiles_p + tiles_s

    tri_shape = (SUB_TOKENS, SUB_TOKENS)
    tri = (lax.broadcasted_iota(I32, tri_shape, 0) < lax.broadcasted_iota(I32, tri_shape, 1)).astype(BF16)
    weights = (g_mix[0][None], w_in[0].astype(BF16), pool_w[0].astype(BF16), pool_scale[0][None],
               conv_w[0], conv_b[0][None], conv_ln_g[0][None], conv_ln_b[0][None],
               w_out[0].astype(BF16), g_moe[0][None], w_router[0].T.astype(BF16),
               b_router[0][:, None], tri)

    time_major = lambda a: jnp.transpose(a, (1, 0, 2))
    zero_pool = jnp.zeros((1, nb_p, POOL_BUF, POOL_CH), F32)
    zero_conv = jnp.zeros((1, nb_p, CONV_BUF, CONV_CH), F32)
    h_p, q_p, gate_p, tbl_p, xs_all, pool_p, conv_p = _front_prompt(
        x_prompt, zero_pool, zero_conv, weights,
        jnp.broadcast_to(conv_w[0][:, None, :], (CONV_TAPS, SUBLANES, CONV_CH)), pos0=0, n_tiles_all=n_tiles)
    h_s, q_s, gate_s, tbl_s, xs_all, pool_s, conv_s = _front_sample(
        x_sample, time_major(state_pool[0]), time_major(state_conv[0]), weights, xs_all,
        pos0=PAST_LEN, tile0=tiles_p)

    sorted_total = n_tiles * SUBTILES * (SUB_TOKENS * TOP_K + N_EXPERTS * (SUBLANES - 1))
    n_blocks = -(-sorted_total // MOE_ROWS) + N_EXPERTS
    tables, used = _block_tables(jnp.concatenate([tbl_p, tbl_s], axis=0), n_blocks)
    ys_all = _moe(xs_all, tables, w_gate_up[0], b_gate_up[0], w_down[0], b_down[0])

    gf = g_final[None]
    y_prompt = _final(h_p, ys_all, q_p, gate_p, used, gf, tile0=0, slabs=1)
    y_sample = _final(h_s, ys_all, q_s, gate_s, used, gf, tile0=tiles_p, slabs=nt_s)
    return (y_prompt.reshape(nb_p, nt_p, D_MODEL), y_sample,
            pool_p, conv_p,
            time_major(pool_s)[None], time_major(conv_s)[None])
```
